```python
import math
import jax, jax.numpy as jnp
from jax import lax
import numpy as np

D_MODEL = 1024
BATCH = 2
SEQ = 16384
DEPTH = 2
DEC_BATCH = 8
DEC_SEQ = 32
PAST_LEN = 2048

CHUNK = 64
CONV_CH = 256
CONV_W = 31
SSM_HEADS = 6
SSM_HEAD_DIM = 64
SSM_INNER = SSM_HEADS * SSM_HEAD_DIM
SSM_STATE = 64
SSM_GROUPS = 2
SSM_CONV_W = 4
SSM_XBC = SSM_INNER + 2 * SSM_GROUPS * SSM_STATE
GDN_HEADS = 6
GDN_HEAD_DIM = 64
GDN_INNER = GDN_HEADS * GDN_HEAD_DIM
GDN_CONV_W = 4
GDN_QKV = 3 * GDN_INNER
MIX_WIDTH = CONV_CH + SSM_INNER + GDN_INNER
IN_SIZES = (2 * CONV_CH, SSM_INNER, SSM_XBC, SSM_HEADS, GDN_QKV, GDN_INNER, GDN_HEADS, GDN_HEADS)
IN_COLS = sum(IN_SIZES)
IN_SPLIT_IDX = tuple(int(i) for i in np.cumsum(IN_SIZES)[:-1])
FFN_DIM = 2816
N_EXPERTS = 8
TOP_K = 2
EXPERT_DIM = 2816
MOE_BLOCK = 128
N_DENSE = (DEPTH + 1) // 2
N_MOE = DEPTH // 2
EPS = 1e-6

MIXER_PARAM_NAMES = ('w_in', 'conv_a_w', 'conv_a_b', 'ln_a_g', 'ln_a_b',
                     'ssm_conv_w', 'ssm_conv_b', 'ssm_dt_bias', 'ssm_a_log', 'ssm_d', 'ssm_norm_g',
                     'gdn_conv_w', 'gdn_a_log', 'gdn_dt_bias', 'gdn_norm_g', 'w_out')

kernel_name = 'hymba_style_conformer_ssd_gdn_stream_step'

F32 = jnp.float32


def rms_normalize(x):
    xf = x.astype(F32)
    return xf * lax.rsqrt(jnp.mean(xf * xf, axis=-1, keepdims=True) + EPS)


def rmsnorm(x, g):
    return (rms_normalize(x) * g.astype(F32)).astype(x.dtype)


def layer_norm(x, g, b):
    xf = x.astype(F32)
    mu = jnp.mean(xf, axis=-1, keepdims=True)
    var = jnp.mean(jnp.square(xf - mu), axis=-1, keepdims=True)
    return (xf - mu) * lax.rsqrt(var + EPS) * g.astype(F32) + b.astype(F32)


def l2_normalize(x):
    return x * lax.rsqrt(jnp.sum(x * x, axis=-1, keepdims=True) + EPS)


def causal_dwconv(x, hist, w):
    xp = jnp.concatenate([hist.astype(x.dtype), x], axis=1)
    y = lax.conv_general_dilated(xp, w[:, None, :].astype(x.dtype), window_strides=(1,), padding='VALID',
                                 dimension_numbers=('NWC', 'WIO', 'NWC'), feature_group_count=x.shape[-1])
    return y, xp[:, -(w.shape[0] - 1):]


def conformer_conv_group(a_in, hist, conv_w, conv_b, ln_g, ln_b):
    val, gate = jnp.split(a_in, 2, axis=-1)
    glu = val * jax.nn.sigmoid(gate)
    y, new_hist = causal_dwconv(glu, hist, conv_w)
    y = layer_norm(y + conv_b, ln_g, ln_b)
    return jax.nn.silu(y).astype(a_in.dtype), new_hist


def ssd_chunked(x, a, b, c, h0):
    bsz, t, nh, hp = x.shape
    n = b.shape[-1]
    L = min(CHUNK, t)
    nc = t // L
    x = x.reshape(bsz, nc, L, nh, hp)
    b = b.reshape(bsz, nc, L, nh, n)
    c = c.reshape(bsz, nc, L, nh, n)
    a_cum = jnp.cumsum(a.reshape(bsz, nc, L, nh), axis=2)
    a_h = jnp.swapaxes(a_cum, 2, 3)
    causal = jnp.tril(jnp.ones((L, L), bool))
    decay = jnp.exp(jnp.where(causal, a_h[..., :, None] - a_h[..., None, :], -jnp.inf))
    scores = jnp.einsum('bclhn,bcshn->bchls', c, b) * decay
    y_diag = jnp.einsum('bchls,bcshp->bclhp', scores, x)
    to_end = jnp.exp(a_cum[:, :, -1:, :] - a_cum)
    chunk_states = jnp.einsum('bclh,bclhn,bclhp->bchpn', to_end, b, x)
    chunk_decay = jnp.exp(a_cum[:, :, -1, :])

    def step(h, inp):
        st, dec = inp
        return h * dec[..., None, None] + st, h

    h_fin, h_prev = lax.scan(step, h0, (jnp.moveaxis(chunk_states, 1, 0), jnp.moveaxis(chunk_decay, 1, 0)))
    h_prev = jnp.moveaxis(h_prev, 0, 1)
    y_off = jnp.einsum('bclhn,bchpn->bclhp', c, h_prev) * jnp.exp(a_cum)[..., None]
    return (y_diag + y_off).reshape(bsz, t, nh, hp), h_fin


def mamba2_group(z, xbc, dt_raw, conv_hist, h0, conv_w, conv_b, dt_bias, a_log, d_skip, norm_g):
    bsz, t, _ = xbc.shape
    xbc_c, new_hist = causal_dwconv(xbc, conv_hist, conv_w)
    xbc_c = jax.nn.silu(xbc_c.astype(F32) + conv_b.astype(F32))
    xs, bm, cm = jnp.split(xbc_c, [SSM_INNER, SSM_INNER + SSM_GROUPS * SSM_STATE], axis=-1)
    xs = xs.reshape(bsz, t, SSM_HEADS, SSM_HEAD_DIM)
    rep = SSM_HEADS // SSM_GROUPS
    bm = jnp.repeat(bm.reshape(bsz, t, SSM_GROUPS, SSM_STATE), rep, axis=2)
    cm = jnp.repeat(cm.reshape(bsz, t, SSM_GROUPS, SSM_STATE), rep, axis=2)
    dt = jax.nn.softplus(dt_raw.astype(F32) + dt_bias.astype(F32))
    a = -jnp.exp(a_log.astype(F32)) * dt
    y, h_new = ssd_chunked(xs * dt[..., None], a, bm, cm, h0.astype(F32))
    y = (y + d_skip.astype(F32)[:, None] * xs).reshape(bsz, t, SSM_INNER)
    y = y * jax.nn.silu(z.astype(F32))
    y = rms_normalize(y.reshape(bsz, t, SSM_GROUPS, SSM_INNER // SSM_GROUPS)).reshape(bsz, t, SSM_INNER)
    y = y * norm_g.astype(F32)
    return y.astype(xbc.dtype), new_hist, h_new.astype(xbc.dtype)


def gated_delta_chunked(q, k, v, g, beta, s0):
    bsz, t, nh, dk = q.shape
    dv = v.shape[-1]
    L = min(CHUNK, t)
    nc = t // L
    blk = lambda m: jnp.swapaxes(m.reshape(bsz, nc, L, nh, -1), 2, 3)
    q, k, v = blk(q), blk(k), blk(v)
    g_cum = jnp.cumsum(jnp.swapaxes(g.reshape(bsz, nc, L, nh), 2, 3), axis=-1)
    beta = jnp.swapaxes(beta.reshape(bsz, nc, L, nh), 2, 3)
    incl = jnp.tril(jnp.ones((L, L), bool))
    strict = jnp.tril(jnp.ones((L, L), bool), -1)
    dec_incl = jnp.exp(jnp.where(incl, g_cum[..., :, None] - g_cum[..., None, :], -jnp.inf))
    dec_strict = jnp.where(strict, dec_incl, 0.0)
    m = beta[..., None] * jnp.einsum('bchld,bchmd->bchlm', k, k) * dec_strict
    rhs = jnp.concatenate([v * beta[..., None], k * (beta * jnp.exp(g_cum))[..., None]], axis=-1)
    sol = lax.linalg.triangular_solve(jnp.eye(L, dtype=m.dtype) + m, rhs,
                                      left_side=True, lower=True, unit_diagonal=True)
    u, w = sol[..., :dv], sol[..., dv:]
    attn = jnp.einsum('bchld,bchmd->bchlm', q, k) * dec_incl
    q_dec = q * jnp.exp(g_cum)[..., None]
    k_dec = k * jnp.exp(g_cum[..., -1:] - g_cum)[..., None]
    chunk_dec = jnp.exp(g_cum[..., -1])

    def step(s, inp):
        u_c, w_c, q_c, k_c, a_c, d_c = inp
        v_new = u_c - jnp.einsum('bhlk,bhkv->bhlv', w_c, s)
        o_c = jnp.einsum('bhlk,bhkv->bhlv', q_c, s) + jnp.einsum('bhlm,bhmv->bhlv', a_c, v_new)
        s = s * d_c[..., None, None] + jnp.einsum('bhlk,bhlv->bhkv', k_c, v_new)
        return s, o_c

    s_fin, o = lax.scan(step, s0, tuple(jnp.moveaxis(m_, 1, 0) for m_ in (u, w, q_dec, k_dec, attn, chunk_dec)))
    o = jnp.moveaxis(o, 0, 1)
    return jnp.swapaxes(o, 2, 3).reshape(bsz, t, nh, dv), s_fin


def gdn_group(qkv, gate, b_raw, a_raw, conv_hist, s0, conv_w, a_log, dt_bias, norm_g):
    bsz, t, _ = qkv.shape
    qkv_c, new_hist = causal_dwconv(qkv, conv_hist, conv_w)
    qkv_c = jax.nn.silu(qkv_c.astype(F32))
    q, k, v = (m.reshape(bsz, t, GDN_HEADS, GDN_HEAD_DIM) for m in jnp.split(qkv_c, 3, axis=-1))
    q = l2_normalize(q) * (GDN_HEAD_DIM ** -0.5)
    k = l2_normalize(k)
    beta = jax.nn.sigmoid(b_raw.astype(F32))
    g = -jnp.exp(a_log.astype(F32)) * jax.nn.softplus(a_raw.astype(F32) + dt_bias.astype(F32))
    o, s_new = gated_delta_chunked(q, k, v, g, beta, s0.astype(F32))
    o = rms_normalize(o) * norm_g.astype(F32) * jax.nn.silu(gate.astype(F32).reshape(bsz, t, GDN_HEADS, GDN_HEAD_DIM))
    return o.reshape(bsz, t, GDN_INNER).astype(qkv.dtype), new_hist, s_new.astype(qkv.dtype)


def mixer_sublayer(u, st_conv_a, st_ssm_conv, st_ssm, st_gdn_conv, st_gdn, p):
    proj = u @ p['w_in']
    a_in, z, xbc, dt_raw, qkv, gdn_gate, b_raw, a_raw = jnp.split(proj, list(IN_SPLIT_IDX), axis=-1)
    ya, conv_a = conformer_conv_group(a_in, st_conv_a, p['conv_a_w'], p['conv_a_b'], p['ln_a_g'], p['ln_a_b'])
    yb, ssm_conv, ssm = mamba2_group(z, xbc, dt_raw, st_ssm_conv, st_ssm, p['ssm_conv_w'], p['ssm_conv_b'],
                                     p['ssm_dt_bias'], p['ssm_a_log'], p['ssm_d'], p['ssm_norm_g'])
    yc, gdn_conv, gdn = gdn_group(qkv, gdn_gate, b_raw, a_raw, st_gdn_conv, st_gdn, p['gdn_conv_w'],
                                  p['gdn_a_log'], p['gdn_dt_bias'], p['gdn_norm_g'])
    out = jnp.concatenate([ya, yb, yc], axis=-1) @ p['w_out']
    return out, conv_a, ssm_conv, ssm, gdn_conv, gdn


def swiglu(h, w_gate, w_up, w_down):
    return (jax.nn.silu(h @ w_gate) * (h @ w_up)) @ w_down


def moe_swiglu(h, w_router, b_router, w_gate, w_up, w_down):
    hf = h.reshape(-1, D_MODEL)
    n_tok = hf.shape[0]
    logits = hf.astype(F32) @ w_router.astype(F32) + b_router.astype(F32)
    top_logit, top_idx = lax.top_k(logits, TOP_K)
    gates = jax.nn.softmax(top_logit, axis=-1)
    n_assign = n_tok * TOP_K
    flat_exp = top_idx.reshape(-1).astype(jnp.int32)
    flat_tok = jnp.arange(n_assign, dtype=jnp.int32) // TOP_K
    order = jnp.argsort(flat_exp)
    sorted_exp = flat_exp[order]
    sorted_tok = flat_tok[order]
    sorted_gate = gates.reshape(-1)[order]
    counts = jnp.bincount(flat_exp, length=N_EXPERTS).astype(jnp.int32)
    padded = (counts + MOE_BLOCK - 1) // MOE_BLOCK * MOE_BLOCK
    pad_end = jnp.cumsum(padded)
    pad_start = pad_end - padded
    start = jnp.cumsum(counts) - counts
    dest = pad_start[sorted_exp] + jnp.arange(n_assign, dtype=jnp.int32) - start[sorted_exp]
    n_blocks = -(-n_assign // MOE_BLOCK) + N_EXPERTS
    slot_tok = jnp.full((n_blocks * MOE_BLOCK,), n_tok, jnp.int32).at[dest].set(sorted_tok)
    block_exp = jnp.minimum(jnp.searchsorted(pad_end, jnp.arange(n_blocks, dtype=jnp.int32) * MOE_BLOCK,
                                             side='right'), N_EXPERTS - 1)
    h_pad = jnp.concatenate([hf, jnp.zeros((1, D_MODEL), hf.dtype)], axis=0)
    xb = h_pad[slot_tok].reshape(n_blocks, MOE_BLOCK, D_MODEL)

    def expert_block(args):
        x_blk, e = args
        return swiglu(x_blk, w_gate[e], w_up[e], w_down[e])

    yb = lax.map(expert_block, (xb, block_exp)).reshape(-1, D_MODEL)
    y_assign = yb[dest] * sorted_gate[:, None].astype(yb.dtype)
    y = jnp.zeros_like(hf).at[sorted_tok].add(y_assign)
    return y.reshape(h.shape)


def encoder_trunk(x, st_conv_a, st_ssm_conv, st_ssm, st_gdn_conv, st_gdn, prm):
    new_conv_a, new_ssm_conv, new_ssm, new_gdn_conv, new_gdn = [], [], [], [], []
    for l in range(DEPTH):
        lp = {name: prm[name][l] for name in MIXER_PARAM_NAMES}
        mix, ca, sc, ss, gc, gs = mixer_sublayer(rmsnorm(x, prm['g_mix'][l]), st_conv_a[l], st_ssm_conv[l],
                                                 st_ssm[l], st_gdn_conv[l], st_gdn[l], lp)
        x = x + mix
        h = rmsnorm(x, prm['g_ffn'][l])
        if l % 2 == 0:
            f = swiglu(h, prm['ffn_w_gate'][l // 2], prm['ffn_w_up'][l // 2], prm['ffn_w_down'][l // 2])
        else:
            f = moe_swiglu(h, prm['moe_w_router'][l // 2], prm['moe_b_router'][l // 2], prm['moe_w_gate'][l // 2],
                           prm['moe_w_up'][l // 2], prm['moe_w_down'][l // 2])
        x = x + f
        new_conv_a.append(ca)
        new_ssm_conv.append(sc)
        new_ssm.append(ss)
        new_gdn_conv.append(gc)
        new_gdn.append(gs)
    y = rmsnorm(x, prm['g_final'])
    return (y, jnp.stack(new_conv_a), jnp.stack(new_ssm_conv), jnp.stack(new_ssm),
            jnp.stack(new_gdn_conv), jnp.stack(new_gdn))


def setup_inputs(seed: int = 0) -> dict:
    key = jax.random.key(seed)
    ks = iter(jax.random.split(key, 48))
    nrm = lambda shape, scale: jax.random.normal(next(ks), shape, F32) * scale
    gain = lambda shape: 1.0 + nrm(shape, 0.01)

    def dt_bias(shape):
        uu = jax.random.uniform(next(ks), shape, F32)
        dt = jnp.exp(uu * (math.log(0.1) - math.log(1e-3)) + math.log(1e-3))
        return dt + jnp.log(-jnp.expm1(-dt))

    def a_log(shape):
        return jnp.log(jax.random.uniform(next(ks), shape, F32, 1.0, 16.0))

    return {
        'x_prompt': nrm((BATCH, SEQ, D_MODEL), 1.0),
        'x_sample': nrm((DEC_BATCH, DEC_SEQ, D_MODEL), 1.0),
        'state_conv_a': nrm((DEPTH, DEC_BATCH, CONV_W - 1, CONV_CH), 0.5),
        'state_ssm_conv': nrm((DEPTH, DEC_BATCH, SSM_CONV_W - 1, SSM_XBC), 1.0),
        'state_ssm': nrm((DEPTH, DEC_BATCH, SSM_HEADS, SSM_HEAD_DIM, SSM_STATE), 0.1),
        'state_gdn_conv': nrm((DEPTH, DEC_BATCH, GDN_CONV_W - 1, GDN_QKV), 1.0),
        'state_gdn': nrm((DEPTH, DEC_BATCH, GDN_HEADS, GDN_HEAD_DIM, GDN_HEAD_DIM), 0.1),
        'g_mix': gain((DEPTH, D_MODEL)),
        'w_in': nrm((DEPTH, D_MODEL, IN_COLS), D_MODEL ** -0.5),
        'conv_a_w': nrm((DEPTH, CONV_W, CONV_CH), CONV_W ** -0.5),
        'conv_a_b': nrm((DEPTH, CONV_CH), 0.01),
        'ln_a_g': gain((DEPTH, CONV_CH)),
        'ln_a_b': nrm((DEPTH, CONV_CH), 0.01),
        'ssm_conv_w': nrm((DEPTH, SSM_CONV_W, SSM_XBC), SSM_CONV_W ** -0.5),
        'ssm_conv_b': nrm((DEPTH, SSM_XBC), 0.01),
        'ssm_dt_bias': dt_bias((DEPTH, SSM_HEADS)),
        'ssm_a_log': a_log((DEPTH, SSM_HEADS)),
        'ssm_d': gain((DEPTH, SSM_HEADS)),
        'ssm_norm_g': gain((DEPTH, SSM_INNER)),
        'gdn_conv_w': nrm((DEPTH, GDN_CONV_W, GDN_QKV), GDN_CONV_W ** -0.5),
        'gdn_a_log': a_log((DEPTH, GDN_HEADS)),
        'gdn_dt_bias': dt_bias((DEPTH, GDN_HEADS)),
        'gdn_norm_g': gain((DEPTH, GDN_HEAD_DIM)),
        'w_out': nrm((DEPTH, MIX_WIDTH, D_MODEL), MIX_WIDTH ** -0.5),
        'g_ffn': gain((DEPTH, D_MODEL)),
        'ffn_w_gate': nrm((N_DENSE, D_MODEL, FFN_DIM), D_MODEL ** -0.5),
        'ffn_w_up': nrm((N_DENSE, D_MODEL, FFN_DIM), D_MODEL ** -0.5),
        'ffn_w_down': nrm((N_DENSE, FFN_DIM, D_MODEL), FFN_DIM ** -0.5),
        'moe_w_router': nrm((N_MOE, D_MODEL, N_EXPERTS), D_MODEL ** -0.5),
        'moe_b_router': nrm((N_MOE, N_EXPERTS), 0.01),
        'moe_w_gate': nrm((N_MOE, N_EXPERTS, D_MODEL, EXPERT_DIM), D_MODEL ** -0.5),
        'moe_w_up': nrm((N_MOE, N_EXPERTS, D_MODEL, EXPERT_DIM), D_MODEL ** -0.5),
        'moe_w_down': nrm((N_MOE, N_EXPERTS, EXPERT_DIM, D_MODEL), EXPERT_DIM ** -0.5),
        'g_final': gain((D_MODEL,)),
    }


def reference(x_prompt, x_sample, state_conv_a, state_ssm_conv, state_ssm, state_gdn_conv, state_gdn,
              g_mix, w_in, conv_a_w, conv_a_b, ln_a_g, ln_a_b,
              ssm_conv_w, ssm_conv_b, ssm_dt_bias, ssm_a_log, ssm_d, ssm_norm_g,
              gdn_conv_w, gdn_a_log, gdn_dt_bias, gdn_norm_g, w_out,
              g_ffn, ffn_w_gate, ffn_w_up, ffn_w_down,
              moe_w_router, moe_b_router, moe_w_gate, moe_w_up, moe_w_down, g_final):
    prm = {'g_mix': g_mix, 'w_in': w_in, 'conv_a_w': conv_a_w, 'conv_a_b': conv_a_b,
           'ln_a_g': ln_a_g, 'ln_a_b': ln_a_b, 'ssm_conv_w': ssm_conv_w, 'ssm_conv_b': ssm_conv_b,
           'ssm_dt_bias': ssm_dt_bias, 'ssm_a_log': ssm_a_log, 'ssm_d': ssm_d, 'ssm_norm_g': ssm_norm_g,
           'gdn_conv_w': gdn_conv_w, 'gdn_a_log': gdn_a_log, 'gdn_dt_bias': gdn_dt_bias,
           'gdn_norm_g': gdn_norm_g, 'w_out': w_out, 'g_ffn': g_ffn,
           'ffn_w_gate': ffn_w_gate, 'ffn_w_up': ffn_w_up, 'ffn_w_down': ffn_w_down,
           'moe_w_router': moe_w_router, 'moe_b_router': moe_b_router, 'moe_w_gate': moe_w_gate,
           'moe_w_up': moe_w_up, 'moe_w_down': moe_w_down, 'g_final': g_final}
    bp = x_prompt.shape[0]
    dt = x_prompt.dtype
    z_conv_a = jnp.zeros((DEPTH, bp, CONV_W - 1, CONV_CH), dt)
    z_ssm_conv = jnp.zeros((DEPTH, bp, SSM_CONV_W - 1, SSM_XBC), dt)
    z_ssm = jnp.zeros((DEPTH, bp, SSM_HEADS, SSM_HEAD_DIM, SSM_STATE), dt)
    z_gdn_conv = jnp.zeros((DEPTH, bp, GDN_CONV_W - 1, GDN_QKV), dt)
    z_gdn = jnp.zeros((DEPTH, bp, GDN_HEADS, GDN_HEAD_DIM, GDN_HEAD_DIM), dt)
    y_prompt, p_conv_a, p_ssm_conv, p_ssm, p_gdn_conv, p_gdn = encoder_trunk(
        x_prompt, z_conv_a, z_ssm_conv, z_ssm, z_gdn_conv, z_gdn, prm)
    y_sample, s_conv_a, s_ssm_conv, s_ssm, s_gdn_conv, s_gdn = encoder_trunk(
        x_sample, state_conv_a, state_ssm_conv, state_ssm, state_gdn_conv, state_gdn, prm)
    return (y_prompt, y_sample, p_conv_a, p_ssm_conv, p_ssm, p_gdn_conv, p_gdn,
            s_conv_a, s_ssm_conv, s_ssm, s_gdn_conv, s_gdn)
```

```python
import functools
import math

import jax
import jax.numpy as jnp
import numpy as np
from jax import lax
from jax.experimental import pallas as pl
from jax.experimental.pallas import tpu as pltpu

F32 = jnp.float32
BF16 = jnp.bfloat16
EPS = 1e-6

LANES = 128
SUBLANES = 8
VMEM_BYTES_V7X = 64 * 1024 * 1024
VMEM_LIMIT = VMEM_BYTES_V7X * 3 // 4

D_MODEL = 1024
CONV_CH = 256
CONV_W = 31
SSM_HEADS = 6
HEAD_DIM = 64
SSM_INNER = SSM_HEADS * HEAD_DIM
SSM_STATE = 64
SSM_GROUPS = 2
SSM_XBC = SSM_INNER + 2 * SSM_GROUPS * SSM_STATE
SSM_CONV_W = 4
GDN_HEADS = 6
GDN_INNER = GDN_HEADS * HEAD_DIM
GDN_QKV = 3 * GDN_INNER
GDN_CONV_W = 4
N_PAIRS = 3
N_EXPERTS = 8
SMALL_W = LANES


def _cparams(*sem):
    return pltpu.CompilerParams(dimension_semantics=sem, vmem_limit_bytes=VMEM_LIMIT)


def _dot(a, b):
    return jnp.dot(a.astype(BF16), b.astype(BF16), preferred_element_type=F32)


def _dot_nt(a, b):
    return lax.dot_general(a.astype(BF16), b.astype(BF16), (((1,), (1,)), ((), ())),
                           preferred_element_type=F32)


def _split3(x):
    hi = x.astype(BF16)
    r1 = x - hi.astype(F32)
    mid = r1.astype(BF16)
    lo = (r1 - mid.astype(F32)).astype(BF16)
    return hi, mid, lo


def _dot_sel(x, sel_bf16):
    hi, mid, lo = _split3(x)
    d = lambda p: jnp.dot(p, sel_bf16, preferred_element_type=F32)
    return d(hi) + d(mid) + d(lo)


def _sel_dot(sel_bf16, x):
    hi, mid, lo = _split3(x)
    d = lambda p: jnp.dot(sel_bf16, p, preferred_element_type=F32)
    return d(hi) + d(mid) + d(lo)


def _sel_dot_nt(sel_bf16, x):
    hi, mid, lo = _split3(x)
    d = lambda p: lax.dot_general(sel_bf16, p, (((1,), (1,)), ((), ())), preferred_element_type=F32)
    return d(hi) + d(mid) + d(lo)


def _dot_hp(a, b):
    ah = a.astype(BF16)
    al = (a - ah.astype(F32)).astype(BF16)
    bh = b.astype(BF16)
    bl = (b - bh.astype(F32)).astype(BF16)
    d = lambda p, q: jnp.dot(p, q, preferred_element_type=F32)
    return d(ah, bh) + (d(ah, bl) + d(al, bh))


def _silu(x):
    return x * (1.0 / (1.0 + jnp.exp(-x)))


def _sigmoid(x):
    return 1.0 / (1.0 + jnp.exp(-x))


def _softplus(x):
    return jnp.maximum(x, 0.0) + jnp.log(1.0 + jnp.exp(-jnp.abs(x)))


def _iota(shape, dim):
    return lax.broadcasted_iota(jnp.int32, shape, dim)


def _norm_proj_kernel(x_ref, g_ref, wa_ref, wzx_ref, wqg_ref, ws_ref, a_ref, zx_ref, qg_ref, s_ref):
    x = x_ref[...]
    u = x * lax.rsqrt(jnp.mean(x * x, axis=-1, keepdims=True) + EPS) * g_ref[...]
    ub = u.astype(BF16)
    a_ref[...] = jnp.dot(ub, wa_ref[...], preferred_element_type=F32)
    zx_ref[...] = jnp.dot(ub, wzx_ref[...], preferred_element_type=F32)
    qg_ref[...] = jnp.dot(ub, wqg_ref[...], preferred_element_type=F32)
    s_ref[...] = jnp.dot(ub, ws_ref[...], preferred_element_type=F32)


def _row_tile(n, want):
    t = min(want, n)
    while n % t:
        t //= 2
    return t


def _norm_proj(x2d, g, wa, wzx, wqg, ws):
    n = x2d.shape[0]
    tm = _row_tile(n, 512)
    full = lambda w: pl.BlockSpec(w.shape, lambda i: (0, 0))
    rows = lambda c: pl.BlockSpec((tm, c), lambda i: (i, 0))
    widths = (wa.shape[1], wzx.shape[1], wqg.shape[1], ws.shape[1])
    return pl.pallas_call(
        _norm_proj_kernel,
        grid=(n // tm,),
        in_specs=[rows(D_MODEL), full(g), full(wa), full(wzx), full(wqg), full(ws)],
        out_specs=[rows(c) for c in widths],
        out_shape=[jax.ShapeDtypeStruct((n, c), F32) for c in widths],
        compiler_params=_cparams("parallel"),
        name="norm_proj",
    )(x2d, g, wa, wzx, wqg, ws)


CONV_PAD = 32
CONV_ROWS = 32


def _conv_a_kernel(a_ref, hist_ref, w_ref, b_ref, lg_ref, lb_ref, y_ref, nh_ref, buf_ref, *, tt):
    t = pl.program_id(1)

    @pl.when(t == 0)
    def _():
        buf_ref[0:CONV_PAD, :] = hist_ref[0]

    a = a_ref[0]
    glu = a[:, :CONV_CH] * _sigmoid(a[:, CONV_CH:])
    buf_ref[CONV_PAD:CONV_PAD + tt, :] = glu
    off = CONV_PAD - (CONV_W - 1)
    for r0 in range(0, tt, CONV_ROWS):
        acc = jnp.zeros((CONV_ROWS, CONV_CH), F32)
        for k in range(CONV_W):
            acc = acc + w_ref[k:k + 1, :] * buf_ref[r0 + off + k:r0 + off + k + CONV_ROWS, :]
        y = acc + b_ref[...]
        mu = jnp.mean(y, axis=-1, keepdims=True)
        yc = y - mu
        var = jnp.mean(yc * yc, axis=-1, keepdims=True)
        y = yc * lax.rsqrt(var + EPS) * lg_ref[...] + lb_ref[...]
        y_ref[0, r0:r0 + CONV_ROWS, :] = _silu(y).astype(y_ref.dtype)
    nh_ref[0] = buf_ref[tt:tt + CONV_PAD, :]
    buf_ref[0:CONV_PAD, :] = buf_ref[tt:tt + CONV_PAD, :]


def _conv_a(a_in, hist, w, b, lg, lb):
    bsz, t, _ = a_in.shape
    tt = _row_tile(t, 256)
    hist_p = jnp.pad(hist, ((0, 0), (CONV_PAD - (CONV_W - 1), 0), (0, 0)))
    vec = lambda v: pl.BlockSpec(v.shape, lambda i, j: (0, 0))
    y, nh = pl.pallas_call(
        functools.partial(_conv_a_kernel, tt=tt),
        grid=(bsz, t // tt),
        in_specs=[pl.BlockSpec((1, tt, 2 * CONV_CH), lambda i, j: (i, j, 0)),
                  pl.BlockSpec((1, CONV_PAD, CONV_CH), lambda i, j: (i, 0, 0)),
                  vec(w), vec(b), vec(lg), vec(lb)],
        out_specs=[pl.BlockSpec((1, tt, CONV_CH), lambda i, j: (i, j, 0)),
                   pl.BlockSpec((1, CONV_PAD, CONV_CH), lambda i, j: (i, 0, 0))],
        out_shape=[jax.ShapeDtypeStruct((bsz, t, CONV_CH), BF16),
                   jax.ShapeDtypeStruct((bsz, CONV_PAD, CONV_CH), F32)],
        scratch_shapes=[pltpu.VMEM((tt + CONV_PAD, CONV_CH), F32)],
        compiler_params=_cparams("arbitrary", "arbitrary"),
        name="conv_a",
    )(a_in, hist_p, w, b, lg, lb)
    return y, nh[:, CONV_PAD - (CONV_W - 1):, :]


HIST_PAD = 8


def _short_conv(x, buf_ref, b, w_ref, width, rows):
    buf_ref[b, HIST_PAD:HIST_PAD + rows, :] = x
    off = HIST_PAD - (width - 1)
    acc = w_ref[0:1, :] * buf_ref[b, off:off + rows, :]
    for k in range(1, width):
        acc = acc + w_ref[k:k + 1, :] * buf_ref[b, off + k:off + k + rows, :]
    buf_ref[b, 0:HIST_PAD, :] = buf_ref[b, rows:rows + HIST_PAD, :]
    return acc


def _tri(rows, lower):
    r = _iota((rows, rows), 0)
    c = _iota((rows, rows), 1)
    return (r >= c) if lower else (r <= c)


def _row_select(first_lane):
    r = _iota((SUBLANES, LANES), 0)
    c = _iota((SUBLANES, LANES), 1)
    return ((c == r + first_lane) & (r < SSM_HEADS)).astype(BF16)


def _cumsum_both(a, first_lane, rows):
    tril = _tri(rows, True).astype(BF16)
    triu = _tri(rows, False).astype(BF16)
    col = _sel_dot(tril, a)
    a_t = _sel_dot_nt(_row_select(first_lane), a)
    row = _dot_sel(a_t, triu)
    return col, row


def _pair_mask():
    return _iota((1, LANES), 1) < HEAD_DIM


def _ssd_kernel(zx_ref, s_ref, hist_ref, h0_ref, cw_ref, cb_ref, dtb_ref, arow_ref, dx_ref, ng_ref,
                ex_ref, gm_ref, y_ref, hout_ref, buf_ref, h_ref, *, rows, bb):
    c = pl.program_id(1)

    @pl.when(c == 0)
    def _():
        buf_ref[:, 0:HIST_PAD, :] = hist_ref[...]
        h_ref[...] = h0_ref[...]

    causal = _tri(rows, True)
    lane = _iota((1, LANES), 1)
    first_half = _pair_mask()
    srow_g = _iota((LANES, LANES), 0) // SSM_STATE
    scol_h = _iota((LANES, LANES), 1) // HEAD_DIM
    ex = ex_ref[...]

    for b in range(bb):
        zx = zx_ref[b]
        z = zx[:, :SSM_INNER]
        conv = _short_conv(zx[:, SSM_INNER:], buf_ref, b, cw_ref, SSM_CONV_W, rows)
        xbc = _silu(conv + cb_ref[...])
        xs = xbc[:, :SSM_INNER]
        bm = xbc[:, SSM_INNER:SSM_INNER + LANES]
        cm = xbc[:, SSM_INNER + LANES:]

        dt = _softplus(s_ref[b] + dtb_ref[...])
        a = dt * arow_ref[...]
        acum, acum_row = _cumsum_both(a, 0, rows)
        a_last = acum[rows - 1:rows, :]
        dt_x = _dot_sel(dt, ex)
        ea_x = _dot_sel(jnp.exp(acum), ex)
        te_x = _dot_sel(jnp.exp(a_last - acum), ex)
        cd_x = _dot_sel(jnp.broadcast_to(jnp.exp(a_last), (SUBLANES, LANES)), ex)[0:1, :]

        scores = []
        for g in range(SSM_GROUPS):
            cm_g = jnp.where(lane // SSM_STATE == g, cm, 0.0)
            scores.append(_dot_nt(cm_g, bm))
        bm_t = jnp.transpose(bm)

        ys = []
        for p in range(N_PAIRS):
            sl = slice(p * LANES, (p + 1) * LANES)
            x_p = xs[:, sl]
            xdt = x_p * dt_x[:, sl]
            yd = []
            for hh in range(2):
                h = 2 * p + hh
                diff = jnp.broadcast_to(acum[:, h:h + 1], (rows, rows)) - acum_row[h:h + 1, :]
                dec = jnp.exp(jnp.where(causal, diff, -1e30))
                yd.append(_dot(scores[h // (SSM_HEADS // SSM_GROUPS)] * dec, xdt))
            y_diag = jnp.where(first_half, yd[0], yd[1])
            h_p = h_ref[b, p]
            y_off = _dot(cm, h_p) * ea_x[:, sl]
            keep = srow_g == (2 * p + scol_h) // (SSM_HEADS // SSM_GROUPS)
            upd = _dot(bm_t, xdt * te_x[:, sl])
            h_ref[b, p] = h_p * cd_x[:, sl] + jnp.where(keep, upd, 0.0)
            ys.append(y_diag + y_off + dx_ref[:, sl] * x_p)
        y = jnp.concatenate(ys, axis=-1) * _silu(z)
        ms = _dot_sel(y * y, gm_ref[...]) * (1.0 / (SSM_INNER // SSM_GROUPS))
        y = y * lax.rsqrt(ms + EPS) * ng_ref[...]
        y_ref[b] = y.astype(y_ref.dtype)

    @pl.when(c == pl.num_programs(1) - 1)
    def _():
        hout_ref[...] = h_ref[...]


def _expand_matrix(first_lane):
    m = np.zeros((LANES, SSM_INNER), np.float32)
    for h in range(SSM_HEADS):
        m[first_lane + h, h * HEAD_DIM:(h + 1) * HEAD_DIM] = 1.0
    return jnp.asarray(m, BF16)


def _group_matrix(width):
    idx = np.arange(SSM_INNER) // width
    return jnp.asarray((idx[:, None] == idx[None, :]).astype(np.float32), BF16)


def _lane_row(vals, first_lane):
    return jnp.zeros((1, LANES), F32).at[0, first_lane:first_lane + vals.shape[0]].set(vals.astype(F32))


def _ssd(zx, small, hist, h0, cw, cb, dt_bias, a_log, d_skip, norm_g, rows, bb):
    bsz, t, _ = zx.shape
    hist_p = jnp.pad(hist, ((0, 0), (HIST_PAD - (SSM_CONV_W - 1), 0), (0, 0)))
    dtb = _lane_row(dt_bias, 0)
    arow = _lane_row(-jnp.exp(a_log.astype(F32)), 0)
    dx = jnp.repeat(d_skip.astype(F32), HEAD_DIM)[None, :]
    full = lambda v: pl.BlockSpec(v.shape, lambda i, j: (0,) * v.ndim)
    blk = lambda c: pl.BlockSpec((bb, rows, c), lambda i, j: (i, j, 0))
    st = pl.BlockSpec((bb, N_PAIRS, LANES, LANES), lambda i, j: (i, 0, 0, 0))
    consts = (cw, cb[None, :], dtb, arow, dx, norm_g[None, :], _expand_matrix(0),
              _group_matrix(SSM_INNER // SSM_GROUPS))
    return pl.pallas_call(
        functools.partial(_ssd_kernel, rows=rows, bb=bb),
        grid=(bsz // bb, t // rows),
        in_specs=[blk(zx.shape[-1]), blk(LANES),
                  pl.BlockSpec((bb, HIST_PAD, SSM_XBC), lambda i, j: (i, 0, 0)), st]
                 + [full(v) for v in consts],
        out_specs=[blk(SSM_INNER), st],
        out_shape=[jax.ShapeDtypeStruct((bsz, t, SSM_INNER), BF16),
                   jax.ShapeDtypeStruct((bsz, N_PAIRS, LANES, LANES), F32)],
        scratch_shapes=[pltpu.VMEM((bb, rows + HIST_PAD, SSM_XBC), F32),
                        pltpu.VMEM((bb, N_PAIRS, LANES, LANES), F32)],
        compiler_params=_cparams("arbitrary", "arbitrary"),
        name="ssd",
    )(zx, small, hist_p, h0, *consts)


def _ssd_state_in(h):
    bsz = h.shape[0]
    out = jnp.zeros((bsz, N_PAIRS, SSM_GROUPS, SSM_STATE, 2, HEAD_DIM), F32)
    for hd in range(SSM_HEADS):
        g = hd // (SSM_HEADS // SSM_GROUPS)
        out = out.at[:, hd // 2, g, :, hd % 2, :].set(jnp.swapaxes(h[:, hd], 1, 2))
    return out.reshape(bsz, N_PAIRS, LANES, LANES)


def _ssd_state_out(hp):
    bsz = hp.shape[0]
    hp = hp.reshape(bsz, N_PAIRS, SSM_GROUPS, SSM_STATE, 2, HEAD_DIM)
    heads = [jnp.swapaxes(hp[:, hd // 2, hd // (SSM_HEADS // SSM_GROUPS), :, hd % 2, :], 1, 2)
             for hd in range(SSM_HEADS)]
    return jnp.stack(heads, axis=1)


BETA_LANE = 6
DECAY_LANE = 12


def _unit_lower_inverse(m, rows):
    x = -m
    eye = (_iota((rows, rows), 0) == _iota((rows, rows), 1)).astype(F32)
    t = eye + x
    p = x
    for _ in range(int(math.log2(rows)) - 1):
        p = _dot_hp(p, p)
        t = t + _dot_hp(t, p)
    return t


def _gdn_kernel(qg_ref, s_ref, hist_ref, s0_ref, cw_ref, bias_ref, arow_ref, ng_ref, eb_ref, eg_ref, hm_ref,
                y_ref, sout_ref, buf_ref, st_ref, *, rows, bb):
    c = pl.program_id(1)

    @pl.when(c == 0)
    def _():
        buf_ref[:, 0:HIST_PAD, :] = hist_ref[...]
        st_ref[...] = s0_ref[...]

    r_i = _iota((rows, rows), 0)
    c_i = _iota((rows, rows), 1)
    incl = r_i >= c_i
    strict = r_i > c_i
    first_half = _pair_mask()
    same_head = (_iota((LANES, LANES), 0) // HEAD_DIM) == (_iota((LANES, LANES), 1) // HEAD_DIM)
    eb = eb_ref[...]
    eg = eg_ref[...]
    hm = hm_ref[...]

    for b in range(bb):
        qg = qg_ref[b]
        gate = qg[:, GDN_QKV:]
        qkv = _silu(_short_conv(qg[:, :GDN_QKV], buf_ref, b, cw_ref, GDN_CONV_W, rows))
        q = qkv[:, :GDN_INNER]
        k = qkv[:, GDN_INNER:2 * GDN_INNER]
        v = qkv[:, 2 * GDN_INNER:]
        q = q * lax.rsqrt(_dot_sel(q * q, hm) + EPS) * (HEAD_DIM ** -0.5)
        k = k * lax.rsqrt(_dot_sel(k * k, hm) + EPS)

        s = s_ref[b]
        beta = _sigmoid(s)
        g = _softplus(s + bias_ref[...]) * arow_ref[...]
        gc, gc_row = _cumsum_both(g, DECAY_LANE, rows)
        g_last = gc[rows - 1:rows, :]
        beta_x = _dot_sel(beta, eb)
        eg_x = _dot_sel(jnp.exp(gc), eg)
        kd_x = _dot_sel(jnp.exp(g_last - gc), eg)
        cd_x = _dot_sel(jnp.broadcast_to(jnp.exp(g_last), (SUBLANES, LANES)), eg)[0:1, :]

        outs = []
        for p in range(N_PAIRS):
            sl = slice(p * LANES, (p + 1) * LANES)
            q_p, k_p, v_p = q[:, sl], k[:, sl], v[:, sl]
            kb_p = k_p * beta_x[:, sl]
            rhs = jnp.concatenate([v_p * beta_x[:, sl], kb_p * eg_x[:, sl]], axis=-1)
            sols, attns = [], []
            for hh in range(2):
                h = 2 * p + hh
                own = first_half if hh == 0 else jnp.logical_not(first_half)
                diff = jnp.broadcast_to(gc[:, DECAY_LANE + h:DECAY_LANE + h + 1], (rows, rows)) - gc_row[h:h + 1, :]
                dec = jnp.exp(jnp.where(incl, diff, -1e30))
                m = _dot_nt(jnp.where(own, kb_p, 0.0), k_p) * jnp.where(strict, dec, 0.0)
                t_inv = _unit_lower_inverse(m, rows)
                sols.append(_dot_hp(t_inv, rhs))
                attns.append(_dot_nt(jnp.where(own, q_p, 0.0), k_p) * dec)
            u = jnp.where(first_half, sols[0][:, :LANES], sols[1][:, :LANES])
            w = jnp.where(first_half, sols[0][:, LANES:], sols[1][:, LANES:])
            s_p = st_ref[b, p]
            v_new = u - _dot(w, s_p)
            o = _dot(q_p * eg_x[:, sl], s_p) + jnp.where(first_half, _dot(attns[0], v_new), _dot(attns[1], v_new))
            kd_t = jnp.transpose(k_p * kd_x[:, sl])
            st_ref[b, p] = s_p * cd_x[:, sl] + jnp.where(same_head, _dot(kd_t, v_new), 0.0)
            outs.append(o)
        o = jnp.concatenate(outs, axis=-1)
        ms = _dot_sel(o * o, hm) * (1.0 / HEAD_DIM)
        o = o * lax.rsqrt(ms + EPS) * ng_ref[...] * _silu(gate)
        y_ref[b] = o.astype(y_ref.dtype)

    @pl.when(c == pl.num_programs(1) - 1)
    def _():
        sout_ref[...] = st_ref[...]


def _gdn(qg, small, hist, s0, cw, a_log, dt_bias, norm_g, rows, bb):
    bsz, t, _ = qg.shape
    hist_p = jnp.pad(hist, ((0, 0), (HIST_PAD - (GDN_CONV_W - 1), 0), (0, 0)))
    bias = _lane_row(dt_bias, DECAY_LANE)
    arow = _lane_row(-jnp.exp(a_log.astype(F32)), DECAY_LANE)
    ng = jnp.tile(norm_g.astype(F32), GDN_HEADS)[None, :]
    full = lambda v: pl.BlockSpec(v.shape, lambda i, j: (0,) * v.ndim)
    blk = lambda c: pl.BlockSpec((bb, rows, c), lambda i, j: (i, j, 0))
    st = pl.BlockSpec((bb, N_PAIRS, LANES, LANES), lambda i, j: (i, 0, 0, 0))
    consts = (cw, bias, arow, ng, _expand_matrix(BETA_LANE), _expand_matrix(DECAY_LANE), _group_matrix(HEAD_DIM))
    return pl.pallas_call(
        functools.partial(_gdn_kernel, rows=rows, bb=bb),
        grid=(bsz // bb, t // rows),
        in_specs=[blk(qg.shape[-1]), blk(LANES),
                  pl.BlockSpec((bb, HIST_PAD, GDN_QKV), lambda i, j: (i, 0, 0)), st]
                 + [full(v) for v in consts],
        out_specs=[blk(GDN_INNER), st],
        out_shape=[jax.ShapeDtypeStruct((bsz, t, GDN_INNER), BF16),
                   jax.ShapeDtypeStruct((bsz, N_PAIRS, LANES, LANES), F32)],
        scratch_shapes=[pltpu.VMEM((bb, rows + HIST_PAD, GDN_QKV), F32),
                        pltpu.VMEM((bb, N_PAIRS, LANES, LANES), F32)],
        compiler_params=_cparams("arbitrary", "arbitrary"),
        name="gdn",
    )(qg, small, hist_p, s0, *consts)


def _gdn_state_in(s):
    bsz = s.shape[0]
    out = jnp.zeros((bsz, N_PAIRS, 2, HEAD_DIM, 2, HEAD_DIM), F32)
    for hd in range(GDN_HEADS):
        out = out.at[:, hd // 2, hd % 2, :, hd % 2, :].set(s[:, hd])
    return out.reshape(bsz, N_PAIRS, LANES, LANES)


def _gdn_state_out(sp):
    bsz = sp.shape[0]
    sp = sp.reshape(bsz, N_PAIRS, 2, HEAD_DIM, 2, HEAD_DIM)
    return jnp.stack([sp[:, hd // 2, hd % 2, :, hd % 2, :] for hd in range(GDN_HEADS)], axis=1)


def _mix_residual(x_ref, ya_ref, yb_ref, yc_ref, woa_ref, wob_ref, woc_ref):
    mix = (jnp.dot(ya_ref[...], woa_ref[...], preferred_element_type=F32)
           + jnp.dot(yb_ref[...], wob_ref[...], preferred_element_type=F32)
           + jnp.dot(yc_ref[...], woc_ref[...], preferred_element_type=F32))
    return x_ref[...] + mix


def _rms(x, g):
    return x * lax.rsqrt(jnp.mean(x * x, axis=-1, keepdims=True) + EPS) * g


def _ffn_kernel(x_ref, ya_ref, yb_ref, yc_ref, woa_ref, wob_ref, woc_ref, g_ref, wg_ref, wu_ref, wd_ref,
                o_ref, xn_ref, h_ref, acc_ref):
    j = pl.program_id(1)

    @pl.when(j == 0)
    def _():
        xn = _mix_residual(x_ref, ya_ref, yb_ref, yc_ref, woa_ref, wob_ref, woc_ref)
        xn_ref[...] = xn
        h_ref[...] = _rms(xn, g_ref[...]).astype(BF16)
        acc_ref[...] = jnp.zeros_like(acc_ref)

    h = h_ref[...]
    act = _silu(jnp.dot(h, wg_ref[...], preferred_element_type=F32)) * jnp.dot(h, wu_ref[...], preferred_element_type=F32)
    acc_ref[...] += jnp.dot(act.astype(BF16), wd_ref[...], preferred_element_type=F32)

    @pl.when(j == pl.num_programs(1) - 1)
    def _():
        o_ref[...] = xn_ref[...] + acc_ref[...]


FFN_CHUNKS = 2


def _out_ffn(x2d, ya, yb, yc, wo, g, wg, wu, wd):
    n = x2d.shape[0]
    tm = _row_tile(n, 512)
    f = wg.shape[1]
    tf = f // FFN_CHUNKS
    rows = lambda c: pl.BlockSpec((tm, c), lambda i, j: (i, 0))
    full = lambda w: pl.BlockSpec(w.shape, lambda i, j: (0, 0))
    return pl.pallas_call(
        _ffn_kernel,
        grid=(n // tm, FFN_CHUNKS),
        in_specs=[rows(D_MODEL), rows(ya.shape[1]), rows(yb.shape[1]), rows(yc.shape[1]),
                  full(wo[0]), full(wo[1]), full(wo[2]), full(g),
                  pl.BlockSpec((D_MODEL, tf), lambda i, j: (0, j)),
                  pl.BlockSpec((D_MODEL, tf), lambda i, j: (0, j)),
                  pl.BlockSpec((tf, D_MODEL), lambda i, j: (j, 0))],
        out_specs=rows(D_MODEL),
        out_shape=jax.ShapeDtypeStruct((n, D_MODEL), F32),
        scratch_shapes=[pltpu.VMEM((tm, D_MODEL), F32), pltpu.VMEM((tm, D_MODEL), BF16),
                        pltpu.VMEM((tm, D_MODEL), F32)],
        compiler_params=_cparams("parallel", "arbitrary"),
        name="out_ffn",
    )(x2d, ya, yb, yc, wo[0], wo[1], wo[2], g, wg, wu, wd)


def _router_kernel(x_ref, ya_ref, yb_ref, yc_ref, woa_ref, wob_ref, woc_ref, g_ref, wr_ref, br_ref,
                   xn_ref, h_ref, gate_ref):
    xn = _mix_residual(x_ref, ya_ref, yb_ref, yc_ref, woa_ref, wob_ref, woc_ref)
    xn_ref[...] = xn
    h = _rms(xn, g_ref[...])
    h_ref[...] = h.astype(BF16)
    lane = _iota((1, LANES), 1)
    logits = jnp.where(lane < N_EXPERTS, _dot_hp(h, wr_ref[...]) + br_ref[...], -jnp.inf)
    m1 = jnp.max(logits, axis=-1, keepdims=True)
    i1 = jnp.min(jnp.where(logits == m1, lane, LANES), axis=-1, keepdims=True)
    rest = jnp.where(lane == i1, -jnp.inf, logits)
    m2 = jnp.max(rest, axis=-1, keepdims=True)
    i2 = jnp.min(jnp.where(rest == m2, lane, LANES), axis=-1, keepdims=True)
    e2 = jnp.exp(m2 - m1)
    inv = 1.0 / (1.0 + e2)
    gate_ref[...] = jnp.where(lane == i1, inv, 0.0) + jnp.where(lane == i2, e2 * inv, 0.0)


def _out_router(x2d, ya, yb, yc, wo, g, w_router, b_router):
    n = x2d.shape[0]
    tm = _row_tile(n, 512)
    wr = jnp.zeros((D_MODEL, LANES), F32).at[:, :N_EXPERTS].set(w_router.astype(F32))
    br = _lane_row(b_router, 0)
    rows = lambda c: pl.BlockSpec((tm, c), lambda i: (i, 0))
    full = lambda w: pl.BlockSpec(w.shape, lambda i: (0, 0))
    return pl.pallas_call(
        _router_kernel,
        grid=(n // tm,),
        in_specs=[rows(D_MODEL), rows(ya.shape[1]), rows(yb.shape[1]), rows(yc.shape[1]),
                  full(wo[0]), full(wo[1]), full(wo[2]), full(g), full(wr), full(br)],
        out_specs=[rows(D_MODEL), rows(D_MODEL), rows(LANES)],
        out_shape=[jax.ShapeDtypeStruct((n, D_MODEL), F32), jax.ShapeDtypeStruct((n, D_MODEL), BF16),
                   jax.ShapeDtypeStruct((n, LANES), F32)],
        compiler_params=_cparams("parallel"),
        name="out_router",
    )(x2d, ya, yb, yc, wo[0], wo[1], wo[2], g, wr, br)


def _moe_dense_kernel(xn_ref, h_ref, gate_ref, wg_ref, wu_ref, wd_ref, gf_ref, o_ref, acc_ref):
    e = pl.program_id(1)
    j = pl.program_id(2)

    @pl.when((e == 0) & (j == 0))
    def _():
        acc_ref[...] = jnp.zeros_like(acc_ref)

    h = h_ref[...]
    lane = _iota((1, LANES), 1)
    ge = jnp.sum(jnp.where(lane == e, gate_ref[...], 0.0), axis=-1, keepdims=True)
    act = _silu(jnp.dot(h, wg_ref[0], preferred_element_type=F32)) * jnp.dot(h, wu_ref[0], preferred_element_type=F32)
    acc_ref[...] += ge * jnp.dot(act.astype(BF16), wd_ref[0], preferred_element_type=F32)

    @pl.when((e == pl.num_programs(1) - 1) & (j == pl.num_programs(2) - 1))
    def _():
        o_ref[...] = _rms(xn_ref[...] + acc_ref[...], gf_ref[...])


def _moe_dense(xn, h, gates, wg, wu, wd, g_final):
    n = xn.shape[0]
    tm = _row_tile(n, 512)
    f = wg.shape[2]
    tf = f // FFN_CHUNKS
    rows = lambda c: pl.BlockSpec((tm, c), lambda i, e, j: (i, 0))
    return pl.pallas_call(
        _moe_dense_kernel,
        grid=(n // tm, N_EXPERTS, FFN_CHUNKS),
        in_specs=[rows(D_MODEL), rows(D_MODEL), rows(LANES),
                  pl.BlockSpec((1, D_MODEL, tf), lambda i, e, j: (e, 0, j)),
                  pl.BlockSpec((1, D_MODEL, tf), lambda i, e, j: (e, 0, j)),
                  pl.BlockSpec((1, tf, D_MODEL), lambda i, e, j: (e, j, 0)),
                  pl.BlockSpec(g_final.shape, lambda i, e, j: (0, 0))],
        out_specs=rows(D_MODEL),
        out_shape=jax.ShapeDtypeStruct((n, D_MODEL), F32),
        scratch_shapes=[pltpu.VMEM((tm, D_MODEL), F32)],
        compiler_params=_cparams("parallel", "arbitrary", "arbitrary"),
        name="moe_dense",
    )(xn, h, gates, wg, wu, wd, g_final)


IN_SIZES = (2 * CONV_CH, SSM_INNER, SSM_XBC, SSM_HEADS, GDN_QKV, GDN_INNER, GDN_HEADS, GDN_HEADS)


def _prep_layer(l, p):
    off = np.concatenate([[0], np.cumsum(IN_SIZES)])
    col = lambda i: p['w_in'][l][:, off[i]:off[i + 1]]
    a_in, z, xbc, dt, qkv, gate, b_raw, a_raw = (col(i) for i in range(8))
    small = jnp.concatenate([dt, b_raw, a_raw, jnp.zeros((D_MODEL, SMALL_W - 3 * SSM_HEADS), F32)], axis=1)
    wo = p['w_out'][l].astype(BF16)
    return dict(
        wa=a_in.astype(BF16),
        wzx=jnp.concatenate([z, xbc], axis=1).astype(BF16),
        wqg=jnp.concatenate([qkv, gate], axis=1).astype(BF16),
        ws=small.astype(BF16),
        wo=(wo[:CONV_CH], wo[CONV_CH:CONV_CH + SSM_INNER], wo[CONV_CH + SSM_INNER:]),
    )


def _trunk(x, st_conv_a, st_ssm_conv, st_ssm, st_gdn_conv, st_gdn, p, prep, ssd_rows, gdn_rows, bb):
    bsz, t, _ = x.shape
    n = bsz * t
    depth = p['g_mix'].shape[0]
    x2d = x.reshape(n, D_MODEL)
    new = [[] for _ in range(5)]
    for l in range(depth):
        w = prep[l]
        a_in, zx, qg, small = _norm_proj(x2d, p['g_mix'][l][None, :], w['wa'], w['wzx'], w['wqg'], w['ws'])
        a_in = a_in.reshape(bsz, t, -1)
        zx = zx.reshape(bsz, t, -1)
        qg = qg.reshape(bsz, t, -1)
        small = small.reshape(bsz, t, -1)
        ya, conv_a = _conv_a(a_in, st_conv_a[l], p['conv_a_w'][l], p['conv_a_b'][l][None, :],
                             p['ln_a_g'][l][None, :], p['ln_a_b'][l][None, :])
        yb, ssm = _ssd(zx, small, st_ssm_conv[l], _ssd_state_in(st_ssm[l]), p['ssm_conv_w'][l], p['ssm_conv_b'][l],
                       p['ssm_dt_bias'][l], p['ssm_a_log'][l], p['ssm_d'][l], p['ssm_norm_g'][l], ssd_rows, bb)
        yc, gdn = _gdn(qg, small, st_gdn_conv[l], _gdn_state_in(st_gdn[l]), p['gdn_conv_w'][l],
                       p['gdn_a_log'][l], p['gdn_dt_bias'][l], p['gdn_norm_g'][l], gdn_rows, bb)
        new[0].append(conv_a)
        new[1].append(jnp.concatenate([st_ssm_conv[l], zx[:, :, SSM_INNER:]], axis=1)[:, -(SSM_CONV_W - 1):])
        new[2].append(_ssd_state_out(ssm))
        new[3].append(jnp.concatenate([st_gdn_conv[l], qg[:, :, :GDN_QKV]], axis=1)[:, -(GDN_CONV_W - 1):])
        new[4].append(_gdn_state_out(gdn))
        flat = lambda y: y.reshape(n, -1)
        g_ffn = p['g_ffn'][l][None, :]
        if l % 2 == 0:
            x2d = _out_ffn(x2d, flat(ya), flat(yb), flat(yc), w['wo'], g_ffn,
                           prep['ffn'][l // 2][0], prep['ffn'][l // 2][1], prep['ffn'][l // 2][2])
        else:
            xn, h, gates = _out_router(x2d, flat(ya), flat(yb), flat(yc), w['wo'], g_ffn,
                                       p['moe_w_router'][l // 2], p['moe_b_router'][l // 2])
            x2d = _moe_dense(xn, h, gates, *prep['moe'][l // 2], p['g_final'][None, :])
    return (x2d.reshape(bsz, t, D_MODEL),) + tuple(jnp.stack(s) for s in new)


def kernel(x_prompt, x_sample, state_conv_a, state_ssm_conv, state_ssm, state_gdn_conv, state_gdn, g_mix, w_in, conv_a_w, conv_a_b, ln_a_g, ln_a_b, ssm_conv_w, ssm_conv_b, ssm_dt_bias, ssm_a_log, ssm_d, ssm_norm_g, gdn_conv_w, gdn_a_log, gdn_dt_bias, gdn_norm_g, w_out, g_ffn, ffn_w_gate, ffn_w_up, ffn_w_down, moe_w_router, moe_b_router, moe_w_gate, moe_w_up, moe_w_down, g_final):
    p = dict(g_mix=g_mix, w_in=w_in, conv_a_w=conv_a_w, conv_a_b=conv_a_b, ln_a_g=ln_a_g, ln_a_b=ln_a_b,
             ssm_conv_w=ssm_conv_w, ssm_conv_b=ssm_conv_b, ssm_dt_bias=ssm_dt_bias, ssm_a_log=ssm_a_log,
             ssm_d=ssm_d, ssm_norm_g=ssm_norm_g, gdn_conv_w=gdn_conv_w, gdn_a_log=gdn_a_log,
             gdn_dt_bias=gdn_dt_bias, gdn_norm_g=gdn_norm_g, w_out=w_out, g_ffn=g_ffn,
             moe_w_router=moe_w_router, moe_b_router=moe_b_router, g_final=g_final)
    depth = g_mix.shape[0]
    assert depth % 2 == 0, "the final RMSNorm is fused into the expert layer, which must come last"
    prep = {l: _prep_layer(l, p) for l in range(depth)}
    prep['ffn'] = [(ffn_w_gate[i].astype(BF16), ffn_w_up[i].astype(BF16), ffn_w_down[i].astype(BF16))
                   for i in range(ffn_w_gate.shape[0])]
    prep['moe'] = [(moe_w_gate[i].astype(BF16), moe_w_up[i].astype(BF16), moe_w_down[i].astype(BF16))
                   for i in range(moe_w_gate.shape[0])]
    bp, dt = x_prompt.shape[0], x_prompt.dtype
    zeros = lambda *s: jnp.zeros((depth, bp) + s, dt)
    outs_p = _trunk(x_prompt, zeros(CONV_W - 1, CONV_CH), zeros(SSM_CONV_W - 1, SSM_XBC),
                    zeros(SSM_HEADS, HEAD_DIM, SSM_STATE), zeros(GDN_CONV_W - 1, GDN_QKV),
                    zeros(GDN_HEADS, HEAD_DIM, HEAD_DIM), p, prep, ssd_rows=min(128, x_prompt.shape[1]),
                    gdn_rows=min(64, x_prompt.shape[1]), bb=2)
    outs_s = _trunk(x_sample, state_conv_a, state_ssm_conv, state_ssm, state_gdn_conv, state_gdn, p, prep,
                    ssd_rows=x_sample.shape[1], gdn_rows=x_sample.shape[1], bb=2)
    return (outs_p[0], outs_s[0]) + outs_p[1:] + outs_s[1:]
```

```python
import functools
import math

import jax
import jax.numpy as jnp
import numpy as np
from jax import lax
from jax.experimental import pallas as pl
from jax.experimental.pallas import tpu as pltpu

F32 = jnp.float32
BF16 = jnp.bfloat16
EPS = 1e-6

LANES = 128
SUBLANES = 8
VMEM_BYTES_V7X = 64 * 1024 * 1024
VMEM_LIMIT = VMEM_BYTES_V7X * 3 // 4

D_MODEL = 1024
CONV_CH = 256
CONV_W = 31
SSM_HEADS = 6
HEAD_DIM = 64
SSM_INNER = SSM_HEADS * HEAD_DIM
SSM_STATE = 64
SSM_GROUPS = 2
SSM_XBC = SSM_INNER + 2 * SSM_GROUPS * SSM_STATE
SSM_CONV_W = 4
GDN_HEADS = 6
GDN_INNER = GDN_HEADS * HEAD_DIM
GDN_QKV = 3 * GDN_INNER
GDN_CONV_W = 4
N_PAIRS = 3
N_EXPERTS = 8
SMALL_W = LANES


def _cparams(*sem):
    return pltpu.CompilerParams(dimension_semantics=sem, vmem_limit_bytes=VMEM_LIMIT)


def _dot(a, b):
    return jnp.dot(a.astype(BF16), b.astype(BF16), preferred_element_type=F32)


def _dot_nt(a, b):
    return lax.dot_general(a.astype(BF16), b.astype(BF16), (((1,), (1,)), ((), ())),
                           preferred_element_type=F32)


def _split3(x):
    hi = x.astype(BF16)
    r1 = x - hi.astype(F32)
    mid = r1.astype(BF16)
    lo = (r1 - mid.astype(F32)).astype(BF16)
    return hi, mid, lo


def _dot_sel(x, sel_bf16):
    hi, mid, lo = _split3(x)
    d = lambda p: jnp.dot(p, sel_bf16, preferred_element_type=F32)
    return d(hi) + d(mid) + d(lo)


def _sel_dot(sel_bf16, x):
    hi, mid, lo = _split3(x)
    d = lambda p: jnp.dot(sel_bf16, p, preferred_element_type=F32)
    return d(hi) + d(mid) + d(lo)


def _sel_dot_nt(sel_bf16, x):
    hi, mid, lo = _split3(x)
    d = lambda p: lax.dot_general(sel_bf16, p, (((1,), (1,)), ((), ())), preferred_element_type=F32)
    return d(hi) + d(mid) + d(lo)


def _dot_hp(a, b):
    ah = a.astype(BF16)
    al = (a - ah.astype(F32)).astype(BF16)
    bh = b.astype(BF16)
    bl = (b - bh.astype(F32)).astype(BF16)
    d = lambda p, q: jnp.dot(p, q, preferred_element_type=F32)
    return d(ah, bh) + (d(ah, bl) + d(al, bh))


def _silu(x):
    return x * (1.0 / (1.0 + jnp.exp(-x)))


def _sigmoid(x):
    return 1.0 / (1.0 + jnp.exp(-x))


def _softplus(x):
    return jnp.maximum(x, 0.0) + jnp.log(1.0 + jnp.exp(-jnp.abs(x)))


def _iota(shape, dim):
    return lax.broadcasted_iota(jnp.int32, shape, dim)


def _norm_proj_kernel(x_ref, g_ref, wa_ref, wzx_ref, wqg_ref, ws_ref, a_ref, zx_ref, qg_ref, s_ref):
    x = x_ref[...]
    u = x * lax.rsqrt(jnp.mean(x * x, axis=-1, keepdims=True) + EPS) * g_ref[...]
    ub = u.astype(BF16)
    a_ref[...] = jnp.dot(ub, wa_ref[...], preferred_element_type=F32)
    zx_ref[...] = jnp.dot(ub, wzx_ref[...], preferred_element_type=F32)
    qg_ref[...] = jnp.dot(ub, wqg_ref[...], preferred_element_type=F32)
    s_ref[...] = jnp.dot(ub, ws_ref[...], preferred_element_type=F32)


def _row_tile(n, want):
    t = min(want, n)
    while n % t:
        t //= 2
    return t


def _norm_proj(x2d, g, wa, wzx, wqg, ws):
    n = x2d.shape[0]
    tm = _row_tile(n, 512)
    full = lambda w: pl.BlockSpec(w.shape, lambda i: (0, 0))
    rows = lambda c: pl.BlockSpec((tm, c), lambda i: (i, 0))
    widths = (wa.shape[1], wzx.shape[1], wqg.shape[1], ws.shape[1])
    return pl.pallas_call(
        _norm_proj_kernel,
        grid=(n // tm,),
        in_specs=[rows(D_MODEL), full(g), full(wa), full(wzx), full(wqg), full(ws)],
        out_specs=[rows(c) for c in widths],
        out_shape=[jax.ShapeDtypeStruct((n, c), F32) for c in widths],
        compiler_params=_cparams("parallel"),
        name="norm_proj",
    )(x2d, g, wa, wzx, wqg, ws)


CONV_PAD = 32
CONV_ROWS = 32


def _conv_a_kernel(a_ref, hist_ref, w_ref, b_ref, lg_ref, lb_ref, y_ref, nh_ref, buf_ref, *, tt):
    t = pl.program_id(1)

    @pl.when(t == 0)
    def _():
        buf_ref[0:CONV_PAD, :] = hist_ref[0]

    a = a_ref[0]
    glu = a[:, :CONV_CH] * _sigmoid(a[:, CONV_CH:])
    buf_ref[CONV_PAD:CONV_PAD + tt, :] = glu
    off = CONV_PAD - (CONV_W - 1)
    for r0 in range(0, tt, CONV_ROWS):
        acc = jnp.zeros((CONV_ROWS, CONV_CH), F32)
        for k in range(CONV_W):
            acc = acc + w_ref[k:k + 1, :] * buf_ref[r0 + off + k:r0 + off + k + CONV_ROWS, :]
        y = acc + b_ref[...]
        mu = jnp.mean(y, axis=-1, keepdims=True)
        yc = y - mu
        var = jnp.mean(yc * yc, axis=-1, keepdims=True)
        y = yc * lax.rsqrt(var + EPS) * lg_ref[...] + lb_ref[...]
        y_ref[0, r0:r0 + CONV_ROWS, :] = _silu(y).astype(y_ref.dtype)
    nh_ref[0] = buf_ref[tt:tt + CONV_PAD, :]
    buf_ref[0:CONV_PAD, :] = buf_ref[tt:tt + CONV_PAD, :]


def _conv_a(a_in, hist, w, b, lg, lb):
    bsz, t, _ = a_in.shape
    tt = _row_tile(t, 256)
    hist_p = jnp.pad(hist, ((0, 0), (CONV_PAD - (CONV_W - 1), 0), (0, 0)))
    vec = lambda v: pl.BlockSpec(v.shape, lambda i, j: (0, 0))
    y, nh = pl.pallas_call(
        functools.partial(_conv_a_kernel, tt=tt),
        grid=(bsz, t // tt),
        in_specs=[pl.BlockSpec((1, tt, 2 * CONV_CH), lambda i, j: (i, j, 0)),
                  pl.BlockSpec((1, CONV_PAD, CONV_CH), lambda i, j: (i, 0, 0)),
                  vec(w), vec(b), vec(lg), vec(lb)],
        out_specs=[pl.BlockSpec((1, tt, CONV_CH), lambda i, j: (i, j, 0)),
                   pl.BlockSpec((1, CONV_PAD, CONV_CH), lambda i, j: (i, 0, 0))],
        out_shape=[jax.ShapeDtypeStruct((bsz, t, CONV_CH), BF16),
                   jax.ShapeDtypeStruct((bsz, CONV_PAD, CONV_CH), F32)],
        scratch_shapes=[pltpu.VMEM((tt + CONV_PAD, CONV_CH), F32)],
        compiler_params=_cparams("arbitrary", "arbitrary"),
        name="conv_a",
    )(a_in, hist_p, w, b, lg, lb)
    return y, nh[:, CONV_PAD - (CONV_W - 1):, :]


HIST_PAD = 8


def _short_conv(x, buf_ref, b, w_ref, width, rows):
    buf_ref[b, HIST_PAD:HIST_PAD + rows, :] = x
    off = HIST_PAD - (width - 1)
    acc = w_ref[0:1, :] * buf_ref[b, off:off + rows, :]
    for k in range(1, width):
        acc = acc + w_ref[k:k + 1, :] * buf_ref[b, off + k:off + k + rows, :]
    buf_ref[b, 0:HIST_PAD, :] = buf_ref[b, rows:rows + HIST_PAD, :]
    return acc


def _tri(rows, lower):
    r = _iota((rows, rows), 0)
    c = _iota((rows, rows), 1)
    return (r >= c) if lower else (r <= c)


def _row_select(first_lane):
    r = _iota((SUBLANES, LANES), 0)
    c = _iota((SUBLANES, LANES), 1)
    return ((c == r + first_lane) & (r < SSM_HEADS)).astype(BF16)


def _cumsum_both(a, first_lane, rows):
    tril = _tri(rows, True).astype(BF16)
    triu = _tri(rows, False).astype(BF16)
    col = _sel_dot(tril, a)
    a_t = _sel_dot_nt(_row_select(first_lane), a)
    row = _dot_sel(a_t, triu)
    return col, row


def _pair_mask():
    return _iota((1, LANES), 1) < HEAD_DIM


def _ssd_kernel(zx_ref, s_ref, hist_ref, h0_ref, cw_ref, cb_ref, dtb_ref, arow_ref, dx_ref, ng_ref,
                ex_ref, gm_ref, y_ref, hout_ref, buf_ref, h_ref, *, rows, bb):
    c = pl.program_id(1)

    @pl.when(c == 0)
    def _():
        buf_ref[:, 0:HIST_PAD, :] = hist_ref[...]
        h_ref[...] = h0_ref[...]

    causal = _tri(rows, True)
    lane = _iota((1, LANES), 1)
    first_half = _pair_mask()
    srow_g = _iota((LANES, LANES), 0) // SSM_STATE
    scol_h = _iota((LANES, LANES), 1) // HEAD_DIM
    ex = ex_ref[...]

    for b in range(bb):
        zx = zx_ref[b]
        z = zx[:, :SSM_INNER]
        conv = _short_conv(zx[:, SSM_INNER:], buf_ref, b, cw_ref, SSM_CONV_W, rows)
        xbc = _silu(conv + cb_ref[...])
        xs = xbc[:, :SSM_INNER]
        bm = xbc[:, SSM_INNER:SSM_INNER + LANES]
        cm = xbc[:, SSM_INNER + LANES:]

        dt = _softplus(s_ref[b] + dtb_ref[...])
        a = dt * arow_ref[...]
        acum, acum_row = _cumsum_both(a, 0, rows)
        a_last = acum[rows - 1:rows, :]
        dt_x = _dot_sel(dt, ex)
        ea_x = _dot_sel(jnp.exp(acum), ex)
        te_x = _dot_sel(jnp.exp(a_last - acum), ex)
        cd_x = _dot_sel(jnp.broadcast_to(jnp.exp(a_last), (SUBLANES, LANES)), ex)[0:1, :]

        scores = []
        for g in range(SSM_GROUPS):
            cm_g = jnp.where(lane // SSM_STATE == g, cm, 0.0)
            scores.append(_dot_nt(cm_g, bm))
        bm_t = jnp.transpose(bm)

        ys = []
        for p in range(N_PAIRS):
            sl = slice(p * LANES, (p + 1) * LANES)
            x_p = xs[:, sl]
            xdt = x_p * dt_x[:, sl]
            yd = []
            for hh in range(2):
                h = 2 * p + hh
                diff = jnp.broadcast_to(acum[:, h:h + 1], (rows, rows)) - acum_row[h:h + 1, :]
                dec = jnp.exp(jnp.where(causal, diff, -1e30))
                yd.append(_dot(scores[h // (SSM_HEADS // SSM_GROUPS)] * dec, xdt))
            y_diag = jnp.where(first_half, yd[0], yd[1])
            h_p = h_ref[b, p]
            y_off = _dot(cm, h_p) * ea_x[:, sl]
            keep = srow_g == (2 * p + scol_h) // (SSM_HEADS // SSM_GROUPS)
            upd = _dot(bm_t, xdt * te_x[:, sl])
            h_ref[b, p] = h_p * cd_x[:, sl] + jnp.where(keep, upd, 0.0)
            ys.append(y_diag + y_off + dx_ref[:, sl] * x_p)
        y = jnp.concatenate(ys, axis=-1) * _silu(z)
        ms = _dot_sel(y * y, gm_ref[...]) * (1.0 / (SSM_INNER // SSM_GROUPS))
        y = y * lax.rsqrt(ms + EPS) * ng_ref[...]
        y_ref[b] = y.astype(y_ref.dtype)

    @pl.when(c == pl.num_programs(1) - 1)
    def _():
        hout_ref[...] = h_ref[...]


def _expand_matrix(first_lane):
    m = np.zeros((LANES, SSM_INNER), np.float32)
    for h in range(SSM_HEADS):
        m[first_lane + h, h * HEAD_DIM:(h + 1) * HEAD_DIM] = 1.0
    return jnp.asarray(m, BF16)


def _group_matrix(width):
    idx = np.arange(SSM_INNER) // width
    return jnp.asarray((idx[:, None] == idx[None, :]).astype(np.float32), BF16)


def _lane_row(vals, first_lane):
    return jnp.zeros((1, LANES), F32).at[0, first_lane:first_lane + vals.shape[0]].set(vals.astype(F32))


def _ssd(zx, small, hist, h0, cw, cb, dt_bias, a_log, d_skip, norm_g, rows, bb):
    bsz, t, _ = zx.shape
    hist_p = jnp.pad(hist, ((0, 0), (HIST_PAD - (SSM_CONV_W - 1), 0), (0, 0)))
    dtb = _lane_row(dt_bias, 0)
    arow = _lane_row(-jnp.exp(a_log.astype(F32)), 0)
    dx = jnp.repeat(d_skip.astype(F32), HEAD_DIM)[None, :]
    full = lambda v: pl.BlockSpec(v.shape, lambda i, j: (0,) * v.ndim)
    blk = lambda c: pl.BlockSpec((bb, rows, c), lambda i, j: (i, j, 0))
    st = pl.BlockSpec((bb, N_PAIRS, LANES, LANES), lambda i, j: (i, 0, 0, 0))
    consts = (cw, cb[None, :], dtb, arow, dx, norm_g[None, :], _expand_matrix(0),
              _group_matrix(SSM_INNER // SSM_GROUPS))
    return pl.pallas_call(
        functools.partial(_ssd_kernel, rows=rows, bb=bb),
        grid=(bsz // bb, t // rows),
        in_specs=[blk(zx.shape[-1]), blk(LANES),
                  pl.BlockSpec((bb, HIST_PAD, SSM_XBC), lambda i, j: (i, 0, 0)), st]
                 + [full(v) for v in consts],
        out_specs=[blk(SSM_INNER), st],
        out_shape=[jax.ShapeDtypeStruct((bsz, t, SSM_INNER), BF16),
                   jax.ShapeDtypeStruct((bsz, N_PAIRS, LANES, LANES), F32)],
        scratch_shapes=[pltpu.VMEM((bb, rows + HIST_PAD, SSM_XBC), F32),
                        pltpu.VMEM((bb, N_PAIRS, LANES, LANES), F32)],
        compiler_params=_cparams("arbitrary", "arbitrary"),
        name="ssd",
    )(zx, small, hist_p, h0, *consts)


def _ssd_state_in(h):
    bsz = h.shape[0]
    out = jnp.zeros((bsz, N_PAIRS, SSM_GROUPS, SSM_STATE, 2, HEAD_DIM), F32)
    for hd in range(SSM_HEADS):
        g = hd // (SSM_HEADS // SSM_GROUPS)
        out = out.at[:, hd // 2, g, :, hd % 2, :].set(jnp.swapaxes(h[:, hd], 1, 2))
    return out.reshape(bsz, N_PAIRS, LANES, LANES)


def _ssd_state_out(hp):
    bsz = hp.shape[0]
    hp = hp.reshape(bsz, N_PAIRS, SSM_GROUPS, SSM_STATE, 2, HEAD_DIM)
    heads = [jnp.swapaxes(hp[:, hd // 2, hd // (SSM_HEADS // SSM_GROUPS), :, hd % 2, :], 1, 2)
             for hd in range(SSM_HEADS)]
    return jnp.stack(heads, axis=1)


BETA_LANE = 6
DECAY_LANE = 12


def _bdot(a, b):
    return lax.dot_general(a.astype(BF16), b.astype(BF16), (((2,), (1,)), ((0,), (0,))),
                           preferred_element_type=F32)


def _bdot_nt(a, b):
    return lax.dot_general(a.astype(BF16), b.astype(BF16), (((2,), (2,)), ((0,), (0,))),
                           preferred_element_type=F32)


def _bdot_hp(a, b):
    ah = a.astype(BF16)
    al = (a - ah.astype(F32)).astype(BF16)
    bh = b.astype(BF16)
    bl = (b - bh.astype(F32)).astype(BF16)
    d = lambda p, q: lax.dot_general(p, q, (((2,), (1,)), ((0,), (0,))), preferred_element_type=F32)
    return d(ah, bh) + (d(ah, bl) + d(al, bh))


def _unit_lower_inverse(m, block):
    rdim = m.shape[-1]
    eye = (_iota((rdim, rdim), 0) == _iota((rdim, rdim), 1)).astype(F32)
    x = -m
    t = eye + x
    p = x
    for _ in range(int(math.log2(block)) - 2):
        p = _bdot(p, p)
        t = t + _bdot(t, p)
    resid = (eye - t) - _bdot_hp(m, t)
    return t + _bdot(t, resid)


def _gdn_kernel(qg_ref, s_ref, hist_ref, s0_ref, cw_ref, bias_ref, arow_ref, ng_ref, eb_ref, eg_ref, hm_ref,
                y_ref, sout_ref, buf_ref, st_ref, *, tt, ck, bb):
    c = pl.program_id(1)
    nck = tt // ck
    rdim = 2 * ck

    @pl.when(c == 0)
    def _():
        buf_ref[:, 0:HIST_PAD, :] = hist_ref[...]
        st_ref[...] = s0_ref[...]

    r_i = _iota((rdim, rdim), 0)
    c_i = _iota((rdim, rdim), 1)
    same_blk = (r_i // ck) == (c_i // ck)
    incl = same_blk & (r_i >= c_i)
    strict = same_blk & (r_i > c_i)
    first_half = _pair_mask()
    same_head = (_iota((LANES, LANES), 0) // HEAD_DIM) == (_iota((LANES, LANES), 1) // HEAD_DIM)
    tr = _iota((tt, tt), 0)
    tc = _iota((tt, tt), 1)
    blk_tril = (((tr // ck) == (tc // ck)) & (tr >= tc)).astype(BF16)
    eb = eb_ref[...]
    eg = eg_ref[...]
    hm = hm_ref[...]

    tiles = {name: [] for name in ('q', 'k', 'kb', 'rhs', 'qd', 'kd', 'col', 'cd')}
    gates = []
    for b in range(bb):
        qg = qg_ref[b]
        gates.append(qg[:, GDN_QKV:])
        qkv = _silu(_short_conv(qg[:, :GDN_QKV], buf_ref, b, cw_ref, GDN_CONV_W, tt))
        q = qkv[:, :GDN_INNER]
        k = qkv[:, GDN_INNER:2 * GDN_INNER]
        v = qkv[:, 2 * GDN_INNER:]
        q = q * lax.rsqrt(_dot_sel(q * q, hm) + EPS) * (HEAD_DIM ** -0.5)
        k = k * lax.rsqrt(_dot_sel(k * k, hm) + EPS)
        s = s_ref[b]
        beta = _sigmoid(s)
        g = _softplus(s + bias_ref[...]) * arow_ref[...]
        gc = _sel_dot(blk_tril, g)
        g_last = jnp.concatenate(
            [jnp.broadcast_to(gc[(i + 1) * ck - 1:(i + 1) * ck, :], (ck, LANES)) for i in range(nck)], axis=0)
        beta_x = _dot_sel(beta, eb)
        eg_x = _dot_sel(jnp.exp(gc), eg)
        kd_x = _dot_sel(jnp.exp(g_last - gc), eg)
        cd_x = _dot_sel(jnp.exp(g_last), eg)
        kb = k * beta_x
        full = dict(q=q, k=k, kb=kb, qd=q * eg_x, kd=k * kd_x, cd=cd_x)
        vb = v * beta_x
        kbe = kb * eg_x
        for i in range(nck):
            rs = slice(i * ck, (i + 1) * ck)
            for p in range(N_PAIRS):
                sl = slice(p * LANES, (p + 1) * LANES)
                for name, arr in full.items():
                    tiles[name].append(arr[rs, sl])
                tiles['rhs'].append(jnp.concatenate([vb[rs, sl], kbe[rs, sl]], axis=-1))
                lane0 = DECAY_LANE + 2 * p
                tiles['col'].append(jnp.concatenate(
                    [jnp.broadcast_to(gc[rs, lane0 + hh:lane0 + hh + 1], (ck, rdim)) for hh in range(2)], axis=0))

    st = lambda name: jnp.stack(tiles[name], axis=0)
    stack2 = lambda x: jnp.concatenate([jnp.where(first_half, x, 0.0), jnp.where(first_half, 0.0, x)], axis=1)
    k_st = stack2(st('k'))
    col = st('col')
    diff = col - jnp.swapaxes(col, 1, 2)
    dec = jnp.exp(jnp.where(incl, diff, -1e30))
    m = _bdot_nt(stack2(st('kb')), k_st) * jnp.where(strict, dec, 0.0)
    t_inv = _unit_lower_inverse(m, ck)
    rhs = st('rhs')
    sol = _bdot(t_inv, jnp.concatenate([rhs, rhs], axis=1))
    u = jnp.where(first_half, sol[:, :ck, :LANES], sol[:, ck:, :LANES])
    w = jnp.where(first_half, sol[:, :ck, LANES:], sol[:, ck:, LANES:])
    attn = _bdot_nt(stack2(st('q')), k_st) * dec
    qd = st('qd')
    kd_t = jnp.swapaxes(st('kd'), 1, 2)
    cd = st('cd')

    gsel = lambda x, i: jnp.stack([x[(b * nck + i) * N_PAIRS + p] for b in range(bb) for p in range(N_PAIRS)], axis=0)
    state = st_ref[...].reshape(bb * N_PAIRS, LANES, LANES)
    o_chunks = []
    for i in range(nck):
        v_new = gsel(u, i) - _bdot(gsel(w, i), state)
        intra = _bdot(gsel(attn, i), jnp.concatenate([v_new, v_new], axis=1))
        o_chunks.append(_bdot(gsel(qd, i), state) + jnp.where(first_half, intra[:, :ck], intra[:, ck:]))
        upd = _bdot(gsel(kd_t, i), v_new)
        state = state * gsel(cd, i)[:, 0:1, :] + jnp.where(same_head, upd, 0.0)
    st_ref[...] = state.reshape(bb, N_PAIRS, LANES, LANES)

    for b in range(bb):
        o = jnp.concatenate(
            [jnp.concatenate([o_chunks[i][b * N_PAIRS + p] for p in range(N_PAIRS)], axis=-1) for i in range(nck)],
            axis=0)
        ms = _dot_sel(o * o, hm) * (1.0 / HEAD_DIM)
        o = o * lax.rsqrt(ms + EPS) * ng_ref[...] * _silu(gates[b])
        y_ref[b] = o.astype(y_ref.dtype)

    @pl.when(c == pl.num_programs(1) - 1)
    def _():
        sout_ref[...] = st_ref[...]


def _gdn(qg, small, hist, s0, cw, a_log, dt_bias, norm_g, tt, ck, bb):
    bsz, t, _ = qg.shape
    hist_p = jnp.pad(hist, ((0, 0), (HIST_PAD - (GDN_CONV_W - 1), 0), (0, 0)))
    bias = _lane_row(dt_bias, DECAY_LANE)
    arow = _lane_row(-jnp.exp(a_log.astype(F32)), DECAY_LANE)
    ng = jnp.tile(norm_g.astype(F32), GDN_HEADS)[None, :]
    full = lambda v: pl.BlockSpec(v.shape, lambda i, j: (0,) * v.ndim)
    blk = lambda c: pl.BlockSpec((bb, tt, c), lambda i, j: (i, j, 0))
    st = pl.BlockSpec((bb, N_PAIRS, LANES, LANES), lambda i, j: (i, 0, 0, 0))
    consts = (cw, bias, arow, ng, _expand_matrix(BETA_LANE), _expand_matrix(DECAY_LANE), _group_matrix(HEAD_DIM))
    return pl.pallas_call(
        functools.partial(_gdn_kernel, tt=tt, ck=ck, bb=bb),
        grid=(bsz // bb, t // tt),
        in_specs=[blk(qg.shape[-1]), blk(LANES),
                  pl.BlockSpec((bb, HIST_PAD, GDN_QKV), lambda i, j: (i, 0, 0)), st]
                 + [full(v) for v in consts],
        out_specs=[blk(GDN_INNER), st],
        out_shape=[jax.ShapeDtypeStruct((bsz, t, GDN_INNER), BF16),
                   jax.ShapeDtypeStruct((bsz, N_PAIRS, LANES, LANES), F32)],
        scratch_shapes=[pltpu.VMEM((bb, tt + HIST_PAD, GDN_QKV), F32),
                        pltpu.VMEM((bb, N_PAIRS, LANES, LANES), F32)],
        compiler_params=_cparams("arbitrary", "arbitrary"),
        name="gdn",
    )(qg, small, hist_p, s0, *consts)


def _gdn_state_in(s):
    bsz = s.shape[0]
    out = jnp.zeros((bsz, N_PAIRS, 2, HEAD_DIM, 2, HEAD_DIM), F32)
    for hd in range(GDN_HEADS):
        out = out.at[:, hd // 2, hd % 2, :, hd % 2, :].set(s[:, hd])
    return out.reshape(bsz, N_PAIRS, LANES, LANES)


def _gdn_state_out(sp):
    bsz = sp.shape[0]
    sp = sp.reshape(bsz, N_PAIRS, 2, HEAD_DIM, 2, HEAD_DIM)
    return jnp.stack([sp[:, hd // 2, hd % 2, :, hd % 2, :] for hd in range(GDN_HEADS)], axis=1)


def _mix_residual(x_ref, ya_ref, yb_ref, yc_ref, woa_ref, wob_ref, woc_ref):
    mix = (jnp.dot(ya_ref[...], woa_ref[...], preferred_element_type=F32)
           + jnp.dot(yb_ref[...], wob_ref[...], preferred_element_type=F32)
           + jnp.dot(yc_ref[...], woc_ref[...], preferred_element_type=F32))
    return x_ref[...] + mix


def _rms(x, g):
    return x * lax.rsqrt(jnp.mean(x * x, axis=-1, keepdims=True) + EPS) * g


def _ffn_kernel(x_ref, ya_ref, yb_ref, yc_ref, woa_ref, wob_ref, woc_ref, g_ref, wg_ref, wu_ref, wd_ref,
                o_ref, xn_ref, h_ref, acc_ref):
    j = pl.program_id(1)

    @pl.when(j == 0)
    def _():
        xn = _mix_residual(x_ref, ya_ref, yb_ref, yc_ref, woa_ref, wob_ref, woc_ref)
        xn_ref[...] = xn
        h_ref[...] = _rms(xn, g_ref[...]).astype(BF16)
        acc_ref[...] = jnp.zeros_like(acc_ref)

    h = h_ref[...]
    act = _silu(jnp.dot(h, wg_ref[...], preferred_element_type=F32)) * jnp.dot(h, wu_ref[...], preferred_element_type=F32)
    acc_ref[...] += jnp.dot(act.astype(BF16), wd_ref[...], preferred_element_type=F32)

    @pl.when(j == pl.num_programs(1) - 1)
    def _():
        o_ref[...] = xn_ref[...] + acc_ref[...]


FFN_CHUNKS = 2


def _out_ffn(x2d, ya, yb, yc, wo, g, wg, wu, wd):
    n = x2d.shape[0]
    tm = _row_tile(n, 512)
    f = wg.shape[1]
    tf = f // FFN_CHUNKS
    rows = lambda c: pl.BlockSpec((tm, c), lambda i, j: (i, 0))
    full = lambda w: pl.BlockSpec(w.shape, lambda i, j: (0, 0))
    return pl.pallas_call(
        _ffn_kernel,
        grid=(n // tm, FFN_CHUNKS),
        in_specs=[rows(D_MODEL), rows(ya.shape[1]), rows(yb.shape[1]), rows(yc.shape[1]),
                  full(wo[0]), full(wo[1]), full(wo[2]), full(g),
                  pl.BlockSpec((D_MODEL, tf), lambda i, j: (0, j)),
                  pl.BlockSpec((D_MODEL, tf), lambda i, j: (0, j)),
                  pl.BlockSpec((tf, D_MODEL), lambda i, j: (j, 0))],
        out_specs=rows(D_MODEL),
        out_shape=jax.ShapeDtypeStruct((n, D_MODEL), F32),
        scratch_shapes=[pltpu.VMEM((tm, D_MODEL), F32), pltpu.VMEM((tm, D_MODEL), BF16),
                        pltpu.VMEM((tm, D_MODEL), F32)],
        compiler_params=_cparams("parallel", "arbitrary"),
        name="out_ffn",
    )(x2d, ya, yb, yc, wo[0], wo[1], wo[2], g, wg, wu, wd)


ROUTE_TILE = 512


def _router_kernel(x_ref, ya_ref, yb_ref, yc_ref, woa_ref, wob_ref, woc_ref, g_ref, wr_ref, br_ref,
                   xn_ref, h_ref, gate_ref, slot_ref, cnt_ref, run_ref, *, cap):
    i = pl.program_id(0)

    @pl.when(i == 0)
    def _():
        run_ref[...] = jnp.zeros_like(run_ref)

    xn = _mix_residual(x_ref, ya_ref, yb_ref, yc_ref, woa_ref, wob_ref, woc_ref)
    xn_ref[...] = xn
    h = _rms(xn, g_ref[...])
    h_ref[...] = h
    tm = h.shape[0]
    lane = _iota((1, LANES), 1)
    logits = jnp.where(lane < N_EXPERTS, _dot_hp(h, wr_ref[...]) + br_ref[...], -jnp.inf)
    m1 = jnp.max(logits, axis=-1, keepdims=True)
    i1 = jnp.min(jnp.where(logits == m1, lane, LANES), axis=-1, keepdims=True)
    rest = jnp.where(lane == i1, -jnp.inf, logits)
    m2 = jnp.max(rest, axis=-1, keepdims=True)
    i2 = jnp.min(jnp.where(rest == m2, lane, LANES), axis=-1, keepdims=True)
    e2 = jnp.exp(m2 - m1)
    inv = 1.0 / (1.0 + e2)
    gate_ref[...] = jnp.where(lane == 0, inv, 0.0) + jnp.where(lane == 1, e2 * inv, 0.0)

    chosen = jnp.where((lane == i1) | (lane == i2), 1.0, 0.0)
    before = _iota((tm, tm), 0) > _iota((tm, tm), 1)
    rank = jnp.dot(before.astype(BF16), chosen.astype(BF16), preferred_element_type=F32) + run_ref[0:1, :]
    base = lane.astype(F32) * float(cap)
    pick = lambda idx: jnp.sum(jnp.where(lane == idx, rank + base, 0.0), axis=-1, keepdims=True)
    slots = jnp.where(lane == 0, pick(i1), 0.0) + jnp.where(lane == 1, pick(i2), 0.0)
    slot_ref[0] = jnp.transpose(slots)[0:SUBLANES, :].astype(jnp.int32)
    run_ref[...] = run_ref[...] + jnp.sum(chosen, axis=0, keepdims=True)
    cnt_ref[...] = run_ref[...].astype(jnp.int32)


def _out_router(x2d, ya, yb, yc, wo, g, w_router, b_router, cap):
    n = x2d.shape[0]
    tm = _row_tile(n, ROUTE_TILE)
    wr = jnp.zeros((D_MODEL, LANES), F32).at[:, :N_EXPERTS].set(w_router.astype(F32))
    br = _lane_row(b_router, 0)
    rows = lambda c: pl.BlockSpec((tm, c), lambda i: (i, 0))
    full = lambda w: pl.BlockSpec(w.shape, lambda i: (0, 0))
    return pl.pallas_call(
        functools.partial(_router_kernel, cap=cap),
        grid=(n // tm,),
        in_specs=[rows(D_MODEL), rows(ya.shape[1]), rows(yb.shape[1]), rows(yc.shape[1]),
                  full(wo[0]), full(wo[1]), full(wo[2]), full(g), full(wr), full(br)],
        out_specs=[rows(D_MODEL), rows(D_MODEL), rows(LANES),
                   pl.BlockSpec((1, SUBLANES, tm), lambda i: (i, 0, 0)),
                   pl.BlockSpec((SUBLANES, LANES), lambda i: (0, 0))],
        out_shape=[jax.ShapeDtypeStruct((n, D_MODEL), F32), jax.ShapeDtypeStruct((n, D_MODEL), F32),
                   jax.ShapeDtypeStruct((n, LANES), F32),
                   jax.ShapeDtypeStruct((n // tm, SUBLANES, tm), jnp.int32),
                   jax.ShapeDtypeStruct((SUBLANES, LANES), jnp.int32)],
        scratch_shapes=[pltpu.VMEM((SUBLANES, LANES), F32)],
        compiler_params=_cparams("arbitrary"),
        name="out_router",
    )(x2d, ya, yb, yc, wo[0], wo[1], wo[2], g, wr, br)


def _row_copy(src_ref, src_row, dst_ref, dst_row, sem):
    return pltpu.make_async_copy(src_ref.at[pl.ds(src_row, 1)], dst_ref.at[pl.ds(dst_row, 1)], sem)


def _slot_fetch(slots_hbm, tile, smem_ref, buf, sem):
    return pltpu.make_async_copy(slots_hbm.at[tile], smem_ref.at[buf], sem.at[buf])


def _dispatch_kernel(cnt_ref, slots_hbm, h_hbm, hs_hbm, slot_smem, zero_ref, slot_sem, row_sem, pad_sem,
                     *, tm, cap, bm):
    i = pl.program_id(0)
    n_tiles = pl.num_programs(0)

    @pl.when(i == 0)
    def _():
        _slot_fetch(slots_hbm, 0, slot_smem, 0, slot_sem).start()

    @pl.when(i + 1 < n_tiles)
    def _():
        _slot_fetch(slots_hbm, i + 1, slot_smem, (i + 1) % 2, slot_sem).start()

    _slot_fetch(slots_hbm, i, slot_smem, i % 2, slot_sem).wait()

    def issue(r, carry):
        for k in range(2):
            _row_copy(h_hbm, i * tm + r, hs_hbm, slot_smem[i % 2, k, r], row_sem).start()
        return carry

    lax.fori_loop(0, tm, issue, 0, unroll=8)

    def drain(r, carry):
        for k in range(2):
            _row_copy(h_hbm, 0, hs_hbm, 0, row_sem).wait()
        return carry

    lax.fori_loop(0, tm, drain, 0, unroll=8)

    @pl.when(i == n_tiles - 1)
    def _():
        zero_ref[...] = jnp.zeros_like(zero_ref)
        for e in range(N_EXPERTS):
            c = cnt_ref[e]
            n_pad = ((c + bm - 1) // bm) * bm - c

            def fill(r, carry):
                _row_copy(zero_ref, 0, hs_hbm, e * cap + c + r, pad_sem).start()
                return carry

            def fill_wait(r, carry):
                _row_copy(zero_ref, 0, hs_hbm, 0, pad_sem).wait()
                return carry

            lax.fori_loop(0, n_pad, fill, 0)
            lax.fori_loop(0, n_pad, fill_wait, 0)


def _dispatch(counts, slots, h, cap, bm):
    n = h.shape[0]
    n_tiles, _, tm = slots.shape
    return pl.pallas_call(
        functools.partial(_dispatch_kernel, tm=tm, cap=cap, bm=bm),
        grid_spec=pltpu.PrefetchScalarGridSpec(
            num_scalar_prefetch=1,
            grid=(n_tiles,),
            in_specs=[pl.BlockSpec(memory_space=pl.ANY), pl.BlockSpec(memory_space=pl.ANY)],
            out_specs=pl.BlockSpec(memory_space=pl.ANY),
            scratch_shapes=[pltpu.SMEM((2, SUBLANES, tm), jnp.int32), pltpu.VMEM((SUBLANES, D_MODEL), F32),
                            pltpu.SemaphoreType.DMA((2,)), pltpu.SemaphoreType.DMA, pltpu.SemaphoreType.DMA],
        ),
        out_shape=jax.ShapeDtypeStruct((N_EXPERTS * cap, D_MODEL), F32),
        compiler_params=_cparams("arbitrary"),
        name="moe_dispatch",
    )(counts, slots, h)


def _experts_kernel(tbl_ref, hs_ref, wg_ref, wu_ref, wd_ref, ys_ref, hb_ref, acc_ref):
    s = pl.program_id(0)
    j = pl.program_id(1)

    @pl.when(tbl_ref[2, s] == 1)
    def _():
        @pl.when(j == 0)
        def _():
            hb_ref[...] = hs_ref[...].astype(BF16)
            acc_ref[...] = jnp.zeros_like(acc_ref)

        h = hb_ref[...]
        act = (_silu(jnp.dot(h, wg_ref[0], preferred_element_type=F32))
               * jnp.dot(h, wu_ref[0], preferred_element_type=F32))
        acc_ref[...] += jnp.dot(act.astype(BF16), wd_ref[0], preferred_element_type=F32)

        @pl.when(j == pl.num_programs(1) - 1)
        def _():
            ys_ref[...] = acc_ref[...]


def _block_table(counts, cap, bm, n_steps):
    nblk = (counts + bm - 1) // bm
    cum = jnp.cumsum(nblk)
    total = cum[-1]
    step = jnp.arange(n_steps, dtype=jnp.int32)
    last = jnp.maximum(total - 1, 0)
    eff = jnp.minimum(step, last)
    expert = jnp.minimum(jnp.searchsorted(cum, eff, side='right'), N_EXPERTS - 1).astype(jnp.int32)
    blk = eff - (cum[expert] - nblk[expert])
    return jnp.stack([expert * (cap // bm) + blk, expert, (step < total).astype(jnp.int32)]).astype(jnp.int32)


def _experts(counts, hs, wg, wu, wd, cap, bm, n_assign):
    n_steps = n_assign // bm + N_EXPERTS
    tbl = _block_table(counts, cap, bm, n_steps)
    f = wg.shape[2]
    tf = f // FFN_CHUNKS
    chunk = lambda s, j, t: jnp.where(t[2, s] == 1, j, FFN_CHUNKS - 1)
    rows = pl.BlockSpec((bm, D_MODEL), lambda s, j, t: (t[0, s], 0))
    return pl.pallas_call(
        _experts_kernel,
        grid_spec=pltpu.PrefetchScalarGridSpec(
            num_scalar_prefetch=1,
            grid=(n_steps, FFN_CHUNKS),
            in_specs=[rows,
                      pl.BlockSpec((1, D_MODEL, tf), lambda s, j, t: (t[1, s], 0, chunk(s, j, t))),
                      pl.BlockSpec((1, D_MODEL, tf), lambda s, j, t: (t[1, s], 0, chunk(s, j, t))),
                      pl.BlockSpec((1, tf, D_MODEL), lambda s, j, t: (t[1, s], chunk(s, j, t), 0))],
            out_specs=rows,
            scratch_shapes=[pltpu.VMEM((bm, D_MODEL), BF16), pltpu.VMEM((bm, D_MODEL), F32)],
        ),
        out_shape=jax.ShapeDtypeStruct(hs.shape, F32),
        compiler_params=_cparams("arbitrary", "arbitrary"),
        name="moe_experts",
    )(tbl, hs, wg, wu, wd)


def _combine_kernel(slots_hbm, ys_hbm, xn_ref, gate_ref, gf_ref, o_ref, slot_smem, ybuf_ref, slot_sem, row_sem,
                    *, tm):
    i = pl.program_id(0)
    n_tiles = pl.num_programs(0)

    def gather(tile, buf):
        def issue(r, carry):
            for k in range(2):
                _row_copy(ys_hbm, slot_smem[buf, k, r], ybuf_ref.at[buf, k], r, row_sem.at[buf]).start()
            return carry
        lax.fori_loop(0, tm, issue, 0, unroll=8)

    @pl.when(i == 0)
    def _():
        first = _slot_fetch(slots_hbm, 0, slot_smem, 0, slot_sem)
        first.start()
        first.wait()
        gather(0, 0)

        @pl.when(n_tiles > 1)
        def _():
            _slot_fetch(slots_hbm, 1, slot_smem, 1, slot_sem).start()

    @pl.when(i + 1 < n_tiles)
    def _():
        _slot_fetch(slots_hbm, i + 1, slot_smem, (i + 1) % 2, slot_sem).wait()
        gather(i + 1, (i + 1) % 2)

    def drain(r, carry):
        for k in range(2):
            _row_copy(ys_hbm, 0, ybuf_ref.at[i % 2, k], 0, row_sem.at[i % 2]).wait()
        return carry

    lax.fori_loop(0, tm, drain, 0, unroll=8)

    @pl.when(i + 2 < n_tiles)
    def _():
        _slot_fetch(slots_hbm, i + 2, slot_smem, i % 2, slot_sem).start()

    g = gate_ref[...]
    y = g[:, 0:1] * ybuf_ref[i % 2, 0] + g[:, 1:2] * ybuf_ref[i % 2, 1]
    o_ref[...] = _rms(xn_ref[...] + y, gf_ref[...])


def _combine(slots, ys, xn, gates, g_final):
    n = xn.shape[0]
    n_tiles, _, tm = slots.shape
    rows = lambda c: pl.BlockSpec((tm, c), lambda i: (i, 0))
    return pl.pallas_call(
        functools.partial(_combine_kernel, tm=tm),
        grid=(n_tiles,),
        in_specs=[pl.BlockSpec(memory_space=pl.ANY), pl.BlockSpec(memory_space=pl.ANY),
                  rows(D_MODEL), rows(LANES), pl.BlockSpec(g_final.shape, lambda i: (0, 0))],
        out_specs=rows(D_MODEL),
        out_shape=jax.ShapeDtypeStruct((n, D_MODEL), F32),
        scratch_shapes=[pltpu.SMEM((2, SUBLANES, tm), jnp.int32), pltpu.VMEM((2, 2, tm, D_MODEL), F32),
                        pltpu.SemaphoreType.DMA((2,)), pltpu.SemaphoreType.DMA((2,))],
        compiler_params=_cparams("arbitrary"),
        name="moe_combine",
    )(slots, ys, xn, gates, g_final)


def _moe(x2d, ya, yb, yc, wo, g_ffn, w_router, b_router, wg, wu, wd, g_final):
    n = x2d.shape[0]
    bm = _row_tile(n, 512)
    cap = n
    xn, h, gates, slots, counts = _out_router(x2d, ya, yb, yc, wo, g_ffn, w_router, b_router, cap)
    counts = counts[0, :N_EXPERTS]
    hs = _dispatch(counts, slots, h, cap, bm)
    ys = _experts(counts, hs, wg, wu, wd, cap, bm, 2 * n)
    return _combine(slots, ys, xn, gates, g_final)


IN_SIZES = (2 * CONV_CH, SSM_INNER, SSM_XBC, SSM_HEADS, GDN_QKV, GDN_INNER, GDN_HEADS, GDN_HEADS)


def _prep_layer(l, p):
    off = np.concatenate([[0], np.cumsum(IN_SIZES)])
    col = lambda i: p['w_in'][l][:, off[i]:off[i + 1]]
    a_in, z, xbc, dt, qkv, gate, b_raw, a_raw = (col(i) for i in range(8))
    small = jnp.concatenate([dt, b_raw, a_raw, jnp.zeros((D_MODEL, SMALL_W - 3 * SSM_HEADS), F32)], axis=1)
    wo = p['w_out'][l].astype(BF16)
    return dict(
        wa=a_in.astype(BF16),
        wzx=jnp.concatenate([z, xbc], axis=1).astype(BF16),
        wqg=jnp.concatenate([qkv, gate], axis=1).astype(BF16),
        ws=small.astype(BF16),
        wo=(wo[:CONV_CH], wo[CONV_CH:CONV_CH + SSM_INNER], wo[CONV_CH + SSM_INNER:]),
    )


def _trunk(x, st_conv_a, st_ssm_conv, st_ssm, st_gdn_conv, st_gdn, p, prep, ssd_rows, gdn_rows, bb):
    bsz, t, _ = x.shape
    n = bsz * t
    depth = p['g_mix'].shape[0]
    x2d = x.reshape(n, D_MODEL)
    new = [[] for _ in range(5)]
    for l in range(depth):
        w = prep[l]
        a_in, zx, qg, small = _norm_proj(x2d, p['g_mix'][l][None, :], w['wa'], w['wzx'], w['wqg'], w['ws'])
        a_in = a_in.reshape(bsz, t, -1)
        zx = zx.reshape(bsz, t, -1)
        qg = qg.reshape(bsz, t, -1)
        small = small.reshape(bsz, t, -1)
        ya, conv_a = _conv_a(a_in, st_conv_a[l], p['conv_a_w'][l], p['conv_a_b'][l][None, :],
                             p['ln_a_g'][l][None, :], p['ln_a_b'][l][None, :])
        yb, ssm = _ssd(zx, small, st_ssm_conv[l], _ssd_state_in(st_ssm[l]), p['ssm_conv_w'][l], p['ssm_conv_b'][l],
                       p['ssm_dt_bias'][l], p['ssm_a_log'][l], p['ssm_d'][l], p['ssm_norm_g'][l], ssd_rows, bb)
        yc, gdn = _gdn(qg, small, st_gdn_conv[l], _gdn_state_in(st_gdn[l]), p['gdn_conv_w'][l],
                       p['gdn_a_log'][l], p['gdn_dt_bias'][l], p['gdn_norm_g'][l], gdn_rows[0], gdn_rows[1], bb)
        assert t >= max(SSM_CONV_W, GDN_CONV_W) - 1
        new[0].append(conv_a)
        new[1].append(zx[:, t - (SSM_CONV_W - 1):, SSM_INNER:])
        new[2].append(_ssd_state_out(ssm))
        new[3].append(qg[:, t - (GDN_CONV_W - 1):, :GDN_QKV])
        new[4].append(_gdn_state_out(gdn))
        flat = lambda y: y.reshape(n, -1)
        g_ffn = p['g_ffn'][l][None, :]
        if l % 2 == 0:
            x2d = _out_ffn(x2d, flat(ya), flat(yb), flat(yc), w['wo'], g_ffn,
                           prep['ffn'][l // 2][0], prep['ffn'][l // 2][1], prep['ffn'][l // 2][2])
        else:
            x2d = _moe(x2d, flat(ya), flat(yb), flat(yc), w['wo'], g_ffn, p['moe_w_router'][l // 2],
                       p['moe_b_router'][l // 2], *prep['moe'][l // 2], p['g_final'][None, :])
    return (x2d.reshape(bsz, t, D_MODEL),) + tuple(jnp.stack(s) for s in new)


def kernel(x_prompt, x_sample, state_conv_a, state_ssm_conv, state_ssm, state_gdn_conv, state_gdn, g_mix, w_in, conv_a_w, conv_a_b, ln_a_g, ln_a_b, ssm_conv_w, ssm_conv_b, ssm_dt_bias, ssm_a_log, ssm_d, ssm_norm_g, gdn_conv_w, gdn_a_log, gdn_dt_bias, gdn_norm_g, w_out, g_ffn, ffn_w_gate, ffn_w_up, ffn_w_down, moe_w_router, moe_b_router, moe_w_gate, moe_w_up, moe_w_down, g_final):
    p = dict(g_mix=g_mix, w_in=w_in, conv_a_w=conv_a_w, conv_a_b=conv_a_b, ln_a_g=ln_a_g, ln_a_b=ln_a_b,
             ssm_conv_w=ssm_conv_w, ssm_conv_b=ssm_conv_b, ssm_dt_bias=ssm_dt_bias, ssm_a_log=ssm_a_log,
             ssm_d=ssm_d, ssm_norm_g=ssm_norm_g, gdn_conv_w=gdn_conv_w, gdn_a_log=gdn_a_log,
             gdn_dt_bias=gdn_dt_bias, gdn_norm_g=gdn_norm_g, w_out=w_out, g_ffn=g_ffn,
             moe_w_router=moe_w_router, moe_b_router=moe_b_router, g_final=g_final)
    depth = g_mix.shape[0]
    assert depth % 2 == 0, "the final RMSNorm is fused into the expert layer, which must come last"
    prep = {l: _prep_layer(l, p) for l in range(depth)}
    prep['ffn'] = [(ffn_w_gate[i].astype(BF16), ffn_w_up[i].astype(BF16), ffn_w_down[i].astype(BF16))
                   for i in range(ffn_w_gate.shape[0])]
    prep['moe'] = [(moe_w_gate[i].astype(BF16), moe_w_up[i].astype(BF16), moe_w_down[i].astype(BF16))
                   for i in range(moe_w_gate.shape[0])]
    bp, dt = x_prompt.shape[0], x_prompt.dtype
    zeros = lambda *s: jnp.zeros((depth, bp) + s, dt)
    outs_p = _trunk(x_prompt, zeros(CONV_W - 1, CONV_CH), zeros(SSM_CONV_W - 1, SSM_XBC),
                    zeros(SSM_HEADS, HEAD_DIM, SSM_STATE), zeros(GDN_CONV_W - 1, GDN_QKV),
                    zeros(GDN_HEADS, HEAD_DIM, HEAD_DIM), p, prep, ssd_rows=min(128, x_prompt.shape[1]),
                    gdn_rows=(min(256, x_prompt.shape[1]), min(64, x_prompt.shape[1])), bb=2)
    outs_s = _trunk(x_sample, state_conv_a, state_ssm_conv, state_ssm, state_gdn_conv, state_gdn, p, prep,
                    ssd_rows=x_sample.shape[1], gdn_rows=(x_sample.shape[1], x_sample.shape[1]), bb=2)
    return (outs_p[0], outs_s[0]) + outs_p[1:] + outs_s[1:]
```

```python
import functools
import math

import jax
import jax.numpy as jnp
import numpy as np
from jax import lax
from jax.experimental import pallas as pl
from jax.experimental.pallas import tpu as pltpu

F32 = jnp.float32
BF16 = jnp.bfloat16
EPS = 1e-6

LANES = 128
SUBLANES = 8
VMEM_BYTES_V7X = 64 * 1024 * 1024
VMEM_LIMIT = VMEM_BYTES_V7X * 3 // 4

D_MODEL = 1024
CONV_CH = 256
CONV_W = 31
SSM_HEADS = 6
HEAD_DIM = 64
SSM_INNER = SSM_HEADS * HEAD_DIM
SSM_STATE = 64
SSM_GROUPS = 2
SSM_XBC = SSM_INNER + 2 * SSM_GROUPS * SSM_STATE
SSM_CONV_W = 4
GDN_HEADS = 6
GDN_INNER = GDN_HEADS * HEAD_DIM
GDN_QKV = 3 * GDN_INNER
GDN_CONV_W = 4
N_PAIRS = 3
N_EXPERTS = 8
SMALL_W = LANES


def _cparams(*sem):
    return pltpu.CompilerParams(dimension_semantics=sem, vmem_limit_bytes=VMEM_LIMIT)


def _dot(a, b):
    return jnp.dot(a.astype(BF16), b.astype(BF16), preferred_element_type=F32)


def _dot_nt(a, b):
    return lax.dot_general(a.astype(BF16), b.astype(BF16), (((1,), (1,)), ((), ())),
                           preferred_element_type=F32)


def _split3(x):
    hi = x.astype(BF16)
    r1 = x - hi.astype(F32)
    mid = r1.astype(BF16)
    lo = (r1 - mid.astype(F32)).astype(BF16)
    return hi, mid, lo


def _dot_sel(x, sel_bf16):
    hi, mid, lo = _split3(x)
    d = lambda p: jnp.dot(p, sel_bf16, preferred_element_type=F32)
    return d(hi) + d(mid) + d(lo)


def _sel_dot(sel_bf16, x):
    hi, mid, lo = _split3(x)
    d = lambda p: jnp.dot(sel_bf16, p, preferred_element_type=F32)
    return d(hi) + d(mid) + d(lo)


def _sel_dot_nt(sel_bf16, x):
    hi, mid, lo = _split3(x)
    d = lambda p: lax.dot_general(sel_bf16, p, (((1,), (1,)), ((), ())), preferred_element_type=F32)
    return d(hi) + d(mid) + d(lo)


def _dot_hp(a, b):
    ah = a.astype(BF16)
    al = (a - ah.astype(F32)).astype(BF16)
    bh = b.astype(BF16)
    bl = (b - bh.astype(F32)).astype(BF16)
    d = lambda p, q: jnp.dot(p, q, preferred_element_type=F32)
    return d(ah, bh) + (d(ah, bl) + d(al, bh))


def _silu(x):
    return x * (1.0 / (1.0 + jnp.exp(-x)))


def _sigmoid(x):
    return 1.0 / (1.0 + jnp.exp(-x))


def _softplus(x):
    return jnp.maximum(x, 0.0) + jnp.log(1.0 + jnp.exp(-jnp.abs(x)))


def _iota(shape, dim):
    return lax.broadcasted_iota(jnp.int32, shape, dim)


def _norm_proj_kernel(x_ref, g_ref, wa_ref, wzx_ref, wqg_ref, ws_ref, a_ref, zx_ref, qg_ref, s_ref):
    x = x_ref[...]
    u = x * lax.rsqrt(jnp.mean(x * x, axis=-1, keepdims=True) + EPS) * g_ref[...]
    ub = u.astype(BF16)
    a_ref[...] = jnp.dot(ub, wa_ref[...], preferred_element_type=F32)
    zx_ref[...] = jnp.dot(ub, wzx_ref[...], preferred_element_type=F32)
    qg_ref[...] = jnp.dot(ub, wqg_ref[...], preferred_element_type=F32)
    s_ref[...] = jnp.dot(ub, ws_ref[...], preferred_element_type=F32)


def _row_tile(n, want):
    t = min(want, n)
    while n % t:
        t //= 2
    return t


def _norm_proj(x2d, g, wa, wzx, wqg, ws):
    n = x2d.shape[0]
    tm = _row_tile(n, 512)
    full = lambda w: pl.BlockSpec(w.shape, lambda i: (0, 0))
    rows = lambda c: pl.BlockSpec((tm, c), lambda i: (i, 0))
    widths = (wa.shape[1], wzx.shape[1], wqg.shape[1], ws.shape[1])
    return pl.pallas_call(
        _norm_proj_kernel,
        grid=(n // tm,),
        in_specs=[rows(D_MODEL), full(g), full(wa), full(wzx), full(wqg), full(ws)],
        out_specs=[rows(c) for c in widths],
        out_shape=[jax.ShapeDtypeStruct((n, c), F32) for c in widths],
        compiler_params=_cparams("parallel"),
        name="norm_proj",
    )(x2d, g, wa, wzx, wqg, ws)


CONV_PAD = 32
CONV_ROWS = 32


def _conv_a_kernel(a_ref, hist_ref, w_ref, b_ref, lg_ref, lb_ref, y_ref, nh_ref, buf_ref, *, tt):
    t = pl.program_id(1)

    @pl.when(t == 0)
    def _():
        buf_ref[0:CONV_PAD, :] = hist_ref[0]

    a = a_ref[0]
    glu = a[:, :CONV_CH] * _sigmoid(a[:, CONV_CH:])
    buf_ref[CONV_PAD:CONV_PAD + tt, :] = glu
    off = CONV_PAD - (CONV_W - 1)
    for r0 in range(0, tt, CONV_ROWS):
        acc = jnp.zeros((CONV_ROWS, CONV_CH), F32)
        for k in range(CONV_W):
            acc = acc + w_ref[k:k + 1, :] * buf_ref[r0 + off + k:r0 + off + k + CONV_ROWS, :]
        y = acc + b_ref[...]
        mu = jnp.mean(y, axis=-1, keepdims=True)
        yc = y - mu
        var = jnp.mean(yc * yc, axis=-1, keepdims=True)
        y = yc * lax.rsqrt(var + EPS) * lg_ref[...] + lb_ref[...]
        y_ref[0, r0:r0 + CONV_ROWS, :] = _silu(y).astype(y_ref.dtype)
    nh_ref[0] = buf_ref[tt:tt + CONV_PAD, :]
    buf_ref[0:CONV_PAD, :] = buf_ref[tt:tt + CONV_PAD, :]


def _conv_a(a_in, hist, w, b, lg, lb):
    bsz, t, _ = a_in.shape
    tt = _row_tile(t, 256)
    hist_p = jnp.pad(hist, ((0, 0), (CONV_PAD - (CONV_W - 1), 0), (0, 0)))
    vec = lambda v: pl.BlockSpec(v.shape, lambda i, j: (0, 0))
    y, nh = pl.pallas_call(
        functools.partial(_conv_a_kernel, tt=tt),
        grid=(bsz, t // tt),
        in_specs=[pl.BlockSpec((1, tt, 2 * CONV_CH), lambda i, j: (i, j, 0)),
                  pl.BlockSpec((1, CONV_PAD, CONV_CH), lambda i, j: (i, 0, 0)),
                  vec(w), vec(b), vec(lg), vec(lb)],
        out_specs=[pl.BlockSpec((1, tt, CONV_CH), lambda i, j: (i, j, 0)),
                   pl.BlockSpec((1, CONV_PAD, CONV_CH), lambda i, j: (i, 0, 0))],
        out_shape=[jax.ShapeDtypeStruct((bsz, t, CONV_CH), BF16),
                   jax.ShapeDtypeStruct((bsz, CONV_PAD, CONV_CH), F32)],
        scratch_shapes=[pltpu.VMEM((tt + CONV_PAD, CONV_CH), F32)],
        compiler_params=_cparams("arbitrary", "arbitrary"),
        name="conv_a",
    )(a_in, hist_p, w, b, lg, lb)
    return y, nh[:, CONV_PAD - (CONV_W - 1):, :]


HIST_PAD = 8


def _short_conv(x, buf_ref, b, w_ref, width, rows):
    buf_ref[b, HIST_PAD:HIST_PAD + rows, :] = x
    off = HIST_PAD - (width - 1)
    acc = w_ref[0:1, :] * buf_ref[b, off:off + rows, :]
    for k in range(1, width):
        acc = acc + w_ref[k:k + 1, :] * buf_ref[b, off + k:off + k + rows, :]
    buf_ref[b, 0:HIST_PAD, :] = buf_ref[b, rows:rows + HIST_PAD, :]
    return acc


def _tri(rows, lower):
    r = _iota((rows, rows), 0)
    c = _iota((rows, rows), 1)
    return (r >= c) if lower else (r <= c)


def _row_select(first_lane):
    r = _iota((SUBLANES, LANES), 0)
    c = _iota((SUBLANES, LANES), 1)
    return ((c == r + first_lane) & (r < SSM_HEADS)).astype(BF16)


def _cumsum_both(a, first_lane, rows):
    tril = _tri(rows, True).astype(BF16)
    triu = _tri(rows, False).astype(BF16)
    col = _sel_dot(tril, a)
    a_t = _sel_dot_nt(_row_select(first_lane), a)
    row = _dot_sel(a_t, triu)
    return col, row


def _pair_mask():
    return _iota((1, LANES), 1) < HEAD_DIM


def _ssd_kernel(zx_ref, s_ref, hist_ref, h0_ref, cw_ref, cb_ref, dtb_ref, arow_ref, dx_ref, ng_ref,
                ex_ref, gm_ref, y_ref, hout_ref, buf_ref, h_ref, *, rows, bb):
    c = pl.program_id(1)

    @pl.when(c == 0)
    def _():
        buf_ref[:, 0:HIST_PAD, :] = hist_ref[...]
        h_ref[...] = h0_ref[...]

    causal = _tri(rows, True)
    lane = _iota((1, LANES), 1)
    first_half = _pair_mask()
    srow_g = _iota((LANES, LANES), 0) // SSM_STATE
    scol_h = _iota((LANES, LANES), 1) // HEAD_DIM
    ex = ex_ref[...]

    for b in range(bb):
        zx = zx_ref[b]
        z = zx[:, :SSM_INNER]
        conv = _short_conv(zx[:, SSM_INNER:], buf_ref, b, cw_ref, SSM_CONV_W, rows)
        xbc = _silu(conv + cb_ref[...])
        xs = xbc[:, :SSM_INNER]
        bm = xbc[:, SSM_INNER:SSM_INNER + LANES]
        cm = xbc[:, SSM_INNER + LANES:]

        dt = _softplus(s_ref[b] + dtb_ref[...])
        a = dt * arow_ref[...]
        acum, acum_row = _cumsum_both(a, 0, rows)
        a_last = acum[rows - 1:rows, :]
        dt_x = _dot_sel(dt, ex)
        ea_x = _dot_sel(jnp.exp(acum), ex)
        te_x = _dot_sel(jnp.exp(a_last - acum), ex)
        cd_x = _dot_sel(jnp.broadcast_to(jnp.exp(a_last), (SUBLANES, LANES)), ex)[0:1, :]

        scores = []
        for g in range(SSM_GROUPS):
            cm_g = jnp.where(lane // SSM_STATE == g, cm, 0.0)
            scores.append(_dot_nt(cm_g, bm))
        bm_t = jnp.transpose(bm)

        ys = []
        for p in range(N_PAIRS):
            sl = slice(p * LANES, (p + 1) * LANES)
            x_p = xs[:, sl]
            xdt = x_p * dt_x[:, sl]
            yd = []
            for hh in range(2):
                h = 2 * p + hh
                diff = jnp.broadcast_to(acum[:, h:h + 1], (rows, rows)) - acum_row[h:h + 1, :]
                dec = jnp.exp(jnp.where(causal, diff, -1e30))
                yd.append(_dot(scores[h // (SSM_HEADS // SSM_GROUPS)] * dec, xdt))
            y_diag = jnp.where(first_half, yd[0], yd[1])
            h_p = h_ref[b, p]
            y_off = _dot(cm, h_p) * ea_x[:, sl]
            keep = srow_g == (2 * p + scol_h) // (SSM_HEADS // SSM_GROUPS)
            upd = _dot(bm_t, xdt * te_x[:, sl])
            h_ref[b, p] = h_p * cd_x[:, sl] + jnp.where(keep, upd, 0.0)
            ys.append(y_diag + y_off + dx_ref[:, sl] * x_p)
        y = jnp.concatenate(ys, axis=-1) * _silu(z)
        ms = _dot_sel(y * y, gm_ref[...]) * (1.0 / (SSM_INNER // SSM_GROUPS))
        y = y * lax.rsqrt(ms + EPS) * ng_ref[...]
        y_ref[b] = y.astype(y_ref.dtype)

    @pl.when(c == pl.num_programs(1) - 1)
    def _():
        hout_ref[...] = h_ref[...]


def _expand_matrix(first_lane):
    m = np.zeros((LANES, SSM_INNER), np.float32)
    for h in range(SSM_HEADS):
        m[first_lane + h, h * HEAD_DIM:(h + 1) * HEAD_DIM] = 1.0
    return jnp.asarray(m, BF16)


def _group_matrix(width):
    idx = np.arange(SSM_INNER) // width
    return jnp.asarray((idx[:, None] == idx[None, :]).astype(np.float32), BF16)


def _lane_row(vals, first_lane):
    return jnp.zeros((1, LANES), F32).at[0, first_lane:first_lane + vals.shape[0]].set(vals.astype(F32))


def _ssd(zx, small, hist, h0, cw, cb, dt_bias, a_log, d_skip, norm_g, rows, bb):
    bsz, t, _ = zx.shape
    hist_p = jnp.pad(hist, ((0, 0), (HIST_PAD - (SSM_CONV_W - 1), 0), (0, 0)))
    dtb = _lane_row(dt_bias, 0)
    arow = _lane_row(-jnp.exp(a_log.astype(F32)), 0)
    dx = jnp.repeat(d_skip.astype(F32), HEAD_DIM)[None, :]
    full = lambda v: pl.BlockSpec(v.shape, lambda i, j: (0,) * v.ndim)
    blk = lambda c: pl.BlockSpec((bb, rows, c), lambda i, j: (i, j, 0))
    st = pl.BlockSpec((bb, N_PAIRS, LANES, LANES), lambda i, j: (i, 0, 0, 0))
    consts = (cw, cb[None, :], dtb, arow, dx, norm_g[None, :], _expand_matrix(0),
              _group_matrix(SSM_INNER // SSM_GROUPS))
    return pl.pallas_call(
        functools.partial(_ssd_kernel, rows=rows, bb=bb),
        grid=(bsz // bb, t // rows),
        in_specs=[blk(zx.shape[-1]), blk(LANES),
                  pl.BlockSpec((bb, HIST_PAD, SSM_XBC), lambda i, j: (i, 0, 0)), st]
                 + [full(v) for v in consts],
        out_specs=[blk(SSM_INNER), st],
        out_shape=[jax.ShapeDtypeStruct((bsz, t, SSM_INNER), BF16),
                   jax.ShapeDtypeStruct((bsz, N_PAIRS, LANES, LANES), F32)],
        scratch_shapes=[pltpu.VMEM((bb, rows + HIST_PAD, SSM_XBC), F32),
                        pltpu.VMEM((bb, N_PAIRS, LANES, LANES), F32)],
        compiler_params=_cparams("arbitrary", "arbitrary"),
        name="ssd",
    )(zx, small, hist_p, h0, *consts)


def _ssd_state_in(h):
    bsz = h.shape[0]
    out = jnp.zeros((bsz, N_PAIRS, SSM_GROUPS, SSM_STATE, 2, HEAD_DIM), F32)
    for hd in range(SSM_HEADS):
        g = hd // (SSM_HEADS // SSM_GROUPS)
        out = out.at[:, hd // 2, g, :, hd % 2, :].set(jnp.swapaxes(h[:, hd], 1, 2))
    return out.reshape(bsz, N_PAIRS, LANES, LANES)


def _ssd_state_out(hp):
    bsz = hp.shape[0]
    hp = hp.reshape(bsz, N_PAIRS, SSM_GROUPS, SSM_STATE, 2, HEAD_DIM)
    heads = [jnp.swapaxes(hp[:, hd // 2, hd // (SSM_HEADS // SSM_GROUPS), :, hd % 2, :], 1, 2)
             for hd in range(SSM_HEADS)]
    return jnp.stack(heads, axis=1)


BETA_LANE = 6
DECAY_LANE = 12


def _bdot(a, b):
    return lax.dot_general(a.astype(BF16), b.astype(BF16), (((2,), (1,)), ((0,), (0,))),
                           preferred_element_type=F32)


def _bdot_nt(a, b):
    return lax.dot_general(a.astype(BF16), b.astype(BF16), (((2,), (2,)), ((0,), (0,))),
                           preferred_element_type=F32)


def _bdot_hp(a, b):
    ah = a.astype(BF16)
    al = (a - ah.astype(F32)).astype(BF16)
    bh = b.astype(BF16)
    bl = (b - bh.astype(F32)).astype(BF16)
    d = lambda p, q: lax.dot_general(p, q, (((2,), (1,)), ((0,), (0,))), preferred_element_type=F32)
    return d(ah, bh) + (d(ah, bl) + d(al, bh))


def _unit_lower_inverse(m, block):
    rdim = m.shape[-1]
    eye = (_iota((rdim, rdim), 0) == _iota((rdim, rdim), 1)).astype(F32)
    x = -m
    t = eye + x
    p = x
    for _ in range(int(math.log2(block)) - 2):
        p = _bdot(p, p)
        t = t + _bdot(t, p)
    resid = (eye - t) - _bdot_hp(m, t)
    return t + _bdot(t, resid)


def _gdn_kernel(qg_ref, s_ref, hist_ref, s0_ref, cw_ref, bias_ref, arow_ref, ng_ref, eb_ref, eg_ref, hm_ref,
                y_ref, sout_ref, buf_ref, st_ref, *, tt, ck, bb):
    c = pl.program_id(1)
    nck = tt // ck
    rdim = 2 * ck

    @pl.when(c == 0)
    def _():
        buf_ref[:, 0:HIST_PAD, :] = hist_ref[...]
        st_ref[...] = s0_ref[...]

    r_i = _iota((rdim, rdim), 0)
    c_i = _iota((rdim, rdim), 1)
    same_blk = (r_i // ck) == (c_i // ck)
    incl = same_blk & (r_i >= c_i)
    strict = same_blk & (r_i > c_i)
    first_half = _pair_mask()
    same_head = (_iota((LANES, LANES), 0) // HEAD_DIM) == (_iota((LANES, LANES), 1) // HEAD_DIM)
    tr = _iota((tt, tt), 0)
    tc = _iota((tt, tt), 1)
    blk_tril = (((tr // ck) == (tc // ck)) & (tr >= tc)).astype(BF16)
    eb = eb_ref[...]
    eg = eg_ref[...]
    hm = hm_ref[...]

    tiles = {name: [] for name in ('q', 'k', 'kb', 'rhs', 'qd', 'kd', 'col', 'cd')}
    gates = []
    for b in range(bb):
        qg = qg_ref[b]
        gates.append(qg[:, GDN_QKV:])
        qkv = _silu(_short_conv(qg[:, :GDN_QKV], buf_ref, b, cw_ref, GDN_CONV_W, tt))
        q = qkv[:, :GDN_INNER]
        k = qkv[:, GDN_INNER:2 * GDN_INNER]
        v = qkv[:, 2 * GDN_INNER:]
        q = q * lax.rsqrt(_dot_sel(q * q, hm) + EPS) * (HEAD_DIM ** -0.5)
        k = k * lax.rsqrt(_dot_sel(k * k, hm) + EPS)
        s = s_ref[b]
        beta = _sigmoid(s)
        g = _softplus(s + bias_ref[...]) * arow_ref[...]
        gc = _sel_dot(blk_tril, g)
        g_last = jnp.concatenate(
            [jnp.broadcast_to(gc[(i + 1) * ck - 1:(i + 1) * ck, :], (ck, LANES)) for i in range(nck)], axis=0)
        beta_x = _dot_sel(beta, eb)
        eg_x = _dot_sel(jnp.exp(gc), eg)
        kd_x = _dot_sel(jnp.exp(g_last - gc), eg)
        cd_x = _dot_sel(jnp.exp(g_last), eg)
        kb = k * beta_x
        full = dict(q=q, k=k, kb=kb, qd=q * eg_x, kd=k * kd_x, cd=cd_x)
        vb = v * beta_x
        kbe = kb * eg_x
        for i in range(nck):
            rs = slice(i * ck, (i + 1) * ck)
            for p in range(N_PAIRS):
                sl = slice(p * LANES, (p + 1) * LANES)
                for name, arr in full.items():
                    tiles[name].append(arr[rs, sl])
                tiles['rhs'].append(jnp.concatenate([vb[rs, sl], kbe[rs, sl]], axis=-1))
                lane0 = DECAY_LANE + 2 * p
                tiles['col'].append(jnp.concatenate(
                    [jnp.broadcast_to(gc[rs, lane0 + hh:lane0 + hh + 1], (ck, rdim)) for hh in range(2)], axis=0))

    st = lambda name: jnp.stack(tiles[name], axis=0)
    stack2 = lambda x: jnp.concatenate([jnp.where(first_half, x, 0.0), jnp.where(first_half, 0.0, x)], axis=1)
    k_st = stack2(st('k'))
    col = st('col')
    diff = col - jnp.swapaxes(col, 1, 2)
    dec = jnp.exp(jnp.where(incl, diff, -1e30))
    m = _bdot_nt(stack2(st('kb')), k_st) * jnp.where(strict, dec, 0.0)
    t_inv = _unit_lower_inverse(m, ck)
    rhs = st('rhs')
    sol = _bdot(t_inv, jnp.concatenate([rhs, rhs], axis=1))
    u = jnp.where(first_half, sol[:, :ck, :LANES], sol[:, ck:, :LANES])
    w = jnp.where(first_half, sol[:, :ck, LANES:], sol[:, ck:, LANES:])
    attn = _bdot_nt(stack2(st('q')), k_st) * dec
    qd = st('qd')
    kd_t = jnp.swapaxes(st('kd'), 1, 2)
    cd = st('cd')

    gsel = lambda x, i: jnp.stack([x[(b * nck + i) * N_PAIRS + p] for b in range(bb) for p in range(N_PAIRS)], axis=0)
    state = st_ref[...].reshape(bb * N_PAIRS, LANES, LANES)
    o_chunks = []
    for i in range(nck):
        v_new = gsel(u, i) - _bdot(gsel(w, i), state)
        intra = _bdot(gsel(attn, i), jnp.concatenate([v_new, v_new], axis=1))
        o_chunks.append(_bdot(gsel(qd, i), state) + jnp.where(first_half, intra[:, :ck], intra[:, ck:]))
        upd = _bdot(gsel(kd_t, i), v_new)
        state = state * gsel(cd, i)[:, 0:1, :] + jnp.where(same_head, upd, 0.0)
    st_ref[...] = state.reshape(bb, N_PAIRS, LANES, LANES)

    for b in range(bb):
        o = jnp.concatenate(
            [jnp.concatenate([o_chunks[i][b * N_PAIRS + p] for p in range(N_PAIRS)], axis=-1) for i in range(nck)],
            axis=0)
        ms = _dot_sel(o * o, hm) * (1.0 / HEAD_DIM)
        o = o * lax.rsqrt(ms + EPS) * ng_ref[...] * _silu(gates[b])
        y_ref[b] = o.astype(y_ref.dtype)

    @pl.when(c == pl.num_programs(1) - 1)
    def _():
        sout_ref[...] = st_ref[...]


def _gdn(qg, small, hist, s0, cw, a_log, dt_bias, norm_g, tt, ck, bb):
    bsz, t, _ = qg.shape
    hist_p = jnp.pad(hist, ((0, 0), (HIST_PAD - (GDN_CONV_W - 1), 0), (0, 0)))
    bias = _lane_row(dt_bias, DECAY_LANE)
    arow = _lane_row(-jnp.exp(a_log.astype(F32)), DECAY_LANE)
    ng = jnp.tile(norm_g.astype(F32), GDN_HEADS)[None, :]
    full = lambda v: pl.BlockSpec(v.shape, lambda i, j: (0,) * v.ndim)
    blk = lambda c: pl.BlockSpec((bb, tt, c), lambda i, j: (i, j, 0))
    st = pl.BlockSpec((bb, N_PAIRS, LANES, LANES), lambda i, j: (i, 0, 0, 0))
    consts = (cw, bias, arow, ng, _expand_matrix(BETA_LANE), _expand_matrix(DECAY_LANE), _group_matrix(HEAD_DIM))
    return pl.pallas_call(
        functools.partial(_gdn_kernel, tt=tt, ck=ck, bb=bb),
        grid=(bsz // bb, t // tt),
        in_specs=[blk(qg.shape[-1]), blk(LANES),
                  pl.BlockSpec((bb, HIST_PAD, GDN_QKV), lambda i, j: (i, 0, 0)), st]
                 + [full(v) for v in consts],
        out_specs=[blk(GDN_INNER), st],
        out_shape=[jax.ShapeDtypeStruct((bsz, t, GDN_INNER), BF16),
                   jax.ShapeDtypeStruct((bsz, N_PAIRS, LANES, LANES), F32)],
        scratch_shapes=[pltpu.VMEM((bb, tt + HIST_PAD, GDN_QKV), F32),
                        pltpu.VMEM((bb, N_PAIRS, LANES, LANES), F32)],
        compiler_params=_cparams("arbitrary", "arbitrary"),
        name="gdn",
    )(qg, small, hist_p, s0, *consts)


def _gdn_state_in(s):
    bsz = s.shape[0]
    out = jnp.zeros((bsz, N_PAIRS, 2, HEAD_DIM, 2, HEAD_DIM), F32)
    for hd in range(GDN_HEADS):
        out = out.at[:, hd // 2, hd % 2, :, hd % 2, :].set(s[:, hd])
    return out.reshape(bsz, N_PAIRS, LANES, LANES)


def _gdn_state_out(sp):
    bsz = sp.shape[0]
    sp = sp.reshape(bsz, N_PAIRS, 2, HEAD_DIM, 2, HEAD_DIM)
    return jnp.stack([sp[:, hd // 2, hd % 2, :, hd % 2, :] for hd in range(GDN_HEADS)], axis=1)


def _mix_residual(x_ref, ya_ref, yb_ref, yc_ref, woa_ref, wob_ref, woc_ref):
    mix = (jnp.dot(ya_ref[...], woa_ref[...], preferred_element_type=F32)
           + jnp.dot(yb_ref[...], wob_ref[...], preferred_element_type=F32)
           + jnp.dot(yc_ref[...], woc_ref[...], preferred_element_type=F32))
    return x_ref[...] + mix


def _rms(x, g):
    return x * lax.rsqrt(jnp.mean(x * x, axis=-1, keepdims=True) + EPS) * g


def _ffn_kernel(x_ref, ya_ref, yb_ref, yc_ref, woa_ref, wob_ref, woc_ref, g_ref, wg_ref, wu_ref, wd_ref,
                o_ref, xn_ref, h_ref, acc_ref):
    j = pl.program_id(1)

    @pl.when(j == 0)
    def _():
        xn = _mix_residual(x_ref, ya_ref, yb_ref, yc_ref, woa_ref, wob_ref, woc_ref)
        xn_ref[...] = xn
        h_ref[...] = _rms(xn, g_ref[...]).astype(BF16)
        acc_ref[...] = jnp.zeros_like(acc_ref)

    h = h_ref[...]
    act = _silu(jnp.dot(h, wg_ref[...], preferred_element_type=F32)) * jnp.dot(h, wu_ref[...], preferred_element_type=F32)
    acc_ref[...] += jnp.dot(act.astype(BF16), wd_ref[...], preferred_element_type=F32)

    @pl.when(j == pl.num_programs(1) - 1)
    def _():
        o_ref[...] = xn_ref[...] + acc_ref[...]


FFN_CHUNKS = 2


def _out_ffn(x2d, ya, yb, yc, wo, g, wg, wu, wd):
    n = x2d.shape[0]
    tm = _row_tile(n, 512)
    f = wg.shape[1]
    tf = f // FFN_CHUNKS
    rows = lambda c: pl.BlockSpec((tm, c), lambda i, j: (i, 0))
    full = lambda w: pl.BlockSpec(w.shape, lambda i, j: (0, 0))
    return pl.pallas_call(
        _ffn_kernel,
        grid=(n // tm, FFN_CHUNKS),
        in_specs=[rows(D_MODEL), rows(ya.shape[1]), rows(yb.shape[1]), rows(yc.shape[1]),
                  full(wo[0]), full(wo[1]), full(wo[2]), full(g),
                  pl.BlockSpec((D_MODEL, tf), lambda i, j: (0, j)),
                  pl.BlockSpec((D_MODEL, tf), lambda i, j: (0, j)),
                  pl.BlockSpec((tf, D_MODEL), lambda i, j: (j, 0))],
        out_specs=rows(D_MODEL),
        out_shape=jax.ShapeDtypeStruct((n, D_MODEL), F32),
        scratch_shapes=[pltpu.VMEM((tm, D_MODEL), F32), pltpu.VMEM((tm, D_MODEL), BF16),
                        pltpu.VMEM((tm, D_MODEL), F32)],
        compiler_params=_cparams("parallel", "arbitrary"),
        name="out_ffn",
    )(x2d, ya, yb, yc, wo[0], wo[1], wo[2], g, wg, wu, wd)


ROUTE_TILE = 512


def _router_kernel(x_ref, ya_ref, yb_ref, yc_ref, woa_ref, wob_ref, woc_ref, g_ref, wr_ref, br_ref,
                   xn_ref, h_ref, gate_ref, slot_ref, cnt_ref, run_ref, *, cap):
    i = pl.program_id(0)

    @pl.when(i == 0)
    def _():
        run_ref[...] = jnp.zeros_like(run_ref)

    xn = _mix_residual(x_ref, ya_ref, yb_ref, yc_ref, woa_ref, wob_ref, woc_ref)
    xn_ref[...] = xn
    h = _rms(xn, g_ref[...])
    h_ref[...] = h
    tm = h.shape[0]
    lane = _iota((1, LANES), 1)
    logits = jnp.where(lane < N_EXPERTS, _dot_hp(h, wr_ref[...]) + br_ref[...], -jnp.inf)
    m1 = jnp.max(logits, axis=-1, keepdims=True)
    i1 = jnp.min(jnp.where(logits == m1, lane, LANES), axis=-1, keepdims=True)
    rest = jnp.where(lane == i1, -jnp.inf, logits)
    m2 = jnp.max(rest, axis=-1, keepdims=True)
    i2 = jnp.min(jnp.where(rest == m2, lane, LANES), axis=-1, keepdims=True)
    e2 = jnp.exp(m2 - m1)
    inv = 1.0 / (1.0 + e2)
    gate_ref[...] = jnp.where(lane == 0, inv, 0.0) + jnp.where(lane == 1, e2 * inv, 0.0)

    chosen = jnp.where((lane == i1) | (lane == i2), 1.0, 0.0)
    before = _iota((tm, tm), 0) > _iota((tm, tm), 1)
    rank = jnp.dot(before.astype(BF16), chosen.astype(BF16), preferred_element_type=F32) + run_ref[0:1, :]
    base = lane.astype(F32) * float(cap)
    pick = lambda idx: jnp.sum(jnp.where(lane == idx, rank + base, 0.0), axis=-1, keepdims=True)
    slots = jnp.where(lane == 0, pick(i1), 0.0) + jnp.where(lane == 1, pick(i2), 0.0)
    slot_ref[0] = jnp.transpose(slots)[0:SUBLANES, :].astype(jnp.int32)
    run_ref[...] = run_ref[...] + jnp.sum(chosen, axis=0, keepdims=True)
    cnt_ref[...] = run_ref[...].astype(jnp.int32)


def _out_router(x2d, ya, yb, yc, wo, g, w_router, b_router, cap):
    n = x2d.shape[0]
    tm = _row_tile(n, ROUTE_TILE)
    wr = jnp.zeros((D_MODEL, LANES), F32).at[:, :N_EXPERTS].set(w_router.astype(F32))
    br = _lane_row(b_router, 0)
    rows = lambda c: pl.BlockSpec((tm, c), lambda i: (i, 0))
    full = lambda w: pl.BlockSpec(w.shape, lambda i: (0, 0))
    return pl.pallas_call(
        functools.partial(_router_kernel, cap=cap),
        grid=(n // tm,),
        in_specs=[rows(D_MODEL), rows(ya.shape[1]), rows(yb.shape[1]), rows(yc.shape[1]),
                  full(wo[0]), full(wo[1]), full(wo[2]), full(g), full(wr), full(br)],
        out_specs=[rows(D_MODEL), rows(D_MODEL), rows(LANES),
                   pl.BlockSpec((1, SUBLANES, tm), lambda i: (i, 0, 0)),
                   pl.BlockSpec((SUBLANES, LANES), lambda i: (0, 0))],
        out_shape=[jax.ShapeDtypeStruct((n, D_MODEL), F32), jax.ShapeDtypeStruct((n, D_MODEL), F32),
                   jax.ShapeDtypeStruct((n, LANES), F32),
                   jax.ShapeDtypeStruct((n // tm, SUBLANES, tm), jnp.int32),
                   jax.ShapeDtypeStruct((SUBLANES, LANES), jnp.int32)],
        scratch_shapes=[pltpu.VMEM((SUBLANES, LANES), F32)],
        compiler_params=_cparams("arbitrary"),
        name="out_router",
    )(x2d, ya, yb, yc, wo[0], wo[1], wo[2], g, wr, br)


def _row_copy(src_ref, src_row, dst_ref, dst_row, sem):
    return pltpu.make_async_copy(src_ref.at[pl.ds(src_row, 1)], dst_ref.at[pl.ds(dst_row, 1)], sem)


def _slot_fetch(slots_hbm, tile, smem_ref, buf, sem):
    return pltpu.make_async_copy(slots_hbm.at[tile], smem_ref.at[buf], sem.at[buf])


def _dispatch_kernel(cnt_ref, slots_hbm, h_ref, hs_hbm, slot_smem, zero_ref, slot_sem, row_sem, pad_sem,
                     *, tm, cap, bm):
    i = pl.program_id(0)
    n_tiles = pl.num_programs(0)

    @pl.when(i == 0)
    def _():
        _slot_fetch(slots_hbm, 0, slot_smem, 0, slot_sem).start()

    @pl.when(i + 1 < n_tiles)
    def _():
        _slot_fetch(slots_hbm, i + 1, slot_smem, (i + 1) % 2, slot_sem).start()

    _slot_fetch(slots_hbm, i, slot_smem, i % 2, slot_sem).wait()

    def issue(r, carry):
        for k in range(2):
            _row_copy(h_ref, r, hs_hbm, slot_smem[i % 2, k, r], row_sem).start()
        return carry

    lax.fori_loop(0, tm, issue, 0, unroll=8)

    def drain(r, carry):
        for k in range(2):
            _row_copy(h_ref, 0, hs_hbm, 0, row_sem).wait()
        return carry

    lax.fori_loop(0, tm, drain, 0, unroll=8)

    @pl.when(i == n_tiles - 1)
    def _():
        zero_ref[...] = jnp.zeros_like(zero_ref)
        for e in range(N_EXPERTS):
            c = cnt_ref[e]
            n_pad = ((c + bm - 1) // bm) * bm - c

            def fill(r, carry):
                _row_copy(zero_ref, 0, hs_hbm, e * cap + c + r, pad_sem).start()
                return carry

            def fill_wait(r, carry):
                _row_copy(zero_ref, 0, hs_hbm, 0, pad_sem).wait()
                return carry

            lax.fori_loop(0, n_pad, fill, 0)
            lax.fori_loop(0, n_pad, fill_wait, 0)


def _dispatch(counts, slots, h, cap, bm):
    n = h.shape[0]
    n_tiles, _, tm = slots.shape
    return pl.pallas_call(
        functools.partial(_dispatch_kernel, tm=tm, cap=cap, bm=bm),
        grid_spec=pltpu.PrefetchScalarGridSpec(
            num_scalar_prefetch=1,
            grid=(n_tiles,),
            in_specs=[pl.BlockSpec(memory_space=pl.ANY), pl.BlockSpec((tm, D_MODEL), lambda i, cnt: (i, 0))],
            out_specs=pl.BlockSpec(memory_space=pl.ANY),
            scratch_shapes=[pltpu.SMEM((2, SUBLANES, tm), jnp.int32), pltpu.VMEM((SUBLANES, D_MODEL), F32),
                            pltpu.SemaphoreType.DMA((2,)), pltpu.SemaphoreType.DMA, pltpu.SemaphoreType.DMA],
        ),
        out_shape=jax.ShapeDtypeStruct((N_EXPERTS * cap, D_MODEL), F32),
        compiler_params=_cparams("arbitrary"),
        name="moe_dispatch",
    )(counts, slots, h)


def _experts_kernel(tbl_ref, hs_ref, wg_ref, wu_ref, wd_ref, ys_ref, hb_ref, acc_ref):
    s = pl.program_id(0)
    j = pl.program_id(1)

    @pl.when(tbl_ref[2, s] == 1)
    def _():
        @pl.when(j == 0)
        def _():
            hb_ref[...] = hs_ref[...].astype(BF16)
            acc_ref[...] = jnp.zeros_like(acc_ref)

        h = hb_ref[...]
        act = (_silu(jnp.dot(h, wg_ref[0], preferred_element_type=F32))
               * jnp.dot(h, wu_ref[0], preferred_element_type=F32))
        acc_ref[...] += jnp.dot(act.astype(BF16), wd_ref[0], preferred_element_type=F32)

        @pl.when(j == pl.num_programs(1) - 1)
        def _():
            ys_ref[...] = acc_ref[...]


def _block_table(counts, cap, bm, n_steps):
    nblk = (counts + bm - 1) // bm
    cum = jnp.cumsum(nblk)
    total = cum[-1]
    step = jnp.arange(n_steps, dtype=jnp.int32)
    last = jnp.maximum(total - 1, 0)
    eff = jnp.minimum(step, last)
    expert = jnp.minimum(jnp.searchsorted(cum, eff, side='right'), N_EXPERTS - 1).astype(jnp.int32)
    blk = eff - (cum[expert] - nblk[expert])
    return jnp.stack([expert * (cap // bm) + blk, expert, (step < total).astype(jnp.int32)]).astype(jnp.int32)


def _experts(counts, hs, wg, wu, wd, cap, bm, n_assign):
    n_steps = n_assign // bm + N_EXPERTS
    tbl = _block_table(counts, cap, bm, n_steps)
    f = wg.shape[2]
    tf = f // FFN_CHUNKS
    chunk = lambda s, j, t: jnp.where(t[2, s] == 1, j, FFN_CHUNKS - 1)
    rows = pl.BlockSpec((bm, D_MODEL), lambda s, j, t: (t[0, s], 0))
    return pl.pallas_call(
        _experts_kernel,
        grid_spec=pltpu.PrefetchScalarGridSpec(
            num_scalar_prefetch=1,
            grid=(n_steps, FFN_CHUNKS),
            in_specs=[rows,
                      pl.BlockSpec((1, D_MODEL, tf), lambda s, j, t: (t[1, s], 0, chunk(s, j, t))),
                      pl.BlockSpec((1, D_MODEL, tf), lambda s, j, t: (t[1, s], 0, chunk(s, j, t))),
                      pl.BlockSpec((1, tf, D_MODEL), lambda s, j, t: (t[1, s], chunk(s, j, t), 0))],
            out_specs=rows,
            scratch_shapes=[pltpu.VMEM((bm, D_MODEL), BF16), pltpu.VMEM((bm, D_MODEL), F32)],
        ),
        out_shape=jax.ShapeDtypeStruct(hs.shape, F32),
        compiler_params=_cparams("arbitrary", "arbitrary"),
        name="moe_experts",
    )(tbl, hs, wg, wu, wd)


def _combine_kernel(slots_hbm, ys_hbm, xn_ref, gate_ref, gf_ref, o_ref, slot_smem, ybuf_ref, slot_sem, row_sem,
                    *, tm):
    i = pl.program_id(0)
    n_tiles = pl.num_programs(0)

    def gather(tile, buf):
        def issue(r, carry):
            for k in range(2):
                _row_copy(ys_hbm, slot_smem[buf, k, r], ybuf_ref.at[buf, k], r, row_sem.at[buf]).start()
            return carry
        lax.fori_loop(0, tm, issue, 0, unroll=8)

    @pl.when(i == 0)
    def _():
        first = _slot_fetch(slots_hbm, 0, slot_smem, 0, slot_sem)
        first.start()
        first.wait()
        gather(0, 0)

        @pl.when(n_tiles > 1)
        def _():
            _slot_fetch(slots_hbm, 1, slot_smem, 1, slot_sem).start()

    @pl.when(i + 1 < n_tiles)
    def _():
        _slot_fetch(slots_hbm, i + 1, slot_smem, (i + 1) % 2, slot_sem).wait()
        gather(i + 1, (i + 1) % 2)

    def drain(r, carry):
        for k in range(2):
            _row_copy(ys_hbm, 0, ybuf_ref.at[i % 2, k], 0, row_sem.at[i % 2]).wait()
        return carry

    lax.fori_loop(0, tm, drain, 0, unroll=8)

    @pl.when(i + 2 < n_tiles)
    def _():
        _slot_fetch(slots_hbm, i + 2, slot_smem, i % 2, slot_sem).start()

    g = gate_ref[...]
    y = g[:, 0:1] * ybuf_ref[i % 2, 0] + g[:, 1:2] * ybuf_ref[i % 2, 1]
    o_ref[...] = _rms(xn_ref[...] + y, gf_ref[...])


def _combine(slots, ys, xn, gates, g_final):
    n = xn.shape[0]
    n_tiles, _, tm = slots.shape
    rows = lambda c: pl.BlockSpec((tm, c), lambda i: (i, 0))
    return pl.pallas_call(
        functools.partial(_combine_kernel, tm=tm),
        grid=(n_tiles,),
        in_specs=[pl.BlockSpec(memory_space=pl.ANY), pl.BlockSpec(memory_space=pl.ANY),
                  rows(D_MODEL), rows(LANES), pl.BlockSpec(g_final.shape, lambda i: (0, 0))],
        out_specs=rows(D_MODEL),
        out_shape=jax.ShapeDtypeStruct((n, D_MODEL), F32),
        scratch_shapes=[pltpu.SMEM((2, SUBLANES, tm), jnp.int32), pltpu.VMEM((2, 2, tm, D_MODEL), F32),
                        pltpu.SemaphoreType.DMA((2,)), pltpu.SemaphoreType.DMA((2,))],
        compiler_params=_cparams("arbitrary"),
        name="moe_combine",
    )(slots, ys, xn, gates, g_final)


def _moe(x2d, ya, yb, yc, wo, g_ffn, w_router, b_router, wg, wu, wd, g_final):
    n = x2d.shape[0]
    bm = _row_tile(n, 512)
    cap = n
    xn, h, gates, slots, counts = _out_router(x2d, ya, yb, yc, wo, g_ffn, w_router, b_router, cap)
    counts = counts[0, :N_EXPERTS]
    hs = _dispatch(counts, slots, h, cap, bm)
    ys = _experts(counts, hs, wg, wu, wd, cap, bm, 2 * n)
    return _combine(slots, ys, xn, gates, g_final)


IN_SIZES = (2 * CONV_CH, SSM_INNER, SSM_XBC, SSM_HEADS, GDN_QKV, GDN_INNER, GDN_HEADS, GDN_HEADS)


def _prep_layer(l, p):
    off = np.concatenate([[0], np.cumsum(IN_SIZES)])
    w_in = p['w_in_bf16'][l]
    col = lambda i: w_in[:, off[i]:off[i + 1]]
    a_in, z, xbc, dt, qkv, gate, b_raw, a_raw = (col(i) for i in range(8))
    small = jnp.concatenate([dt, b_raw, a_raw, jnp.zeros((D_MODEL, SMALL_W - 3 * SSM_HEADS), BF16)], axis=1)
    wo = p['w_out'][l].astype(BF16)
    return dict(
        wa=a_in,
        wzx=jnp.concatenate([z, xbc], axis=1),
        wqg=jnp.concatenate([qkv, gate], axis=1),
        ws=small,
        wo=(wo[:CONV_CH], wo[CONV_CH:CONV_CH + SSM_INNER], wo[CONV_CH + SSM_INNER:]),
    )


def _trunk(x, st_conv_a, st_ssm_conv, st_ssm, st_gdn_conv, st_gdn, p, prep, ssd_rows, gdn_rows, bb):
    bsz, t, _ = x.shape
    n = bsz * t
    depth = p['g_mix'].shape[0]
    x2d = x.reshape(n, D_MODEL)
    new = [[] for _ in range(5)]
    for l in range(depth):
        w = prep[l]
        a_in, zx, qg, small = _norm_proj(x2d, p['g_mix'][l][None, :], w['wa'], w['wzx'], w['wqg'], w['ws'])
        a_in = a_in.reshape(bsz, t, -1)
        zx = zx.reshape(bsz, t, -1)
        qg = qg.reshape(bsz, t, -1)
        small = small.reshape(bsz, t, -1)
        ya, conv_a = _conv_a(a_in, st_conv_a[l], p['conv_a_w'][l], p['conv_a_b'][l][None, :],
                             p['ln_a_g'][l][None, :], p['ln_a_b'][l][None, :])
        yb, ssm = _ssd(zx, small, st_ssm_conv[l], _ssd_state_in(st_ssm[l]), p['ssm_conv_w'][l], p['ssm_conv_b'][l],
                       p['ssm_dt_bias'][l], p['ssm_a_log'][l], p['ssm_d'][l], p['ssm_norm_g'][l], ssd_rows, bb)
        yc, gdn = _gdn(qg, small, st_gdn_conv[l], _gdn_state_in(st_gdn[l]), p['gdn_conv_w'][l],
                       p['gdn_a_log'][l], p['gdn_dt_bias'][l], p['gdn_norm_g'][l], gdn_rows[0], gdn_rows[1], bb)
        assert t >= max(SSM_CONV_W, GDN_CONV_W) - 1
        new[0].append(conv_a)
        new[1].append(zx[:, t - (SSM_CONV_W - 1):, SSM_INNER:])
        new[2].append(_ssd_state_out(ssm))
        new[3].append(qg[:, t - (GDN_CONV_W - 1):, :GDN_QKV])
        new[4].append(_gdn_state_out(gdn))
        flat = lambda y: y.reshape(n, -1)
        g_ffn = p['g_ffn'][l][None, :]
        if l % 2 == 0:
            x2d = _out_ffn(x2d, flat(ya), flat(yb), flat(yc), w['wo'], g_ffn,
                           prep['ffn'][l // 2][0], prep['ffn'][l // 2][1], prep['ffn'][l // 2][2])
        else:
            x2d = _moe(x2d, flat(ya), flat(yb), flat(yc), w['wo'], g_ffn, p['moe_w_router'][l // 2],
                       p['moe_b_router'][l // 2], *prep['moe'][l // 2], p['g_final'][None, :])
    return (x2d.reshape(bsz, t, D_MODEL),) + tuple(jnp.stack(s) for s in new)


def kernel(x_prompt, x_sample, state_conv_a, state_ssm_conv, state_ssm, state_gdn_conv, state_gdn, g_mix, w_in, conv_a_w, conv_a_b, ln_a_g, ln_a_b, ssm_conv_w, ssm_conv_b, ssm_dt_bias, ssm_a_log, ssm_d, ssm_norm_g, gdn_conv_w, gdn_a_log, gdn_dt_bias, gdn_norm_g, w_out, g_ffn, ffn_w_gate, ffn_w_up, ffn_w_down, moe_w_router, moe_b_router, moe_w_gate, moe_w_up, moe_w_down, g_final):
    p = dict(g_mix=g_mix, w_in=w_in, conv_a_w=conv_a_w, conv_a_b=conv_a_b, ln_a_g=ln_a_g, ln_a_b=ln_a_b,
             ssm_conv_w=ssm_conv_w, ssm_conv_b=ssm_conv_b, ssm_dt_bias=ssm_dt_bias, ssm_a_log=ssm_a_log,
             ssm_d=ssm_d, ssm_norm_g=ssm_norm_g, gdn_conv_w=gdn_conv_w, gdn_a_log=gdn_a_log,
             gdn_dt_bias=gdn_dt_bias, gdn_norm_g=gdn_norm_g, w_out=w_out, g_ffn=g_ffn,
             moe_w_router=moe_w_router, moe_b_router=moe_b_router, g_final=g_final)
    depth = g_mix.shape[0]
    assert depth % 2 == 0, "the final RMSNorm is fused into the expert layer, which must come last"
    p['w_in_bf16'] = w_in.astype(BF16)
    prep = {l: _prep_layer(l, p) for l in range(depth)}
    prep['ffn'] = [(ffn_w_gate[i].astype(BF16), ffn_w_up[i].astype(BF16), ffn_w_down[i].astype(BF16))
                   for i in range(ffn_w_gate.shape[0])]
    prep['moe'] = [(moe_w_gate[i].astype(BF16), moe_w_up[i].astype(BF16), moe_w_down[i].astype(BF16))
                   for i in range(moe_w_gate.shape[0])]
    bp, dt = x_prompt.shape[0], x_prompt.dtype
    zeros = lambda *s: jnp.zeros((depth, bp) + s, dt)
    outs_p = _trunk(x_prompt, zeros(CONV_W - 1, CONV_CH), zeros(SSM_CONV_W - 1, SSM_XBC),
                    zeros(SSM_HEADS, HEAD_DIM, SSM_STATE), zeros(GDN_CONV_W - 1, GDN_QKV),
                    zeros(GDN_HEADS, HEAD_DIM, HEAD_DIM), p, prep, ssd_rows=min(128, x_prompt.shape[1]),
                    gdn_rows=(min(256, x_prompt.shape[1]), min(64, x_prompt.shape[1])), bb=2)
    outs_s = _trunk(x_sample, state_conv_a, state_ssm_conv, state_ssm, state_gdn_conv, state_gdn, p, prep,
                    ssd_rows=x_sample.shape[1], gdn_rows=(x_sample.shape[1], x_sample.shape[1]), bb=2)
    return (outs_p[0], outs_s[0]) + outs_p[1:] + outs_s[1:]
```

```python
import functools
import math

import jax
import jax.numpy as jnp
import numpy as np
from jax import lax
from jax.experimental import pallas as pl
from jax.experimental.pallas import tpu as pltpu

F32 = jnp.float32
BF16 = jnp.bfloat16
EPS = 1e-6

LANES = 128
SUBLANES = 8
VMEM_BYTES_V7X = 64 * 1024 * 1024
VMEM_LIMIT = VMEM_BYTES_V7X * 3 // 4

D_MODEL = 1024
CONV_CH = 256
CONV_W = 31
SSM_HEADS = 6
HEAD_DIM = 64
SSM_INNER = SSM_HEADS * HEAD_DIM
SSM_STATE = 64
SSM_GROUPS = 2
SSM_XBC = SSM_INNER + 2 * SSM_GROUPS * SSM_STATE
SSM_CONV_W = 4
GDN_HEADS = 6
GDN_INNER = GDN_HEADS * HEAD_DIM
GDN_QKV = 3 * GDN_INNER
GDN_CONV_W = 4
N_PAIRS = 3
N_EXPERTS = 8
SMALL_W = LANES


def _cparams(*sem):
    return pltpu.CompilerParams(dimension_semantics=sem, vmem_limit_bytes=VMEM_LIMIT)


def _dot(a, b):
    return jnp.dot(a.astype(BF16), b.astype(BF16), preferred_element_type=F32)


def _dot_nt(a, b):
    return lax.dot_general(a.astype(BF16), b.astype(BF16), (((1,), (1,)), ((), ())),
                           preferred_element_type=F32)


def _split3(x):
    hi = x.astype(BF16)
    r1 = x - hi.astype(F32)
    mid = r1.astype(BF16)
    lo = (r1 - mid.astype(F32)).astype(BF16)
    return hi, mid, lo


def _dot_sel(x, sel_bf16, pieces=2):
    d = lambda p: jnp.dot(p, sel_bf16, preferred_element_type=F32)
    hi = x.astype(BF16)
    r1 = x - hi.astype(F32)
    mid = r1.astype(BF16)
    if pieces == 2:
        return d(hi) + d(mid)
    return d(hi) + d(mid) + d((r1 - mid.astype(F32)).astype(BF16))


def _sel_dot(sel_bf16, x):
    hi, mid, lo = _split3(x)
    d = lambda p: jnp.dot(sel_bf16, p, preferred_element_type=F32)
    return d(hi) + d(mid) + d(lo)


def _sel_dot_nt(sel_bf16, x):
    hi, mid, lo = _split3(x)
    d = lambda p: lax.dot_general(sel_bf16, p, (((1,), (1,)), ((), ())), preferred_element_type=F32)
    return d(hi) + d(mid) + d(lo)


def _dot_hp(a, b):
    ah = a.astype(BF16)
    al = (a - ah.astype(F32)).astype(BF16)
    bh = b.astype(BF16)
    bl = (b - bh.astype(F32)).astype(BF16)
    d = lambda p, q: jnp.dot(p, q, preferred_element_type=F32)
    return d(ah, bh) + (d(ah, bl) + d(al, bh))


def _silu(x):
    return x * (1.0 / (1.0 + jnp.exp(-x)))


def _sigmoid(x):
    return 1.0 / (1.0 + jnp.exp(-x))


def _softplus(x):
    return jnp.maximum(x, 0.0) + jnp.log(1.0 + jnp.exp(-jnp.abs(x)))


def _iota(shape, dim):
    return lax.broadcasted_iota(jnp.int32, shape, dim)


def _norm_proj_kernel(x_ref, g_ref, wa_ref, wzx_ref, wqg_ref, ws_ref, a_ref, zx_ref, qg_ref, s_ref):
    x = x_ref[...]
    u = x * lax.rsqrt(jnp.mean(x * x, axis=-1, keepdims=True) + EPS) * g_ref[...]
    ub = u.astype(BF16)
    a_ref[...] = jnp.dot(ub, wa_ref[...], preferred_element_type=F32)
    zx_ref[...] = jnp.dot(ub, wzx_ref[...], preferred_element_type=F32)
    qg_ref[...] = jnp.dot(ub, wqg_ref[...], preferred_element_type=F32)
    s_ref[...] = jnp.dot(ub, ws_ref[...], preferred_element_type=F32)


def _row_tile(n, want):
    t = min(want, n)
    while n % t:
        t //= 2
    return t


def _norm_proj(x2d, g, wa, wzx, wqg, ws):
    n = x2d.shape[0]
    tm = _row_tile(n, 512)
    full = lambda w: pl.BlockSpec(w.shape, lambda i: (0, 0))
    rows = lambda c: pl.BlockSpec((tm, c), lambda i: (i, 0))
    widths = (wa.shape[1], wzx.shape[1], wqg.shape[1], ws.shape[1])
    return pl.pallas_call(
        _norm_proj_kernel,
        grid=(n // tm,),
        in_specs=[rows(D_MODEL), full(g), full(wa), full(wzx), full(wqg), full(ws)],
        out_specs=[rows(c) for c in widths],
        out_shape=[jax.ShapeDtypeStruct((n, c), F32) for c in widths],
        compiler_params=_cparams("parallel"),
        name="norm_proj",
    )(x2d, g, wa, wzx, wqg, ws)


CONV_PAD = 32
CONV_ROWS = 32


def _conv_a_kernel(a_ref, hist_ref, w_ref, b_ref, lg_ref, lb_ref, y_ref, nh_ref, buf_ref, sh_ref, *, tt):
    t = pl.program_id(1)

    @pl.when(t == 0)
    def _():
        buf_ref[0:CONV_PAD, :] = hist_ref[0]

    a = a_ref[0]
    glu = a[:, :CONV_CH] * _sigmoid(a[:, CONV_CH:])
    buf_ref[CONV_PAD:CONV_PAD + tt, :] = glu
    full = buf_ref[...]
    n = tt + CONV_PAD
    sh_ref[0] = full
    for s in range(1, SUBLANES):
        sh_ref[s] = pltpu.roll(full, n - s, axis=0)
    off = CONV_PAD - (CONV_W - 1)
    for r0 in range(0, tt, CONV_ROWS):
        acc = jnp.zeros((CONV_ROWS, CONV_CH), F32)
        for k in range(CONV_W):
            s = (off + k) % SUBLANES
            base = r0 + off + k - s
            acc = acc + w_ref[k:k + 1, :] * sh_ref[s, base:base + CONV_ROWS, :]
        y = acc + b_ref[...]
        mu = jnp.mean(y, axis=-1, keepdims=True)
        yc = y - mu
        var = jnp.mean(yc * yc, axis=-1, keepdims=True)
        y = yc * lax.rsqrt(var + EPS) * lg_ref[...] + lb_ref[...]
        y_ref[0, r0:r0 + CONV_ROWS, :] = _silu(y).astype(y_ref.dtype)
    nh_ref[0] = buf_ref[tt:tt + CONV_PAD, :]
    buf_ref[0:CONV_PAD, :] = buf_ref[tt:tt + CONV_PAD, :]


def _conv_a(a_in, hist, w, b, lg, lb):
    bsz, t, _ = a_in.shape
    tt = _row_tile(t, 256)
    hist_p = jnp.pad(hist, ((0, 0), (CONV_PAD - (CONV_W - 1), 0), (0, 0)))
    vec = lambda v: pl.BlockSpec(v.shape, lambda i, j: (0, 0))
    y, nh = pl.pallas_call(
        functools.partial(_conv_a_kernel, tt=tt),
        grid=(bsz, t // tt),
        in_specs=[pl.BlockSpec((1, tt, 2 * CONV_CH), lambda i, j: (i, j, 0)),
                  pl.BlockSpec((1, CONV_PAD, CONV_CH), lambda i, j: (i, 0, 0)),
                  vec(w), vec(b), vec(lg), vec(lb)],
        out_specs=[pl.BlockSpec((1, tt, CONV_CH), lambda i, j: (i, j, 0)),
                   pl.BlockSpec((1, CONV_PAD, CONV_CH), lambda i, j: (i, 0, 0))],
        out_shape=[jax.ShapeDtypeStruct((bsz, t, CONV_CH), BF16),
                   jax.ShapeDtypeStruct((bsz, CONV_PAD, CONV_CH), F32)],
        scratch_shapes=[pltpu.VMEM((tt + CONV_PAD, CONV_CH), F32),
                        pltpu.VMEM((SUBLANES, tt + CONV_PAD, CONV_CH), F32)],
        compiler_params=_cparams("arbitrary", "arbitrary"),
        name="conv_a",
    )(a_in, hist_p, w, b, lg, lb)
    return y, nh[:, CONV_PAD - (CONV_W - 1):, :]


HIST_PAD = 8


def _short_conv(x, buf_ref, b, w_ref, width, rows):
    buf_ref[b, HIST_PAD:HIST_PAD + rows, :] = x
    off = HIST_PAD - (width - 1)
    acc = w_ref[0:1, :] * buf_ref[b, off:off + rows, :]
    for k in range(1, width):
        acc = acc + w_ref[k:k + 1, :] * buf_ref[b, off + k:off + k + rows, :]
    buf_ref[b, 0:HIST_PAD, :] = buf_ref[b, rows:rows + HIST_PAD, :]
    return acc


def _tri(rows, lower):
    r = _iota((rows, rows), 0)
    c = _iota((rows, rows), 1)
    return (r >= c) if lower else (r <= c)


def _row_select(first_lane):
    r = _iota((SUBLANES, LANES), 0)
    c = _iota((SUBLANES, LANES), 1)
    return ((c == r + first_lane) & (r < SSM_HEADS)).astype(BF16)


def _cumsum_both(a, first_lane, rows):
    tril = _tri(rows, True).astype(BF16)
    triu = _tri(rows, False).astype(BF16)
    col = _sel_dot(tril, a)
    a_t = _sel_dot_nt(_row_select(first_lane), a)
    row = _dot_sel(a_t, triu, pieces=3)
    return col, row


def _pair_mask():
    return _iota((1, LANES), 1) < HEAD_DIM


def _ssd_kernel(zx_ref, s_ref, hist_ref, h0_ref, cw_ref, cb_ref, dtb_ref, arow_ref, dx_ref, ng_ref,
                ex_ref, gm_ref, y_ref, hout_ref, buf_ref, h_ref, *, rows, bb):
    c = pl.program_id(1)

    @pl.when(c == 0)
    def _():
        buf_ref[:, 0:HIST_PAD, :] = hist_ref[...]
        h_ref[...] = h0_ref[...]

    causal = _tri(rows, True)
    lane = _iota((1, LANES), 1)
    first_half = _pair_mask()
    srow_g = _iota((LANES, LANES), 0) // SSM_STATE
    scol_h = _iota((LANES, LANES), 1) // HEAD_DIM
    ex = ex_ref[...]

    for b in range(bb):
        zx = zx_ref[b]
        z = zx[:, :SSM_INNER]
        conv = _short_conv(zx[:, SSM_INNER:], buf_ref, b, cw_ref, SSM_CONV_W, rows)
        xbc = _silu(conv + cb_ref[...])
        xs = xbc[:, :SSM_INNER]
        bm = xbc[:, SSM_INNER:SSM_INNER + LANES]
        cm = xbc[:, SSM_INNER + LANES:]

        dt = _softplus(s_ref[b] + dtb_ref[...])
        a = dt * arow_ref[...]
        acum, acum_row = _cumsum_both(a, 0, rows)
        a_last = acum[rows - 1:rows, :]
        dt_x = _dot_sel(dt, ex)
        ea_x = _dot_sel(jnp.exp(acum), ex)
        te_x = _dot_sel(jnp.exp(a_last - acum), ex)
        cd_x = _dot_sel(jnp.broadcast_to(jnp.exp(a_last), (SUBLANES, LANES)), ex)[0:1, :]

        scores = []
        for g in range(SSM_GROUPS):
            cm_g = jnp.where(lane // SSM_STATE == g, cm, 0.0)
            scores.append(_dot_nt(cm_g, bm))
        bm_t = jnp.transpose(bm)

        ys = []
        for p in range(N_PAIRS):
            sl = slice(p * LANES, (p + 1) * LANES)
            x_p = xs[:, sl]
            xdt = x_p * dt_x[:, sl]
            yd = []
            for hh in range(2):
                h = 2 * p + hh
                diff = jnp.broadcast_to(acum[:, h:h + 1], (rows, rows)) - acum_row[h:h + 1, :]
                dec = jnp.exp(jnp.where(causal, diff, -1e30))
                yd.append(_dot(scores[h // (SSM_HEADS // SSM_GROUPS)] * dec, xdt))
            y_diag = jnp.where(first_half, yd[0], yd[1])
            h_p = h_ref[b, p]
            y_off = _dot(cm, h_p) * ea_x[:, sl]
            keep = srow_g == (2 * p + scol_h) // (SSM_HEADS // SSM_GROUPS)
            upd = _dot(bm_t, xdt * te_x[:, sl])
            h_ref[b, p] = h_p * cd_x[:, sl] + jnp.where(keep, upd, 0.0)
            ys.append(y_diag + y_off + dx_ref[:, sl] * x_p)
        y = jnp.concatenate(ys, axis=-1) * _silu(z)
        ms = _dot_sel(y * y, gm_ref[...]) * (1.0 / (SSM_INNER // SSM_GROUPS))
        y = y * lax.rsqrt(ms + EPS) * ng_ref[...]
        y_ref[b] = y.astype(y_ref.dtype)

    @pl.when(c == pl.num_programs(1) - 1)
    def _():
        hout_ref[...] = h_ref[...]


def _expand_matrix(first_lane):
    m = np.zeros((LANES, SSM_INNER), np.float32)
    for h in range(SSM_HEADS):
        m[first_lane + h, h * HEAD_DIM:(h + 1) * HEAD_DIM] = 1.0
    return jnp.asarray(m, BF16)


def _group_matrix(width):
    idx = np.arange(SSM_INNER) // width
    return jnp.asarray((idx[:, None] == idx[None, :]).astype(np.float32), BF16)


def _lane_row(vals, first_lane):
    return jnp.zeros((1, LANES), F32).at[0, first_lane:first_lane + vals.shape[0]].set(vals.astype(F32))


def _ssd(zx, small, hist, h0, cw, cb, dt_bias, a_log, d_skip, norm_g, rows, bb):
    bsz, t, _ = zx.shape
    hist_p = jnp.pad(hist, ((0, 0), (HIST_PAD - (SSM_CONV_W - 1), 0), (0, 0)))
    dtb = _lane_row(dt_bias, 0)
    arow = _lane_row(-jnp.exp(a_log.astype(F32)), 0)
    dx = jnp.repeat(d_skip.astype(F32), HEAD_DIM)[None, :]
    full = lambda v: pl.BlockSpec(v.shape, lambda i, j: (0,) * v.ndim)
    blk = lambda c: pl.BlockSpec((bb, rows, c), lambda i, j: (i, j, 0))
    st = pl.BlockSpec((bb, N_PAIRS, LANES, LANES), lambda i, j: (i, 0, 0, 0))
    consts = (cw, cb[None, :], dtb, arow, dx, norm_g[None, :], _expand_matrix(0),
              _group_matrix(SSM_INNER // SSM_GROUPS))
    return pl.pallas_call(
        functools.partial(_ssd_kernel, rows=rows, bb=bb),
        grid=(bsz // bb, t // rows),
        in_specs=[blk(zx.shape[-1]), blk(LANES),
                  pl.BlockSpec((bb, HIST_PAD, SSM_XBC), lambda i, j: (i, 0, 0)), st]
                 + [full(v) for v in consts],
        out_specs=[blk(SSM_INNER), st],
        out_shape=[jax.ShapeDtypeStruct((bsz, t, SSM_INNER), BF16),
                   jax.ShapeDtypeStruct((bsz, N_PAIRS, LANES, LANES), F32)],
        scratch_shapes=[pltpu.VMEM((bb, rows + HIST_PAD, SSM_XBC), F32),
                        pltpu.VMEM((bb, N_PAIRS, LANES, LANES), F32)],
        compiler_params=_cparams("arbitrary", "arbitrary"),
        name="ssd",
    )(zx, small, hist_p, h0, *consts)


def _ssd_state_in(h):
    bsz = h.shape[0]
    out = jnp.zeros((bsz, N_PAIRS, SSM_GROUPS, SSM_STATE, 2, HEAD_DIM), F32)
    for hd in range(SSM_HEADS):
        g = hd // (SSM_HEADS // SSM_GROUPS)
        out = out.at[:, hd // 2, g, :, hd % 2, :].set(jnp.swapaxes(h[:, hd], 1, 2))
    return out.reshape(bsz, N_PAIRS, LANES, LANES)


def _ssd_state_out(hp):
    bsz = hp.shape[0]
    hp = hp.reshape(bsz, N_PAIRS, SSM_GROUPS, SSM_STATE, 2, HEAD_DIM)
    heads = [jnp.swapaxes(hp[:, hd // 2, hd // (SSM_HEADS // SSM_GROUPS), :, hd % 2, :], 1, 2)
             for hd in range(SSM_HEADS)]
    return jnp.stack(heads, axis=1)


BETA_LANE = 6
DECAY_LANE = 12


def _bdot(a, b):
    return lax.dot_general(a.astype(BF16), b.astype(BF16), (((2,), (1,)), ((0,), (0,))),
                           preferred_element_type=F32)


def _bdot_nt(a, b):
    return lax.dot_general(a.astype(BF16), b.astype(BF16), (((2,), (2,)), ((0,), (0,))),
                           preferred_element_type=F32)


def _unit_lower_inverse(m, block):
    rdim = m.shape[-1]
    eye = (_iota((rdim, rdim), 0) == _iota((rdim, rdim), 1)).astype(F32)
    x = -m
    t = eye + x
    p = x
    for _ in range(int(math.log2(block)) - 2):
        p = _bdot(p, p)
        t = t + _bdot(t, p)
    resid = (eye - t) - _bdot(m, t)
    return t + _bdot(t, resid)


def _gdn_kernel(qg_ref, s_ref, hist_ref, s0_ref, cw_ref, bias_ref, arow_ref, ng_ref, eb_ref, eg_ref,
                y_ref, sout_ref, buf_ref, st_ref, *, tt, ck, bb):
    c = pl.program_id(1)
    nck = tt // ck
    rdim = 2 * ck

    @pl.when(c == 0)
    def _():
        buf_ref[:, 0:HIST_PAD, :] = hist_ref[...]
        st_ref[...] = s0_ref[...]

    r_i = _iota((rdim, rdim), 0)
    c_i = _iota((rdim, rdim), 1)
    same_blk = (r_i // ck) == (c_i // ck)
    incl = same_blk & (r_i >= c_i)
    strict = same_blk & (r_i > c_i)
    first_half = _pair_mask()
    same_head = (_iota((LANES, LANES), 0) // HEAD_DIM) == (_iota((LANES, LANES), 1) // HEAD_DIM)
    tr = _iota((tt, tt), 0)
    tc = _iota((tt, tt), 1)
    blk_tril = (((tr // ck) == (tc // ck)) & (tr >= tc)).astype(BF16)
    eb = eb_ref[...]
    eg = eg_ref[...]
    hm = same_head.astype(BF16)
    head_sumsq = lambda x: jnp.concatenate(
        [_dot_sel(jnp.square(x[:, p * LANES:(p + 1) * LANES]), hm) for p in range(N_PAIRS)], axis=-1)

    tiles = {name: [] for name in ('q', 'k', 'kb', 'rhs', 'qd', 'kd', 'col', 'cd')}
    gates = []
    for b in range(bb):
        qg = qg_ref[b]
        gates.append(qg[:, GDN_QKV:])
        qkv = _silu(_short_conv(qg[:, :GDN_QKV], buf_ref, b, cw_ref, GDN_CONV_W, tt))
        q = qkv[:, :GDN_INNER]
        k = qkv[:, GDN_INNER:2 * GDN_INNER]
        v = qkv[:, 2 * GDN_INNER:]
        q = q * lax.rsqrt(head_sumsq(q) + EPS) * (HEAD_DIM ** -0.5)
        k = k * lax.rsqrt(head_sumsq(k) + EPS)
        s = s_ref[b]
        beta = _sigmoid(s)
        g = _softplus(s + bias_ref[...]) * arow_ref[...]
        gc = _sel_dot(blk_tril, g)
        g_last = jnp.concatenate(
            [jnp.broadcast_to(gc[(i + 1) * ck - 1:(i + 1) * ck, :], (ck, LANES)) for i in range(nck)], axis=0)
        beta_x = _dot_sel(beta, eb)
        eg_x = _dot_sel(jnp.exp(gc), eg)
        kd_x = _dot_sel(jnp.exp(g_last - gc), eg)
        cd_x = _dot_sel(jnp.exp(g_last), eg)
        kb = k * beta_x
        full = dict(q=q, k=k, kb=kb, qd=q * eg_x, kd=k * kd_x, cd=cd_x)
        vb = v * beta_x
        kbe = kb * eg_x
        for i in range(nck):
            rs = slice(i * ck, (i + 1) * ck)
            for p in range(N_PAIRS):
                sl = slice(p * LANES, (p + 1) * LANES)
                for name, arr in full.items():
                    tiles[name].append(arr[rs, sl])
                tiles['rhs'].append(jnp.concatenate([vb[rs, sl], kbe[rs, sl]], axis=-1))
                lane0 = DECAY_LANE + 2 * p
                tiles['col'].append(jnp.concatenate(
                    [jnp.broadcast_to(gc[rs, lane0 + hh:lane0 + hh + 1], (ck, rdim)) for hh in range(2)], axis=0))

    st = lambda name: jnp.stack(tiles[name], axis=0)
    stack2 = lambda x: jnp.concatenate([jnp.where(first_half, x, 0.0), jnp.where(first_half, 0.0, x)], axis=1)
    k_st = stack2(st('k'))
    col = st('col')
    diff = col - jnp.swapaxes(col, 1, 2)
    dec = jnp.exp(jnp.where(incl, diff, -1e30))
    m = _bdot_nt(stack2(st('kb')), k_st) * jnp.where(strict, dec, 0.0)
    t_inv = _unit_lower_inverse(m, ck)
    rhs = st('rhs')
    sol = _bdot(t_inv, jnp.concatenate([rhs, rhs], axis=1))
    u = jnp.where(first_half, sol[:, :ck, :LANES], sol[:, ck:, :LANES])
    w = jnp.where(first_half, sol[:, :ck, LANES:], sol[:, ck:, LANES:])
    attn = _bdot_nt(stack2(st('q')), k_st) * dec
    qd = st('qd')
    kd_t = jnp.swapaxes(st('kd'), 1, 2)
    cd = st('cd')

    gsel = lambda x, i: jnp.stack([x[(b * nck + i) * N_PAIRS + p] for b in range(bb) for p in range(N_PAIRS)], axis=0)
    state = st_ref[...].reshape(bb * N_PAIRS, LANES, LANES)
    o_chunks = []
    for i in range(nck):
        v_new = gsel(u, i) - _bdot(gsel(w, i), state)
        intra = _bdot(gsel(attn, i), jnp.concatenate([v_new, v_new], axis=1))
        o_chunks.append(_bdot(gsel(qd, i), state) + jnp.where(first_half, intra[:, :ck], intra[:, ck:]))
        upd = _bdot(gsel(kd_t, i), v_new)
        state = state * gsel(cd, i)[:, 0:1, :] + jnp.where(same_head, upd, 0.0)
    st_ref[...] = state.reshape(bb, N_PAIRS, LANES, LANES)

    for b in range(bb):
        o = jnp.concatenate(
            [jnp.concatenate([o_chunks[i][b * N_PAIRS + p] for p in range(N_PAIRS)], axis=-1) for i in range(nck)],
            axis=0)
        ms = head_sumsq(o) * (1.0 / HEAD_DIM)
        o = o * lax.rsqrt(ms + EPS) * ng_ref[...] * _silu(gates[b])
        y_ref[b] = o.astype(y_ref.dtype)

    @pl.when(c == pl.num_programs(1) - 1)
    def _():
        sout_ref[...] = st_ref[...]


def _gdn(qg, small, hist, s0, cw, a_log, dt_bias, norm_g, tt, ck, bb):
    bsz, t, _ = qg.shape
    hist_p = jnp.pad(hist, ((0, 0), (HIST_PAD - (GDN_CONV_W - 1), 0), (0, 0)))
    bias = _lane_row(dt_bias, DECAY_LANE)
    arow = _lane_row(-jnp.exp(a_log.astype(F32)), DECAY_LANE)
    ng = jnp.tile(norm_g.astype(F32), GDN_HEADS)[None, :]
    full = lambda v: pl.BlockSpec(v.shape, lambda i, j: (0,) * v.ndim)
    blk = lambda c: pl.BlockSpec((bb, tt, c), lambda i, j: (i, j, 0))
    st = pl.BlockSpec((bb, N_PAIRS, LANES, LANES), lambda i, j: (i, 0, 0, 0))
    consts = (cw, bias, arow, ng, _expand_matrix(BETA_LANE), _expand_matrix(DECAY_LANE))
    return pl.pallas_call(
        functools.partial(_gdn_kernel, tt=tt, ck=ck, bb=bb),
        grid=(bsz // bb, t // tt),
        in_specs=[blk(qg.shape[-1]), blk(LANES),
                  pl.BlockSpec((bb, HIST_PAD, GDN_QKV), lambda i, j: (i, 0, 0)), st]
                 + [full(v) for v in consts],
        out_specs=[blk(GDN_INNER), st],
        out_shape=[jax.ShapeDtypeStruct((bsz, t, GDN_INNER), BF16),
                   jax.ShapeDtypeStruct((bsz, N_PAIRS, LANES, LANES), F32)],
        scratch_shapes=[pltpu.VMEM((bb, tt + HIST_PAD, GDN_QKV), F32),
                        pltpu.VMEM((bb, N_PAIRS, LANES, LANES), F32)],
        compiler_params=_cparams("arbitrary", "arbitrary"),
        name="gdn",
    )(qg, small, hist_p, s0, *consts)


def _gdn_state_in(s):
    bsz = s.shape[0]
    out = jnp.zeros((bsz, N_PAIRS, 2, HEAD_DIM, 2, HEAD_DIM), F32)
    for hd in range(GDN_HEADS):
        out = out.at[:, hd // 2, hd % 2, :, hd % 2, :].set(s[:, hd])
    return out.reshape(bsz, N_PAIRS, LANES, LANES)


def _gdn_state_out(sp):
    bsz = sp.shape[0]
    sp = sp.reshape(bsz, N_PAIRS, 2, HEAD_DIM, 2, HEAD_DIM)
    return jnp.stack([sp[:, hd // 2, hd % 2, :, hd % 2, :] for hd in range(GDN_HEADS)], axis=1)


def _mix_residual(x_ref, ya_ref, yb_ref, yc_ref, woa_ref, wob_ref, woc_ref):
    mix = (jnp.dot(ya_ref[...], woa_ref[...], preferred_element_type=F32)
           + jnp.dot(yb_ref[...], wob_ref[...], preferred_element_type=F32)
           + jnp.dot(yc_ref[...], woc_ref[...], preferred_element_type=F32))
    return x_ref[...] + mix


def _rms(x, g):
    return x * lax.rsqrt(jnp.mean(x * x, axis=-1, keepdims=True) + EPS) * g


def _ffn_kernel(x_ref, ya_ref, yb_ref, yc_ref, woa_ref, wob_ref, woc_ref, g_ref, wg_ref, wu_ref, wd_ref,
                o_ref, xn_ref, h_ref, acc_ref):
    j = pl.program_id(1)

    @pl.when(j == 0)
    def _():
        xn = _mix_residual(x_ref, ya_ref, yb_ref, yc_ref, woa_ref, wob_ref, woc_ref)
        xn_ref[...] = xn
        h_ref[...] = _rms(xn, g_ref[...]).astype(BF16)
        acc_ref[...] = jnp.zeros_like(acc_ref)

    h = h_ref[...]
    act = _silu(jnp.dot(h, wg_ref[...], preferred_element_type=F32)) * jnp.dot(h, wu_ref[...], preferred_element_type=F32)
    acc_ref[...] += jnp.dot(act.astype(BF16), wd_ref[...], preferred_element_type=F32)

    @pl.when(j == pl.num_programs(1) - 1)
    def _():
        o_ref[...] = xn_ref[...] + acc_ref[...]


FFN_CHUNKS = 2


def _out_ffn(x2d, ya, yb, yc, wo, g, wg, wu, wd):
    n = x2d.shape[0]
    tm = _row_tile(n, 512)
    f = wg.shape[1]
    tf = f // FFN_CHUNKS
    rows = lambda c: pl.BlockSpec((tm, c), lambda i, j: (i, 0))
    full = lambda w: pl.BlockSpec(w.shape, lambda i, j: (0, 0))
    return pl.pallas_call(
        _ffn_kernel,
        grid=(n // tm, FFN_CHUNKS),
        in_specs=[rows(D_MODEL), rows(ya.shape[1]), rows(yb.shape[1]), rows(yc.shape[1]),
                  full(wo[0]), full(wo[1]), full(wo[2]), full(g),
                  pl.BlockSpec((D_MODEL, tf), lambda i, j: (0, j)),
                  pl.BlockSpec((D_MODEL, tf), lambda i, j: (0, j)),
                  pl.BlockSpec((tf, D_MODEL), lambda i, j: (j, 0))],
        out_specs=rows(D_MODEL),
        out_shape=jax.ShapeDtypeStruct((n, D_MODEL), F32),
        scratch_shapes=[pltpu.VMEM((tm, D_MODEL), F32), pltpu.VMEM((tm, D_MODEL), BF16),
                        pltpu.VMEM((tm, D_MODEL), F32)],
        compiler_params=_cparams("parallel", "arbitrary"),
        name="out_ffn",
    )(x2d, ya, yb, yc, wo[0], wo[1], wo[2], g, wg, wu, wd)


ROUTE_TILE = 512


def _router_kernel(x_ref, ya_ref, yb_ref, yc_ref, woa_ref, wob_ref, woc_ref, g_ref, wr_ref, br_ref,
                   xn_ref, h_ref, gate_ref, slot_ref, cnt_ref, run_ref, *, cap):
    i = pl.program_id(0)

    @pl.when(i == 0)
    def _():
        run_ref[...] = jnp.zeros_like(run_ref)

    xn = _mix_residual(x_ref, ya_ref, yb_ref, yc_ref, woa_ref, wob_ref, woc_ref)
    xn_ref[...] = xn
    h = _rms(xn, g_ref[...])
    h_ref[...] = h
    tm = h.shape[0]
    lane = _iota((1, LANES), 1)
    logits = jnp.where(lane < N_EXPERTS, _dot_hp(h, wr_ref[...]) + br_ref[...], -jnp.inf)
    m1 = jnp.max(logits, axis=-1, keepdims=True)
    i1 = jnp.min(jnp.where(logits == m1, lane, LANES), axis=-1, keepdims=True)
    rest = jnp.where(lane == i1, -jnp.inf, logits)
    m2 = jnp.max(rest, axis=-1, keepdims=True)
    i2 = jnp.min(jnp.where(rest == m2, lane, LANES), axis=-1, keepdims=True)
    e2 = jnp.exp(m2 - m1)
    inv = 1.0 / (1.0 + e2)
    gate_ref[...] = jnp.where(lane == 0, inv, 0.0) + jnp.where(lane == 1, e2 * inv, 0.0)

    chosen = jnp.where((lane == i1) | (lane == i2), 1.0, 0.0)
    before = _iota((tm, tm), 0) > _iota((tm, tm), 1)
    rank = jnp.dot(before.astype(BF16), chosen.astype(BF16), preferred_element_type=F32) + run_ref[0:1, :]
    base = lane.astype(F32) * float(cap)
    pick = lambda idx: jnp.sum(jnp.where(lane == idx, rank + base, 0.0), axis=-1, keepdims=True)
    slots = jnp.where(lane == 0, pick(i1), 0.0) + jnp.where(lane == 1, pick(i2), 0.0)
    slot_ref[0] = jnp.transpose(slots)[0:SUBLANES, :].astype(jnp.int32)
    run_ref[...] = run_ref[...] + jnp.sum(chosen, axis=0, keepdims=True)
    cnt_ref[...] = run_ref[...].astype(jnp.int32)


def _out_router(x2d, ya, yb, yc, wo, g, w_router, b_router, cap):
    n = x2d.shape[0]
    tm = _row_tile(n, ROUTE_TILE)
    wr = jnp.zeros((D_MODEL, LANES), F32).at[:, :N_EXPERTS].set(w_router.astype(F32))
    br = _lane_row(b_router, 0)
    rows = lambda c: pl.BlockSpec((tm, c), lambda i: (i, 0))
    full = lambda w: pl.BlockSpec(w.shape, lambda i: (0, 0))
    return pl.pallas_call(
        functools.partial(_router_kernel, cap=cap),
        grid=(n // tm,),
        in_specs=[rows(D_MODEL), rows(ya.shape[1]), rows(yb.shape[1]), rows(yc.shape[1]),
                  full(wo[0]), full(wo[1]), full(wo[2]), full(g), full(wr), full(br)],
        out_specs=[rows(D_MODEL), rows(D_MODEL), rows(LANES),
                   pl.BlockSpec((1, SUBLANES, tm), lambda i: (i, 0, 0)),
                   pl.BlockSpec((SUBLANES, LANES), lambda i: (0, 0))],
        out_shape=[jax.ShapeDtypeStruct((n, D_MODEL), F32), jax.ShapeDtypeStruct((n, D_MODEL), F32),
                   jax.ShapeDtypeStruct((n, LANES), F32),
                   jax.ShapeDtypeStruct((n // tm, SUBLANES, tm), jnp.int32),
                   jax.ShapeDtypeStruct((SUBLANES, LANES), jnp.int32)],
        scratch_shapes=[pltpu.VMEM((SUBLANES, LANES), F32)],
        compiler_params=_cparams("arbitrary"),
        name="out_router",
    )(x2d, ya, yb, yc, wo[0], wo[1], wo[2], g, wr, br)


def _row_copy(src_ref, src_row, dst_ref, dst_row, sem):
    return pltpu.make_async_copy(src_ref.at[pl.ds(src_row, 1)], dst_ref.at[pl.ds(dst_row, 1)], sem)


def _slot_fetch(slots_hbm, tile, smem_ref, buf, sem):
    return pltpu.make_async_copy(slots_hbm.at[tile], smem_ref.at[buf], sem.at[buf])


def _for_rows(tm, fn):
    def body(o, carry):
        base = pl.multiple_of(o * SUBLANES, SUBLANES)
        for j in range(SUBLANES):
            fn(base, j)
        return carry
    lax.fori_loop(0, tm // SUBLANES, body, 0, unroll=2)


def _vmem_row(ref, base, j):
    return ref.at[pl.ds(base, SUBLANES)].at[pl.ds(j, 1)]


def _dispatch_kernel(cnt_ref, slots_hbm, h_ref, hs_hbm, slot_smem, zero_ref, slot_sem, row_sem, pad_sem,
                     *, tm, cap, bm):
    i = pl.program_id(0)
    n_tiles = pl.num_programs(0)

    @pl.when(i == 0)
    def _():
        _slot_fetch(slots_hbm, 0, slot_smem, 0, slot_sem).start()

    @pl.when(i + 1 < n_tiles)
    def _():
        _slot_fetch(slots_hbm, i + 1, slot_smem, (i + 1) % 2, slot_sem).start()

    _slot_fetch(slots_hbm, i, slot_smem, i % 2, slot_sem).wait()

    def issue(base, j):
        for k in range(2):
            dst = hs_hbm.at[pl.ds(slot_smem[i % 2, k, base + j], 1)]
            pltpu.make_async_copy(_vmem_row(h_ref, base, j), dst, row_sem).start()

    _for_rows(tm, issue)

    def drain(r, carry):
        for k in range(2):
            _row_copy(h_ref, 0, hs_hbm, 0, row_sem).wait()
        return carry

    lax.fori_loop(0, tm, drain, 0, unroll=8)

    @pl.when(i == n_tiles - 1)
    def _():
        zero_ref[...] = jnp.zeros_like(zero_ref)
        for e in range(N_EXPERTS):
            c = cnt_ref[e]
            n_pad = ((c + bm - 1) // bm) * bm - c

            def fill(r, carry):
                _row_copy(zero_ref, 0, hs_hbm, e * cap + c + r, pad_sem).start()
                return carry

            def fill_wait(r, carry):
                _row_copy(zero_ref, 0, hs_hbm, 0, pad_sem).wait()
                return carry

            lax.fori_loop(0, n_pad, fill, 0)
            lax.fori_loop(0, n_pad, fill_wait, 0)


def _dispatch(counts, slots, h, cap, bm):
    n = h.shape[0]
    n_tiles, _, tm = slots.shape
    return pl.pallas_call(
        functools.partial(_dispatch_kernel, tm=tm, cap=cap, bm=bm),
        grid_spec=pltpu.PrefetchScalarGridSpec(
            num_scalar_prefetch=1,
            grid=(n_tiles,),
            in_specs=[pl.BlockSpec(memory_space=pl.ANY), pl.BlockSpec((tm, D_MODEL), lambda i, cnt: (i, 0))],
            out_specs=pl.BlockSpec(memory_space=pl.ANY),
            scratch_shapes=[pltpu.SMEM((2, SUBLANES, tm), jnp.int32), pltpu.VMEM((SUBLANES, D_MODEL), F32),
                            pltpu.SemaphoreType.DMA((2,)), pltpu.SemaphoreType.DMA, pltpu.SemaphoreType.DMA],
        ),
        out_shape=jax.ShapeDtypeStruct((N_EXPERTS * cap, D_MODEL), F32),
        compiler_params=_cparams("arbitrary"),
        name="moe_dispatch",
    )(counts, slots, h)


def _experts_kernel(tbl_ref, hs_ref, wg_ref, wu_ref, wd_ref, ys_ref, hb_ref, acc_ref):
    s = pl.program_id(0)
    j = pl.program_id(1)

    @pl.when(tbl_ref[2, s] == 1)
    def _():
        @pl.when(j == 0)
        def _():
            hb_ref[...] = hs_ref[...].astype(BF16)
            acc_ref[...] = jnp.zeros_like(acc_ref)

        h = hb_ref[...]
        act = (_silu(jnp.dot(h, wg_ref[0], preferred_element_type=F32))
               * jnp.dot(h, wu_ref[0], preferred_element_type=F32))
        acc_ref[...] += jnp.dot(act.astype(BF16), wd_ref[0], preferred_element_type=F32)

        @pl.when(j == pl.num_programs(1) - 1)
        def _():
            ys_ref[...] = acc_ref[...]


def _block_table(counts, cap, bm, n_steps):
    nblk = (counts + bm - 1) // bm
    cum = jnp.cumsum(nblk)
    total = cum[-1]
    step = jnp.arange(n_steps, dtype=jnp.int32)
    last = jnp.maximum(total - 1, 0)
    eff = jnp.minimum(step, last)
    expert = jnp.minimum(jnp.searchsorted(cum, eff, side='right'), N_EXPERTS - 1).astype(jnp.int32)
    blk = eff - (cum[expert] - nblk[expert])
    return jnp.stack([expert * (cap // bm) + blk, expert, (step < total).astype(jnp.int32)]).astype(jnp.int32)


def _experts(counts, hs, wg, wu, wd, cap, bm, n_assign):
    n_steps = n_assign // bm + N_EXPERTS
    tbl = _block_table(counts, cap, bm, n_steps)
    f = wg.shape[2]
    tf = f // FFN_CHUNKS
    chunk = lambda s, j, t: jnp.where(t[2, s] == 1, j, FFN_CHUNKS - 1)
    rows = pl.BlockSpec((bm, D_MODEL), lambda s, j, t: (t[0, s], 0))
    return pl.pallas_call(
        _experts_kernel,
        grid_spec=pltpu.PrefetchScalarGridSpec(
            num_scalar_prefetch=1,
            grid=(n_steps, FFN_CHUNKS),
            in_specs=[rows,
                      pl.BlockSpec((1, D_MODEL, tf), lambda s, j, t: (t[1, s], 0, chunk(s, j, t))),
                      pl.BlockSpec((1, D_MODEL, tf), lambda s, j, t: (t[1, s], 0, chunk(s, j, t))),
                      pl.BlockSpec((1, tf, D_MODEL), lambda s, j, t: (t[1, s], chunk(s, j, t), 0))],
            out_specs=rows,
            scratch_shapes=[pltpu.VMEM((bm, D_MODEL), BF16), pltpu.VMEM((bm, D_MODEL), F32)],
        ),
        out_shape=jax.ShapeDtypeStruct(hs.shape, F32),
        compiler_params=_cparams("arbitrary", "arbitrary"),
        name="moe_experts",
    )(tbl, hs, wg, wu, wd)


def _combine_kernel(slots_hbm, ys_hbm, xn_ref, gate_ref, gf_ref, o_ref, slot_smem, ybuf_ref, slot_sem, row_sem,
                    *, tm):
    i = pl.program_id(0)
    n_tiles = pl.num_programs(0)

    def gather(tile, buf):
        def issue(base, j):
            for k in range(2):
                src = ys_hbm.at[pl.ds(slot_smem[buf, k, base + j], 1)]
                pltpu.make_async_copy(src, _vmem_row(ybuf_ref.at[buf, k], base, j), row_sem.at[buf]).start()
        _for_rows(tm, issue)

    @pl.when(i == 0)
    def _():
        first = _slot_fetch(slots_hbm, 0, slot_smem, 0, slot_sem)
        first.start()
        first.wait()
        gather(0, 0)

        @pl.when(n_tiles > 1)
        def _():
            _slot_fetch(slots_hbm, 1, slot_smem, 1, slot_sem).start()

    @pl.when(i + 1 < n_tiles)
    def _():
        _slot_fetch(slots_hbm, i + 1, slot_smem, (i + 1) % 2, slot_sem).wait()
        gather(i + 1, (i + 1) % 2)

    def drain(r, carry):
        for k in range(2):
            _row_copy(ys_hbm, 0, ybuf_ref.at[i % 2, k], 0, row_sem.at[i % 2]).wait()
        return carry

    lax.fori_loop(0, tm, drain, 0, unroll=8)

    @pl.when(i + 2 < n_tiles)
    def _():
        _slot_fetch(slots_hbm, i + 2, slot_smem, i % 2, slot_sem).start()

    g = gate_ref[...]
    y = g[:, 0:1] * ybuf_ref[i % 2, 0] + g[:, 1:2] * ybuf_ref[i % 2, 1]
    o_ref[...] = _rms(xn_ref[...] + y, gf_ref[...])


def _combine(slots, ys, xn, gates, g_final):
    n = xn.shape[0]
    n_tiles, _, tm = slots.shape
    rows = lambda c: pl.BlockSpec((tm, c), lambda i: (i, 0))
    return pl.pallas_call(
        functools.partial(_combine_kernel, tm=tm),
        grid=(n_tiles,),
        in_specs=[pl.BlockSpec(memory_space=pl.ANY), pl.BlockSpec(memory_space=pl.ANY),
                  rows(D_MODEL), rows(LANES), pl.BlockSpec(g_final.shape, lambda i: (0, 0))],
        out_specs=rows(D_MODEL),
        out_shape=jax.ShapeDtypeStruct((n, D_MODEL), F32),
        scratch_shapes=[pltpu.SMEM((2, SUBLANES, tm), jnp.int32), pltpu.VMEM((2, 2, tm, D_MODEL), F32),
                        pltpu.SemaphoreType.DMA((2,)), pltpu.SemaphoreType.DMA((2,))],
        compiler_params=_cparams("arbitrary"),
        name="moe_combine",
    )(slots, ys, xn, gates, g_final)


def _moe(x2d, ya, yb, yc, wo, g_ffn, w_router, b_router, wg, wu, wd, g_final):
    n = x2d.shape[0]
    bm = _row_tile(n, 512)
    cap = n
    xn, h, gates, slots, counts = _out_router(x2d, ya, yb, yc, wo, g_ffn, w_router, b_router, cap)
    counts = counts[0, :N_EXPERTS]
    hs = _dispatch(counts, slots, h, cap, bm)
    ys = _experts(counts, hs, wg, wu, wd, cap, bm, 2 * n)
    return _combine(slots, ys, xn, gates, g_final)


IN_SIZES = (2 * CONV_CH, SSM_INNER, SSM_XBC, SSM_HEADS, GDN_QKV, GDN_INNER, GDN_HEADS, GDN_HEADS)


def _prep_layer(l, p):
    off = np.concatenate([[0], np.cumsum(IN_SIZES)])
    w_in = p['w_in_bf16'][l]
    col = lambda i: w_in[:, off[i]:off[i + 1]]
    a_in, z, xbc, dt, qkv, gate, b_raw, a_raw = (col(i) for i in range(8))
    small = jnp.concatenate([dt, b_raw, a_raw, jnp.zeros((D_MODEL, SMALL_W - 3 * SSM_HEADS), BF16)], axis=1)
    wo = p['w_out'][l].astype(BF16)
    return dict(
        wa=a_in,
        wzx=jnp.concatenate([z, xbc], axis=1),
        wqg=jnp.concatenate([qkv, gate], axis=1),
        ws=small,
        wo=(wo[:CONV_CH], wo[CONV_CH:CONV_CH + SSM_INNER], wo[CONV_CH + SSM_INNER:]),
    )


def _trunk(x, st_conv_a, st_ssm_conv, st_ssm, st_gdn_conv, st_gdn, p, prep, ssd_rows, gdn_rows, bb):
    bsz, t, _ = x.shape
    n = bsz * t
    depth = p['g_mix'].shape[0]
    x2d = x.reshape(n, D_MODEL)
    new = [[] for _ in range(5)]
    for l in range(depth):
        w = prep[l]
        a_in, zx, qg, small = _norm_proj(x2d, p['g_mix'][l][None, :], w['wa'], w['wzx'], w['wqg'], w['ws'])
        a_in = a_in.reshape(bsz, t, -1)
        zx = zx.reshape(bsz, t, -1)
        qg = qg.reshape(bsz, t, -1)
        small = small.reshape(bsz, t, -1)
        ya, conv_a = _conv_a(a_in, st_conv_a[l], p['conv_a_w'][l], p['conv_a_b'][l][None, :],
                             p['ln_a_g'][l][None, :], p['ln_a_b'][l][None, :])
        yb, ssm = _ssd(zx, small, st_ssm_conv[l], _ssd_state_in(st_ssm[l]), p['ssm_conv_w'][l], p['ssm_conv_b'][l],
                       p['ssm_dt_bias'][l], p['ssm_a_log'][l], p['ssm_d'][l], p['ssm_norm_g'][l], ssd_rows, bb)
        yc, gdn = _gdn(qg, small, st_gdn_conv[l], _gdn_state_in(st_gdn[l]), p['gdn_conv_w'][l],
                       p['gdn_a_log'][l], p['gdn_dt_bias'][l], p['gdn_norm_g'][l], gdn_rows[0], gdn_rows[1], bb)
        assert t >= max(SSM_CONV_W, GDN_CONV_W) - 1
        new[0].append(conv_a)
        new[1].append(zx[:, t - (SSM_CONV_W - 1):, SSM_INNER:])
        new[2].append(_ssd_state_out(ssm))
        new[3].append(qg[:, t - (GDN_CONV_W - 1):, :GDN_QKV])
        new[4].append(_gdn_state_out(gdn))
        flat = lambda y: y.reshape(n, -1)
        g_ffn = p['g_ffn'][l][None, :]
        if l % 2 == 0:
            x2d = _out_ffn(x2d, flat(ya), flat(yb), flat(yc), w['wo'], g_ffn,
                           prep['ffn'][l // 2][0], prep['ffn'][l // 2][1], prep['ffn'][l // 2][2])
        else:
            x2d = _moe(x2d, flat(ya), flat(yb), flat(yc), w['wo'], g_ffn, p['moe_w_router'][l // 2],
                       p['moe_b_router'][l // 2], *prep['moe'][l // 2], p['g_final'][None, :])
    return (x2d.reshape(bsz, t, D_MODEL),) + tuple(jnp.stack(s) for s in new)


def kernel(x_prompt, x_sample, state_conv_a, state_ssm_conv, state_ssm, state_gdn_conv, state_gdn, g_mix, w_in, conv_a_w, conv_a_b, ln_a_g, ln_a_b, ssm_conv_w, ssm_conv_b, ssm_dt_bias, ssm_a_log, ssm_d, ssm_norm_g, gdn_conv_w, gdn_a_log, gdn_dt_bias, gdn_norm_g, w_out, g_ffn, ffn_w_gate, ffn_w_up, ffn_w_down, moe_w_router, moe_b_router, moe_w_gate, moe_w_up, moe_w_down, g_final):
    p = dict(g_mix=g_mix, w_in=w_in, conv_a_w=conv_a_w, conv_a_b=conv_a_b, ln_a_g=ln_a_g, ln_a_b=ln_a_b,
             ssm_conv_w=ssm_conv_w, ssm_conv_b=ssm_conv_b, ssm_dt_bias=ssm_dt_bias, ssm_a_log=ssm_a_log,
             ssm_d=ssm_d, ssm_norm_g=ssm_norm_g, gdn_conv_w=gdn_conv_w, gdn_a_log=gdn_a_log,
             gdn_dt_bias=gdn_dt_bias, gdn_norm_g=gdn_norm_g, w_out=w_out, g_ffn=g_ffn,
             moe_w_router=moe_w_router, moe_b_router=moe_b_router, g_final=g_final)
    depth = g_mix.shape[0]
    assert depth % 2 == 0, "the final RMSNorm is fused into the expert layer, which must come last"
    p['w_in_bf16'] = w_in.astype(BF16)
    prep = {l: _prep_layer(l, p) for l in range(depth)}
    prep['ffn'] = [(ffn_w_gate[i].astype(BF16), ffn_w_up[i].astype(BF16), ffn_w_down[i].astype(BF16))
                   for i in range(ffn_w_gate.shape[0])]
    prep['moe'] = [(moe_w_gate[i].astype(BF16), moe_w_up[i].astype(BF16), moe_w_down[i].astype(BF16))
                   for i in range(moe_w_gate.shape[0])]
    bp, dt = x_prompt.shape[0], x_prompt.dtype
    zeros = lambda *s: jnp.zeros((depth, bp) + s, dt)
    outs_p = _trunk(x_prompt, zeros(CONV_W - 1, CONV_CH), zeros(SSM_CONV_W - 1, SSM_XBC),
                    zeros(SSM_HEADS, HEAD_DIM, SSM_STATE), zeros(GDN_CONV_W - 1, GDN_QKV),
                    zeros(GDN_HEADS, HEAD_DIM, HEAD_DIM), p, prep, ssd_rows=min(128, x_prompt.shape[1]),
                    gdn_rows=(min(256, x_prompt.shape[1]), min(64, x_prompt.shape[1])), bb=2)
    outs_s = _trunk(x_sample, state_conv_a, state_ssm_conv, state_ssm, state_gdn_conv, state_gdn, p, prep,
                    ssd_rows=x_sample.shape[1], gdn_rows=(x_sample.shape[1], x_sample.shape[1]), bb=2)
    return (outs_p[0], outs_s[0]) + outs_p[1:] + outs_s[1:]
```

```python
import functools
import math

import jax
import jax.numpy as jnp
import numpy as np
from jax import lax
from jax.experimental import pallas as pl
from jax.experimental.pallas import tpu as pltpu

F32 = jnp.float32
BF16 = jnp.bfloat16
EPS = 1e-6

LANES = 128
SUBLANES = 8
VMEM_BYTES_V7X = 64 * 1024 * 1024
VMEM_LIMIT = VMEM_BYTES_V7X * 3 // 4

D_MODEL = 1024
CONV_CH = 256
CONV_W = 31
SSM_HEADS = 6
HEAD_DIM = 64
SSM_INNER = SSM_HEADS * HEAD_DIM
SSM_STATE = 64
SSM_GROUPS = 2
SSM_XBC = SSM_INNER + 2 * SSM_GROUPS * SSM_STATE
SSM_CONV_W = 4
GDN_HEADS = 6
GDN_INNER = GDN_HEADS * HEAD_DIM
GDN_QKV = 3 * GDN_INNER
GDN_CONV_W = 4
N_PAIRS = 3
N_EXPERTS = 8
SMALL_W = LANES


def _cparams(*sem):
    return pltpu.CompilerParams(dimension_semantics=sem, vmem_limit_bytes=VMEM_LIMIT)


def _dot(a, b):
    return jnp.dot(a.astype(BF16), b.astype(BF16), preferred_element_type=F32)


def _dot_nt(a, b):
    return lax.dot_general(a.astype(BF16), b.astype(BF16), (((1,), (1,)), ((), ())),
                           preferred_element_type=F32)


def _split3(x):
    hi = x.astype(BF16)
    r1 = x - hi.astype(F32)
    mid = r1.astype(BF16)
    lo = (r1 - mid.astype(F32)).astype(BF16)
    return hi, mid, lo


def _dot_sel(x, sel_bf16, pieces=2):
    d = lambda p: jnp.dot(p, sel_bf16, preferred_element_type=F32)
    hi = x.astype(BF16)
    r1 = x - hi.astype(F32)
    mid = r1.astype(BF16)
    if pieces == 2:
        return d(hi) + d(mid)
    return d(hi) + d(mid) + d((r1 - mid.astype(F32)).astype(BF16))


def _sel_dot(sel_bf16, x):
    hi, mid, lo = _split3(x)
    d = lambda p: jnp.dot(sel_bf16, p, preferred_element_type=F32)
    return d(hi) + d(mid) + d(lo)


def _dot_hp(a, b):
    ah = a.astype(BF16)
    al = (a - ah.astype(F32)).astype(BF16)
    bh = b.astype(BF16)
    bl = (b - bh.astype(F32)).astype(BF16)
    d = lambda p, q: jnp.dot(p, q, preferred_element_type=F32)
    return d(ah, bh) + (d(ah, bl) + d(al, bh))


def _silu(x):
    return x * (1.0 / (1.0 + jnp.exp(-x)))


def _sigmoid(x):
    return 1.0 / (1.0 + jnp.exp(-x))


def _softplus(x):
    return jnp.maximum(x, 0.0) + jnp.log(1.0 + jnp.exp(-jnp.abs(x)))


def _iota(shape, dim):
    return lax.broadcasted_iota(jnp.int32, shape, dim)


def _norm_proj_kernel(x_ref, g_ref, wa_ref, wzx_ref, wqg_ref, ws_ref, a_ref, zx_ref, qg_ref, s_ref):
    x = x_ref[...]
    u = x * lax.rsqrt(jnp.mean(x * x, axis=-1, keepdims=True) + EPS) * g_ref[...]
    ub = u.astype(BF16)
    a_ref[...] = jnp.dot(ub, wa_ref[...], preferred_element_type=F32)
    zx_ref[...] = jnp.dot(ub, wzx_ref[...], preferred_element_type=F32)
    qg_ref[...] = jnp.dot(ub, wqg_ref[...], preferred_element_type=F32)
    s_ref[...] = jnp.dot(ub, ws_ref[...], preferred_element_type=F32)


def _row_tile(n, want):
    t = min(want, n)
    while n % t:
        t //= 2
    return t


def _norm_proj(x2d, g, wa, wzx, wqg, ws):
    n = x2d.shape[0]
    tm = _row_tile(n, 512)
    full = lambda w: pl.BlockSpec(w.shape, lambda i: (0, 0))
    rows = lambda c: pl.BlockSpec((tm, c), lambda i: (i, 0))
    widths = (wa.shape[1], wzx.shape[1], wqg.shape[1], ws.shape[1])
    return pl.pallas_call(
        _norm_proj_kernel,
        grid=(n // tm,),
        in_specs=[rows(D_MODEL), full(g), full(wa), full(wzx), full(wqg), full(ws)],
        out_specs=[rows(c) for c in widths],
        out_shape=[jax.ShapeDtypeStruct((n, c), F32) for c in widths],
        compiler_params=_cparams("parallel"),
        name="norm_proj",
    )(x2d, g, wa, wzx, wqg, ws)


CONV_PAD = 32
CONV_ROWS = 32


def _conv_a_kernel(a_ref, hist_ref, w_ref, b_ref, lg_ref, lb_ref, y_ref, nh_ref, buf_ref, sh_ref, *, tt):
    t = pl.program_id(1)

    @pl.when(t == 0)
    def _():
        buf_ref[0:CONV_PAD, :] = hist_ref[0]

    a = a_ref[0]
    glu = a[:, :CONV_CH] * _sigmoid(a[:, CONV_CH:])
    buf_ref[CONV_PAD:CONV_PAD + tt, :] = glu
    full = buf_ref[...]
    n = tt + CONV_PAD
    sh_ref[0] = full
    for s in range(1, SUBLANES):
        sh_ref[s] = pltpu.roll(full, n - s, axis=0)
    off = CONV_PAD - (CONV_W - 1)
    for r0 in range(0, tt, CONV_ROWS):
        acc = jnp.zeros((CONV_ROWS, CONV_CH), F32)
        for k in range(CONV_W):
            s = (off + k) % SUBLANES
            base = r0 + off + k - s
            acc = acc + w_ref[k:k + 1, :] * sh_ref[s, base:base + CONV_ROWS, :]
        y = acc + b_ref[...]
        mu = jnp.mean(y, axis=-1, keepdims=True)
        yc = y - mu
        var = jnp.mean(yc * yc, axis=-1, keepdims=True)
        y = yc * lax.rsqrt(var + EPS) * lg_ref[...] + lb_ref[...]
        y_ref[0, r0:r0 + CONV_ROWS, :] = _silu(y).astype(y_ref.dtype)
    nh_ref[0] = buf_ref[tt:tt + CONV_PAD, :]
    buf_ref[0:CONV_PAD, :] = buf_ref[tt:tt + CONV_PAD, :]


def _conv_a(a_in, hist, w, b, lg, lb):
    bsz, t, _ = a_in.shape
    tt = _row_tile(t, 256)
    hist_p = jnp.pad(hist, ((0, 0), (CONV_PAD - (CONV_W - 1), 0), (0, 0)))
    vec = lambda v: pl.BlockSpec(v.shape, lambda i, j: (0, 0))
    y, nh = pl.pallas_call(
        functools.partial(_conv_a_kernel, tt=tt),
        grid=(bsz, t // tt),
        in_specs=[pl.BlockSpec((1, tt, 2 * CONV_CH), lambda i, j: (i, j, 0)),
                  pl.BlockSpec((1, CONV_PAD, CONV_CH), lambda i, j: (i, 0, 0)),
                  vec(w), vec(b), vec(lg), vec(lb)],
        out_specs=[pl.BlockSpec((1, tt, CONV_CH), lambda i, j: (i, j, 0)),
                   pl.BlockSpec((1, CONV_PAD, CONV_CH), lambda i, j: (i, 0, 0))],
        out_shape=[jax.ShapeDtypeStruct((bsz, t, CONV_CH), BF16),
                   jax.ShapeDtypeStruct((bsz, CONV_PAD, CONV_CH), F32)],
        scratch_shapes=[pltpu.VMEM((tt + CONV_PAD, CONV_CH), F32),
                        pltpu.VMEM((SUBLANES, tt + CONV_PAD, CONV_CH), F32)],
        compiler_params=_cparams("arbitrary", "arbitrary"),
        name="conv_a",
    )(a_in, hist_p, w, b, lg, lb)
    return y, nh[:, CONV_PAD - (CONV_W - 1):, :]


HIST_PAD = 8


def _short_conv(x, buf_ref, b, w_ref, width, rows):
    buf_ref[b, HIST_PAD:HIST_PAD + rows, :] = x
    off = HIST_PAD - (width - 1)
    acc = w_ref[0:1, :] * buf_ref[b, off:off + rows, :]
    for k in range(1, width):
        acc = acc + w_ref[k:k + 1, :] * buf_ref[b, off + k:off + k + rows, :]
    buf_ref[b, 0:HIST_PAD, :] = buf_ref[b, rows:rows + HIST_PAD, :]
    return acc


def _lower_tri(rows):
    return _iota((rows, rows), 0) >= _iota((rows, rows), 1)


def _pair_mask():
    return _iota((1, LANES), 1) < HEAD_DIM


def _ssd_kernel(zx_ref, s_ref, hist_ref, h0_ref, cw_ref, cb_ref, dtb_ref, arow_ref, dx_ref, ng_ref,
                ex_ref, y_ref, hout_ref, buf_ref, h_ref, *, rows, bb):
    c = pl.program_id(1)

    @pl.when(c == 0)
    def _():
        buf_ref[:, 0:HIST_PAD, :] = hist_ref[...]
        h_ref[...] = h0_ref[...]

    causal = _lower_tri(rows)
    lane = _iota((1, LANES), 1)
    first_half = _pair_mask()
    srow_g = _iota((LANES, LANES), 0) // SSM_STATE
    scol_h = _iota((LANES, LANES), 1) // HEAD_DIM
    ex = ex_ref[...]
    tril = causal.astype(BF16)
    assert SSM_INNER // SSM_GROUPS == LANES + HEAD_DIM and N_PAIRS == 3
    top_rows = _iota((LANES, LANES), 0) < HEAD_DIM
    ones_all = jnp.ones((LANES, LANES), BF16)
    ones_top = top_rows.astype(BF16)
    ones_bot = jnp.logical_not(top_rows).astype(BF16)

    for b in range(bb):
        zx = zx_ref[b]
        z = zx[:, :SSM_INNER]
        conv = _short_conv(zx[:, SSM_INNER:], buf_ref, b, cw_ref, SSM_CONV_W, rows)
        xbc = _silu(conv + cb_ref[...])
        xs = xbc[:, :SSM_INNER]
        bm = xbc[:, SSM_INNER:SSM_INNER + LANES]
        cm = xbc[:, SSM_INNER + LANES:]

        dt = _softplus(s_ref[b] + dtb_ref[...])
        a = dt * arow_ref[...]
        acum = _sel_dot(tril, a)
        a_last = acum[rows - 1:rows, :]
        dt_x = _dot_sel(dt, ex)
        ea_x = _dot_sel(jnp.exp(acum), ex)
        te_x = _dot_sel(jnp.exp(a_last - acum), ex)
        cd_x = _dot_sel(jnp.broadcast_to(jnp.exp(a_last), (SUBLANES, LANES)), ex)[0:1, :]

        scores = []
        for g in range(SSM_GROUPS):
            cm_g = jnp.where(lane // SSM_STATE == g, cm, 0.0)
            scores.append(_dot_nt(cm_g, bm))
        bm_t = jnp.transpose(bm)

        ys = []
        for p in range(N_PAIRS):
            sl = slice(p * LANES, (p + 1) * LANES)
            x_p = xs[:, sl]
            xdt = x_p * dt_x[:, sl]
            yd = []
            for hh in range(2):
                h = 2 * p + hh
                col = jnp.broadcast_to(acum[:, h:h + 1], (rows, rows))
                dec = jnp.exp(jnp.where(causal, col - jnp.transpose(col), -1e30))
                yd.append(_dot(scores[h // (SSM_HEADS // SSM_GROUPS)] * dec, xdt))
            y_diag = jnp.where(first_half, yd[0], yd[1])
            h_p = h_ref[b, p]
            y_off = _dot(cm, h_p) * ea_x[:, sl]
            keep = srow_g == (2 * p + scol_h) // (SSM_HEADS // SSM_GROUPS)
            upd = _dot(bm_t, xdt * te_x[:, sl])
            h_ref[b, p] = h_p * cd_x[:, sl] + jnp.where(keep, upd, 0.0)
            ys.append(y_diag + y_off + dx_ref[:, sl] * x_p)
        y = jnp.concatenate(ys, axis=-1) * _silu(z)
        sq = [jnp.square(y[:, p * LANES:(p + 1) * LANES]) for p in range(N_PAIRS)]
        g0 = _dot_sel(sq[0], ones_all) + _dot_sel(sq[1], ones_top)
        g1 = _dot_sel(sq[1], ones_bot) + _dot_sel(sq[2], ones_all)
        ms = jnp.concatenate([g0, jnp.where(first_half, g0, g1), g1], axis=-1) * (1.0 / (SSM_INNER // SSM_GROUPS))
        y = y * lax.rsqrt(ms + EPS) * ng_ref[...]
        y_ref[b] = y.astype(y_ref.dtype)

    @pl.when(c == pl.num_programs(1) - 1)
    def _():
        hout_ref[...] = h_ref[...]


def _expand_matrix(first_lane):
    m = np.zeros((LANES, SSM_INNER), np.float32)
    for h in range(SSM_HEADS):
        m[first_lane + h, h * HEAD_DIM:(h + 1) * HEAD_DIM] = 1.0
    return jnp.asarray(m, BF16)


def _lane_row(vals, first_lane):
    return jnp.zeros((1, LANES), F32).at[0, first_lane:first_lane + vals.shape[0]].set(vals.astype(F32))


def _ssd(zx, small, hist, h0, cw, cb, dt_bias, a_log, d_skip, norm_g, rows, bb):
    bsz, t, _ = zx.shape
    hist_p = jnp.pad(hist, ((0, 0), (HIST_PAD - (SSM_CONV_W - 1), 0), (0, 0)))
    dtb = _lane_row(dt_bias, 0)
    arow = _lane_row(-jnp.exp(a_log.astype(F32)), 0)
    dx = jnp.repeat(d_skip.astype(F32), HEAD_DIM)[None, :]
    full = lambda v: pl.BlockSpec(v.shape, lambda i, j: (0,) * v.ndim)
    blk = lambda c: pl.BlockSpec((bb, rows, c), lambda i, j: (i, j, 0))
    st = pl.BlockSpec((bb, N_PAIRS, LANES, LANES), lambda i, j: (i, 0, 0, 0))
    consts = (cw, cb[None, :], dtb, arow, dx, norm_g[None, :], _expand_matrix(0))
    return pl.pallas_call(
        functools.partial(_ssd_kernel, rows=rows, bb=bb),
        grid=(bsz // bb, t // rows),
        in_specs=[blk(zx.shape[-1]), blk(LANES),
                  pl.BlockSpec((bb, HIST_PAD, SSM_XBC), lambda i, j: (i, 0, 0)), st]
                 + [full(v) for v in consts],
        out_specs=[blk(SSM_INNER), st],
        out_shape=[jax.ShapeDtypeStruct((bsz, t, SSM_INNER), BF16),
                   jax.ShapeDtypeStruct((bsz, N_PAIRS, LANES, LANES), F32)],
        scratch_shapes=[pltpu.VMEM((bb, rows + HIST_PAD, SSM_XBC), F32),
                        pltpu.VMEM((bb, N_PAIRS, LANES, LANES), F32)],
        compiler_params=_cparams("arbitrary", "arbitrary"),
        name="ssd",
    )(zx, small, hist_p, h0, *consts)


def _ssd_state_in(h):
    bsz = h.shape[0]
    out = jnp.zeros((bsz, N_PAIRS, SSM_GROUPS, SSM_STATE, 2, HEAD_DIM), F32)
    for hd in range(SSM_HEADS):
        g = hd // (SSM_HEADS // SSM_GROUPS)
        out = out.at[:, hd // 2, g, :, hd % 2, :].set(jnp.swapaxes(h[:, hd], 1, 2))
    return out.reshape(bsz, N_PAIRS, LANES, LANES)


def _ssd_state_out(hp):
    bsz = hp.shape[0]
    hp = hp.reshape(bsz, N_PAIRS, SSM_GROUPS, SSM_STATE, 2, HEAD_DIM)
    heads = [jnp.swapaxes(hp[:, hd // 2, hd // (SSM_HEADS // SSM_GROUPS), :, hd % 2, :], 1, 2)
             for hd in range(SSM_HEADS)]
    return jnp.stack(heads, axis=1)


BETA_LANE = 6
DECAY_LANE = 12


def _bdot(a, b):
    return lax.dot_general(a.astype(BF16), b.astype(BF16), (((2,), (1,)), ((0,), (0,))),
                           preferred_element_type=F32)


def _bdot_nt(a, b):
    return lax.dot_general(a.astype(BF16), b.astype(BF16), (((2,), (2,)), ((0,), (0,))),
                           preferred_element_type=F32)


def _unit_lower_inverse(m, block):
    rdim = m.shape[-1]
    eye = (_iota((rdim, rdim), 0) == _iota((rdim, rdim), 1)).astype(F32)
    x = -m
    t = eye + x
    p = x
    for _ in range(int(math.log2(block)) - 2):
        p = _bdot(p, p)
        t = t + _bdot(t, p)
    resid = (eye - t) - _bdot(m, t)
    return t + _bdot(t, resid)


def _gdn_kernel(qg_ref, s_ref, hist_ref, s0_ref, cw_ref, bias_ref, arow_ref, ng_ref, eb_ref, eg_ref,
                y_ref, sout_ref, buf_ref, st_ref, *, tt, ck, bb):
    c = pl.program_id(1)
    nck = tt // ck
    rdim = 2 * ck

    @pl.when(c == 0)
    def _():
        buf_ref[:, 0:HIST_PAD, :] = hist_ref[...]
        st_ref[...] = s0_ref[...]

    r_i = _iota((rdim, rdim), 0)
    c_i = _iota((rdim, rdim), 1)
    same_blk = (r_i // ck) == (c_i // ck)
    incl = same_blk & (r_i >= c_i)
    strict = same_blk & (r_i > c_i)
    first_half = _pair_mask()
    same_head = (_iota((LANES, LANES), 0) // HEAD_DIM) == (_iota((LANES, LANES), 1) // HEAD_DIM)
    tr = _iota((tt, tt), 0)
    tc = _iota((tt, tt), 1)
    blk_tril = (((tr // ck) == (tc // ck)) & (tr >= tc)).astype(BF16)
    eb = eb_ref[...]
    eg = eg_ref[...]
    hm = same_head.astype(BF16)
    head_sumsq = lambda x: jnp.concatenate(
        [_dot_sel(jnp.square(x[:, p * LANES:(p + 1) * LANES]), hm) for p in range(N_PAIRS)], axis=-1)

    tiles = {name: [] for name in ('q', 'k', 'kb', 'rhs', 'qd', 'kd', 'col', 'cd')}
    gates = []
    for b in range(bb):
        qg = qg_ref[b]
        gates.append(qg[:, GDN_QKV:])
        qkv = _silu(_short_conv(qg[:, :GDN_QKV], buf_ref, b, cw_ref, GDN_CONV_W, tt))
        q = qkv[:, :GDN_INNER]
        k = qkv[:, GDN_INNER:2 * GDN_INNER]
        v = qkv[:, 2 * GDN_INNER:]
        q = q * lax.rsqrt(head_sumsq(q) + EPS) * (HEAD_DIM ** -0.5)
        k = k * lax.rsqrt(head_sumsq(k) + EPS)
        s = s_ref[b]
        beta = _sigmoid(s)
        g = _softplus(s + bias_ref[...]) * arow_ref[...]
        gc = _sel_dot(blk_tril, g)
        g_last = jnp.concatenate(
            [jnp.broadcast_to(gc[(i + 1) * ck - 1:(i + 1) * ck, :], (ck, LANES)) for i in range(nck)], axis=0)
        beta_x = _dot_sel(beta, eb)
        eg_x = _dot_sel(jnp.exp(gc), eg)
        kd_x = _dot_sel(jnp.exp(g_last - gc), eg)
        cd_x = _dot_sel(jnp.exp(g_last), eg)
        kb = k * beta_x
        full = dict(q=q, k=k, kb=kb, qd=q * eg_x, kd=k * kd_x, cd=cd_x)
        vb = v * beta_x
        kbe = kb * eg_x
        for i in range(nck):
            rs = slice(i * ck, (i + 1) * ck)
            for p in range(N_PAIRS):
                sl = slice(p * LANES, (p + 1) * LANES)
                for name, arr in full.items():
                    tiles[name].append(arr[rs, sl])
                tiles['rhs'].append(jnp.concatenate([vb[rs, sl], kbe[rs, sl]], axis=-1))
                lane0 = DECAY_LANE + 2 * p
                tiles['col'].append(jnp.concatenate(
                    [jnp.broadcast_to(gc[rs, lane0 + hh:lane0 + hh + 1], (ck, rdim)) for hh in range(2)], axis=0))

    st = lambda name: jnp.stack(tiles[name], axis=0)
    stack2 = lambda x: jnp.concatenate([jnp.where(first_half, x, 0.0), jnp.where(first_half, 0.0, x)], axis=1)
    k_st = stack2(st('k'))
    col = st('col')
    diff = col - jnp.swapaxes(col, 1, 2)
    dec = jnp.exp(jnp.where(incl, diff, -1e30))
    m = _bdot_nt(stack2(st('kb')), k_st) * jnp.where(strict, dec, 0.0)
    t_inv = _unit_lower_inverse(m, ck)
    rhs = st('rhs')
    sol = _bdot(t_inv, jnp.concatenate([rhs, rhs], axis=1))
    u = jnp.where(first_half, sol[:, :ck, :LANES], sol[:, ck:, :LANES])
    w = jnp.where(first_half, sol[:, :ck, LANES:], sol[:, ck:, LANES:])
    attn = _bdot_nt(stack2(st('q')), k_st) * dec
    qd = st('qd')
    kd_t = jnp.swapaxes(st('kd'), 1, 2)
    cd = st('cd')

    gsel = lambda x, i: jnp.stack([x[(b * nck + i) * N_PAIRS + p] for b in range(bb) for p in range(N_PAIRS)], axis=0)
    state = st_ref[...].reshape(bb * N_PAIRS, LANES, LANES)
    o_chunks = []
    for i in range(nck):
        v_new = gsel(u, i) - _bdot(gsel(w, i), state)
        intra = _bdot(gsel(attn, i), jnp.concatenate([v_new, v_new], axis=1))
        o_chunks.append(_bdot(gsel(qd, i), state) + jnp.where(first_half, intra[:, :ck], intra[:, ck:]))
        upd = _bdot(gsel(kd_t, i), v_new)
        state = state * gsel(cd, i)[:, 0:1, :] + jnp.where(same_head, upd, 0.0)
    st_ref[...] = state.reshape(bb, N_PAIRS, LANES, LANES)

    for b in range(bb):
        o = jnp.concatenate(
            [jnp.concatenate([o_chunks[i][b * N_PAIRS + p] for p in range(N_PAIRS)], axis=-1) for i in range(nck)],
            axis=0)
        ms = head_sumsq(o) * (1.0 / HEAD_DIM)
        o = o * lax.rsqrt(ms + EPS) * ng_ref[...] * _silu(gates[b])
        y_ref[b] = o.astype(y_ref.dtype)

    @pl.when(c == pl.num_programs(1) - 1)
    def _():
        sout_ref[...] = st_ref[...]


def _gdn(qg, small, hist, s0, cw, a_log, dt_bias, norm_g, tt, ck, bb):
    bsz, t, _ = qg.shape
    hist_p = jnp.pad(hist, ((0, 0), (HIST_PAD - (GDN_CONV_W - 1), 0), (0, 0)))
    bias = _lane_row(dt_bias, DECAY_LANE)
    arow = _lane_row(-jnp.exp(a_log.astype(F32)), DECAY_LANE)
    ng = jnp.tile(norm_g.astype(F32), GDN_HEADS)[None, :]
    full = lambda v: pl.BlockSpec(v.shape, lambda i, j: (0,) * v.ndim)
    blk = lambda c: pl.BlockSpec((bb, tt, c), lambda i, j: (i, j, 0))
    st = pl.BlockSpec((bb, N_PAIRS, LANES, LANES), lambda i, j: (i, 0, 0, 0))
    consts = (cw, bias, arow, ng, _expand_matrix(BETA_LANE), _expand_matrix(DECAY_LANE))
    return pl.pallas_call(
        functools.partial(_gdn_kernel, tt=tt, ck=ck, bb=bb),
        grid=(bsz // bb, t // tt),
        in_specs=[blk(qg.shape[-1]), blk(LANES),
                  pl.BlockSpec((bb, HIST_PAD, GDN_QKV), lambda i, j: (i, 0, 0)), st]
                 + [full(v) for v in consts],
        out_specs=[blk(GDN_INNER), st],
        out_shape=[jax.ShapeDtypeStruct((bsz, t, GDN_INNER), BF16),
                   jax.ShapeDtypeStruct((bsz, N_PAIRS, LANES, LANES), F32)],
        scratch_shapes=[pltpu.VMEM((bb, tt + HIST_PAD, GDN_QKV), F32),
                        pltpu.VMEM((bb, N_PAIRS, LANES, LANES), F32)],
        compiler_params=_cparams("arbitrary", "arbitrary"),
        name="gdn",
    )(qg, small, hist_p, s0, *consts)


def _gdn_state_in(s):
    bsz = s.shape[0]
    out = jnp.zeros((bsz, N_PAIRS, 2, HEAD_DIM, 2, HEAD_DIM), F32)
    for hd in range(GDN_HEADS):
        out = out.at[:, hd // 2, hd % 2, :, hd % 2, :].set(s[:, hd])
    return out.reshape(bsz, N_PAIRS, LANES, LANES)


def _gdn_state_out(sp):
    bsz = sp.shape[0]
    sp = sp.reshape(bsz, N_PAIRS, 2, HEAD_DIM, 2, HEAD_DIM)
    return jnp.stack([sp[:, hd // 2, hd % 2, :, hd % 2, :] for hd in range(GDN_HEADS)], axis=1)


def _mix_residual(x_ref, ya_ref, yb_ref, yc_ref, woa_ref, wob_ref, woc_ref):
    mix = (jnp.dot(ya_ref[...], woa_ref[...], preferred_element_type=F32)
           + jnp.dot(yb_ref[...], wob_ref[...], preferred_element_type=F32)
           + jnp.dot(yc_ref[...], woc_ref[...], preferred_element_type=F32))
    return x_ref[...] + mix


def _rms(x, g):
    return x * lax.rsqrt(jnp.mean(x * x, axis=-1, keepdims=True) + EPS) * g


def _ffn_kernel(x_ref, ya_ref, yb_ref, yc_ref, woa_ref, wob_ref, woc_ref, g_ref, wg_ref, wu_ref, wd_ref,
                o_ref, xn_ref, h_ref, acc_ref):
    j = pl.program_id(1)

    @pl.when(j == 0)
    def _():
        xn = _mix_residual(x_ref, ya_ref, yb_ref, yc_ref, woa_ref, wob_ref, woc_ref)
        xn_ref[...] = xn
        h_ref[...] = _rms(xn, g_ref[...]).astype(BF16)
        acc_ref[...] = jnp.zeros_like(acc_ref)

    h = h_ref[...]
    act = _silu(jnp.dot(h, wg_ref[...], preferred_element_type=F32)) * jnp.dot(h, wu_ref[...], preferred_element_type=F32)
    acc_ref[...] += jnp.dot(act.astype(BF16), wd_ref[...], preferred_element_type=F32)

    @pl.when(j == pl.num_programs(1) - 1)
    def _():
        o_ref[...] = xn_ref[...] + acc_ref[...]


FFN_CHUNKS = 2


def _out_ffn(x2d, ya, yb, yc, wo, g, wg, wu, wd):
    n = x2d.shape[0]
    tm = _row_tile(n, 512)
    f = wg.shape[1]
    tf = f // FFN_CHUNKS
    rows = lambda c: pl.BlockSpec((tm, c), lambda i, j: (i, 0))
    full = lambda w: pl.BlockSpec(w.shape, lambda i, j: (0, 0))
    return pl.pallas_call(
        _ffn_kernel,
        grid=(n // tm, FFN_CHUNKS),
        in_specs=[rows(D_MODEL), rows(ya.shape[1]), rows(yb.shape[1]), rows(yc.shape[1]),
                  full(wo[0]), full(wo[1]), full(wo[2]), full(g),
                  pl.BlockSpec((D_MODEL, tf), lambda i, j: (0, j)),
                  pl.BlockSpec((D_MODEL, tf), lambda i, j: (0, j)),
                  pl.BlockSpec((tf, D_MODEL), lambda i, j: (j, 0))],
        out_specs=rows(D_MODEL),
        out_shape=jax.ShapeDtypeStruct((n, D_MODEL), F32),
        scratch_shapes=[pltpu.VMEM((tm, D_MODEL), F32), pltpu.VMEM((tm, D_MODEL), BF16),
                        pltpu.VMEM((tm, D_MODEL), F32)],
        compiler_params=_cparams("parallel", "arbitrary"),
        name="out_ffn",
    )(x2d, ya, yb, yc, wo[0], wo[1], wo[2], g, wg, wu, wd)


ROUTE_TILE = 512


def _router_kernel(x_ref, ya_ref, yb_ref, yc_ref, woa_ref, wob_ref, woc_ref, g_ref, wr_ref, br_ref,
                   xn_ref, h_ref, gate_ref, slot_ref, cnt_ref, run_ref, *, cap):
    i = pl.program_id(0)

    @pl.when(i == 0)
    def _():
        run_ref[...] = jnp.zeros_like(run_ref)

    xn = _mix_residual(x_ref, ya_ref, yb_ref, yc_ref, woa_ref, wob_ref, woc_ref)
    xn_ref[...] = xn
    h = _rms(xn, g_ref[...])
    h_ref[...] = h
    tm = h.shape[0]
    lane = _iota((1, LANES), 1)
    logits = jnp.where(lane < N_EXPERTS, _dot_hp(h, wr_ref[...]) + br_ref[...], -jnp.inf)
    m1 = jnp.max(logits, axis=-1, keepdims=True)
    i1 = jnp.min(jnp.where(logits == m1, lane, LANES), axis=-1, keepdims=True)
    rest = jnp.where(lane == i1, -jnp.inf, logits)
    m2 = jnp.max(rest, axis=-1, keepdims=True)
    i2 = jnp.min(jnp.where(rest == m2, lane, LANES), axis=-1, keepdims=True)
    e2 = jnp.exp(m2 - m1)
    inv = 1.0 / (1.0 + e2)
    gate_ref[...] = jnp.where(lane == 0, inv, 0.0) + jnp.where(lane == 1, e2 * inv, 0.0)

    chosen = jnp.where((lane == i1) | (lane == i2), 1.0, 0.0)
    before = _iota((tm, tm), 0) > _iota((tm, tm), 1)
    rank = jnp.dot(before.astype(BF16), chosen.astype(BF16), preferred_element_type=F32) + run_ref[0:1, :]
    base = lane.astype(F32) * float(cap)
    pick = lambda idx: jnp.sum(jnp.where(lane == idx, rank + base, 0.0), axis=-1, keepdims=True)
    slots = jnp.where(lane == 0, pick(i1), 0.0) + jnp.where(lane == 1, pick(i2), 0.0)
    slots_t = jnp.transpose(slots).astype(jnp.int32)
    for k in range(2):
        slot_ref[0, k] = jnp.concatenate(
            [slots_t[k:k + 1, c * LANES:(c + 1) * LANES] for c in range(tm // LANES)], axis=0)
    run_ref[...] = run_ref[...] + jnp.sum(chosen, axis=0, keepdims=True)
    cnt_ref[...] = run_ref[...].astype(jnp.int32)


def _out_router(x2d, ya, yb, yc, wo, g, w_router, b_router, cap):
    n = x2d.shape[0]
    tm = _row_tile(n, ROUTE_TILE)
    wr = jnp.zeros((D_MODEL, LANES), F32).at[:, :N_EXPERTS].set(w_router.astype(F32))
    br = _lane_row(b_router, 0)
    rows = lambda c: pl.BlockSpec((tm, c), lambda i: (i, 0))
    full = lambda w: pl.BlockSpec(w.shape, lambda i: (0, 0))
    return pl.pallas_call(
        functools.partial(_router_kernel, cap=cap),
        grid=(n // tm,),
        in_specs=[rows(D_MODEL), rows(ya.shape[1]), rows(yb.shape[1]), rows(yc.shape[1]),
                  full(wo[0]), full(wo[1]), full(wo[2]), full(g), full(wr), full(br)],
        out_specs=[rows(D_MODEL), rows(D_MODEL), rows(LANES),
                   pl.BlockSpec((1, 2, tm // LANES, LANES), lambda i: (i, 0, 0, 0)),
                   pl.BlockSpec((SUBLANES, LANES), lambda i: (0, 0))],
        out_shape=[jax.ShapeDtypeStruct((n, D_MODEL), F32), jax.ShapeDtypeStruct((n, D_MODEL), F32),
                   jax.ShapeDtypeStruct((n, LANES), F32),
                   jax.ShapeDtypeStruct((n // tm, 2, tm // LANES, LANES), jnp.int32),
                   jax.ShapeDtypeStruct((SUBLANES, LANES), jnp.int32)],
        scratch_shapes=[pltpu.VMEM((SUBLANES, LANES), F32)],
        compiler_params=_cparams("arbitrary"),
        name="out_router",
    )(x2d, ya, yb, yc, wo[0], wo[1], wo[2], g, wr, br)


def _row_copy(src_ref, src_row, dst_ref, dst_row, sem):
    return pltpu.make_async_copy(src_ref.at[pl.ds(src_row, 1)], dst_ref.at[pl.ds(dst_row, 1)], sem)


def _slot_fetch(slots_hbm, tile, smem_ref, buf, sem):
    return pltpu.make_async_copy(slots_hbm.at[tile], smem_ref.at[buf], sem.at[buf])


def _for_rows(tm, fn):
    for c in range(tm // LANES):
        def body(o, carry, c=c):
            base = pl.multiple_of(o * SUBLANES, SUBLANES)
            for j in range(SUBLANES):
                fn(c, base, j)
            return carry
        lax.fori_loop(0, LANES // SUBLANES, body, 0, unroll=2)


def _vmem_row(ref, base, j):
    return ref.at[pl.ds(base, SUBLANES)].at[pl.ds(j, 1)]


def _dispatch_kernel(cnt_ref, slots_hbm, h_ref, hs_hbm, slot_smem, zero_ref, slot_sem, row_sem, pad_sem,
                     *, tm, cap, bm):
    i = pl.program_id(0)
    n_tiles = pl.num_programs(0)

    @pl.when(i == 0)
    def _():
        _slot_fetch(slots_hbm, 0, slot_smem, 0, slot_sem).start()

    @pl.when(i + 1 < n_tiles)
    def _():
        _slot_fetch(slots_hbm, i + 1, slot_smem, (i + 1) % 2, slot_sem).start()

    _slot_fetch(slots_hbm, i, slot_smem, i % 2, slot_sem).wait()

    def issue(c, base, j):
        for k in range(2):
            dst = hs_hbm.at[pl.ds(slot_smem[i % 2, k, c, base + j], 1)]
            pltpu.make_async_copy(_vmem_row(h_ref, c * LANES + base, j), dst, row_sem).start(priority=k)

    _for_rows(tm, issue)

    def drain(r, carry):
        for k in range(2):
            _row_copy(h_ref, 0, hs_hbm, 0, row_sem).wait()
        return carry

    lax.fori_loop(0, tm, drain, 0, unroll=8)

    @pl.when(i == n_tiles - 1)
    def _():
        zero_ref[...] = jnp.zeros_like(zero_ref)
        for e in range(N_EXPERTS):
            c = cnt_ref[e]
            n_pad = ((c + bm - 1) // bm) * bm - c

            def fill(r, carry):
                _row_copy(zero_ref, 0, hs_hbm, e * cap + c + r, pad_sem).start()
                return carry

            def fill_wait(r, carry):
                _row_copy(zero_ref, 0, hs_hbm, 0, pad_sem).wait()
                return carry

            lax.fori_loop(0, n_pad, fill, 0)
            lax.fori_loop(0, n_pad, fill_wait, 0)


def _dispatch(counts, slots, h, cap, bm):
    n = h.shape[0]
    n_tiles = slots.shape[0]
    tm = slots.shape[2] * LANES
    return pl.pallas_call(
        functools.partial(_dispatch_kernel, tm=tm, cap=cap, bm=bm),
        grid_spec=pltpu.PrefetchScalarGridSpec(
            num_scalar_prefetch=1,
            grid=(n_tiles,),
            in_specs=[pl.BlockSpec(memory_space=pl.ANY), pl.BlockSpec((tm, D_MODEL), lambda i, cnt: (i, 0))],
            out_specs=pl.BlockSpec(memory_space=pl.ANY),
            scratch_shapes=[pltpu.SMEM((2, 2, tm // LANES, LANES), jnp.int32), pltpu.VMEM((SUBLANES, D_MODEL), F32),
                            pltpu.SemaphoreType.DMA((2,)), pltpu.SemaphoreType.DMA, pltpu.SemaphoreType.DMA],
        ),
        out_shape=jax.ShapeDtypeStruct((N_EXPERTS * cap, D_MODEL), F32),
        compiler_params=_cparams("arbitrary"),
        name="moe_dispatch",
    )(counts, slots, h)


def _experts_kernel(tbl_ref, hs_ref, wg_ref, wu_ref, wd_ref, ys_ref, hb_ref, acc_ref):
    s = pl.program_id(0)
    j = pl.program_id(1)

    @pl.when(tbl_ref[2, s] == 1)
    def _():
        @pl.when(j == 0)
        def _():
            hb_ref[...] = hs_ref[...].astype(BF16)
            acc_ref[...] = jnp.zeros_like(acc_ref)

        h = hb_ref[...]
        act = (_silu(jnp.dot(h, wg_ref[0], preferred_element_type=F32))
               * jnp.dot(h, wu_ref[0], preferred_element_type=F32))
        acc_ref[...] += jnp.dot(act.astype(BF16), wd_ref[0], preferred_element_type=F32)

        @pl.when(j == pl.num_programs(1) - 1)
        def _():
            ys_ref[...] = acc_ref[...]


def _block_table(counts, cap, bm, n_steps):
    nblk = (counts + bm - 1) // bm
    cum = jnp.cumsum(nblk)
    total = cum[-1]
    step = jnp.arange(n_steps, dtype=jnp.int32)
    last = jnp.maximum(total - 1, 0)
    eff = jnp.minimum(step, last)
    expert = jnp.minimum(jnp.searchsorted(cum, eff, side='right'), N_EXPERTS - 1).astype(jnp.int32)
    blk = eff - (cum[expert] - nblk[expert])
    return jnp.stack([expert * (cap // bm) + blk, expert, (step < total).astype(jnp.int32)]).astype(jnp.int32)


def _experts(counts, hs, wg, wu, wd, cap, bm, n_assign):
    n_steps = n_assign // bm + N_EXPERTS
    tbl = _block_table(counts, cap, bm, n_steps)
    f = wg.shape[2]
    tf = f // FFN_CHUNKS
    chunk = lambda s, j, t: jnp.where(t[2, s] == 1, j, FFN_CHUNKS - 1)
    rows = pl.BlockSpec((bm, D_MODEL), lambda s, j, t: (t[0, s], 0))
    return pl.pallas_call(
        _experts_kernel,
        grid_spec=pltpu.PrefetchScalarGridSpec(
            num_scalar_prefetch=1,
            grid=(n_steps, FFN_CHUNKS),
            in_specs=[rows,
                      pl.BlockSpec((1, D_MODEL, tf), lambda s, j, t: (t[1, s], 0, chunk(s, j, t))),
                      pl.BlockSpec((1, D_MODEL, tf), lambda s, j, t: (t[1, s], 0, chunk(s, j, t))),
                      pl.BlockSpec((1, tf, D_MODEL), lambda s, j, t: (t[1, s], chunk(s, j, t), 0))],
            out_specs=rows,
            scratch_shapes=[pltpu.VMEM((bm, D_MODEL), BF16), pltpu.VMEM((bm, D_MODEL), F32)],
        ),
        out_shape=jax.ShapeDtypeStruct(hs.shape, F32),
        compiler_params=_cparams("arbitrary", "arbitrary"),
        name="moe_experts",
    )(tbl, hs, wg, wu, wd)


def _combine_kernel(slots_hbm, ys_hbm, xn_ref, gate_ref, gf_ref, o_ref, slot_smem, ybuf_ref, slot_sem, row_sem,
                    *, tm):
    i = pl.program_id(0)
    n_tiles = pl.num_programs(0)

    def gather(tile, buf):
        def issue(c, base, j):
            for k in range(2):
                src = ys_hbm.at[pl.ds(slot_smem[buf, k, c, base + j], 1)]
                dst = _vmem_row(ybuf_ref.at[buf, k], c * LANES + base, j)
                pltpu.make_async_copy(src, dst, row_sem.at[buf]).start(priority=k)
        _for_rows(tm, issue)

    @pl.when(i == 0)
    def _():
        first = _slot_fetch(slots_hbm, 0, slot_smem, 0, slot_sem)
        first.start()
        first.wait()
        gather(0, 0)

        @pl.when(n_tiles > 1)
        def _():
            _slot_fetch(slots_hbm, 1, slot_smem, 1, slot_sem).start()

    @pl.when(i + 1 < n_tiles)
    def _():
        _slot_fetch(slots_hbm, i + 1, slot_smem, (i + 1) % 2, slot_sem).wait()
        gather(i + 1, (i + 1) % 2)

    def drain(r, carry):
        for k in range(2):
            _row_copy(ys_hbm, 0, ybuf_ref.at[i % 2, k], 0, row_sem.at[i % 2]).wait()
        return carry

    lax.fori_loop(0, tm, drain, 0, unroll=8)

    @pl.when(i + 2 < n_tiles)
    def _():
        _slot_fetch(slots_hbm, i + 2, slot_smem, i % 2, slot_sem).start()

    g = gate_ref[...]
    y = g[:, 0:1] * ybuf_ref[i % 2, 0] + g[:, 1:2] * ybuf_ref[i % 2, 1]
    o_ref[...] = _rms(xn_ref[...] + y, gf_ref[...])


def _combine(slots, ys, xn, gates, g_final):
    n = xn.shape[0]
    n_tiles = slots.shape[0]
    tm = slots.shape[2] * LANES
    rows = lambda c: pl.BlockSpec((tm, c), lambda i: (i, 0))
    return pl.pallas_call(
        functools.partial(_combine_kernel, tm=tm),
        grid=(n_tiles,),
        in_specs=[pl.BlockSpec(memory_space=pl.ANY), pl.BlockSpec(memory_space=pl.ANY),
                  rows(D_MODEL), rows(LANES), pl.BlockSpec(g_final.shape, lambda i: (0, 0))],
        out_specs=rows(D_MODEL),
        out_shape=jax.ShapeDtypeStruct((n, D_MODEL), F32),
        scratch_shapes=[pltpu.SMEM((2, 2, tm // LANES, LANES), jnp.int32), pltpu.VMEM((2, 2, tm, D_MODEL), F32),
                        pltpu.SemaphoreType.DMA((2,)), pltpu.SemaphoreType.DMA((2,))],
        compiler_params=_cparams("arbitrary"),
        name="moe_combine",
    )(slots, ys, xn, gates, g_final)


def _moe(x2d, ya, yb, yc, wo, g_ffn, w_router, b_router, wg, wu, wd, g_final):
    n = x2d.shape[0]
    bm = _row_tile(n, 512)
    cap = n
    xn, h, gates, slots, counts = _out_router(x2d, ya, yb, yc, wo, g_ffn, w_router, b_router, cap)
    counts = counts[0, :N_EXPERTS]
    hs = _dispatch(counts, slots, h, cap, bm)
    ys = _experts(counts, hs, wg, wu, wd, cap, bm, 2 * n)
    return _combine(slots, ys, xn, gates, g_final)


IN_SIZES = (2 * CONV_CH, SSM_INNER, SSM_XBC, SSM_HEADS, GDN_QKV, GDN_INNER, GDN_HEADS, GDN_HEADS)


def _prep_layer(l, p):
    off = np.concatenate([[0], np.cumsum(IN_SIZES)])
    w_in = p['w_in_bf16'][l]
    col = lambda i: w_in[:, off[i]:off[i + 1]]
    a_in, z, xbc, dt, qkv, gate, b_raw, a_raw = (col(i) for i in range(8))
    small = jnp.concatenate([dt, b_raw, a_raw, jnp.zeros((D_MODEL, SMALL_W - 3 * SSM_HEADS), BF16)], axis=1)
    wo = p['w_out'][l].astype(BF16)
    return dict(
        wa=a_in,
        wzx=jnp.concatenate([z, xbc], axis=1),
        wqg=jnp.concatenate([qkv, gate], axis=1),
        ws=small,
        wo=(wo[:CONV_CH], wo[CONV_CH:CONV_CH + SSM_INNER], wo[CONV_CH + SSM_INNER:]),
    )


def _trunk(x, st_conv_a, st_ssm_conv, st_ssm, st_gdn_conv, st_gdn, p, prep, ssd_rows, gdn_rows, bb):
    bsz, t, _ = x.shape
    n = bsz * t
    depth = p['g_mix'].shape[0]
    x2d = x.reshape(n, D_MODEL)
    new = [[] for _ in range(5)]
    for l in range(depth):
        w = prep[l]
        a_in, zx, qg, small = _norm_proj(x2d, p['g_mix'][l][None, :], w['wa'], w['wzx'], w['wqg'], w['ws'])
        a_in = a_in.reshape(bsz, t, -1)
        zx = zx.reshape(bsz, t, -1)
        qg = qg.reshape(bsz, t, -1)
        small = small.reshape(bsz, t, -1)
        ya, conv_a = _conv_a(a_in, st_conv_a[l], p['conv_a_w'][l], p['conv_a_b'][l][None, :],
                             p['ln_a_g'][l][None, :], p['ln_a_b'][l][None, :])
        yb, ssm = _ssd(zx, small, st_ssm_conv[l], _ssd_state_in(st_ssm[l]), p['ssm_conv_w'][l], p['ssm_conv_b'][l],
                       p['ssm_dt_bias'][l], p['ssm_a_log'][l], p['ssm_d'][l], p['ssm_norm_g'][l], ssd_rows, bb)
        yc, gdn = _gdn(qg, small, st_gdn_conv[l], _gdn_state_in(st_gdn[l]), p['gdn_conv_w'][l],
                       p['gdn_a_log'][l], p['gdn_dt_bias'][l], p['gdn_norm_g'][l], gdn_rows[0], gdn_rows[1], bb)
        assert t >= max(SSM_CONV_W, GDN_CONV_W) - 1
        new[0].append(conv_a)
        new[1].append(zx[:, t - (SSM_CONV_W - 1):, SSM_INNER:])
        new[2].append(_ssd_state_out(ssm))
        new[3].append(qg[:, t - (GDN_CONV_W - 1):, :GDN_QKV])
        new[4].append(_gdn_state_out(gdn))
        flat = lambda y: y.reshape(n, -1)
        g_ffn = p['g_ffn'][l][None, :]
        if l % 2 == 0:
            x2d = _out_ffn(x2d, flat(ya), flat(yb), flat(yc), w['wo'], g_ffn,
                           prep['ffn'][l // 2][0], prep['ffn'][l // 2][1], prep['ffn'][l // 2][2])
        else:
            x2d = _moe(x2d, flat(ya), flat(yb), flat(yc), w['wo'], g_ffn, p['moe_w_router'][l // 2],
                       p['moe_b_router'][l // 2], *prep['moe'][l // 2], p['g_final'][None, :])
    return (x2d.reshape(bsz, t, D_MODEL),) + tuple(jnp.stack(s) for s in new)


def kernel(x_prompt, x_sample, state_conv_a, state_ssm_conv, state_ssm, state_gdn_conv, state_gdn, g_mix, w_in, conv_a_w, conv_a_b, ln_a_g, ln_a_b, ssm_conv_w, ssm_conv_b, ssm_dt_bias, ssm_a_log, ssm_d, ssm_norm_g, gdn_conv_w, gdn_a_log, gdn_dt_bias, gdn_norm_g, w_out, g_ffn, ffn_w_gate, ffn_w_up, ffn_w_down, moe_w_router, moe_b_router, moe_w_gate, moe_w_up, moe_w_down, g_final):
    p = dict(g_mix=g_mix, w_in=w_in, conv_a_w=conv_a_w, conv_a_b=conv_a_b, ln_a_g=ln_a_g, ln_a_b=ln_a_b,
             ssm_conv_w=ssm_conv_w, ssm_conv_b=ssm_conv_b, ssm_dt_bias=ssm_dt_bias, ssm_a_log=ssm_a_log,
             ssm_d=ssm_d, ssm_norm_g=ssm_norm_g, gdn_conv_w=gdn_conv_w, gdn_a_log=gdn_a_log,
             gdn_dt_bias=gdn_dt_bias, gdn_norm_g=gdn_norm_g, w_out=w_out, g_ffn=g_ffn,
             moe_w_router=moe_w_router, moe_b_router=moe_b_router, g_final=g_final)
    depth = g_mix.shape[0]
    assert depth % 2 == 0, "the final RMSNorm is fused into the expert layer, which must come last"
    p['w_in_bf16'] = w_in.astype(BF16)
    prep = {l: _prep_layer(l, p) for l in range(depth)}
    prep['ffn'] = [(ffn_w_gate[i].astype(BF16), ffn_w_up[i].astype(BF16), ffn_w_down[i].astype(BF16))
                   for i in range(ffn_w_gate.shape[0])]
    prep['moe'] = [(moe_w_gate[i].astype(BF16), moe_w_up[i].astype(BF16), moe_w_down[i].astype(BF16))
                   for i in range(moe_w_gate.shape[0])]
    bp, dt = x_prompt.shape[0], x_prompt.dtype
    zeros = lambda *s: jnp.zeros((depth, bp) + s, dt)
    outs_p = _trunk(x_prompt, zeros(CONV_W - 1, CONV_CH), zeros(SSM_CONV_W - 1, SSM_XBC),
                    zeros(SSM_HEADS, HEAD_DIM, SSM_STATE), zeros(GDN_CONV_W - 1, GDN_QKV),
                    zeros(GDN_HEADS, HEAD_DIM, HEAD_DIM), p, prep, ssd_rows=min(128, x_prompt.shape[1]),
                    gdn_rows=(min(256, x_prompt.shape[1]), min(64, x_prompt.shape[1])), bb=2)
    outs_s = _trunk(x_sample, state_conv_a, state_ssm_conv, state_ssm, state_gdn_conv, state_gdn, p, prep,
                    ssd_rows=x_sample.shape[1], gdn_rows=(x_sample.shape[1], x_sample.shape[1]), bb=2)
    return (outs_p[0], outs_s[0]) + outs_p[1:] + outs_s[1:]
```

```python
import functools
import math

import jax
import jax.numpy as jnp
import numpy as np
from jax import lax
from jax.experimental import pallas as pl
from jax.experimental.pallas import tpu as pltpu

F32 = jnp.float32
BF16 = jnp.bfloat16
EPS = 1e-6

LANES = 128
SUBLANES = 8
VMEM_BYTES_V7X = 64 * 1024 * 1024
VMEM_LIMIT = VMEM_BYTES_V7X * 3 // 4

D_MODEL = 1024
CONV_CH = 256
CONV_W = 31
SSM_HEADS = 6
HEAD_DIM = 64
SSM_INNER = SSM_HEADS * HEAD_DIM
SSM_STATE = 64
SSM_GROUPS = 2
SSM_XBC = SSM_INNER + 2 * SSM_GROUPS * SSM_STATE
SSM_CONV_W = 4
GDN_HEADS = 6
GDN_INNER = GDN_HEADS * HEAD_DIM
GDN_QKV = 3 * GDN_INNER
GDN_CONV_W = 4
N_PAIRS = 3
N_EXPERTS = 8
SMALL_W = LANES


def _cparams(*sem):
    return pltpu.CompilerParams(dimension_semantics=sem, vmem_limit_bytes=VMEM_LIMIT)


def _dot(a, b):
    return jnp.dot(a.astype(BF16), b.astype(BF16), preferred_element_type=F32)


def _dot_nt(a, b):
    return lax.dot_general(a.astype(BF16), b.astype(BF16), (((1,), (1,)), ((), ())),
                           preferred_element_type=F32)


def _split3(x):
    hi = x.astype(BF16)
    r1 = x - hi.astype(F32)
    mid = r1.astype(BF16)
    lo = (r1 - mid.astype(F32)).astype(BF16)
    return hi, mid, lo


def _dot_sel(x, sel_bf16, pieces=2):
    d = lambda p: jnp.dot(p, sel_bf16, preferred_element_type=F32)
    hi = x.astype(BF16)
    r1 = x - hi.astype(F32)
    mid = r1.astype(BF16)
    if pieces == 2:
        return d(hi) + d(mid)
    return d(hi) + d(mid) + d((r1 - mid.astype(F32)).astype(BF16))


def _sel_dot(sel_bf16, x):
    hi, mid, lo = _split3(x)
    d = lambda p: jnp.dot(sel_bf16, p, preferred_element_type=F32)
    return d(hi) + d(mid) + d(lo)


def _dot_hp(a, b):
    ah = a.astype(BF16)
    al = (a - ah.astype(F32)).astype(BF16)
    bh = b.astype(BF16)
    bl = (b - bh.astype(F32)).astype(BF16)
    d = lambda p, q: jnp.dot(p, q, preferred_element_type=F32)
    return d(ah, bh) + (d(ah, bl) + d(al, bh))


def _silu(x):
    return x * (1.0 / (1.0 + jnp.exp(-x)))


def _sigmoid(x):
    return 1.0 / (1.0 + jnp.exp(-x))


def _softplus(x):
    return jnp.maximum(x, 0.0) + jnp.log(1.0 + jnp.exp(-jnp.abs(x)))


def _iota(shape, dim):
    return lax.broadcasted_iota(jnp.int32, shape, dim)


def _norm_proj_kernel(x_ref, g_ref, wa_ref, wzx_ref, wqg_ref, ws_ref, a_ref, zx_ref, qg_ref, s_ref):
    x = x_ref[...]
    u = x * lax.rsqrt(jnp.mean(x * x, axis=-1, keepdims=True) + EPS) * g_ref[...]
    ub = u.astype(BF16)
    a_ref[...] = jnp.dot(ub, wa_ref[...], preferred_element_type=F32)
    zx_ref[...] = jnp.dot(ub, wzx_ref[...], preferred_element_type=F32)
    qg_ref[...] = jnp.dot(ub, wqg_ref[...], preferred_element_type=F32)
    s_ref[...] = jnp.dot(ub, ws_ref[...], preferred_element_type=F32)


def _row_tile(n, want):
    t = min(want, n)
    while n % t:
        t //= 2
    return t


def _norm_proj(x2d, g, wa, wzx, wqg, ws):
    n = x2d.shape[0]
    tm = _row_tile(n, 512)
    full = lambda w: pl.BlockSpec(w.shape, lambda i: (0, 0))
    rows = lambda c: pl.BlockSpec((tm, c), lambda i: (i, 0))
    widths = (wa.shape[1], wzx.shape[1], wqg.shape[1], ws.shape[1])
    return pl.pallas_call(
        _norm_proj_kernel,
        grid=(n // tm,),
        in_specs=[rows(D_MODEL), full(g), full(wa), full(wzx), full(wqg), full(ws)],
        out_specs=[rows(c) for c in widths],
        out_shape=[jax.ShapeDtypeStruct((n, c), F32) for c in widths],
        compiler_params=_cparams("parallel"),
        name="norm_proj",
    )(x2d, g, wa, wzx, wqg, ws)


CONV_PAD = 32
CONV_ROWS = 32


def _conv_a_kernel(a_ref, hist_ref, w_ref, b_ref, lg_ref, lb_ref, y_ref, nh_ref, buf_ref, sh_ref, *, tt):
    t = pl.program_id(1)

    @pl.when(t == 0)
    def _():
        buf_ref[0:CONV_PAD, :] = hist_ref[0]

    a = a_ref[0]
    glu = a[:, :CONV_CH] * _sigmoid(a[:, CONV_CH:])
    buf_ref[CONV_PAD:CONV_PAD + tt, :] = glu
    full = buf_ref[...]
    n = tt + CONV_PAD
    sh_ref[0] = full
    for s in range(1, SUBLANES):
        sh_ref[s] = pltpu.roll(full, n - s, axis=0)
    off = CONV_PAD - (CONV_W - 1)
    for r0 in range(0, tt, CONV_ROWS):
        acc = jnp.zeros((CONV_ROWS, CONV_CH), F32)
        for k in range(CONV_W):
            s = (off + k) % SUBLANES
            base = r0 + off + k - s
            acc = acc + w_ref[k:k + 1, :] * sh_ref[s, base:base + CONV_ROWS, :]
        y = acc + b_ref[...]
        mu = jnp.mean(y, axis=-1, keepdims=True)
        yc = y - mu
        var = jnp.mean(yc * yc, axis=-1, keepdims=True)
        y = yc * lax.rsqrt(var + EPS) * lg_ref[...] + lb_ref[...]
        y_ref[0, r0:r0 + CONV_ROWS, :] = _silu(y).astype(y_ref.dtype)
    nh_ref[0] = buf_ref[tt:tt + CONV_PAD, :]
    buf_ref[0:CONV_PAD, :] = buf_ref[tt:tt + CONV_PAD, :]


def _conv_a(a_in, hist, w, b, lg, lb):
    bsz, t, _ = a_in.shape
    tt = _row_tile(t, 256)
    hist_p = jnp.pad(hist, ((0, 0), (CONV_PAD - (CONV_W - 1), 0), (0, 0)))
    vec = lambda v: pl.BlockSpec(v.shape, lambda i, j: (0, 0))
    y, nh = pl.pallas_call(
        functools.partial(_conv_a_kernel, tt=tt),
        grid=(bsz, t // tt),
        in_specs=[pl.BlockSpec((1, tt, 2 * CONV_CH), lambda i, j: (i, j, 0)),
                  pl.BlockSpec((1, CONV_PAD, CONV_CH), lambda i, j: (i, 0, 0)),
                  vec(w), vec(b), vec(lg), vec(lb)],
        out_specs=[pl.BlockSpec((1, tt, CONV_CH), lambda i, j: (i, j, 0)),
                   pl.BlockSpec((1, CONV_PAD, CONV_CH), lambda i, j: (i, 0, 0))],
        out_shape=[jax.ShapeDtypeStruct((bsz, t, CONV_CH), BF16),
                   jax.ShapeDtypeStruct((bsz, CONV_PAD, CONV_CH), F32)],
        scratch_shapes=[pltpu.VMEM((tt + CONV_PAD, CONV_CH), F32),
                        pltpu.VMEM((SUBLANES, tt + CONV_PAD, CONV_CH), F32)],
        compiler_params=_cparams("arbitrary", "arbitrary"),
        name="conv_a",
    )(a_in, hist_p, w, b, lg, lb)
    return y, nh[:, CONV_PAD - (CONV_W - 1):, :]


HIST_PAD = 8


def _short_conv(x, buf_ref, b, w_ref, width, rows):
    buf_ref[b, HIST_PAD:HIST_PAD + rows, :] = x
    off = HIST_PAD - (width - 1)
    acc = w_ref[0:1, :] * buf_ref[b, off:off + rows, :]
    for k in range(1, width):
        acc = acc + w_ref[k:k + 1, :] * buf_ref[b, off + k:off + k + rows, :]
    buf_ref[b, 0:HIST_PAD, :] = buf_ref[b, rows:rows + HIST_PAD, :]
    return acc


def _lower_tri(rows):
    return _iota((rows, rows), 0) >= _iota((rows, rows), 1)


def _pair_mask():
    return _iota((1, LANES), 1) < HEAD_DIM


def _ssd_kernel(zx_ref, s_ref, hist_ref, h0_ref, cw_ref, cb_ref, dtb_ref, arow_ref, dx_ref, ng_ref,
                ex_ref, y_ref, hout_ref, buf_ref, h_ref, *, rows, bb):
    c = pl.program_id(1)

    @pl.when(c == 0)
    def _():
        buf_ref[:, 0:HIST_PAD, :] = hist_ref[...]
        h_ref[...] = h0_ref[...]

    causal = _lower_tri(rows)
    lane = _iota((1, LANES), 1)
    first_half = _pair_mask()
    srow_g = _iota((LANES, LANES), 0) // SSM_STATE
    scol_h = _iota((LANES, LANES), 1) // HEAD_DIM
    ex = ex_ref[...]
    tril = causal.astype(BF16)
    assert SSM_INNER // SSM_GROUPS == LANES + HEAD_DIM and N_PAIRS == 3
    top_rows = _iota((LANES, LANES), 0) < HEAD_DIM
    ones_all = jnp.ones((LANES, LANES), BF16)
    ones_top = top_rows.astype(BF16)
    ones_bot = jnp.logical_not(top_rows).astype(BF16)

    for b in range(bb):
        zx = zx_ref[b]
        z = zx[:, :SSM_INNER]
        conv = _short_conv(zx[:, SSM_INNER:], buf_ref, b, cw_ref, SSM_CONV_W, rows)
        xbc = _silu(conv + cb_ref[...])
        xs = xbc[:, :SSM_INNER]
        bm = xbc[:, SSM_INNER:SSM_INNER + LANES]
        cm = xbc[:, SSM_INNER + LANES:]

        dt = _softplus(s_ref[b] + dtb_ref[...])
        a = dt * arow_ref[...]
        acum = _sel_dot(tril, a)
        a_last = acum[rows - 1:rows, :]
        dt_x = _dot_sel(dt, ex)
        ea_x = _dot_sel(jnp.exp(acum), ex)
        te_x = _dot_sel(jnp.exp(a_last - acum), ex)
        cd_x = _dot_sel(jnp.broadcast_to(jnp.exp(a_last), (SUBLANES, LANES)), ex)[0:1, :]

        scores = []
        for g in range(SSM_GROUPS):
            cm_g = jnp.where(lane // SSM_STATE == g, cm, 0.0)
            scores.append(_dot_nt(cm_g, bm))
        bm_t = jnp.transpose(bm)

        ys = []
        for p in range(N_PAIRS):
            sl = slice(p * LANES, (p + 1) * LANES)
            x_p = xs[:, sl]
            xdt = x_p * dt_x[:, sl]
            yd = []
            for hh in range(2):
                h = 2 * p + hh
                col = jnp.broadcast_to(acum[:, h:h + 1], (rows, rows))
                dec = jnp.exp(jnp.where(causal, col - jnp.transpose(col), -1e30))
                yd.append(_dot(scores[h // (SSM_HEADS // SSM_GROUPS)] * dec, xdt))
            y_diag = jnp.where(first_half, yd[0], yd[1])
            h_p = h_ref[b, p]
            y_off = _dot(cm, h_p) * ea_x[:, sl]
            keep = srow_g == (2 * p + scol_h) // (SSM_HEADS // SSM_GROUPS)
            upd = _dot(bm_t, xdt * te_x[:, sl])
            h_ref[b, p] = h_p * cd_x[:, sl] + jnp.where(keep, upd, 0.0)
            ys.append(y_diag + y_off + dx_ref[:, sl] * x_p)
        y = jnp.concatenate(ys, axis=-1) * _silu(z)
        sq = [jnp.square(y[:, p * LANES:(p + 1) * LANES]) for p in range(N_PAIRS)]
        g0 = _dot_sel(sq[0], ones_all) + _dot_sel(sq[1], ones_top)
        g1 = _dot_sel(sq[1], ones_bot) + _dot_sel(sq[2], ones_all)
        ms = jnp.concatenate([g0, jnp.where(first_half, g0, g1), g1], axis=-1) * (1.0 / (SSM_INNER // SSM_GROUPS))
        y = y * lax.rsqrt(ms + EPS) * ng_ref[...]
        y_ref[b] = y.astype(y_ref.dtype)

    @pl.when(c == pl.num_programs(1) - 1)
    def _():
        hout_ref[...] = h_ref[...]


def _expand_matrix(first_lane):
    m = np.zeros((LANES, SSM_INNER), np.float32)
    for h in range(SSM_HEADS):
        m[first_lane + h, h * HEAD_DIM:(h + 1) * HEAD_DIM] = 1.0
    return jnp.asarray(m, BF16)


def _lane_row(vals, first_lane):
    return jnp.zeros((1, LANES), F32).at[0, first_lane:first_lane + vals.shape[0]].set(vals.astype(F32))


def _ssd(zx, small, hist, h0, cw, cb, dt_bias, a_log, d_skip, norm_g, rows, bb):
    bsz, t, _ = zx.shape
    hist_p = jnp.pad(hist, ((0, 0), (HIST_PAD - (SSM_CONV_W - 1), 0), (0, 0)))
    dtb = _lane_row(dt_bias, 0)
    arow = _lane_row(-jnp.exp(a_log.astype(F32)), 0)
    dx = jnp.repeat(d_skip.astype(F32), HEAD_DIM)[None, :]
    full = lambda v: pl.BlockSpec(v.shape, lambda i, j: (0,) * v.ndim)
    blk = lambda c: pl.BlockSpec((bb, rows, c), lambda i, j: (i, j, 0))
    st = pl.BlockSpec((bb, N_PAIRS, LANES, LANES), lambda i, j: (i, 0, 0, 0))
    consts = (cw, cb[None, :], dtb, arow, dx, norm_g[None, :], _expand_matrix(0))
    return pl.pallas_call(
        functools.partial(_ssd_kernel, rows=rows, bb=bb),
        grid=(bsz // bb, t // rows),
        in_specs=[blk(zx.shape[-1]), blk(LANES),
                  pl.BlockSpec((bb, HIST_PAD, SSM_XBC), lambda i, j: (i, 0, 0)), st]
                 + [full(v) for v in consts],
        out_specs=[blk(SSM_INNER), st],
        out_shape=[jax.ShapeDtypeStruct((bsz, t, SSM_INNER), BF16),
                   jax.ShapeDtypeStruct((bsz, N_PAIRS, LANES, LANES), F32)],
        scratch_shapes=[pltpu.VMEM((bb, rows + HIST_PAD, SSM_XBC), F32),
                        pltpu.VMEM((bb, N_PAIRS, LANES, LANES), F32)],
        compiler_params=_cparams("arbitrary", "arbitrary"),
        name="ssd",
    )(zx, small, hist_p, h0, *consts)


def _ssd_state_in(h):
    bsz = h.shape[0]
    out = jnp.zeros((bsz, N_PAIRS, SSM_GROUPS, SSM_STATE, 2, HEAD_DIM), F32)
    for hd in range(SSM_HEADS):
        g = hd // (SSM_HEADS // SSM_GROUPS)
        out = out.at[:, hd // 2, g, :, hd % 2, :].set(jnp.swapaxes(h[:, hd], 1, 2))
    return out.reshape(bsz, N_PAIRS, LANES, LANES)


def _ssd_state_out(hp):
    bsz = hp.shape[0]
    hp = hp.reshape(bsz, N_PAIRS, SSM_GROUPS, SSM_STATE, 2, HEAD_DIM)
    heads = [jnp.swapaxes(hp[:, hd // 2, hd // (SSM_HEADS // SSM_GROUPS), :, hd % 2, :], 1, 2)
             for hd in range(SSM_HEADS)]
    return jnp.stack(heads, axis=1)


BETA_LANE = 6
DECAY_LANE = 12


def _bdot(a, b):
    return lax.dot_general(a.astype(BF16), b.astype(BF16), (((2,), (1,)), ((0,), (0,))),
                           preferred_element_type=F32)


def _bdot_nt(a, b):
    return lax.dot_general(a.astype(BF16), b.astype(BF16), (((2,), (2,)), ((0,), (0,))),
                           preferred_element_type=F32)


def _unit_lower_inverse(m, block):
    rdim = m.shape[-1]
    eye = (_iota((rdim, rdim), 0) == _iota((rdim, rdim), 1)).astype(F32)
    x = -m
    t = eye + x
    p = x
    for _ in range(int(math.log2(block)) - 2):
        p = _bdot(p, p)
        t = t + _bdot(t, p)
    resid = (eye - t) - _bdot(m, t)
    return t + _bdot(t, resid)


def _gdn_kernel(qg_ref, s_ref, hist_ref, s0_ref, cw_ref, bias_ref, arow_ref, ng_ref, eb_ref, eg_ref,
                y_ref, sout_ref, buf_ref, st_ref, *, tt, ck, bb):
    c = pl.program_id(1)
    nck = tt // ck
    rdim = 2 * ck

    @pl.when(c == 0)
    def _():
        buf_ref[:, 0:HIST_PAD, :] = hist_ref[...]
        st_ref[...] = s0_ref[...]

    r_i = _iota((rdim, rdim), 0)
    c_i = _iota((rdim, rdim), 1)
    same_blk = (r_i // ck) == (c_i // ck)
    incl = same_blk & (r_i >= c_i)
    strict = same_blk & (r_i > c_i)
    first_half = _pair_mask()
    same_head = (_iota((LANES, LANES), 0) // HEAD_DIM) == (_iota((LANES, LANES), 1) // HEAD_DIM)
    tr = _iota((tt, tt), 0)
    tc = _iota((tt, tt), 1)
    blk_tril = (((tr // ck) == (tc // ck)) & (tr >= tc)).astype(BF16)
    eb = eb_ref[...]
    eg = eg_ref[...]
    hm = same_head.astype(BF16)
    head_sumsq = lambda x: jnp.concatenate(
        [_dot_sel(jnp.square(x[:, p * LANES:(p + 1) * LANES]), hm) for p in range(N_PAIRS)], axis=-1)

    tiles = {name: [] for name in ('q', 'k', 'kb', 'rhs', 'qd', 'kd', 'col', 'cd')}
    gates = []
    for b in range(bb):
        qg = qg_ref[b]
        gates.append(qg[:, GDN_QKV:])
        qkv = _silu(_short_conv(qg[:, :GDN_QKV], buf_ref, b, cw_ref, GDN_CONV_W, tt))
        q = qkv[:, :GDN_INNER]
        k = qkv[:, GDN_INNER:2 * GDN_INNER]
        v = qkv[:, 2 * GDN_INNER:]
        q = q * lax.rsqrt(head_sumsq(q) + EPS) * (HEAD_DIM ** -0.5)
        k = k * lax.rsqrt(head_sumsq(k) + EPS)
        s = s_ref[b]
        beta = _sigmoid(s)
        g = _softplus(s + bias_ref[...]) * arow_ref[...]
        gc = _sel_dot(blk_tril, g)
        g_last = jnp.concatenate(
            [jnp.broadcast_to(gc[(i + 1) * ck - 1:(i + 1) * ck, :], (ck, LANES)) for i in range(nck)], axis=0)
        beta_x = _dot_sel(beta, eb)
        eg_x = _dot_sel(jnp.exp(gc), eg)
        kd_x = _dot_sel(jnp.exp(g_last - gc), eg)
        cd_x = _dot_sel(jnp.exp(g_last), eg)
        kb = k * beta_x
        full = dict(q=q, k=k, kb=kb, qd=q * eg_x, kd=k * kd_x, cd=cd_x)
        vb = v * beta_x
        kbe = kb * eg_x
        for i in range(nck):
            rs = slice(i * ck, (i + 1) * ck)
            for p in range(N_PAIRS):
                sl = slice(p * LANES, (p + 1) * LANES)
                for name, arr in full.items():
                    tiles[name].append(arr[rs, sl])
                tiles['rhs'].append(jnp.concatenate([vb[rs, sl], kbe[rs, sl]], axis=-1))
                lane0 = DECAY_LANE + 2 * p
                tiles['col'].append(jnp.concatenate(
                    [jnp.broadcast_to(gc[rs, lane0 + hh:lane0 + hh + 1], (ck, rdim)) for hh in range(2)], axis=0))

    st = lambda name: jnp.stack(tiles[name], axis=0)
    stack2 = lambda x: jnp.concatenate([jnp.where(first_half, x, 0.0), jnp.where(first_half, 0.0, x)], axis=1)
    k_st = stack2(st('k'))
    col = st('col')
    diff = col - jnp.swapaxes(col, 1, 2)
    dec = jnp.exp(jnp.where(incl, diff, -1e30))
    m = _bdot_nt(stack2(st('kb')), k_st) * jnp.where(strict, dec, 0.0)
    t_inv = _unit_lower_inverse(m, ck)
    rhs = st('rhs')
    sol = _bdot(t_inv, jnp.concatenate([rhs, rhs], axis=1))
    u = jnp.where(first_half, sol[:, :ck, :LANES], sol[:, ck:, :LANES])
    w = jnp.where(first_half, sol[:, :ck, LANES:], sol[:, ck:, LANES:])
    attn = _bdot_nt(stack2(st('q')), k_st) * dec
    qd = st('qd')
    kd_t = jnp.swapaxes(st('kd'), 1, 2)
    cd = st('cd')

    gsel = lambda x, i: jnp.stack([x[(b * nck + i) * N_PAIRS + p] for b in range(bb) for p in range(N_PAIRS)], axis=0)
    state = st_ref[...].reshape(bb * N_PAIRS, LANES, LANES)
    o_chunks = []
    for i in range(nck):
        v_new = gsel(u, i) - _bdot(gsel(w, i), state)
        intra = _bdot(gsel(attn, i), jnp.concatenate([v_new, v_new], axis=1))
        o_chunks.append(_bdot(gsel(qd, i), state) + jnp.where(first_half, intra[:, :ck], intra[:, ck:]))
        upd = _bdot(gsel(kd_t, i), v_new)
        state = state * gsel(cd, i)[:, 0:1, :] + jnp.where(same_head, upd, 0.0)
    st_ref[...] = state.reshape(bb, N_PAIRS, LANES, LANES)

    for b in range(bb):
        o = jnp.concatenate(
            [jnp.concatenate([o_chunks[i][b * N_PAIRS + p] for p in range(N_PAIRS)], axis=-1) for i in range(nck)],
            axis=0)
        ms = head_sumsq(o) * (1.0 / HEAD_DIM)
        o = o * lax.rsqrt(ms + EPS) * ng_ref[...] * _silu(gates[b])
        y_ref[b] = o.astype(y_ref.dtype)

    @pl.when(c == pl.num_programs(1) - 1)
    def _():
        sout_ref[...] = st_ref[...]


def _gdn(qg, small, hist, s0, cw, a_log, dt_bias, norm_g, tt, ck, bb):
    bsz, t, _ = qg.shape
    hist_p = jnp.pad(hist, ((0, 0), (HIST_PAD - (GDN_CONV_W - 1), 0), (0, 0)))
    bias = _lane_row(dt_bias, DECAY_LANE)
    arow = _lane_row(-jnp.exp(a_log.astype(F32)), DECAY_LANE)
    ng = jnp.tile(norm_g.astype(F32), GDN_HEADS)[None, :]
    full = lambda v: pl.BlockSpec(v.shape, lambda i, j: (0,) * v.ndim)
    blk = lambda c: pl.BlockSpec((bb, tt, c), lambda i, j: (i, j, 0))
    st = pl.BlockSpec((bb, N_PAIRS, LANES, LANES), lambda i, j: (i, 0, 0, 0))
    consts = (cw, bias, arow, ng, _expand_matrix(BETA_LANE), _expand_matrix(DECAY_LANE))
    return pl.pallas_call(
        functools.partial(_gdn_kernel, tt=tt, ck=ck, bb=bb),
        grid=(bsz // bb, t // tt),
        in_specs=[blk(qg.shape[-1]), blk(LANES),
                  pl.BlockSpec((bb, HIST_PAD, GDN_QKV), lambda i, j: (i, 0, 0)), st]
                 + [full(v) for v in consts],
        out_specs=[blk(GDN_INNER), st],
        out_shape=[jax.ShapeDtypeStruct((bsz, t, GDN_INNER), BF16),
                   jax.ShapeDtypeStruct((bsz, N_PAIRS, LANES, LANES), F32)],
        scratch_shapes=[pltpu.VMEM((bb, tt + HIST_PAD, GDN_QKV), F32),
                        pltpu.VMEM((bb, N_PAIRS, LANES, LANES), F32)],
        compiler_params=_cparams("arbitrary", "arbitrary"),
        name="gdn",
    )(qg, small, hist_p, s0, *consts)


def _gdn_state_in(s):
    bsz = s.shape[0]
    out = jnp.zeros((bsz, N_PAIRS, 2, HEAD_DIM, 2, HEAD_DIM), F32)
    for hd in range(GDN_HEADS):
        out = out.at[:, hd // 2, hd % 2, :, hd % 2, :].set(s[:, hd])
    return out.reshape(bsz, N_PAIRS, LANES, LANES)


def _gdn_state_out(sp):
    bsz = sp.shape[0]
    sp = sp.reshape(bsz, N_PAIRS, 2, HEAD_DIM, 2, HEAD_DIM)
    return jnp.stack([sp[:, hd // 2, hd % 2, :, hd % 2, :] for hd in range(GDN_HEADS)], axis=1)


def _mix_residual(x_ref, ya_ref, yb_ref, yc_ref, woa_ref, wob_ref, woc_ref):
    mix = (jnp.dot(ya_ref[...], woa_ref[...], preferred_element_type=F32)
           + jnp.dot(yb_ref[...], wob_ref[...], preferred_element_type=F32)
           + jnp.dot(yc_ref[...], woc_ref[...], preferred_element_type=F32))
    return x_ref[...] + mix


def _rms(x, g):
    return x * lax.rsqrt(jnp.mean(x * x, axis=-1, keepdims=True) + EPS) * g


def _ffn_kernel(x_ref, ya_ref, yb_ref, yc_ref, woa_ref, wob_ref, woc_ref, g_ref, wg_ref, wu_ref, wd_ref,
                o_ref, xn_ref, h_ref, acc_ref):
    j = pl.program_id(1)

    @pl.when(j == 0)
    def _():
        xn = _mix_residual(x_ref, ya_ref, yb_ref, yc_ref, woa_ref, wob_ref, woc_ref)
        xn_ref[...] = xn
        h_ref[...] = _rms(xn, g_ref[...]).astype(BF16)
        acc_ref[...] = jnp.zeros_like(acc_ref)

    h = h_ref[...]
    act = _silu(jnp.dot(h, wg_ref[...], preferred_element_type=F32)) * jnp.dot(h, wu_ref[...], preferred_element_type=F32)
    acc_ref[...] += jnp.dot(act.astype(BF16), wd_ref[...], preferred_element_type=F32)

    @pl.when(j == pl.num_programs(1) - 1)
    def _():
        o_ref[...] = xn_ref[...] + acc_ref[...]


FFN_CHUNKS = 2


def _out_ffn(x2d, ya, yb, yc, wo, g, wg, wu, wd):
    n = x2d.shape[0]
    tm = _row_tile(n, 512)
    f = wg.shape[1]
    tf = f // FFN_CHUNKS
    rows = lambda c: pl.BlockSpec((tm, c), lambda i, j: (i, 0))
    full = lambda w: pl.BlockSpec(w.shape, lambda i, j: (0, 0))
    return pl.pallas_call(
        _ffn_kernel,
        grid=(n // tm, FFN_CHUNKS),
        in_specs=[rows(D_MODEL), rows(ya.shape[1]), rows(yb.shape[1]), rows(yc.shape[1]),
                  full(wo[0]), full(wo[1]), full(wo[2]), full(g),
                  pl.BlockSpec((D_MODEL, tf), lambda i, j: (0, j)),
                  pl.BlockSpec((D_MODEL, tf), lambda i, j: (0, j)),
                  pl.BlockSpec((tf, D_MODEL), lambda i, j: (j, 0))],
        out_specs=rows(D_MODEL),
        out_shape=jax.ShapeDtypeStruct((n, D_MODEL), F32),
        scratch_shapes=[pltpu.VMEM((tm, D_MODEL), F32), pltpu.VMEM((tm, D_MODEL), BF16),
                        pltpu.VMEM((tm, D_MODEL), F32)],
        compiler_params=_cparams("parallel", "arbitrary"),
        name="out_ffn",
    )(x2d, ya, yb, yc, wo[0], wo[1], wo[2], g, wg, wu, wd)


ROUTE_TILE = 512


def _router_kernel(x_ref, ya_ref, yb_ref, yc_ref, woa_ref, wob_ref, woc_ref, g_ref, wr_ref, br_ref,
                   xn_ref, h_ref, gate_ref, slot_ref, cnt_ref, run_ref, *, cap):
    i = pl.program_id(0)

    @pl.when(i == 0)
    def _():
        run_ref[...] = jnp.zeros_like(run_ref)

    xn = _mix_residual(x_ref, ya_ref, yb_ref, yc_ref, woa_ref, wob_ref, woc_ref)
    xn_ref[...] = xn
    h = _rms(xn, g_ref[...])
    h_ref[...] = h
    tm = h.shape[0]
    lane = _iota((1, LANES), 1)
    logits = jnp.where(lane < N_EXPERTS, _dot_hp(h, wr_ref[...]) + br_ref[...], -jnp.inf)
    m1 = jnp.max(logits, axis=-1, keepdims=True)
    i1 = jnp.min(jnp.where(logits == m1, lane, LANES), axis=-1, keepdims=True)
    rest = jnp.where(lane == i1, -jnp.inf, logits)
    m2 = jnp.max(rest, axis=-1, keepdims=True)
    i2 = jnp.min(jnp.where(rest == m2, lane, LANES), axis=-1, keepdims=True)
    e2 = jnp.exp(m2 - m1)
    inv = 1.0 / (1.0 + e2)
    gate_ref[...] = jnp.where(lane == 0, inv, 0.0) + jnp.where(lane == 1, e2 * inv, 0.0)

    chosen = jnp.where((lane == i1) | (lane == i2), 1.0, 0.0)
    before = _iota((tm, tm), 0) > _iota((tm, tm), 1)
    rank = jnp.dot(before.astype(BF16), chosen.astype(BF16), preferred_element_type=F32) + run_ref[0:1, :]
    base = lane.astype(F32) * float(cap)
    pick = lambda idx: jnp.sum(jnp.where(lane == idx, rank + base, 0.0), axis=-1, keepdims=True)
    slots = jnp.where(lane == 0, pick(i1), 0.0) + jnp.where(lane == 1, pick(i2), 0.0)
    slots_t = jnp.transpose(slots).astype(jnp.int32)
    for k in range(2):
        slot_ref[0, k] = jnp.concatenate(
            [slots_t[k:k + 1, c * LANES:(c + 1) * LANES] for c in range(tm // LANES)], axis=0)
    run_ref[...] = run_ref[...] + jnp.sum(chosen, axis=0, keepdims=True)
    cnt_ref[...] = run_ref[...].astype(jnp.int32)


def _out_router(x2d, ya, yb, yc, wo, g, w_router, b_router, cap):
    n = x2d.shape[0]
    tm = _row_tile(n, ROUTE_TILE)
    wr = jnp.zeros((D_MODEL, LANES), F32).at[:, :N_EXPERTS].set(w_router.astype(F32))
    br = _lane_row(b_router, 0)
    rows = lambda c: pl.BlockSpec((tm, c), lambda i: (i, 0))
    full = lambda w: pl.BlockSpec(w.shape, lambda i: (0, 0))
    return pl.pallas_call(
        functools.partial(_router_kernel, cap=cap),
        grid=(n // tm,),
        in_specs=[rows(D_MODEL), rows(ya.shape[1]), rows(yb.shape[1]), rows(yc.shape[1]),
                  full(wo[0]), full(wo[1]), full(wo[2]), full(g), full(wr), full(br)],
        out_specs=[rows(D_MODEL), rows(D_MODEL), rows(LANES),
                   pl.BlockSpec((1, 2, tm // LANES, LANES), lambda i: (i, 0, 0, 0)),
                   pl.BlockSpec((SUBLANES, LANES), lambda i: (0, 0))],
        out_shape=[jax.ShapeDtypeStruct((n, D_MODEL), F32), jax.ShapeDtypeStruct((n, D_MODEL), F32),
                   jax.ShapeDtypeStruct((n, LANES), F32),
                   jax.ShapeDtypeStruct((n // tm, 2, tm // LANES, LANES), jnp.int32),
                   jax.ShapeDtypeStruct((SUBLANES, LANES), jnp.int32)],
        scratch_shapes=[pltpu.VMEM((SUBLANES, LANES), F32)],
        compiler_params=_cparams("arbitrary"),
        name="out_router",
    )(x2d, ya, yb, yc, wo[0], wo[1], wo[2], g, wr, br)


def _row_copy(src_ref, src_row, dst_ref, dst_row, sem):
    return pltpu.make_async_copy(src_ref.at[pl.ds(src_row, 1)], dst_ref.at[pl.ds(dst_row, 1)], sem)


def _slot_fetch(slots_hbm, tile, smem_ref, buf, sem):
    return pltpu.make_async_copy(slots_hbm.at[tile], smem_ref.at[buf], sem.at[buf])


def _for_rows(tm, fn):
    for c in range(tm // LANES):
        for base in range(0, LANES, SUBLANES):
            for j in range(SUBLANES):
                fn(c, base, j)


def _vmem_row(ref, base, j):
    return ref.at[pl.ds(base, SUBLANES)].at[pl.ds(j, 1)]


def _dispatch_kernel(cnt_ref, slots_hbm, h_ref, hs_hbm, slot_smem, zero_ref, slot_sem, row_sem, pad_sem,
                     *, tm, cap, bm):
    i = pl.program_id(0)
    n_tiles = pl.num_programs(0)

    @pl.when(i == 0)
    def _():
        _slot_fetch(slots_hbm, 0, slot_smem, 0, slot_sem).start()

    @pl.when(i + 1 < n_tiles)
    def _():
        _slot_fetch(slots_hbm, i + 1, slot_smem, (i + 1) % 2, slot_sem).start()

    _slot_fetch(slots_hbm, i, slot_smem, i % 2, slot_sem).wait()

    def issue(c, base, j):
        for k in range(2):
            dst = hs_hbm.at[pl.ds(slot_smem[i % 2, k, c, base + j], 1)]
            pltpu.make_async_copy(_vmem_row(h_ref, c * LANES + base, j), dst, row_sem).start(priority=k)

    _for_rows(tm, issue)

    def drain(r, carry):
        for k in range(2):
            _row_copy(h_ref, 0, hs_hbm, 0, row_sem).wait()
        return carry

    lax.fori_loop(0, tm, drain, 0, unroll=8)

    @pl.when(i == n_tiles - 1)
    def _():
        zero_ref[...] = jnp.zeros_like(zero_ref)
        for e in range(N_EXPERTS):
            c = cnt_ref[e]
            n_pad = ((c + bm - 1) // bm) * bm - c

            def fill(r, carry):
                _row_copy(zero_ref, 0, hs_hbm, e * cap + c + r, pad_sem).start()
                return carry

            def fill_wait(r, carry):
                _row_copy(zero_ref, 0, hs_hbm, 0, pad_sem).wait()
                return carry

            lax.fori_loop(0, n_pad, fill, 0)
            lax.fori_loop(0, n_pad, fill_wait, 0)


def _dispatch(counts, slots, h, cap, bm):
    n = h.shape[0]
    n_tiles = slots.shape[0]
    tm = slots.shape[2] * LANES
    return pl.pallas_call(
        functools.partial(_dispatch_kernel, tm=tm, cap=cap, bm=bm),
        grid_spec=pltpu.PrefetchScalarGridSpec(
            num_scalar_prefetch=1,
            grid=(n_tiles,),
            in_specs=[pl.BlockSpec(memory_space=pl.ANY), pl.BlockSpec((tm, D_MODEL), lambda i, cnt: (i, 0))],
            out_specs=pl.BlockSpec(memory_space=pl.ANY),
            scratch_shapes=[pltpu.SMEM((2, 2, tm // LANES, LANES), jnp.int32), pltpu.VMEM((SUBLANES, D_MODEL), F32),
                            pltpu.SemaphoreType.DMA((2,)), pltpu.SemaphoreType.DMA, pltpu.SemaphoreType.DMA],
        ),
        out_shape=jax.ShapeDtypeStruct((N_EXPERTS * cap, D_MODEL), F32),
        compiler_params=_cparams("arbitrary"),
        name="moe_dispatch",
    )(counts, slots, h)


def _experts_kernel(tbl_ref, hs_ref, wg_ref, wu_ref, wd_ref, ys_ref, hb_ref, acc_ref):
    s = pl.program_id(0)
    j = pl.program_id(1)

    @pl.when(tbl_ref[2, s] == 1)
    def _():
        @pl.when(j == 0)
        def _():
            hb_ref[...] = hs_ref[...].astype(BF16)
            acc_ref[...] = jnp.zeros_like(acc_ref)

        h = hb_ref[...]
        act = (_silu(jnp.dot(h, wg_ref[0], preferred_element_type=F32))
               * jnp.dot(h, wu_ref[0], preferred_element_type=F32))
        acc_ref[...] += jnp.dot(act.astype(BF16), wd_ref[0], preferred_element_type=F32)

        @pl.when(j == pl.num_programs(1) - 1)
        def _():
            ys_ref[...] = acc_ref[...]


def _block_table(counts, cap, bm, n_steps):
    nblk = (counts + bm - 1) // bm
    cum = jnp.cumsum(nblk)
    total = cum[-1]
    step = jnp.arange(n_steps, dtype=jnp.int32)
    last = jnp.maximum(total - 1, 0)
    eff = jnp.minimum(step, last)
    expert = jnp.minimum(jnp.searchsorted(cum, eff, side='right'), N_EXPERTS - 1).astype(jnp.int32)
    blk = eff - (cum[expert] - nblk[expert])
    return jnp.stack([expert * (cap // bm) + blk, expert, (step < total).astype(jnp.int32)]).astype(jnp.int32)


def _experts(counts, hs, wg, wu, wd, cap, bm, n_assign):
    n_steps = n_assign // bm + N_EXPERTS
    tbl = _block_table(counts, cap, bm, n_steps)
    f = wg.shape[2]
    tf = f // FFN_CHUNKS
    chunk = lambda s, j, t: jnp.where(t[2, s] == 1, j, FFN_CHUNKS - 1)
    rows = pl.BlockSpec((bm, D_MODEL), lambda s, j, t: (t[0, s], 0))
    return pl.pallas_call(
        _experts_kernel,
        grid_spec=pltpu.PrefetchScalarGridSpec(
            num_scalar_prefetch=1,
            grid=(n_steps, FFN_CHUNKS),
            in_specs=[rows,
                      pl.BlockSpec((1, D_MODEL, tf), lambda s, j, t: (t[1, s], 0, chunk(s, j, t))),
                      pl.BlockSpec((1, D_MODEL, tf), lambda s, j, t: (t[1, s], 0, chunk(s, j, t))),
                      pl.BlockSpec((1, tf, D_MODEL), lambda s, j, t: (t[1, s], chunk(s, j, t), 0))],
            out_specs=rows,
            scratch_shapes=[pltpu.VMEM((bm, D_MODEL), BF16), pltpu.VMEM((bm, D_MODEL), F32)],
        ),
        out_shape=jax.ShapeDtypeStruct(hs.shape, F32),
        compiler_params=_cparams("arbitrary", "arbitrary"),
        name="moe_experts",
    )(tbl, hs, wg, wu, wd)


def _combine_kernel(slots_hbm, ys_hbm, xn_ref, gate_ref, gf_ref, o_ref, slot_smem, ybuf_ref, slot_sem, row_sem,
                    *, tm):
    i = pl.program_id(0)
    n_tiles = pl.num_programs(0)

    def gather(tile, buf):
        def issue(c, base, j):
            for k in range(2):
                src = ys_hbm.at[pl.ds(slot_smem[buf, k, c, base + j], 1)]
                dst = _vmem_row(ybuf_ref.at[buf, k], c * LANES + base, j)
                pltpu.make_async_copy(src, dst, row_sem.at[buf]).start(priority=k)
        _for_rows(tm, issue)

    @pl.when(i == 0)
    def _():
        first = _slot_fetch(slots_hbm, 0, slot_smem, 0, slot_sem)
        first.start()
        first.wait()
        gather(0, 0)

        @pl.when(n_tiles > 1)
        def _():
            _slot_fetch(slots_hbm, 1, slot_smem, 1, slot_sem).start()

    @pl.when(i + 1 < n_tiles)
    def _():
        _slot_fetch(slots_hbm, i + 1, slot_smem, (i + 1) % 2, slot_sem).wait()
        gather(i + 1, (i + 1) % 2)

    def drain(r, carry):
        for k in range(2):
            _row_copy(ys_hbm, 0, ybuf_ref.at[i % 2, k], 0, row_sem.at[i % 2]).wait()
        return carry

    lax.fori_loop(0, tm, drain, 0, unroll=8)

    @pl.when(i + 2 < n_tiles)
    def _():
        _slot_fetch(slots_hbm, i + 2, slot_smem, i % 2, slot_sem).start()

    g = gate_ref[...]
    y = g[:, 0:1] * ybuf_ref[i % 2, 0] + g[:, 1:2] * ybuf_ref[i % 2, 1]
    o_ref[...] = _rms(xn_ref[...] + y, gf_ref[...])


def _combine(slots, ys, xn, gates, g_final):
    n = xn.shape[0]
    n_tiles = slots.shape[0]
    tm = slots.shape[2] * LANES
    rows = lambda c: pl.BlockSpec((tm, c), lambda i: (i, 0))
    return pl.pallas_call(
        functools.partial(_combine_kernel, tm=tm),
        grid=(n_tiles,),
        in_specs=[pl.BlockSpec(memory_space=pl.ANY), pl.BlockSpec(memory_space=pl.ANY),
                  rows(D_MODEL), rows(LANES), pl.BlockSpec(g_final.shape, lambda i: (0, 0))],
        out_specs=rows(D_MODEL),
        out_shape=jax.ShapeDtypeStruct((n, D_MODEL), F32),
        scratch_shapes=[pltpu.SMEM((2, 2, tm // LANES, LANES), jnp.int32), pltpu.VMEM((2, 2, tm, D_MODEL), F32),
                        pltpu.SemaphoreType.DMA((2,)), pltpu.SemaphoreType.DMA((2,))],
        compiler_params=_cparams("arbitrary"),
        name="moe_combine",
    )(slots, ys, xn, gates, g_final)


def _moe(x2d, ya, yb, yc, wo, g_ffn, w_router, b_router, wg, wu, wd, g_final):
    n = x2d.shape[0]
    bm = _row_tile(n, 512)
    cap = n
    xn, h, gates, slots, counts = _out_router(x2d, ya, yb, yc, wo, g_ffn, w_router, b_router, cap)
    counts = counts[0, :N_EXPERTS]
    hs = _dispatch(counts, slots, h, cap, bm)
    ys = _experts(counts, hs, wg, wu, wd, cap, bm, 2 * n)
    return _combine(slots, ys, xn, gates, g_final)


IN_SIZES = (2 * CONV_CH, SSM_INNER, SSM_XBC, SSM_HEADS, GDN_QKV, GDN_INNER, GDN_HEADS, GDN_HEADS)


def _prep_layer(l, p):
    off = np.concatenate([[0], np.cumsum(IN_SIZES)])
    w_in = p['w_in_bf16'][l]
    col = lambda i: w_in[:, off[i]:off[i + 1]]
    a_in, z, xbc, dt, qkv, gate, b_raw, a_raw = (col(i) for i in range(8))
    small = jnp.concatenate([dt, b_raw, a_raw, jnp.zeros((D_MODEL, SMALL_W - 3 * SSM_HEADS), BF16)], axis=1)
    wo = p['w_out'][l].astype(BF16)
    return dict(
        wa=a_in,
        wzx=jnp.concatenate([z, xbc], axis=1),
        wqg=jnp.concatenate([qkv, gate], axis=1),
        ws=small,
        wo=(wo[:CONV_CH], wo[CONV_CH:CONV_CH + SSM_INNER], wo[CONV_CH + SSM_INNER:]),
    )


def _trunk(x, st_conv_a, st_ssm_conv, st_ssm, st_gdn_conv, st_gdn, p, prep, ssd_rows, gdn_rows, bb):
    bsz, t, _ = x.shape
    n = bsz * t
    depth = p['g_mix'].shape[0]
    x2d = x.reshape(n, D_MODEL)
    new = [[] for _ in range(5)]
    for l in range(depth):
        w = prep[l]
        a_in, zx, qg, small = _norm_proj(x2d, p['g_mix'][l][None, :], w['wa'], w['wzx'], w['wqg'], w['ws'])
        a_in = a_in.reshape(bsz, t, -1)
        zx = zx.reshape(bsz, t, -1)
        qg = qg.reshape(bsz, t, -1)
        small = small.reshape(bsz, t, -1)
        ya, conv_a = _conv_a(a_in, st_conv_a[l], p['conv_a_w'][l], p['conv_a_b'][l][None, :],
                             p['ln_a_g'][l][None, :], p['ln_a_b'][l][None, :])
        yb, ssm = _ssd(zx, small, st_ssm_conv[l], _ssd_state_in(st_ssm[l]), p['ssm_conv_w'][l], p['ssm_conv_b'][l],
                       p['ssm_dt_bias'][l], p['ssm_a_log'][l], p['ssm_d'][l], p['ssm_norm_g'][l], ssd_rows, bb)
        yc, gdn = _gdn(qg, small, st_gdn_conv[l], _gdn_state_in(st_gdn[l]), p['gdn_conv_w'][l],
                       p['gdn_a_log'][l], p['gdn_dt_bias'][l], p['gdn_norm_g'][l], gdn_rows[0], gdn_rows[1], bb)
        assert t >= max(SSM_CONV_W, GDN_CONV_W) - 1
        new[0].append(conv_a)
        new[1].append(zx[:, t - (SSM_CONV_W - 1):, SSM_INNER:])
        new[2].append(_ssd_state_out(ssm))
        new[3].append(qg[:, t - (GDN_CONV_W - 1):, :GDN_QKV])
        new[4].append(_gdn_state_out(gdn))
        flat = lambda y: y.reshape(n, -1)
        g_ffn = p['g_ffn'][l][None, :]
        if l % 2 == 0:
            x2d = _out_ffn(x2d, flat(ya), flat(yb), flat(yc), w['wo'], g_ffn,
                           prep['ffn'][l // 2][0], prep['ffn'][l // 2][1], prep['ffn'][l // 2][2])
        else:
            x2d = _moe(x2d, flat(ya), flat(yb), flat(yc), w['wo'], g_ffn, p['moe_w_router'][l // 2],
                       p['moe_b_router'][l // 2], *prep['moe'][l // 2], p['g_final'][None, :])
    return (x2d.reshape(bsz, t, D_MODEL),) + tuple(jnp.stack(s) for s in new)


def kernel(x_prompt, x_sample, state_conv_a, state_ssm_conv, state_ssm, state_gdn_conv, state_gdn, g_mix, w_in, conv_a_w, conv_a_b, ln_a_g, ln_a_b, ssm_conv_w, ssm_conv_b, ssm_dt_bias, ssm_a_log, ssm_d, ssm_norm_g, gdn_conv_w, gdn_a_log, gdn_dt_bias, gdn_norm_g, w_out, g_ffn, ffn_w_gate, ffn_w_up, ffn_w_down, moe_w_router, moe_b_router, moe_w_gate, moe_w_up, moe_w_down, g_final):
    p = dict(g_mix=g_mix, w_in=w_in, conv_a_w=conv_a_w, conv_a_b=conv_a_b, ln_a_g=ln_a_g, ln_a_b=ln_a_b,
             ssm_conv_w=ssm_conv_w, ssm_conv_b=ssm_conv_b, ssm_dt_bias=ssm_dt_bias, ssm_a_log=ssm_a_log,
             ssm_d=ssm_d, ssm_norm_g=ssm_norm_g, gdn_conv_w=gdn_conv_w, gdn_a_log=gdn_a_log,
             gdn_dt_bias=gdn_dt_bias, gdn_norm_g=gdn_norm_g, w_out=w_out, g_ffn=g_ffn,
             moe_w_router=moe_w_router, moe_b_router=moe_b_router, g_final=g_final)
    depth = g_mix.shape[0]
    assert depth % 2 == 0, "the final RMSNorm is fused into the expert layer, which must come last"
    p['w_in_bf16'] = w_in.astype(BF16)
    prep = {l: _prep_layer(l, p) for l in range(depth)}
    prep['ffn'] = [(ffn_w_gate[i].astype(BF16), ffn_w_up[i].astype(BF16), ffn_w_down[i].astype(BF16))
                   for i in range(ffn_w_gate.shape[0])]
    prep['moe'] = [(moe_w_gate[i].astype(BF16), moe_w_up[i].astype(BF16), moe_w_down[i].astype(BF16))
                   for i in range(moe_w_gate.shape[0])]
    bp, dt = x_prompt.shape[0], x_prompt.dtype
    zeros = lambda *s: jnp.zeros((depth, bp) + s, dt)
    outs_p = _trunk(x_prompt, zeros(CONV_W - 1, CONV_CH), zeros(SSM_CONV_W - 1, SSM_XBC),
                    zeros(SSM_HEADS, HEAD_DIM, SSM_STATE), zeros(GDN_CONV_W - 1, GDN_QKV),
                    zeros(GDN_HEADS, HEAD_DIM, HEAD_DIM), p, prep, ssd_rows=min(128, x_prompt.shape[1]),
                    gdn_rows=(min(256, x_prompt.shape[1]), min(64, x_prompt.shape[1])), bb=2)
    outs_s = _trunk(x_sample, state_conv_a, state_ssm_conv, state_ssm, state_gdn_conv, state_gdn, p, prep,
                    ssd_rows=x_sample.shape[1], gdn_rows=(x_sample.shape[1], x_sample.shape[1]), bb=2)
    return (outs_p[0], outs_s[0]) + outs_p[1:] + outs_s[1:]
```

```python
import functools
import math

import jax
import jax.numpy as jnp
import numpy as np
from jax import lax
from jax.experimental import pallas as pl
from jax.experimental.pallas import tpu as pltpu

F32 = jnp.float32
BF16 = jnp.bfloat16
EPS = 1e-6

LANES = 128
SUBLANES = 8
VMEM_BYTES_V7X = 64 * 1024 * 1024
VMEM_LIMIT = VMEM_BYTES_V7X * 3 // 4

D_MODEL = 1024
CONV_CH = 256
CONV_W = 31
SSM_HEADS = 6
HEAD_DIM = 64
SSM_INNER = SSM_HEADS * HEAD_DIM
SSM_STATE = 64
SSM_GROUPS = 2
SSM_XBC = SSM_INNER + 2 * SSM_GROUPS * SSM_STATE
SSM_CONV_W = 4
GDN_HEADS = 6
GDN_INNER = GDN_HEADS * HEAD_DIM
GDN_QKV = 3 * GDN_INNER
GDN_CONV_W = 4
N_PAIRS = 3
N_EXPERTS = 8
SMALL_W = LANES


def _cparams(*sem):
    return pltpu.CompilerParams(dimension_semantics=sem, vmem_limit_bytes=VMEM_LIMIT)


def _dot(a, b):
    return jnp.dot(a.astype(BF16), b.astype(BF16), preferred_element_type=F32)


def _dot_nt(a, b):
    return lax.dot_general(a.astype(BF16), b.astype(BF16), (((1,), (1,)), ((), ())),
                           preferred_element_type=F32)


def _split3(x):
    hi = x.astype(BF16)
    r1 = x - hi.astype(F32)
    mid = r1.astype(BF16)
    lo = (r1 - mid.astype(F32)).astype(BF16)
    return hi, mid, lo


def _dot_sel(x, sel_bf16, pieces=2):
    d = lambda p: jnp.dot(p, sel_bf16, preferred_element_type=F32)
    hi = x.astype(BF16)
    r1 = x - hi.astype(F32)
    mid = r1.astype(BF16)
    if pieces == 2:
        return d(hi) + d(mid)
    return d(hi) + d(mid) + d((r1 - mid.astype(F32)).astype(BF16))


def _sel_dot(sel_bf16, x):
    hi, mid, lo = _split3(x)
    d = lambda p: jnp.dot(sel_bf16, p, preferred_element_type=F32)
    return d(hi) + d(mid) + d(lo)


def _dot_hp(a, b):
    ah = a.astype(BF16)
    al = (a - ah.astype(F32)).astype(BF16)
    bh = b.astype(BF16)
    bl = (b - bh.astype(F32)).astype(BF16)
    d = lambda p, q: jnp.dot(p, q, preferred_element_type=F32)
    return d(ah, bh) + (d(ah, bl) + d(al, bh))


def _silu(x):
    return x * (1.0 / (1.0 + jnp.exp(-x)))


def _sigmoid(x):
    return 1.0 / (1.0 + jnp.exp(-x))


def _softplus(x):
    return jnp.maximum(x, 0.0) + jnp.log(1.0 + jnp.exp(-jnp.abs(x)))


def _iota(shape, dim):
    return lax.broadcasted_iota(jnp.int32, shape, dim)


def _norm_proj_kernel(x_ref, g_ref, wa_ref, wzx_ref, wqg_ref, ws_ref, hx_ref, hq_ref, cwx_ref, cbx_ref, cwq_ref,
                      a_ref, zx_ref, qg_ref, s_ref, nhx_ref, nhq_ref, bx_ref, bq_ref, *, nseq, seg, tiles_per_seq):
    first = (pl.program_id(0) % tiles_per_seq) == 0

    @pl.when(first)
    def _():
        bx_ref[:, 0:HIST_PAD, :] = hx_ref[...]
        bq_ref[:, 0:HIST_PAD, :] = hq_ref[...]

    x = x_ref[...]
    u = x * lax.rsqrt(jnp.mean(x * x, axis=-1, keepdims=True) + EPS) * g_ref[...]
    ub = u.astype(BF16)
    a_ref[...] = jnp.dot(ub, wa_ref[...], preferred_element_type=F32)
    s_ref[...] = jnp.dot(ub, ws_ref[...], preferred_element_type=F32)
    zx = jnp.dot(ub, wzx_ref[...], preferred_element_type=F32)
    qg = jnp.dot(ub, wqg_ref[...], preferred_element_type=F32)
    zx_ref[:, :SSM_INNER] = zx[:, :SSM_INNER]
    qg_ref[:, GDN_QKV:] = qg[:, GDN_QKV:]
    for s in range(nseq):
        rows = slice(s * seg, (s + 1) * seg)
        xbc = zx[rows, SSM_INNER:]
        nhx_ref[s] = _last_rows(bx_ref, s, xbc, seg)
        zx_ref[rows, SSM_INNER:] = _silu(_short_conv(xbc, bx_ref, s, cwx_ref, SSM_CONV_W, seg) + cbx_ref[...])
        qkv = qg[rows, :GDN_QKV]
        nhq_ref[s] = _last_rows(bq_ref, s, qkv, seg)
        qg_ref[rows, :GDN_QKV] = _silu(_short_conv(qkv, bq_ref, s, cwq_ref, GDN_CONV_W, seg))


def _last_rows(buf_ref, s, x, rows):
    if rows >= HIST_PAD:
        return x[rows - HIST_PAD:, :]
    return jnp.concatenate([buf_ref[s, rows:HIST_PAD, :], x], axis=0)


def _row_tile(n, want):
    t = min(want, n)
    while n % t:
        t //= 2
    return t


def _norm_proj(x, g, wa, wzx, wqg, ws, hist_x, hist_q, cwx, cbx, cwq):
    bsz, t, _ = x.shape
    n = bsz * t
    tm = _row_tile(n, 512)
    nseq = max(1, tm // t)
    seg = tm // nseq
    tiles_per_seq = max(1, t // tm)
    assert nseq * seg == tm and (t % tm == 0 or tm % t == 0)
    pad = lambda h: jnp.pad(h, ((0, 0), (HIST_PAD - h.shape[1], 0), (0, 0)))
    full = lambda w: pl.BlockSpec(w.shape, lambda i: (0, 0))
    rows = lambda c: pl.BlockSpec((tm, c), lambda i: (i, 0))
    hist = lambda c: pl.BlockSpec((nseq, HIST_PAD, c), lambda i: (i // tiles_per_seq, 0, 0))
    widths = (wa.shape[1], wzx.shape[1], wqg.shape[1], ws.shape[1])
    return pl.pallas_call(
        functools.partial(_norm_proj_kernel, nseq=nseq, seg=seg, tiles_per_seq=tiles_per_seq),
        grid=(n // tm,),
        in_specs=[rows(D_MODEL), full(g), full(wa), full(wzx), full(wqg), full(ws),
                  hist(SSM_XBC), hist(GDN_QKV), full(cwx), full(cbx), full(cwq)],
        out_specs=[rows(c) for c in widths] + [hist(SSM_XBC), hist(GDN_QKV)],
        out_shape=[jax.ShapeDtypeStruct((n, c), F32) for c in widths]
                  + [jax.ShapeDtypeStruct((bsz, HIST_PAD, SSM_XBC), F32),
                     jax.ShapeDtypeStruct((bsz, HIST_PAD, GDN_QKV), F32)],
        scratch_shapes=[pltpu.VMEM((nseq, seg + HIST_PAD, SSM_XBC), F32),
                        pltpu.VMEM((nseq, seg + HIST_PAD, GDN_QKV), F32)],
        compiler_params=_cparams("arbitrary"),
        name="norm_proj",
    )(x.reshape(n, D_MODEL), g, wa, wzx, wqg, ws, pad(hist_x), pad(hist_q), cwx, cbx, cwq)


CONV_PAD = 32
CONV_ROWS = 32


def _conv_a_kernel(a_ref, hist_ref, w_ref, b_ref, lg_ref, lb_ref, y_ref, nh_ref, buf_ref, sh_ref, *, tt):
    t = pl.program_id(1)

    @pl.when(t == 0)
    def _():
        buf_ref[0:CONV_PAD, :] = hist_ref[0]

    a = a_ref[0]
    glu = a[:, :CONV_CH] * _sigmoid(a[:, CONV_CH:])
    buf_ref[CONV_PAD:CONV_PAD + tt, :] = glu
    full = buf_ref[...]
    n = tt + CONV_PAD
    sh_ref[0] = full
    for s in range(1, SUBLANES):
        sh_ref[s] = pltpu.roll(full, n - s, axis=0)
    off = CONV_PAD - (CONV_W - 1)
    for r0 in range(0, tt, CONV_ROWS):
        acc = jnp.zeros((CONV_ROWS, CONV_CH), F32)
        for k in range(CONV_W):
            s = (off + k) % SUBLANES
            base = r0 + off + k - s
            acc = acc + w_ref[k:k + 1, :] * sh_ref[s, base:base + CONV_ROWS, :]
        y = acc + b_ref[...]
        mu = jnp.mean(y, axis=-1, keepdims=True)
        yc = y - mu
        var = jnp.mean(yc * yc, axis=-1, keepdims=True)
        y = yc * lax.rsqrt(var + EPS) * lg_ref[...] + lb_ref[...]
        y_ref[0, r0:r0 + CONV_ROWS, :] = _silu(y).astype(y_ref.dtype)
    nh_ref[0] = buf_ref[tt:tt + CONV_PAD, :]
    buf_ref[0:CONV_PAD, :] = buf_ref[tt:tt + CONV_PAD, :]


def _conv_a(a_in, hist, w, b, lg, lb):
    bsz, t, _ = a_in.shape
    tt = _row_tile(t, 256)
    hist_p = jnp.pad(hist, ((0, 0), (CONV_PAD - (CONV_W - 1), 0), (0, 0)))
    vec = lambda v: pl.BlockSpec(v.shape, lambda i, j: (0, 0))
    y, nh = pl.pallas_call(
        functools.partial(_conv_a_kernel, tt=tt),
        grid=(bsz, t // tt),
        in_specs=[pl.BlockSpec((1, tt, 2 * CONV_CH), lambda i, j: (i, j, 0)),
                  pl.BlockSpec((1, CONV_PAD, CONV_CH), lambda i, j: (i, 0, 0)),
                  vec(w), vec(b), vec(lg), vec(lb)],
        out_specs=[pl.BlockSpec((1, tt, CONV_CH), lambda i, j: (i, j, 0)),
                   pl.BlockSpec((1, CONV_PAD, CONV_CH), lambda i, j: (i, 0, 0))],
        out_shape=[jax.ShapeDtypeStruct((bsz, t, CONV_CH), BF16),
                   jax.ShapeDtypeStruct((bsz, CONV_PAD, CONV_CH), F32)],
        scratch_shapes=[pltpu.VMEM((tt + CONV_PAD, CONV_CH), F32),
                        pltpu.VMEM((SUBLANES, tt + CONV_PAD, CONV_CH), F32)],
        compiler_params=_cparams("arbitrary", "arbitrary"),
        name="conv_a",
    )(a_in, hist_p, w, b, lg, lb)
    return y, nh[:, CONV_PAD - (CONV_W - 1):, :]


HIST_PAD = 8


def _short_conv(x, buf_ref, b, w_ref, width, rows):
    buf_ref[b, HIST_PAD:HIST_PAD + rows, :] = x
    off = HIST_PAD - (width - 1)
    acc = w_ref[0:1, :] * buf_ref[b, off:off + rows, :]
    for k in range(1, width):
        acc = acc + w_ref[k:k + 1, :] * buf_ref[b, off + k:off + k + rows, :]
    buf_ref[b, 0:HIST_PAD, :] = buf_ref[b, rows:rows + HIST_PAD, :]
    return acc


def _lower_tri(rows):
    return _iota((rows, rows), 0) >= _iota((rows, rows), 1)


def _pair_mask():
    return _iota((1, LANES), 1) < HEAD_DIM


def _ssd_kernel(zx_ref, s_ref, h0_ref, dtb_ref, arow_ref, dx_ref, ng_ref,
                ex_ref, y_ref, hout_ref, h_ref, *, rows, bb):
    c = pl.program_id(1)

    @pl.when(c == 0)
    def _():
        h_ref[...] = h0_ref[...]

    causal = _lower_tri(rows)
    lane = _iota((1, LANES), 1)
    first_half = _pair_mask()
    srow_g = _iota((LANES, LANES), 0) // SSM_STATE
    scol_h = _iota((LANES, LANES), 1) // HEAD_DIM
    ex = ex_ref[...]
    tril = causal.astype(BF16)
    assert SSM_INNER // SSM_GROUPS == LANES + HEAD_DIM and N_PAIRS == 3
    top_rows = _iota((LANES, LANES), 0) < HEAD_DIM
    ones_all = jnp.ones((LANES, LANES), BF16)
    ones_top = top_rows.astype(BF16)
    ones_bot = jnp.logical_not(top_rows).astype(BF16)

    for b in range(bb):
        zx = zx_ref[b]
        z = zx[:, :SSM_INNER]
        xbc = zx[:, SSM_INNER:]
        xs = xbc[:, :SSM_INNER]
        bm = xbc[:, SSM_INNER:SSM_INNER + LANES]
        cm = xbc[:, SSM_INNER + LANES:]

        dt = _softplus(s_ref[b] + dtb_ref[...])
        a = dt * arow_ref[...]
        acum = _sel_dot(tril, a)
        a_last = acum[rows - 1:rows, :]
        dt_x = _dot_sel(dt, ex)
        ea_x = _dot_sel(jnp.exp(acum), ex)
        te_x = _dot_sel(jnp.exp(a_last - acum), ex)
        cd_x = _dot_sel(jnp.broadcast_to(jnp.exp(a_last), (SUBLANES, LANES)), ex)[0:1, :]

        scores = []
        for g in range(SSM_GROUPS):
            cm_g = jnp.where(lane // SSM_STATE == g, cm, 0.0)
            scores.append(_dot_nt(cm_g, bm))
        bm_t = jnp.transpose(bm)

        ys = []
        for p in range(N_PAIRS):
            sl = slice(p * LANES, (p + 1) * LANES)
            x_p = xs[:, sl]
            xdt = x_p * dt_x[:, sl]
            yd = []
            for hh in range(2):
                h = 2 * p + hh
                col = jnp.broadcast_to(acum[:, h:h + 1], (rows, rows))
                dec = jnp.exp(jnp.where(causal, col - jnp.transpose(col), -1e30))
                yd.append(_dot(scores[h // (SSM_HEADS // SSM_GROUPS)] * dec, xdt))
            y_diag = jnp.where(first_half, yd[0], yd[1])
            h_p = h_ref[b, p]
            y_off = _dot(cm, h_p) * ea_x[:, sl]
            keep = srow_g == (2 * p + scol_h) // (SSM_HEADS // SSM_GROUPS)
            upd = _dot(bm_t, xdt * te_x[:, sl])
            h_ref[b, p] = h_p * cd_x[:, sl] + jnp.where(keep, upd, 0.0)
            ys.append(y_diag + y_off + dx_ref[:, sl] * x_p)
        y = jnp.concatenate(ys, axis=-1) * _silu(z)
        sq = [jnp.square(y[:, p * LANES:(p + 1) * LANES]) for p in range(N_PAIRS)]
        g0 = _dot_sel(sq[0], ones_all) + _dot_sel(sq[1], ones_top)
        g1 = _dot_sel(sq[1], ones_bot) + _dot_sel(sq[2], ones_all)
        ms = jnp.concatenate([g0, jnp.where(first_half, g0, g1), g1], axis=-1) * (1.0 / (SSM_INNER // SSM_GROUPS))
        y = y * lax.rsqrt(ms + EPS) * ng_ref[...]
        y_ref[b] = y.astype(y_ref.dtype)

    @pl.when(c == pl.num_programs(1) - 1)
    def _():
        hout_ref[...] = h_ref[...]


def _expand_matrix(first_lane):
    m = np.zeros((LANES, SSM_INNER), np.float32)
    for h in range(SSM_HEADS):
        m[first_lane + h, h * HEAD_DIM:(h + 1) * HEAD_DIM] = 1.0
    return jnp.asarray(m, BF16)


def _lane_row(vals, first_lane):
    return jnp.zeros((1, LANES), F32).at[0, first_lane:first_lane + vals.shape[0]].set(vals.astype(F32))


def _ssd(zx, small, h0, dt_bias, a_log, d_skip, norm_g, rows, bb):
    bsz, t, _ = zx.shape
    dtb = _lane_row(dt_bias, 0)
    arow = _lane_row(-jnp.exp(a_log.astype(F32)), 0)
    dx = jnp.repeat(d_skip.astype(F32), HEAD_DIM)[None, :]
    full = lambda v: pl.BlockSpec(v.shape, lambda i, j: (0,) * v.ndim)
    blk = lambda c: pl.BlockSpec((bb, rows, c), lambda i, j: (i, j, 0))
    st = pl.BlockSpec((bb, N_PAIRS, LANES, LANES), lambda i, j: (i, 0, 0, 0))
    consts = (dtb, arow, dx, norm_g[None, :], _expand_matrix(0))
    return pl.pallas_call(
        functools.partial(_ssd_kernel, rows=rows, bb=bb),
        grid=(bsz // bb, t // rows),
        in_specs=[blk(zx.shape[-1]), blk(LANES), st] + [full(v) for v in consts],
        out_specs=[blk(SSM_INNER), st],
        out_shape=[jax.ShapeDtypeStruct((bsz, t, SSM_INNER), BF16),
                   jax.ShapeDtypeStruct((bsz, N_PAIRS, LANES, LANES), F32)],
        scratch_shapes=[pltpu.VMEM((bb, N_PAIRS, LANES, LANES), F32)],
        compiler_params=_cparams("arbitrary", "arbitrary"),
        name="ssd",
    )(zx, small, h0, *consts)


def _ssd_state_in(h):
    bsz = h.shape[0]
    out = jnp.zeros((bsz, N_PAIRS, SSM_GROUPS, SSM_STATE, 2, HEAD_DIM), F32)
    for hd in range(SSM_HEADS):
        g = hd // (SSM_HEADS // SSM_GROUPS)
        out = out.at[:, hd // 2, g, :, hd % 2, :].set(jnp.swapaxes(h[:, hd], 1, 2))
    return out.reshape(bsz, N_PAIRS, LANES, LANES)


def _ssd_state_out(hp):
    bsz = hp.shape[0]
    hp = hp.reshape(bsz, N_PAIRS, SSM_GROUPS, SSM_STATE, 2, HEAD_DIM)
    heads = [jnp.swapaxes(hp[:, hd // 2, hd // (SSM_HEADS // SSM_GROUPS), :, hd % 2, :], 1, 2)
             for hd in range(SSM_HEADS)]
    return jnp.stack(heads, axis=1)


BETA_LANE = 6
DECAY_LANE = 12


def _bdot(a, b):
    return lax.dot_general(a.astype(BF16), b.astype(BF16), (((2,), (1,)), ((0,), (0,))),
                           preferred_element_type=F32)


def _bdot_nt(a, b):
    return lax.dot_general(a.astype(BF16), b.astype(BF16), (((2,), (2,)), ((0,), (0,))),
                           preferred_element_type=F32)


def _unit_lower_inverse(m, block):
    rdim = m.shape[-1]
    eye = (_iota((rdim, rdim), 0) == _iota((rdim, rdim), 1)).astype(F32)
    x = -m
    t = eye + x
    p = x
    for _ in range(int(math.log2(block)) - 2):
        p = _bdot(p, p)
        t = t + _bdot(t, p)
    resid = (eye - t) - _bdot(m, t)
    return t + _bdot(t, resid)


def _gdn_kernel(qg_ref, s_ref, s0_ref, bias_ref, arow_ref, ng_ref, eb_ref, eg_ref,
                y_ref, sout_ref, st_ref, *, tt, ck, bb):
    c = pl.program_id(1)
    nck = tt // ck
    rdim = 2 * ck

    @pl.when(c == 0)
    def _():
        st_ref[...] = s0_ref[...]

    r_i = _iota((rdim, rdim), 0)
    c_i = _iota((rdim, rdim), 1)
    same_blk = (r_i // ck) == (c_i // ck)
    incl = same_blk & (r_i >= c_i)
    strict = same_blk & (r_i > c_i)
    first_half = _pair_mask()
    same_head = (_iota((LANES, LANES), 0) // HEAD_DIM) == (_iota((LANES, LANES), 1) // HEAD_DIM)
    tr = _iota((tt, tt), 0)
    tc = _iota((tt, tt), 1)
    blk_tril = (((tr // ck) == (tc // ck)) & (tr >= tc)).astype(BF16)
    eb = eb_ref[...]
    eg = eg_ref[...]
    hm = same_head.astype(BF16)
    head_sumsq = lambda x: jnp.concatenate(
        [_dot_sel(jnp.square(x[:, p * LANES:(p + 1) * LANES]), hm) for p in range(N_PAIRS)], axis=-1)

    tiles = {name: [] for name in ('q', 'k', 'kb', 'rhs', 'qd', 'kd', 'col', 'cd')}
    gates = []
    for b in range(bb):
        qg = qg_ref[b]
        gates.append(qg[:, GDN_QKV:])
        qkv = qg[:, :GDN_QKV]
        q = qkv[:, :GDN_INNER]
        k = qkv[:, GDN_INNER:2 * GDN_INNER]
        v = qkv[:, 2 * GDN_INNER:]
        q = q * lax.rsqrt(head_sumsq(q) + EPS) * (HEAD_DIM ** -0.5)
        k = k * lax.rsqrt(head_sumsq(k) + EPS)
        s = s_ref[b]
        beta = _sigmoid(s)
        g = _softplus(s + bias_ref[...]) * arow_ref[...]
        gc = _sel_dot(blk_tril, g)
        g_last = jnp.concatenate(
            [jnp.broadcast_to(gc[(i + 1) * ck - 1:(i + 1) * ck, :], (ck, LANES)) for i in range(nck)], axis=0)
        beta_x = _dot_sel(beta, eb)
        eg_x = _dot_sel(jnp.exp(gc), eg)
        kd_x = _dot_sel(jnp.exp(g_last - gc), eg)
        cd_x = _dot_sel(jnp.exp(g_last), eg)
        kb = k * beta_x
        full = dict(q=q, k=k, kb=kb, qd=q * eg_x, kd=k * kd_x, cd=cd_x)
        vb = v * beta_x
        kbe = kb * eg_x
        for i in range(nck):
            rs = slice(i * ck, (i + 1) * ck)
            for p in range(N_PAIRS):
                sl = slice(p * LANES, (p + 1) * LANES)
                for name, arr in full.items():
                    tiles[name].append(arr[rs, sl])
                tiles['rhs'].append(jnp.concatenate([vb[rs, sl], kbe[rs, sl]], axis=-1))
                lane0 = DECAY_LANE + 2 * p
                tiles['col'].append(jnp.concatenate(
                    [jnp.broadcast_to(gc[rs, lane0 + hh:lane0 + hh + 1], (ck, rdim)) for hh in range(2)], axis=0))

    st = lambda name: jnp.stack(tiles[name], axis=0)
    stack2 = lambda x: jnp.concatenate([jnp.where(first_half, x, 0.0), jnp.where(first_half, 0.0, x)], axis=1)
    k_st = stack2(st('k'))
    col = st('col')
    diff = col - jnp.swapaxes(col, 1, 2)
    dec = jnp.exp(jnp.where(incl, diff, -1e30))
    m = _bdot_nt(stack2(st('kb')), k_st) * jnp.where(strict, dec, 0.0)
    t_inv = _unit_lower_inverse(m, ck)
    rhs = st('rhs')
    sol = _bdot(t_inv, jnp.concatenate([rhs, rhs], axis=1))
    u = jnp.where(first_half, sol[:, :ck, :LANES], sol[:, ck:, :LANES])
    w = jnp.where(first_half, sol[:, :ck, LANES:], sol[:, ck:, LANES:])
    attn = _bdot_nt(stack2(st('q')), k_st) * dec
    qd = st('qd')
    kd_t = jnp.swapaxes(st('kd'), 1, 2)
    cd = st('cd')

    gsel = lambda x, i: jnp.stack([x[(b * nck + i) * N_PAIRS + p] for b in range(bb) for p in range(N_PAIRS)], axis=0)
    state = st_ref[...].reshape(bb * N_PAIRS, LANES, LANES)
    o_chunks = []
    for i in range(nck):
        v_new = gsel(u, i) - _bdot(gsel(w, i), state)
        intra = _bdot(gsel(attn, i), jnp.concatenate([v_new, v_new], axis=1))
        o_chunks.append(_bdot(gsel(qd, i), state) + jnp.where(first_half, intra[:, :ck], intra[:, ck:]))
        upd = _bdot(gsel(kd_t, i), v_new)
        state = state * gsel(cd, i)[:, 0:1, :] + jnp.where(same_head, upd, 0.0)
    st_ref[...] = state.reshape(bb, N_PAIRS, LANES, LANES)

    for b in range(bb):
        o = jnp.concatenate(
            [jnp.concatenate([o_chunks[i][b * N_PAIRS + p] for p in range(N_PAIRS)], axis=-1) for i in range(nck)],
            axis=0)
        ms = head_sumsq(o) * (1.0 / HEAD_DIM)
        o = o * lax.rsqrt(ms + EPS) * ng_ref[...] * _silu(gates[b])
        y_ref[b] = o.astype(y_ref.dtype)

    @pl.when(c == pl.num_programs(1) - 1)
    def _():
        sout_ref[...] = st_ref[...]


def _gdn(qg, small, s0, a_log, dt_bias, norm_g, tt, ck, bb):
    bsz, t, _ = qg.shape
    bias = _lane_row(dt_bias, DECAY_LANE)
    arow = _lane_row(-jnp.exp(a_log.astype(F32)), DECAY_LANE)
    ng = jnp.tile(norm_g.astype(F32), GDN_HEADS)[None, :]
    full = lambda v: pl.BlockSpec(v.shape, lambda i, j: (0,) * v.ndim)
    blk = lambda c: pl.BlockSpec((bb, tt, c), lambda i, j: (i, j, 0))
    st = pl.BlockSpec((bb, N_PAIRS, LANES, LANES), lambda i, j: (i, 0, 0, 0))
    consts = (bias, arow, ng, _expand_matrix(BETA_LANE), _expand_matrix(DECAY_LANE))
    return pl.pallas_call(
        functools.partial(_gdn_kernel, tt=tt, ck=ck, bb=bb),
        grid=(bsz // bb, t // tt),
        in_specs=[blk(qg.shape[-1]), blk(LANES), st] + [full(v) for v in consts],
        out_specs=[blk(GDN_INNER), st],
        out_shape=[jax.ShapeDtypeStruct((bsz, t, GDN_INNER), BF16),
                   jax.ShapeDtypeStruct((bsz, N_PAIRS, LANES, LANES), F32)],
        scratch_shapes=[pltpu.VMEM((bb, N_PAIRS, LANES, LANES), F32)],
        compiler_params=_cparams("arbitrary", "arbitrary"),
        name="gdn",
    )(qg, small, s0, *consts)


def _gdn_state_in(s):
    bsz = s.shape[0]
    out = jnp.zeros((bsz, N_PAIRS, 2, HEAD_DIM, 2, HEAD_DIM), F32)
    for hd in range(GDN_HEADS):
        out = out.at[:, hd // 2, hd % 2, :, hd % 2, :].set(s[:, hd])
    return out.reshape(bsz, N_PAIRS, LANES, LANES)


def _gdn_state_out(sp):
    bsz = sp.shape[0]
    sp = sp.reshape(bsz, N_PAIRS, 2, HEAD_DIM, 2, HEAD_DIM)
    return jnp.stack([sp[:, hd // 2, hd % 2, :, hd % 2, :] for hd in range(GDN_HEADS)], axis=1)


def _mix_residual(x_ref, ya_ref, yb_ref, yc_ref, woa_ref, wob_ref, woc_ref):
    mix = (jnp.dot(ya_ref[...], woa_ref[...], preferred_element_type=F32)
           + jnp.dot(yb_ref[...], wob_ref[...], preferred_element_type=F32)
           + jnp.dot(yc_ref[...], woc_ref[...], preferred_element_type=F32))
    return x_ref[...] + mix


def _rms(x, g):
    return x * lax.rsqrt(jnp.mean(x * x, axis=-1, keepdims=True) + EPS) * g


def _ffn_kernel(x_ref, ya_ref, yb_ref, yc_ref, woa_ref, wob_ref, woc_ref, g_ref, wg_ref, wu_ref, wd_ref,
                o_ref, xn_ref, h_ref, acc_ref):
    j = pl.program_id(1)

    @pl.when(j == 0)
    def _():
        xn = _mix_residual(x_ref, ya_ref, yb_ref, yc_ref, woa_ref, wob_ref, woc_ref)
        xn_ref[...] = xn
        h_ref[...] = _rms(xn, g_ref[...]).astype(BF16)
        acc_ref[...] = jnp.zeros_like(acc_ref)

    h = h_ref[...]
    act = _silu(jnp.dot(h, wg_ref[...], preferred_element_type=F32)) * jnp.dot(h, wu_ref[...], preferred_element_type=F32)
    acc_ref[...] += jnp.dot(act.astype(BF16), wd_ref[...], preferred_element_type=F32)

    @pl.when(j == pl.num_programs(1) - 1)
    def _():
        o_ref[...] = xn_ref[...] + acc_ref[...]


FFN_CHUNKS = 2


def _out_ffn(x2d, ya, yb, yc, wo, g, wg, wu, wd):
    n = x2d.shape[0]
    tm = _row_tile(n, 512)
    f = wg.shape[1]
    tf = f // FFN_CHUNKS
    rows = lambda c: pl.BlockSpec((tm, c), lambda i, j: (i, 0))
    full = lambda w: pl.BlockSpec(w.shape, lambda i, j: (0, 0))
    return pl.pallas_call(
        _ffn_kernel,
        grid=(n // tm, FFN_CHUNKS),
        in_specs=[rows(D_MODEL), rows(ya.shape[1]), rows(yb.shape[1]), rows(yc.shape[1]),
                  full(wo[0]), full(wo[1]), full(wo[2]), full(g),
                  pl.BlockSpec((D_MODEL, tf), lambda i, j: (0, j)),
                  pl.BlockSpec((D_MODEL, tf), lambda i, j: (0, j)),
                  pl.BlockSpec((tf, D_MODEL), lambda i, j: (j, 0))],
        out_specs=rows(D_MODEL),
        out_shape=jax.ShapeDtypeStruct((n, D_MODEL), F32),
        scratch_shapes=[pltpu.VMEM((tm, D_MODEL), F32), pltpu.VMEM((tm, D_MODEL), BF16),
                        pltpu.VMEM((tm, D_MODEL), F32)],
        compiler_params=_cparams("parallel", "arbitrary"),
        name="out_ffn",
    )(x2d, ya, yb, yc, wo[0], wo[1], wo[2], g, wg, wu, wd)


ROUTE_TILE = 512


def _router_kernel(x_ref, ya_ref, yb_ref, yc_ref, woa_ref, wob_ref, woc_ref, g_ref, wr_ref, br_ref,
                   xn_ref, h_ref, gate_ref, slot_ref, cnt_ref, run_ref, *, cap):
    i = pl.program_id(0)

    @pl.when(i == 0)
    def _():
        run_ref[...] = jnp.zeros_like(run_ref)

    xn = _mix_residual(x_ref, ya_ref, yb_ref, yc_ref, woa_ref, wob_ref, woc_ref)
    xn_ref[...] = xn
    h = _rms(xn, g_ref[...])
    h_ref[...] = h
    tm = h.shape[0]
    lane = _iota((1, LANES), 1)
    logits = jnp.where(lane < N_EXPERTS, _dot_hp(h, wr_ref[...]) + br_ref[...], -jnp.inf)
    m1 = jnp.max(logits, axis=-1, keepdims=True)
    i1 = jnp.min(jnp.where(logits == m1, lane, LANES), axis=-1, keepdims=True)
    rest = jnp.where(lane == i1, -jnp.inf, logits)
    m2 = jnp.max(rest, axis=-1, keepdims=True)
    i2 = jnp.min(jnp.where(rest == m2, lane, LANES), axis=-1, keepdims=True)
    e2 = jnp.exp(m2 - m1)
    inv = 1.0 / (1.0 + e2)
    gate_ref[...] = jnp.where(lane == 0, inv, 0.0) + jnp.where(lane == 1, e2 * inv, 0.0)

    chosen = jnp.where((lane == i1) | (lane == i2), 1.0, 0.0)
    before = _iota((tm, tm), 0) > _iota((tm, tm), 1)
    rank = jnp.dot(before.astype(BF16), chosen.astype(BF16), preferred_element_type=F32) + run_ref[0:1, :]
    base = lane.astype(F32) * float(cap)
    pick = lambda idx: jnp.sum(jnp.where(lane == idx, rank + base, 0.0), axis=-1, keepdims=True)
    slots = jnp.where(lane == 0, pick(i1), 0.0) + jnp.where(lane == 1, pick(i2), 0.0)
    slots_t = jnp.transpose(slots).astype(jnp.int32)
    for k in range(2):
        slot_ref[0, k] = jnp.concatenate(
            [slots_t[k:k + 1, c * LANES:(c + 1) * LANES] for c in range(tm // LANES)], axis=0)
    run_ref[...] = run_ref[...] + jnp.sum(chosen, axis=0, keepdims=True)
    cnt_ref[...] = run_ref[...].astype(jnp.int32)


def _out_router(x2d, ya, yb, yc, wo, g, w_router, b_router, cap):
    n = x2d.shape[0]
    tm = _row_tile(n, ROUTE_TILE)
    wr = jnp.zeros((D_MODEL, LANES), F32).at[:, :N_EXPERTS].set(w_router.astype(F32))
    br = _lane_row(b_router, 0)
    rows = lambda c: pl.BlockSpec((tm, c), lambda i: (i, 0))
    full = lambda w: pl.BlockSpec(w.shape, lambda i: (0, 0))
    return pl.pallas_call(
        functools.partial(_router_kernel, cap=cap),
        grid=(n // tm,),
        in_specs=[rows(D_MODEL), rows(ya.shape[1]), rows(yb.shape[1]), rows(yc.shape[1]),
                  full(wo[0]), full(wo[1]), full(wo[2]), full(g), full(wr), full(br)],
        out_specs=[rows(D_MODEL), rows(D_MODEL), rows(LANES),
                   pl.BlockSpec((1, 2, tm // LANES, LANES), lambda i: (i, 0, 0, 0)),
                   pl.BlockSpec((SUBLANES, LANES), lambda i: (0, 0))],
        out_shape=[jax.ShapeDtypeStruct((n, D_MODEL), F32), jax.ShapeDtypeStruct((n, D_MODEL), F32),
                   jax.ShapeDtypeStruct((n, LANES), F32),
                   jax.ShapeDtypeStruct((n // tm, 2, tm // LANES, LANES), jnp.int32),
                   jax.ShapeDtypeStruct((SUBLANES, LANES), jnp.int32)],
        scratch_shapes=[pltpu.VMEM((SUBLANES, LANES), F32)],
        compiler_params=_cparams("arbitrary"),
        name="out_router",
    )(x2d, ya, yb, yc, wo[0], wo[1], wo[2], g, wr, br)


def _row_copy(src_ref, src_row, dst_ref, dst_row, sem):
    return pltpu.make_async_copy(src_ref.at[pl.ds(src_row, 1)], dst_ref.at[pl.ds(dst_row, 1)], sem)


def _slot_fetch(slots_hbm, tile, smem_ref, buf, sem):
    return pltpu.make_async_copy(slots_hbm.at[tile], smem_ref.at[buf], sem.at[buf])


def _for_rows(tm, fn):
    for c in range(tm // LANES):
        for base in range(0, LANES, SUBLANES):
            for j in range(SUBLANES):
                fn(c, base, j)


def _vmem_row(ref, base, j):
    return ref.at[pl.ds(base, SUBLANES)].at[pl.ds(j, 1)]


def _dispatch_kernel(cnt_ref, slots_hbm, h_ref, hs_hbm, slot_smem, zero_ref, slot_sem, row_sem, pad_sem,
                     *, tm, cap, bm):
    i = pl.program_id(0)
    n_tiles = pl.num_programs(0)

    @pl.when(i == 0)
    def _():
        _slot_fetch(slots_hbm, 0, slot_smem, 0, slot_sem).start()

    @pl.when(i + 1 < n_tiles)
    def _():
        _slot_fetch(slots_hbm, i + 1, slot_smem, (i + 1) % 2, slot_sem).start()

    _slot_fetch(slots_hbm, i, slot_smem, i % 2, slot_sem).wait()

    def issue(c, base, j):
        for k in range(2):
            dst = hs_hbm.at[pl.ds(slot_smem[i % 2, k, c, base + j], 1)]
            pltpu.make_async_copy(_vmem_row(h_ref, c * LANES + base, j), dst, row_sem).start(priority=k)

    _for_rows(tm, issue)

    def drain(r, carry):
        for k in range(2):
            _row_copy(h_ref, 0, hs_hbm, 0, row_sem).wait()
        return carry

    lax.fori_loop(0, tm, drain, 0, unroll=8)

    @pl.when(i == n_tiles - 1)
    def _():
        zero_ref[...] = jnp.zeros_like(zero_ref)
        for e in range(N_EXPERTS):
            c = cnt_ref[e]
            n_pad = ((c + bm - 1) // bm) * bm - c

            def fill(r, carry):
                _row_copy(zero_ref, 0, hs_hbm, e * cap + c + r, pad_sem).start()
                return carry

            def fill_wait(r, carry):
                _row_copy(zero_ref, 0, hs_hbm, 0, pad_sem).wait()
                return carry

            lax.fori_loop(0, n_pad, fill, 0)
            lax.fori_loop(0, n_pad, fill_wait, 0)


def _dispatch(counts, slots, h, cap, bm):
    n = h.shape[0]
    n_tiles = slots.shape[0]
    tm = slots.shape[2] * LANES
    return pl.pallas_call(
        functools.partial(_dispatch_kernel, tm=tm, cap=cap, bm=bm),
        grid_spec=pltpu.PrefetchScalarGridSpec(
            num_scalar_prefetch=1,
            grid=(n_tiles,),
            in_specs=[pl.BlockSpec(memory_space=pl.ANY), pl.BlockSpec((tm, D_MODEL), lambda i, cnt: (i, 0))],
            out_specs=pl.BlockSpec(memory_space=pl.ANY),
            scratch_shapes=[pltpu.SMEM((2, 2, tm // LANES, LANES), jnp.int32), pltpu.VMEM((SUBLANES, D_MODEL), F32),
                            pltpu.SemaphoreType.DMA((2,)), pltpu.SemaphoreType.DMA, pltpu.SemaphoreType.DMA],
        ),
        out_shape=jax.ShapeDtypeStruct((N_EXPERTS * cap, D_MODEL), F32),
        compiler_params=_cparams("arbitrary"),
        name="moe_dispatch",
    )(counts, slots, h)


def _experts_kernel(tbl_ref, hs_ref, wg_ref, wu_ref, wd_ref, ys_ref, hb_ref, acc_ref):
    s = pl.program_id(0)
    j = pl.program_id(1)

    @pl.when(tbl_ref[2, s] == 1)
    def _():
        @pl.when(j == 0)
        def _():
            hb_ref[...] = hs_ref[...].astype(BF16)
            acc_ref[...] = jnp.zeros_like(acc_ref)

        h = hb_ref[...]
        act = (_silu(jnp.dot(h, wg_ref[0], preferred_element_type=F32))
               * jnp.dot(h, wu_ref[0], preferred_element_type=F32))
        acc_ref[...] += jnp.dot(act.astype(BF16), wd_ref[0], preferred_element_type=F32)

        @pl.when(j == pl.num_programs(1) - 1)
        def _():
            ys_ref[...] = acc_ref[...]


def _block_table(counts, cap, bm, n_steps):
    nblk = (counts + bm - 1) // bm
    cum = jnp.cumsum(nblk)
    total = cum[-1]
    step = jnp.arange(n_steps, dtype=jnp.int32)
    last = jnp.maximum(total - 1, 0)
    eff = jnp.minimum(step, last)
    expert = jnp.minimum(jnp.searchsorted(cum, eff, side='right'), N_EXPERTS - 1).astype(jnp.int32)
    blk = eff - (cum[expert] - nblk[expert])
    return jnp.stack([expert * (cap // bm) + blk, expert, (step < total).astype(jnp.int32)]).astype(jnp.int32)


def _experts(counts, hs, wg, wu, wd, cap, bm, n_assign):
    n_steps = n_assign // bm + N_EXPERTS
    tbl = _block_table(counts, cap, bm, n_steps)
    f = wg.shape[2]
    tf = f // FFN_CHUNKS
    chunk = lambda s, j, t: jnp.where(t[2, s] == 1, j, FFN_CHUNKS - 1)
    rows = pl.BlockSpec((bm, D_MODEL), lambda s, j, t: (t[0, s], 0))
    return pl.pallas_call(
        _experts_kernel,
        grid_spec=pltpu.PrefetchScalarGridSpec(
            num_scalar_prefetch=1,
            grid=(n_steps, FFN_CHUNKS),
            in_specs=[rows,
                      pl.BlockSpec((1, D_MODEL, tf), lambda s, j, t: (t[1, s], 0, chunk(s, j, t))),
                      pl.BlockSpec((1, D_MODEL, tf), lambda s, j, t: (t[1, s], 0, chunk(s, j, t))),
                      pl.BlockSpec((1, tf, D_MODEL), lambda s, j, t: (t[1, s], chunk(s, j, t), 0))],
            out_specs=rows,
            scratch_shapes=[pltpu.VMEM((bm, D_MODEL), BF16), pltpu.VMEM((bm, D_MODEL), F32)],
        ),
        out_shape=jax.ShapeDtypeStruct(hs.shape, F32),
        compiler_params=_cparams("arbitrary", "arbitrary"),
        name="moe_experts",
    )(tbl, hs, wg, wu, wd)


def _combine_kernel(slots_hbm, ys_hbm, xn_ref, gate_ref, gf_ref, o_ref, slot_smem, ybuf_ref, slot_sem, row_sem,
                    *, tm):
    i = pl.program_id(0)
    n_tiles = pl.num_programs(0)

    def gather(tile, buf):
        def issue(c, base, j):
            for k in range(2):
                src = ys_hbm.at[pl.ds(slot_smem[buf, k, c, base + j], 1)]
                dst = _vmem_row(ybuf_ref.at[buf, k], c * LANES + base, j)
                pltpu.make_async_copy(src, dst, row_sem.at[buf]).start(priority=k)
        _for_rows(tm, issue)

    @pl.when(i == 0)
    def _():
        first = _slot_fetch(slots_hbm, 0, slot_smem, 0, slot_sem)
        first.start()
        first.wait()
        gather(0, 0)

        @pl.when(n_tiles > 1)
        def _():
            _slot_fetch(slots_hbm, 1, slot_smem, 1, slot_sem).start()

    @pl.when(i + 1 < n_tiles)
    def _():
        _slot_fetch(slots_hbm, i + 1, slot_smem, (i + 1) % 2, slot_sem).wait()
        gather(i + 1, (i + 1) % 2)

    def drain(r, carry):
        for k in range(2):
            _row_copy(ys_hbm, 0, ybuf_ref.at[i % 2, k], 0, row_sem.at[i % 2]).wait()
        return carry

    lax.fori_loop(0, tm, drain, 0, unroll=8)

    @pl.when(i + 2 < n_tiles)
    def _():
        _slot_fetch(slots_hbm, i + 2, slot_smem, i % 2, slot_sem).start()

    g = gate_ref[...]
    y = g[:, 0:1] * ybuf_ref[i % 2, 0] + g[:, 1:2] * ybuf_ref[i % 2, 1]
    o_ref[...] = _rms(xn_ref[...] + y, gf_ref[...])


def _combine(slots, ys, xn, gates, g_final):
    n = xn.shape[0]
    n_tiles = slots.shape[0]
    tm = slots.shape[2] * LANES
    rows = lambda c: pl.BlockSpec((tm, c), lambda i: (i, 0))
    return pl.pallas_call(
        functools.partial(_combine_kernel, tm=tm),
        grid=(n_tiles,),
        in_specs=[pl.BlockSpec(memory_space=pl.ANY), pl.BlockSpec(memory_space=pl.ANY),
                  rows(D_MODEL), rows(LANES), pl.BlockSpec(g_final.shape, lambda i: (0, 0))],
        out_specs=rows(D_MODEL),
        out_shape=jax.ShapeDtypeStruct((n, D_MODEL), F32),
        scratch_shapes=[pltpu.SMEM((2, 2, tm // LANES, LANES), jnp.int32), pltpu.VMEM((2, 2, tm, D_MODEL), F32),
                        pltpu.SemaphoreType.DMA((2,)), pltpu.SemaphoreType.DMA((2,))],
        compiler_params=_cparams("arbitrary"),
        name="moe_combine",
    )(slots, ys, xn, gates, g_final)


def _moe(x2d, ya, yb, yc, wo, g_ffn, w_router, b_router, wg, wu, wd, g_final):
    n = x2d.shape[0]
    bm = _row_tile(n, 512)
    cap = n
    xn, h, gates, slots, counts = _out_router(x2d, ya, yb, yc, wo, g_ffn, w_router, b_router, cap)
    counts = counts[0, :N_EXPERTS]
    hs = _dispatch(counts, slots, h, cap, bm)
    ys = _experts(counts, hs, wg, wu, wd, cap, bm, 2 * n)
    return _combine(slots, ys, xn, gates, g_final)


IN_SIZES = (2 * CONV_CH, SSM_INNER, SSM_XBC, SSM_HEADS, GDN_QKV, GDN_INNER, GDN_HEADS, GDN_HEADS)


def _prep_layer(l, p):
    off = np.concatenate([[0], np.cumsum(IN_SIZES)])
    w_in = p['w_in_bf16'][l]
    col = lambda i: w_in[:, off[i]:off[i + 1]]
    a_in, z, xbc, dt, qkv, gate, b_raw, a_raw = (col(i) for i in range(8))
    small = jnp.concatenate([dt, b_raw, a_raw, jnp.zeros((D_MODEL, SMALL_W - 3 * SSM_HEADS), BF16)], axis=1)
    wo = p['w_out'][l].astype(BF16)
    return dict(
        wa=a_in,
        wzx=jnp.concatenate([z, xbc], axis=1),
        wqg=jnp.concatenate([qkv, gate], axis=1),
        ws=small,
        wo=(wo[:CONV_CH], wo[CONV_CH:CONV_CH + SSM_INNER], wo[CONV_CH + SSM_INNER:]),
    )


def _trunk(x, st_conv_a, st_ssm_conv, st_ssm, st_gdn_conv, st_gdn, p, prep, ssd_rows, gdn_rows, bb):
    bsz, t, _ = x.shape
    n = bsz * t
    depth = p['g_mix'].shape[0]
    x2d = x.reshape(n, D_MODEL)
    new = [[] for _ in range(5)]
    for l in range(depth):
        w = prep[l]
        a_in, zx, qg, small, hist_x, hist_q = _norm_proj(
            x2d.reshape(bsz, t, D_MODEL), p['g_mix'][l][None, :], w['wa'], w['wzx'], w['wqg'], w['ws'],
            st_ssm_conv[l], st_gdn_conv[l], p['ssm_conv_w'][l], p['ssm_conv_b'][l][None, :], p['gdn_conv_w'][l])
        a_in = a_in.reshape(bsz, t, -1)
        zx = zx.reshape(bsz, t, -1)
        qg = qg.reshape(bsz, t, -1)
        small = small.reshape(bsz, t, -1)
        ya, conv_a = _conv_a(a_in, st_conv_a[l], p['conv_a_w'][l], p['conv_a_b'][l][None, :],
                             p['ln_a_g'][l][None, :], p['ln_a_b'][l][None, :])
        yb, ssm = _ssd(zx, small, _ssd_state_in(st_ssm[l]), p['ssm_dt_bias'][l], p['ssm_a_log'][l],
                       p['ssm_d'][l], p['ssm_norm_g'][l], ssd_rows, bb)
        yc, gdn = _gdn(qg, small, _gdn_state_in(st_gdn[l]), p['gdn_a_log'][l], p['gdn_dt_bias'][l],
                       p['gdn_norm_g'][l], gdn_rows[0], gdn_rows[1], bb)
        new[0].append(conv_a)
        new[1].append(hist_x[:, HIST_PAD - (SSM_CONV_W - 1):])
        new[2].append(_ssd_state_out(ssm))
        new[3].append(hist_q[:, HIST_PAD - (GDN_CONV_W - 1):])
        new[4].append(_gdn_state_out(gdn))
        flat = lambda y: y.reshape(n, -1)
        g_ffn = p['g_ffn'][l][None, :]
        if l % 2 == 0:
            x2d = _out_ffn(x2d, flat(ya), flat(yb), flat(yc), w['wo'], g_ffn,
                           prep['ffn'][l // 2][0], prep['ffn'][l // 2][1], prep['ffn'][l // 2][2])
        else:
            x2d = _moe(x2d, flat(ya), flat(yb), flat(yc), w['wo'], g_ffn, p['moe_w_router'][l // 2],
                       p['moe_b_router'][l // 2], *prep['moe'][l // 2], p['g_final'][None, :])
    return (x2d.reshape(bsz, t, D_MODEL),) + tuple(jnp.stack(s) for s in new)


def kernel(x_prompt, x_sample, state_conv_a, state_ssm_conv, state_ssm, state_gdn_conv, state_gdn, g_mix, w_in, conv_a_w, conv_a_b, ln_a_g, ln_a_b, ssm_conv_w, ssm_conv_b, ssm_dt_bias, ssm_a_log, ssm_d, ssm_norm_g, gdn_conv_w, gdn_a_log, gdn_dt_bias, gdn_norm_g, w_out, g_ffn, ffn_w_gate, ffn_w_up, ffn_w_down, moe_w_router, moe_b_router, moe_w_gate, moe_w_up, moe_w_down, g_final):
    p = dict(g_mix=g_mix, w_in=w_in, conv_a_w=conv_a_w, conv_a_b=conv_a_b, ln_a_g=ln_a_g, ln_a_b=ln_a_b,
             ssm_conv_w=ssm_conv_w, ssm_conv_b=ssm_conv_b, ssm_dt_bias=ssm_dt_bias, ssm_a_log=ssm_a_log,
             ssm_d=ssm_d, ssm_norm_g=ssm_norm_g, gdn_conv_w=gdn_conv_w, gdn_a_log=gdn_a_log,
             gdn_dt_bias=gdn_dt_bias, gdn_norm_g=gdn_norm_g, w_out=w_out, g_ffn=g_ffn,
             moe_w_router=moe_w_router, moe_b_router=moe_b_router, g_final=g_final)
    depth = g_mix.shape[0]
    assert depth % 2 == 0, "the final RMSNorm is fused into the expert layer, which must come last"
    p['w_in_bf16'] = w_in.astype(BF16)
    prep = {l: _prep_layer(l, p) for l in range(depth)}
    prep['ffn'] = [(ffn_w_gate[i].astype(BF16), ffn_w_up[i].astype(BF16), ffn_w_down[i].astype(BF16))
                   for i in range(ffn_w_gate.shape[0])]
    prep['moe'] = [(moe_w_gate[i].astype(BF16), moe_w_up[i].astype(BF16), moe_w_down[i].astype(BF16))
                   for i in range(moe_w_gate.shape[0])]
    bp, dt = x_prompt.shape[0], x_prompt.dtype
    zeros = lambda *s: jnp.zeros((depth, bp) + s, dt)
    outs_p = _trunk(x_prompt, zeros(CONV_W - 1, CONV_CH), zeros(SSM_CONV_W - 1, SSM_XBC),
                    zeros(SSM_HEADS, HEAD_DIM, SSM_STATE), zeros(GDN_CONV_W - 1, GDN_QKV),
                    zeros(GDN_HEADS, HEAD_DIM, HEAD_DIM), p, prep, ssd_rows=min(128, x_prompt.shape[1]),
                    gdn_rows=(min(256, x_prompt.shape[1]), min(64, x_prompt.shape[1])), bb=2)
    outs_s = _trunk(x_sample, state_conv_a, state_ssm_conv, state_ssm, state_gdn_conv, state_gdn, p, prep,
                    ssd_rows=x_sample.shape[1], gdn_rows=(x_sample.shape[1], x_sample.shape[1]), bb=2)
    return (outs_p[0], outs_s[0]) + outs_p[1:] + outs_s[1:]
```

```python
import functools
import math

import jax
import jax.numpy as jnp
import numpy as np
from jax import lax
from jax.experimental import pallas as pl
from jax.experimental.pallas import tpu as pltpu

F32 = jnp.float32
BF16 = jnp.bfloat16
EPS = 1e-6

LANES = 128
SUBLANES = 8
VMEM_BYTES_V7X = 64 * 1024 * 1024
VMEM_LIMIT = VMEM_BYTES_V7X * 3 // 4

D_MODEL = 1024
CONV_CH = 256
CONV_W = 31
SSM_HEADS = 6
HEAD_DIM = 64
SSM_INNER = SSM_HEADS * HEAD_DIM
SSM_STATE = 64
SSM_GROUPS = 2
SSM_XBC = SSM_INNER + 2 * SSM_GROUPS * SSM_STATE
SSM_CONV_W = 4
GDN_HEADS = 6
GDN_INNER = GDN_HEADS * HEAD_DIM
GDN_QKV = 3 * GDN_INNER
GDN_CONV_W = 4
N_PAIRS = 3
N_EXPERTS = 8
SMALL_W = LANES


def _cparams(*sem):
    return pltpu.CompilerParams(dimension_semantics=sem, vmem_limit_bytes=VMEM_LIMIT)


def _dot(a, b):
    return jnp.dot(a.astype(BF16), b.astype(BF16), preferred_element_type=F32)


def _dot_nt(a, b):
    return lax.dot_general(a.astype(BF16), b.astype(BF16), (((1,), (1,)), ((), ())),
                           preferred_element_type=F32)


def _split3(x):
    hi = x.astype(BF16)
    r1 = x - hi.astype(F32)
    mid = r1.astype(BF16)
    lo = (r1 - mid.astype(F32)).astype(BF16)
    return hi, mid, lo


def _dot_sel(x, sel_bf16, pieces=2):
    d = lambda p: jnp.dot(p, sel_bf16, preferred_element_type=F32)
    hi = x.astype(BF16)
    r1 = x - hi.astype(F32)
    mid = r1.astype(BF16)
    if pieces == 2:
        return d(hi) + d(mid)
    return d(hi) + d(mid) + d((r1 - mid.astype(F32)).astype(BF16))


def _sel_dot(sel_bf16, x):
    hi, mid, lo = _split3(x)
    d = lambda p: jnp.dot(sel_bf16, p, preferred_element_type=F32)
    return d(hi) + d(mid) + d(lo)


def _dot_hp(a, b):
    ah = a.astype(BF16)
    al = (a - ah.astype(F32)).astype(BF16)
    bh = b.astype(BF16)
    bl = (b - bh.astype(F32)).astype(BF16)
    d = lambda p, q: jnp.dot(p, q, preferred_element_type=F32)
    return d(ah, bh) + (d(ah, bl) + d(al, bh))


def _silu(x):
    return x * (1.0 / (1.0 + jnp.exp(-x)))


def _sigmoid(x):
    return 1.0 / (1.0 + jnp.exp(-x))


def _softplus(x):
    return jnp.maximum(x, 0.0) + jnp.log(1.0 + jnp.exp(-jnp.abs(x)))


def _iota(shape, dim):
    return lax.broadcasted_iota(jnp.int32, shape, dim)


def _norm_proj_kernel(x_ref, g_ref, wa_ref, wzx_ref, wqg_ref, ws_ref, hx_ref, hq_ref, cwx_ref, cbx_ref, cwq_ref,
                      a_ref, zx_ref, qg_ref, s_ref, nhx_ref, nhq_ref, bx_ref, bq_ref, *, nseq, seg, tiles_per_seq):
    first = (pl.program_id(0) % tiles_per_seq) == 0

    @pl.when(first)
    def _():
        bx_ref[:, 0:HIST_PAD, :] = hx_ref[...]
        bq_ref[:, 0:HIST_PAD, :] = hq_ref[...]

    x = x_ref[...]
    u = x * lax.rsqrt(jnp.mean(x * x, axis=-1, keepdims=True) + EPS) * g_ref[...]
    ub = u.astype(BF16)
    a_ref[...] = jnp.dot(ub, wa_ref[...], preferred_element_type=F32)
    s_ref[...] = jnp.dot(ub, ws_ref[...], preferred_element_type=F32)
    zx = jnp.dot(ub, wzx_ref[...], preferred_element_type=F32)
    qg = jnp.dot(ub, wqg_ref[...], preferred_element_type=F32)
    zx_ref[:, :SSM_INNER] = zx[:, :SSM_INNER]
    qg_ref[:, GDN_QKV:] = qg[:, GDN_QKV:]
    for s in range(nseq):
        rows = slice(s * seg, (s + 1) * seg)
        xbc = zx[rows, SSM_INNER:]
        nhx_ref[s] = _last_rows(bx_ref, s, xbc, seg)
        zx_ref[rows, SSM_INNER:] = _silu(_short_conv(xbc, bx_ref, s, cwx_ref, SSM_CONV_W, seg) + cbx_ref[...])
        qkv = qg[rows, :GDN_QKV]
        nhq_ref[s] = _last_rows(bq_ref, s, qkv, seg)
        qg_ref[rows, :GDN_QKV] = _silu(_short_conv(qkv, bq_ref, s, cwq_ref, GDN_CONV_W, seg))


def _last_rows(buf_ref, s, x, rows):
    if rows >= HIST_PAD:
        return x[rows - HIST_PAD:, :]
    return jnp.concatenate([buf_ref[s, rows:HIST_PAD, :], x], axis=0)


def _row_tile(n, want):
    t = min(want, n)
    while n % t:
        t //= 2
    return t


def _norm_proj(x, g, wa, wzx, wqg, ws, hist_x, hist_q, cwx, cbx, cwq):
    bsz, t, _ = x.shape
    n = bsz * t
    tm = _row_tile(n, 512)
    nseq = max(1, tm // t)
    seg = tm // nseq
    tiles_per_seq = max(1, t // tm)
    assert nseq * seg == tm and (t % tm == 0 or tm % t == 0)
    pad = lambda h: jnp.pad(h, ((0, 0), (HIST_PAD - h.shape[1], 0), (0, 0)))
    full = lambda w: pl.BlockSpec(w.shape, lambda i: (0, 0))
    rows = lambda c: pl.BlockSpec((tm, c), lambda i: (i, 0))
    hist = lambda c: pl.BlockSpec((nseq, HIST_PAD, c), lambda i: (i // tiles_per_seq, 0, 0))
    widths = (wa.shape[1], wzx.shape[1], wqg.shape[1], ws.shape[1])
    return pl.pallas_call(
        functools.partial(_norm_proj_kernel, nseq=nseq, seg=seg, tiles_per_seq=tiles_per_seq),
        grid=(n // tm,),
        in_specs=[rows(D_MODEL), full(g), full(wa), full(wzx), full(wqg), full(ws),
                  hist(SSM_XBC), hist(GDN_QKV), full(cwx), full(cbx), full(cwq)],
        out_specs=[rows(c) for c in widths] + [hist(SSM_XBC), hist(GDN_QKV)],
        out_shape=[jax.ShapeDtypeStruct((n, c), F32) for c in widths]
                  + [jax.ShapeDtypeStruct((bsz, HIST_PAD, SSM_XBC), F32),
                     jax.ShapeDtypeStruct((bsz, HIST_PAD, GDN_QKV), F32)],
        scratch_shapes=[pltpu.VMEM((nseq, seg + HIST_PAD, SSM_XBC), F32),
                        pltpu.VMEM((nseq, seg + HIST_PAD, GDN_QKV), F32)],
        compiler_params=_cparams("arbitrary"),
        name="norm_proj",
    )(x.reshape(n, D_MODEL), g, wa, wzx, wqg, ws, pad(hist_x), pad(hist_q), cwx, cbx, cwq)


CONV_PAD = 32
CONV_ROWS = 32


def _conv_a_kernel(a_ref, hist_ref, w_ref, b_ref, lg_ref, lb_ref, y_ref, nh_ref, buf_ref, sh_ref, *, tt):
    t = pl.program_id(1)

    @pl.when(t == 0)
    def _():
        buf_ref[0:CONV_PAD, :] = hist_ref[0]

    a = a_ref[0]
    glu = a[:, :CONV_CH] * _sigmoid(a[:, CONV_CH:])
    buf_ref[CONV_PAD:CONV_PAD + tt, :] = glu
    full = buf_ref[...]
    n = tt + CONV_PAD
    sh_ref[0] = full
    for s in range(1, SUBLANES):
        sh_ref[s] = pltpu.roll(full, n - s, axis=0)
    off = CONV_PAD - (CONV_W - 1)
    for r0 in range(0, tt, CONV_ROWS):
        acc = jnp.zeros((CONV_ROWS, CONV_CH), F32)
        for k in range(CONV_W):
            s = (off + k) % SUBLANES
            base = r0 + off + k - s
            acc = acc + w_ref[k:k + 1, :] * sh_ref[s, base:base + CONV_ROWS, :]
        y = acc + b_ref[...]
        mu = jnp.mean(y, axis=-1, keepdims=True)
        yc = y - mu
        var = jnp.mean(yc * yc, axis=-1, keepdims=True)
        y = yc * lax.rsqrt(var + EPS) * lg_ref[...] + lb_ref[...]
        y_ref[0, r0:r0 + CONV_ROWS, :] = _silu(y).astype(y_ref.dtype)
    nh_ref[0] = buf_ref[tt:tt + CONV_PAD, :]
    buf_ref[0:CONV_PAD, :] = buf_ref[tt:tt + CONV_PAD, :]


def _conv_a(a_in, hist, w, b, lg, lb):
    bsz, t, _ = a_in.shape
    tt = _row_tile(t, 256)
    hist_p = jnp.pad(hist, ((0, 0), (CONV_PAD - (CONV_W - 1), 0), (0, 0)))
    vec = lambda v: pl.BlockSpec(v.shape, lambda i, j: (0, 0))
    y, nh = pl.pallas_call(
        functools.partial(_conv_a_kernel, tt=tt),
        grid=(bsz, t // tt),
        in_specs=[pl.BlockSpec((1, tt, 2 * CONV_CH), lambda i, j: (i, j, 0)),
                  pl.BlockSpec((1, CONV_PAD, CONV_CH), lambda i, j: (i, 0, 0)),
                  vec(w), vec(b), vec(lg), vec(lb)],
        out_specs=[pl.BlockSpec((1, tt, CONV_CH), lambda i, j: (i, j, 0)),
                   pl.BlockSpec((1, CONV_PAD, CONV_CH), lambda i, j: (i, 0, 0))],
        out_shape=[jax.ShapeDtypeStruct((bsz, t, CONV_CH), BF16),
                   jax.ShapeDtypeStruct((bsz, CONV_PAD, CONV_CH), F32)],
        scratch_shapes=[pltpu.VMEM((tt + CONV_PAD, CONV_CH), F32),
                        pltpu.VMEM((SUBLANES, tt + CONV_PAD, CONV_CH), F32)],
        compiler_params=_cparams("arbitrary", "arbitrary"),
        name="conv_a",
    )(a_in, hist_p, w, b, lg, lb)
    return y, nh[:, CONV_PAD - (CONV_W - 1):, :]


HIST_PAD = 8


def _short_conv(x, buf_ref, b, w_ref, width, rows):
    buf_ref[b, HIST_PAD:HIST_PAD + rows, :] = x
    off = HIST_PAD - (width - 1)
    acc = w_ref[0:1, :] * buf_ref[b, off:off + rows, :]
    for k in range(1, width):
        acc = acc + w_ref[k:k + 1, :] * buf_ref[b, off + k:off + k + rows, :]
    buf_ref[b, 0:HIST_PAD, :] = buf_ref[b, rows:rows + HIST_PAD, :]
    return acc


def _lower_tri(rows):
    return _iota((rows, rows), 0) >= _iota((rows, rows), 1)


def _pair_mask():
    return _iota((1, LANES), 1) < HEAD_DIM


def _ssd_kernel(zx_ref, s_ref, h0_ref, dtb_ref, arow_ref, dx_ref, ng_ref,
                ex_ref, y_ref, hout_ref, h_ref, *, rows, bb):
    c = pl.program_id(1)

    hpg = SSM_HEADS // SSM_GROUPS

    @pl.when(c == 0)
    def _():
        for b in range(bb):
            for p in range(N_PAIRS):
                cols = []
                for hh in range(2):
                    ht = jnp.transpose(h0_ref[b, 2 * p + hh])
                    z = jnp.zeros_like(ht)
                    cols.append(jnp.concatenate([ht, z] if (2 * p + hh) // hpg == 0 else [z, ht], axis=0))
                h_ref[b, p] = jnp.concatenate(cols, axis=1)

    causal = _lower_tri(rows)
    lane = _iota((1, LANES), 1)
    first_half = _pair_mask()
    srow_g = _iota((LANES, LANES), 0) // SSM_STATE
    scol_h = _iota((LANES, LANES), 1) // HEAD_DIM
    ex = ex_ref[...]
    tril = causal.astype(BF16)
    assert SSM_INNER // SSM_GROUPS == LANES + HEAD_DIM and N_PAIRS == 3
    top_rows = _iota((LANES, LANES), 0) < HEAD_DIM
    ones_all = jnp.ones((LANES, LANES), BF16)
    ones_top = top_rows.astype(BF16)
    ones_bot = jnp.logical_not(top_rows).astype(BF16)

    for b in range(bb):
        zx = zx_ref[b]
        z = zx[:, :SSM_INNER]
        xbc = zx[:, SSM_INNER:]
        xs = xbc[:, :SSM_INNER]
        bm = xbc[:, SSM_INNER:SSM_INNER + LANES]
        cm = xbc[:, SSM_INNER + LANES:]

        dt = _softplus(s_ref[b] + dtb_ref[...])
        a = dt * arow_ref[...]
        acum = _sel_dot(tril, a)
        a_last = acum[rows - 1:rows, :]
        dt_x = _dot_sel(dt, ex)
        ea_x = _dot_sel(jnp.exp(acum), ex)
        te_x = _dot_sel(jnp.exp(a_last - acum), ex)
        cd_x = _dot_sel(jnp.broadcast_to(jnp.exp(a_last), (SUBLANES, LANES)), ex)[0:1, :]

        scores = []
        for g in range(SSM_GROUPS):
            cm_g = jnp.where(lane // SSM_STATE == g, cm, 0.0)
            scores.append(_dot_nt(cm_g, bm))
        bm_t = jnp.transpose(bm)

        ys = []
        for p in range(N_PAIRS):
            sl = slice(p * LANES, (p + 1) * LANES)
            x_p = xs[:, sl]
            xdt = x_p * dt_x[:, sl]
            yd = []
            for hh in range(2):
                h = 2 * p + hh
                col = jnp.broadcast_to(acum[:, h:h + 1], (rows, rows))
                dec = jnp.exp(jnp.where(causal, col - jnp.transpose(col), -1e30))
                yd.append(_dot(scores[h // (SSM_HEADS // SSM_GROUPS)] * dec, xdt))
            y_diag = jnp.where(first_half, yd[0], yd[1])
            h_p = h_ref[b, p]
            y_off = _dot(cm, h_p) * ea_x[:, sl]
            keep = srow_g == (2 * p + scol_h) // (SSM_HEADS // SSM_GROUPS)
            upd = _dot(bm_t, xdt * te_x[:, sl])
            h_ref[b, p] = h_p * cd_x[:, sl] + jnp.where(keep, upd, 0.0)
            ys.append(y_diag + y_off + dx_ref[:, sl] * x_p)
        y = jnp.concatenate(ys, axis=-1) * _silu(z)
        sq = [jnp.square(y[:, p * LANES:(p + 1) * LANES]) for p in range(N_PAIRS)]
        g0 = _dot_sel(sq[0], ones_all) + _dot_sel(sq[1], ones_top)
        g1 = _dot_sel(sq[1], ones_bot) + _dot_sel(sq[2], ones_all)
        ms = jnp.concatenate([g0, jnp.where(first_half, g0, g1), g1], axis=-1) * (1.0 / (SSM_INNER // SSM_GROUPS))
        y = y * lax.rsqrt(ms + EPS) * ng_ref[...]
        y_ref[b] = y.astype(y_ref.dtype)

    @pl.when(c == pl.num_programs(1) - 1)
    def _():
        for b in range(bb):
            for h in range(SSM_HEADS):
                g, hh = h // hpg, h % 2
                blk = h_ref[b, h // 2][g * SSM_STATE:(g + 1) * SSM_STATE, hh * HEAD_DIM:(hh + 1) * HEAD_DIM]
                hout_ref[b, h] = jnp.transpose(blk)


def _expand_matrix(first_lane):
    m = np.zeros((LANES, SSM_INNER), np.float32)
    for h in range(SSM_HEADS):
        m[first_lane + h, h * HEAD_DIM:(h + 1) * HEAD_DIM] = 1.0
    return jnp.asarray(m, BF16)


def _lane_row(vals, first_lane):
    return jnp.zeros((1, LANES), F32).at[0, first_lane:first_lane + vals.shape[0]].set(vals.astype(F32))


def _ssd(zx, small, h0, dt_bias, a_log, d_skip, norm_g, rows, bb):
    bsz, t, _ = zx.shape
    dtb = _lane_row(dt_bias, 0)
    arow = _lane_row(-jnp.exp(a_log.astype(F32)), 0)
    dx = jnp.repeat(d_skip.astype(F32), HEAD_DIM)[None, :]
    full = lambda v: pl.BlockSpec(v.shape, lambda i, j: (0,) * v.ndim)
    blk = lambda c: pl.BlockSpec((bb, rows, c), lambda i, j: (i, j, 0))
    st = pl.BlockSpec((bb, SSM_HEADS, HEAD_DIM, SSM_STATE), lambda i, j: (i, 0, 0, 0))
    consts = (dtb, arow, dx, norm_g[None, :], _expand_matrix(0))
    return pl.pallas_call(
        functools.partial(_ssd_kernel, rows=rows, bb=bb),
        grid=(bsz // bb, t // rows),
        in_specs=[blk(zx.shape[-1]), blk(LANES), st] + [full(v) for v in consts],
        out_specs=[blk(SSM_INNER), st],
        out_shape=[jax.ShapeDtypeStruct((bsz, t, SSM_INNER), BF16),
                   jax.ShapeDtypeStruct((bsz, SSM_HEADS, HEAD_DIM, SSM_STATE), F32)],
        scratch_shapes=[pltpu.VMEM((bb, N_PAIRS, LANES, LANES), F32)],
        compiler_params=_cparams("arbitrary", "arbitrary"),
        name="ssd",
    )(zx, small, h0, *consts)


BETA_LANE = 6
DECAY_LANE = 12


def _bdot(a, b):
    return lax.dot_general(a.astype(BF16), b.astype(BF16), (((2,), (1,)), ((0,), (0,))),
                           preferred_element_type=F32)


def _bdot_nt(a, b):
    return lax.dot_general(a.astype(BF16), b.astype(BF16), (((2,), (2,)), ((0,), (0,))),
                           preferred_element_type=F32)


def _unit_lower_inverse(m, block):
    rdim = m.shape[-1]
    eye = (_iota((rdim, rdim), 0) == _iota((rdim, rdim), 1)).astype(F32)
    x = -m
    t = eye + x
    p = x
    for _ in range(int(math.log2(block)) - 2):
        p = _bdot(p, p)
        t = t + _bdot(t, p)
    resid = (eye - t) - _bdot(m, t)
    return t + _bdot(t, resid)


def _gdn_kernel(qg_ref, s_ref, s0_ref, bias_ref, arow_ref, ng_ref, eb_ref, eg_ref,
                y_ref, sout_ref, st_ref, *, tt, ck, bb):
    c = pl.program_id(1)
    nck = tt // ck
    rdim = 2 * ck

    @pl.when(c == 0)
    def _():
        z = jnp.zeros((HEAD_DIM, HEAD_DIM), F32)
        for b in range(bb):
            for p in range(N_PAIRS):
                top = jnp.concatenate([s0_ref[b, 2 * p], z], axis=1)
                bot = jnp.concatenate([z, s0_ref[b, 2 * p + 1]], axis=1)
                st_ref[b, p] = jnp.concatenate([top, bot], axis=0)

    r_i = _iota((rdim, rdim), 0)
    c_i = _iota((rdim, rdim), 1)
    same_blk = (r_i // ck) == (c_i // ck)
    incl = same_blk & (r_i >= c_i)
    strict = same_blk & (r_i > c_i)
    first_half = _pair_mask()
    same_head = (_iota((LANES, LANES), 0) // HEAD_DIM) == (_iota((LANES, LANES), 1) // HEAD_DIM)
    tr = _iota((tt, tt), 0)
    tc = _iota((tt, tt), 1)
    blk_tril = (((tr // ck) == (tc // ck)) & (tr >= tc)).astype(BF16)
    eb = eb_ref[...]
    eg = eg_ref[...]
    hm = same_head.astype(BF16)
    head_sumsq = lambda x: jnp.concatenate(
        [_dot_sel(jnp.square(x[:, p * LANES:(p + 1) * LANES]), hm) for p in range(N_PAIRS)], axis=-1)

    tiles = {name: [] for name in ('q', 'k', 'kb', 'rhs', 'qd', 'kd', 'col', 'cd')}
    gates = []
    for b in range(bb):
        qg = qg_ref[b]
        gates.append(qg[:, GDN_QKV:])
        qkv = qg[:, :GDN_QKV]
        q = qkv[:, :GDN_INNER]
        k = qkv[:, GDN_INNER:2 * GDN_INNER]
        v = qkv[:, 2 * GDN_INNER:]
        q = q * lax.rsqrt(head_sumsq(q) + EPS) * (HEAD_DIM ** -0.5)
        k = k * lax.rsqrt(head_sumsq(k) + EPS)
        s = s_ref[b]
        beta = _sigmoid(s)
        g = _softplus(s + bias_ref[...]) * arow_ref[...]
        gc = _sel_dot(blk_tril, g)
        g_last = jnp.concatenate(
            [jnp.broadcast_to(gc[(i + 1) * ck - 1:(i + 1) * ck, :], (ck, LANES)) for i in range(nck)], axis=0)
        beta_x = _dot_sel(beta, eb)
        eg_x = _dot_sel(jnp.exp(gc), eg)
        kd_x = _dot_sel(jnp.exp(g_last - gc), eg)
        cd_x = _dot_sel(jnp.exp(g_last), eg)
        kb = k * beta_x
        full = dict(q=q, k=k, kb=kb, qd=q * eg_x, kd=k * kd_x, cd=cd_x)
        vb = v * beta_x
        kbe = kb * eg_x
        for i in range(nck):
            rs = slice(i * ck, (i + 1) * ck)
            for p in range(N_PAIRS):
                sl = slice(p * LANES, (p + 1) * LANES)
                for name, arr in full.items():
                    tiles[name].append(arr[rs, sl])
                tiles['rhs'].append(jnp.concatenate([vb[rs, sl], kbe[rs, sl]], axis=-1))
                lane0 = DECAY_LANE + 2 * p
                tiles['col'].append(jnp.concatenate(
                    [jnp.broadcast_to(gc[rs, lane0 + hh:lane0 + hh + 1], (ck, rdim)) for hh in range(2)], axis=0))

    st = lambda name: jnp.stack(tiles[name], axis=0)
    stack2 = lambda x: jnp.concatenate([jnp.where(first_half, x, 0.0), jnp.where(first_half, 0.0, x)], axis=1)
    k_st = stack2(st('k'))
    col = st('col')
    diff = col - jnp.swapaxes(col, 1, 2)
    dec = jnp.exp(jnp.where(incl, diff, -1e30))
    m = _bdot_nt(stack2(st('kb')), k_st) * jnp.where(strict, dec, 0.0)
    t_inv = _unit_lower_inverse(m, ck)
    rhs = st('rhs')
    sol = _bdot(t_inv, jnp.concatenate([rhs, rhs], axis=1))
    u = jnp.where(first_half, sol[:, :ck, :LANES], sol[:, ck:, :LANES])
    w = jnp.where(first_half, sol[:, :ck, LANES:], sol[:, ck:, LANES:])
    attn = _bdot_nt(stack2(st('q')), k_st) * dec
    qd = st('qd')
    kd_t = jnp.swapaxes(st('kd'), 1, 2)
    cd = st('cd')

    gsel = lambda x, i: jnp.stack([x[(b * nck + i) * N_PAIRS + p] for b in range(bb) for p in range(N_PAIRS)], axis=0)
    state = st_ref[...].reshape(bb * N_PAIRS, LANES, LANES)
    o_chunks = []
    for i in range(nck):
        v_new = gsel(u, i) - _bdot(gsel(w, i), state)
        intra = _bdot(gsel(attn, i), jnp.concatenate([v_new, v_new], axis=1))
        o_chunks.append(_bdot(gsel(qd, i), state) + jnp.where(first_half, intra[:, :ck], intra[:, ck:]))
        upd = _bdot(gsel(kd_t, i), v_new)
        state = state * gsel(cd, i)[:, 0:1, :] + jnp.where(same_head, upd, 0.0)
    st_ref[...] = state.reshape(bb, N_PAIRS, LANES, LANES)

    for b in range(bb):
        o = jnp.concatenate(
            [jnp.concatenate([o_chunks[i][b * N_PAIRS + p] for p in range(N_PAIRS)], axis=-1) for i in range(nck)],
            axis=0)
        ms = head_sumsq(o) * (1.0 / HEAD_DIM)
        o = o * lax.rsqrt(ms + EPS) * ng_ref[...] * _silu(gates[b])
        y_ref[b] = o.astype(y_ref.dtype)

    @pl.when(c == pl.num_programs(1) - 1)
    def _():
        for b in range(bb):
            for h in range(GDN_HEADS):
                lo = (h % 2) * HEAD_DIM
                sout_ref[b, h] = st_ref[b, h // 2][lo:lo + HEAD_DIM, lo:lo + HEAD_DIM]


def _gdn(qg, small, s0, a_log, dt_bias, norm_g, tt, ck, bb):
    bsz, t, _ = qg.shape
    bias = _lane_row(dt_bias, DECAY_LANE)
    arow = _lane_row(-jnp.exp(a_log.astype(F32)), DECAY_LANE)
    ng = jnp.tile(norm_g.astype(F32), GDN_HEADS)[None, :]
    full = lambda v: pl.BlockSpec(v.shape, lambda i, j: (0,) * v.ndim)
    blk = lambda c: pl.BlockSpec((bb, tt, c), lambda i, j: (i, j, 0))
    st = pl.BlockSpec((bb, GDN_HEADS, HEAD_DIM, HEAD_DIM), lambda i, j: (i, 0, 0, 0))
    consts = (bias, arow, ng, _expand_matrix(BETA_LANE), _expand_matrix(DECAY_LANE))
    return pl.pallas_call(
        functools.partial(_gdn_kernel, tt=tt, ck=ck, bb=bb),
        grid=(bsz // bb, t // tt),
        in_specs=[blk(qg.shape[-1]), blk(LANES), st] + [full(v) for v in consts],
        out_specs=[blk(GDN_INNER), st],
        out_shape=[jax.ShapeDtypeStruct((bsz, t, GDN_INNER), BF16),
                   jax.ShapeDtypeStruct((bsz, GDN_HEADS, HEAD_DIM, HEAD_DIM), F32)],
        scratch_shapes=[pltpu.VMEM((bb, N_PAIRS, LANES, LANES), F32)],
        compiler_params=_cparams("arbitrary", "arbitrary"),
        name="gdn",
    )(qg, small, s0, *consts)


def _mix_residual(x_ref, ya_ref, yb_ref, yc_ref, woa_ref, wob_ref, woc_ref):
    mix = (jnp.dot(ya_ref[...], woa_ref[...], preferred_element_type=F32)
           + jnp.dot(yb_ref[...], wob_ref[...], preferred_element_type=F32)
           + jnp.dot(yc_ref[...], woc_ref[...], preferred_element_type=F32))
    return x_ref[...] + mix


def _rms(x, g):
    return x * lax.rsqrt(jnp.mean(x * x, axis=-1, keepdims=True) + EPS) * g


def _ffn_kernel(x_ref, ya_ref, yb_ref, yc_ref, woa_ref, wob_ref, woc_ref, g_ref, wg_ref, wu_ref, wd_ref,
                o_ref, xn_ref, h_ref, acc_ref):
    j = pl.program_id(1)

    @pl.when(j == 0)
    def _():
        xn = _mix_residual(x_ref, ya_ref, yb_ref, yc_ref, woa_ref, wob_ref, woc_ref)
        xn_ref[...] = xn
        h_ref[...] = _rms(xn, g_ref[...]).astype(BF16)
        acc_ref[...] = jnp.zeros_like(acc_ref)

    h = h_ref[...]
    act = _silu(jnp.dot(h, wg_ref[...], preferred_element_type=F32)) * jnp.dot(h, wu_ref[...], preferred_element_type=F32)
    acc_ref[...] += jnp.dot(act.astype(BF16), wd_ref[...], preferred_element_type=F32)

    @pl.when(j == pl.num_programs(1) - 1)
    def _():
        o_ref[...] = xn_ref[...] + acc_ref[...]


FFN_CHUNKS = 2


def _out_ffn(x2d, ya, yb, yc, wo, g, wg, wu, wd):
    n = x2d.shape[0]
    tm = _row_tile(n, 512)
    f = wg.shape[1]
    tf = f // FFN_CHUNKS
    rows = lambda c: pl.BlockSpec((tm, c), lambda i, j: (i, 0))
    full = lambda w: pl.BlockSpec(w.shape, lambda i, j: (0, 0))
    return pl.pallas_call(
        _ffn_kernel,
        grid=(n // tm, FFN_CHUNKS),
        in_specs=[rows(D_MODEL), rows(ya.shape[1]), rows(yb.shape[1]), rows(yc.shape[1]),
                  full(wo[0]), full(wo[1]), full(wo[2]), full(g),
                  pl.BlockSpec((D_MODEL, tf), lambda i, j: (0, j)),
                  pl.BlockSpec((D_MODEL, tf), lambda i, j: (0, j)),
                  pl.BlockSpec((tf, D_MODEL), lambda i, j: (j, 0))],
        out_specs=rows(D_MODEL),
        out_shape=jax.ShapeDtypeStruct((n, D_MODEL), F32),
        scratch_shapes=[pltpu.VMEM((tm, D_MODEL), F32), pltpu.VMEM((tm, D_MODEL), BF16),
                        pltpu.VMEM((tm, D_MODEL), F32)],
        compiler_params=_cparams("parallel", "arbitrary"),
        name="out_ffn",
    )(x2d, ya, yb, yc, wo[0], wo[1], wo[2], g, wg, wu, wd)


ROUTE_TILE = 512


def _router_kernel(x_ref, ya_ref, yb_ref, yc_ref, woa_ref, wob_ref, woc_ref, g_ref, wr_ref, br_ref,
                   xn_ref, h_ref, gate_ref, slot_ref, cnt_ref, run_ref, *, cap):
    i = pl.program_id(0)

    @pl.when(i == 0)
    def _():
        run_ref[...] = jnp.zeros_like(run_ref)

    xn = _mix_residual(x_ref, ya_ref, yb_ref, yc_ref, woa_ref, wob_ref, woc_ref)
    xn_ref[...] = xn
    h = _rms(xn, g_ref[...])
    h_ref[...] = h
    tm = h.shape[0]
    lane = _iota((1, LANES), 1)
    logits = jnp.where(lane < N_EXPERTS, _dot_hp(h, wr_ref[...]) + br_ref[...], -jnp.inf)
    m1 = jnp.max(logits, axis=-1, keepdims=True)
    i1 = jnp.min(jnp.where(logits == m1, lane, LANES), axis=-1, keepdims=True)
    rest = jnp.where(lane == i1, -jnp.inf, logits)
    m2 = jnp.max(rest, axis=-1, keepdims=True)
    i2 = jnp.min(jnp.where(rest == m2, lane, LANES), axis=-1, keepdims=True)
    e2 = jnp.exp(m2 - m1)
    inv = 1.0 / (1.0 + e2)
    gate_ref[...] = jnp.where(lane == 0, inv, 0.0) + jnp.where(lane == 1, e2 * inv, 0.0)

    chosen = jnp.where((lane == i1) | (lane == i2), 1.0, 0.0)
    before = _iota((tm, tm), 0) > _iota((tm, tm), 1)
    rank = jnp.dot(before.astype(BF16), chosen.astype(BF16), preferred_element_type=F32) + run_ref[0:1, :]
    base = lane.astype(F32) * float(cap)
    pick = lambda idx: jnp.sum(jnp.where(lane == idx, rank + base, 0.0), axis=-1, keepdims=True)
    slots = jnp.where(lane == 0, pick(i1), 0.0) + jnp.where(lane == 1, pick(i2), 0.0)
    slots_t = jnp.transpose(slots).astype(jnp.int32)
    for k in range(2):
        slot_ref[0, k] = jnp.concatenate(
            [slots_t[k:k + 1, c * LANES:(c + 1) * LANES] for c in range(tm // LANES)], axis=0)
    run_ref[...] = run_ref[...] + jnp.sum(chosen, axis=0, keepdims=True)
    cnt_ref[...] = run_ref[...].astype(jnp.int32)


def _out_router(x2d, ya, yb, yc, wo, g, w_router, b_router, cap):
    n = x2d.shape[0]
    tm = _row_tile(n, ROUTE_TILE)
    wr = jnp.zeros((D_MODEL, LANES), F32).at[:, :N_EXPERTS].set(w_router.astype(F32))
    br = _lane_row(b_router, 0)
    rows = lambda c: pl.BlockSpec((tm, c), lambda i: (i, 0))
    full = lambda w: pl.BlockSpec(w.shape, lambda i: (0, 0))
    return pl.pallas_call(
        functools.partial(_router_kernel, cap=cap),
        grid=(n // tm,),
        in_specs=[rows(D_MODEL), rows(ya.shape[1]), rows(yb.shape[1]), rows(yc.shape[1]),
                  full(wo[0]), full(wo[1]), full(wo[2]), full(g), full(wr), full(br)],
        out_specs=[rows(D_MODEL), rows(D_MODEL), rows(LANES),
                   pl.BlockSpec((1, 2, tm // LANES, LANES), lambda i: (i, 0, 0, 0)),
                   pl.BlockSpec((SUBLANES, LANES), lambda i: (0, 0))],
        out_shape=[jax.ShapeDtypeStruct((n, D_MODEL), F32), jax.ShapeDtypeStruct((n, D_MODEL), F32),
                   jax.ShapeDtypeStruct((n, LANES), F32),
                   jax.ShapeDtypeStruct((n // tm, 2, tm // LANES, LANES), jnp.int32),
                   jax.ShapeDtypeStruct((SUBLANES, LANES), jnp.int32)],
        scratch_shapes=[pltpu.VMEM((SUBLANES, LANES), F32)],
        compiler_params=_cparams("arbitrary"),
        name="out_router",
    )(x2d, ya, yb, yc, wo[0], wo[1], wo[2], g, wr, br)


def _row_copy(src_ref, src_row, dst_ref, dst_row, sem):
    return pltpu.make_async_copy(src_ref.at[pl.ds(src_row, 1)], dst_ref.at[pl.ds(dst_row, 1)], sem)


def _slot_fetch(slots_hbm, tile, smem_ref, buf, sem):
    return pltpu.make_async_copy(slots_hbm.at[tile], smem_ref.at[buf], sem.at[buf])


def _for_rows(tm, fn):
    for c in range(tm // LANES):
        for base in range(0, LANES, SUBLANES):
            for j in range(SUBLANES):
                fn(c, base, j)


def _vmem_row(ref, base, j):
    return ref.at[pl.ds(base, SUBLANES)].at[pl.ds(j, 1)]


def _dispatch_kernel(cnt_ref, slots_hbm, h_ref, hs_hbm, slot_smem, zero_ref, slot_sem, row_sem, pad_sem,
                     *, tm, cap, bm):
    i = pl.program_id(0)
    n_tiles = pl.num_programs(0)

    @pl.when(i == 0)
    def _():
        _slot_fetch(slots_hbm, 0, slot_smem, 0, slot_sem).start()

    @pl.when(i + 1 < n_tiles)
    def _():
        _slot_fetch(slots_hbm, i + 1, slot_smem, (i + 1) % 2, slot_sem).start()

    _slot_fetch(slots_hbm, i, slot_smem, i % 2, slot_sem).wait()

    def issue(c, base, j):
        for k in range(2):
            dst = hs_hbm.at[pl.ds(slot_smem[i % 2, k, c, base + j], 1)]
            pltpu.make_async_copy(_vmem_row(h_ref, c * LANES + base, j), dst, row_sem).start(priority=k)

    _for_rows(tm, issue)

    def drain(r, carry):
        for k in range(2):
            _row_copy(h_ref, 0, hs_hbm, 0, row_sem).wait()
        return carry

    lax.fori_loop(0, tm, drain, 0, unroll=8)

    @pl.when(i == n_tiles - 1)
    def _():
        zero_ref[...] = jnp.zeros_like(zero_ref)
        for e in range(N_EXPERTS):
            c = cnt_ref[e]
            n_pad = ((c + bm - 1) // bm) * bm - c

            def fill(r, carry):
                _row_copy(zero_ref, 0, hs_hbm, e * cap + c + r, pad_sem).start()
                return carry

            def fill_wait(r, carry):
                _row_copy(zero_ref, 0, hs_hbm, 0, pad_sem).wait()
                return carry

            lax.fori_loop(0, n_pad, fill, 0)
            lax.fori_loop(0, n_pad, fill_wait, 0)


def _dispatch(counts, slots, h, cap, bm):
    n = h.shape[0]
    n_tiles = slots.shape[0]
    tm = slots.shape[2] * LANES
    return pl.pallas_call(
        functools.partial(_dispatch_kernel, tm=tm, cap=cap, bm=bm),
        grid_spec=pltpu.PrefetchScalarGridSpec(
            num_scalar_prefetch=1,
            grid=(n_tiles,),
            in_specs=[pl.BlockSpec(memory_space=pl.ANY), pl.BlockSpec((tm, D_MODEL), lambda i, cnt: (i, 0))],
            out_specs=pl.BlockSpec(memory_space=pl.ANY),
            scratch_shapes=[pltpu.SMEM((2, 2, tm // LANES, LANES), jnp.int32), pltpu.VMEM((SUBLANES, D_MODEL), F32),
                            pltpu.SemaphoreType.DMA((2,)), pltpu.SemaphoreType.DMA, pltpu.SemaphoreType.DMA],
        ),
        out_shape=jax.ShapeDtypeStruct((N_EXPERTS * cap, D_MODEL), F32),
        compiler_params=_cparams("arbitrary"),
        name="moe_dispatch",
    )(counts, slots, h)


def _experts_kernel(tbl_ref, hs_ref, wg_ref, wu_ref, wd_ref, ys_ref, hb_ref, acc_ref):
    s = pl.program_id(0)
    j = pl.program_id(1)

    @pl.when(tbl_ref[2, s] == 1)
    def _():
        @pl.when(j == 0)
        def _():
            hb_ref[...] = hs_ref[...].astype(BF16)
            acc_ref[...] = jnp.zeros_like(acc_ref)

        h = hb_ref[...]
        act = (_silu(jnp.dot(h, wg_ref[0], preferred_element_type=F32))
               * jnp.dot(h, wu_ref[0], preferred_element_type=F32))
        acc_ref[...] += jnp.dot(act.astype(BF16), wd_ref[0], preferred_element_type=F32)

        @pl.when(j == pl.num_programs(1) - 1)
        def _():
            ys_ref[...] = acc_ref[...]


def _block_table(counts, cap, bm, n_steps):
    nblk = (counts + bm - 1) // bm
    cum = jnp.cumsum(nblk)
    total = cum[-1]
    step = jnp.arange(n_steps, dtype=jnp.int32)
    last = jnp.maximum(total - 1, 0)
    eff = jnp.minimum(step, last)
    expert = jnp.minimum(jnp.sum(eff[:, None] >= cum[None, :], axis=1), N_EXPERTS - 1).astype(jnp.int32)
    blk = eff - (cum[expert] - nblk[expert])
    return jnp.stack([expert * (cap // bm) + blk, expert, (step < total).astype(jnp.int32)]).astype(jnp.int32)


def _experts(counts, hs, wg, wu, wd, cap, bm, n_assign):
    n_steps = n_assign // bm + N_EXPERTS
    tbl = _block_table(counts, cap, bm, n_steps)
    f = wg.shape[2]
    tf = f // FFN_CHUNKS
    chunk = lambda s, j, t: jnp.where(t[2, s] == 1, j, FFN_CHUNKS - 1)
    rows = pl.BlockSpec((bm, D_MODEL), lambda s, j, t: (t[0, s], 0))
    return pl.pallas_call(
        _experts_kernel,
        grid_spec=pltpu.PrefetchScalarGridSpec(
            num_scalar_prefetch=1,
            grid=(n_steps, FFN_CHUNKS),
            in_specs=[rows,
                      pl.BlockSpec((1, D_MODEL, tf), lambda s, j, t: (t[1, s], 0, chunk(s, j, t))),
                      pl.BlockSpec((1, D_MODEL, tf), lambda s, j, t: (t[1, s], 0, chunk(s, j, t))),
                      pl.BlockSpec((1, tf, D_MODEL), lambda s, j, t: (t[1, s], chunk(s, j, t), 0))],
            out_specs=rows,
            scratch_shapes=[pltpu.VMEM((bm, D_MODEL), BF16), pltpu.VMEM((bm, D_MODEL), F32)],
        ),
        out_shape=jax.ShapeDtypeStruct(hs.shape, F32),
        compiler_params=_cparams("arbitrary", "arbitrary"),
        name="moe_experts",
    )(tbl, hs, wg, wu, wd)


def _combine_kernel(slots_hbm, ys_hbm, xn_ref, gate_ref, gf_ref, o_ref, slot_smem, ybuf_ref, slot_sem, row_sem,
                    *, tm):
    i = pl.program_id(0)
    n_tiles = pl.num_programs(0)

    def gather(tile, buf):
        def issue(c, base, j):
            for k in range(2):
                src = ys_hbm.at[pl.ds(slot_smem[buf, k, c, base + j], 1)]
                dst = _vmem_row(ybuf_ref.at[buf, k], c * LANES + base, j)
                pltpu.make_async_copy(src, dst, row_sem.at[buf]).start(priority=k)
        _for_rows(tm, issue)

    @pl.when(i == 0)
    def _():
        first = _slot_fetch(slots_hbm, 0, slot_smem, 0, slot_sem)
        first.start()
        first.wait()
        gather(0, 0)

        @pl.when(n_tiles > 1)
        def _():
            _slot_fetch(slots_hbm, 1, slot_smem, 1, slot_sem).start()

    @pl.when(i + 1 < n_tiles)
    def _():
        _slot_fetch(slots_hbm, i + 1, slot_smem, (i + 1) % 2, slot_sem).wait()
        gather(i + 1, (i + 1) % 2)

    def drain(r, carry):
        for k in range(2):
            _row_copy(ys_hbm, 0, ybuf_ref.at[i % 2, k], 0, row_sem.at[i % 2]).wait()
        return carry

    lax.fori_loop(0, tm, drain, 0, unroll=8)

    @pl.when(i + 2 < n_tiles)
    def _():
        _slot_fetch(slots_hbm, i + 2, slot_smem, i % 2, slot_sem).start()

    g = gate_ref[...]
    y = g[:, 0:1] * ybuf_ref[i % 2, 0] + g[:, 1:2] * ybuf_ref[i % 2, 1]
    o_ref[...] = _rms(xn_ref[...] + y, gf_ref[...])


def _combine(slots, ys, xn, gates, g_final):
    n = xn.shape[0]
    n_tiles = slots.shape[0]
    tm = slots.shape[2] * LANES
    rows = lambda c: pl.BlockSpec((tm, c), lambda i: (i, 0))
    return pl.pallas_call(
        functools.partial(_combine_kernel, tm=tm),
        grid=(n_tiles,),
        in_specs=[pl.BlockSpec(memory_space=pl.ANY), pl.BlockSpec(memory_space=pl.ANY),
                  rows(D_MODEL), rows(LANES), pl.BlockSpec(g_final.shape, lambda i: (0, 0))],
        out_specs=rows(D_MODEL),
        out_shape=jax.ShapeDtypeStruct((n, D_MODEL), F32),
        scratch_shapes=[pltpu.SMEM((2, 2, tm // LANES, LANES), jnp.int32), pltpu.VMEM((2, 2, tm, D_MODEL), F32),
                        pltpu.SemaphoreType.DMA((2,)), pltpu.SemaphoreType.DMA((2,))],
        compiler_params=_cparams("arbitrary"),
        name="moe_combine",
    )(slots, ys, xn, gates, g_final)


def _moe(x2d, ya, yb, yc, wo, g_ffn, w_router, b_router, wg, wu, wd, g_final):
    n = x2d.shape[0]
    bm = _row_tile(n, 512)
    cap = n
    xn, h, gates, slots, counts = _out_router(x2d, ya, yb, yc, wo, g_ffn, w_router, b_router, cap)
    counts = counts[0, :N_EXPERTS]
    hs = _dispatch(counts, slots, h, cap, bm)
    ys = _experts(counts, hs, wg, wu, wd, cap, bm, 2 * n)
    return _combine(slots, ys, xn, gates, g_final)


IN_SIZES = (2 * CONV_CH, SSM_INNER, SSM_XBC, SSM_HEADS, GDN_QKV, GDN_INNER, GDN_HEADS, GDN_HEADS)


def _prep_layer(l, p):
    off = np.concatenate([[0], np.cumsum(IN_SIZES)])
    w_in = p['w_in_bf16'][l]
    col = lambda i: w_in[:, off[i]:off[i + 1]]
    a_in, z, xbc, dt, qkv, gate, b_raw, a_raw = (col(i) for i in range(8))
    small = jnp.concatenate([dt, b_raw, a_raw, jnp.zeros((D_MODEL, SMALL_W - 3 * SSM_HEADS), BF16)], axis=1)
    wo = p['w_out'][l].astype(BF16)
    return dict(
        wa=a_in,
        wzx=jnp.concatenate([z, xbc], axis=1),
        wqg=jnp.concatenate([qkv, gate], axis=1),
        ws=small,
        wo=(wo[:CONV_CH], wo[CONV_CH:CONV_CH + SSM_INNER], wo[CONV_CH + SSM_INNER:]),
    )


def _trunk(x, st_conv_a, st_ssm_conv, st_ssm, st_gdn_conv, st_gdn, p, prep, ssd_rows, gdn_rows, bb):
    bsz, t, _ = x.shape
    n = bsz * t
    depth = p['g_mix'].shape[0]
    x2d = x.reshape(n, D_MODEL)
    new = [[] for _ in range(5)]
    for l in range(depth):
        w = prep[l]
        a_in, zx, qg, small, hist_x, hist_q = _norm_proj(
            x2d.reshape(bsz, t, D_MODEL), p['g_mix'][l][None, :], w['wa'], w['wzx'], w['wqg'], w['ws'],
            st_ssm_conv[l], st_gdn_conv[l], p['ssm_conv_w'][l], p['ssm_conv_b'][l][None, :], p['gdn_conv_w'][l])
        a_in = a_in.reshape(bsz, t, -1)
        zx = zx.reshape(bsz, t, -1)
        qg = qg.reshape(bsz, t, -1)
        small = small.reshape(bsz, t, -1)
        ya, conv_a = _conv_a(a_in, st_conv_a[l], p['conv_a_w'][l], p['conv_a_b'][l][None, :],
                             p['ln_a_g'][l][None, :], p['ln_a_b'][l][None, :])
        yb, ssm = _ssd(zx, small, st_ssm[l].astype(F32), p['ssm_dt_bias'][l], p['ssm_a_log'][l],
                       p['ssm_d'][l], p['ssm_norm_g'][l], ssd_rows, bb)
        yc, gdn = _gdn(qg, small, st_gdn[l].astype(F32), p['gdn_a_log'][l], p['gdn_dt_bias'][l],
                       p['gdn_norm_g'][l], gdn_rows[0], gdn_rows[1], bb)
        new[0].append(conv_a)
        new[1].append(hist_x[:, HIST_PAD - (SSM_CONV_W - 1):])
        new[2].append(ssm)
        new[3].append(hist_q[:, HIST_PAD - (GDN_CONV_W - 1):])
        new[4].append(gdn)
        flat = lambda y: y.reshape(n, -1)
        g_ffn = p['g_ffn'][l][None, :]
        if l % 2 == 0:
            x2d = _out_ffn(x2d, flat(ya), flat(yb), flat(yc), w['wo'], g_ffn,
                           prep['ffn'][l // 2][0], prep['ffn'][l // 2][1], prep['ffn'][l // 2][2])
        else:
            x2d = _moe(x2d, flat(ya), flat(yb), flat(yc), w['wo'], g_ffn, p['moe_w_router'][l // 2],
                       p['moe_b_router'][l // 2], *prep['moe'][l // 2], p['g_final'][None, :])
    return (x2d.reshape(bsz, t, D_MODEL),) + tuple(jnp.stack(s) for s in new)


def kernel(x_prompt, x_sample, state_conv_a, state_ssm_conv, state_ssm, state_gdn_conv, state_gdn, g_mix, w_in, conv_a_w, conv_a_b, ln_a_g, ln_a_b, ssm_conv_w, ssm_conv_b, ssm_dt_bias, ssm_a_log, ssm_d, ssm_norm_g, gdn_conv_w, gdn_a_log, gdn_dt_bias, gdn_norm_g, w_out, g_ffn, ffn_w_gate, ffn_w_up, ffn_w_down, moe_w_router, moe_b_router, moe_w_gate, moe_w_up, moe_w_down, g_final):
    p = dict(g_mix=g_mix, w_in=w_in, conv_a_w=conv_a_w, conv_a_b=conv_a_b, ln_a_g=ln_a_g, ln_a_b=ln_a_b,
             ssm_conv_w=ssm_conv_w, ssm_conv_b=ssm_conv_b, ssm_dt_bias=ssm_dt_bias, ssm_a_log=ssm_a_log,
             ssm_d=ssm_d, ssm_norm_g=ssm_norm_g, gdn_conv_w=gdn_conv_w, gdn_a_log=gdn_a_log,
             gdn_dt_bias=gdn_dt_bias, gdn_norm_g=gdn_norm_g, w_out=w_out, g_ffn=g_ffn,
             moe_w_router=moe_w_router, moe_b_router=moe_b_router, g_final=g_final)
    depth = g_mix.shape[0]
    assert depth % 2 == 0, "the final RMSNorm is fused into the expert layer, which must come last"
    p['w_in_bf16'] = w_in.astype(BF16)
    prep = {l: _prep_layer(l, p) for l in range(depth)}
    prep['ffn'] = [(ffn_w_gate[i].astype(BF16), ffn_w_up[i].astype(BF16), ffn_w_down[i].astype(BF16))
                   for i in range(ffn_w_gate.shape[0])]
    prep['moe'] = [(moe_w_gate[i].astype(BF16), moe_w_up[i].astype(BF16), moe_w_down[i].astype(BF16))
                   for i in range(moe_w_gate.shape[0])]
    bp, dt = x_prompt.shape[0], x_prompt.dtype
    zeros = lambda *s: jnp.zeros((depth, bp) + s, dt)
    outs_p = _trunk(x_prompt, zeros(CONV_W - 1, CONV_CH), zeros(SSM_CONV_W - 1, SSM_XBC),
                    zeros(SSM_HEADS, HEAD_DIM, SSM_STATE), zeros(GDN_CONV_W - 1, GDN_QKV),
                    zeros(GDN_HEADS, HEAD_DIM, HEAD_DIM), p, prep, ssd_rows=min(128, x_prompt.shape[1]),
                    gdn_rows=(min(256, x_prompt.shape[1]), min(64, x_prompt.shape[1])), bb=2)
    outs_s = _trunk(x_sample, state_conv_a, state_ssm_conv, state_ssm, state_gdn_conv, state_gdn, p, prep,
                    ssd_rows=x_sample.shape[1], gdn_rows=(x_sample.shape[1], x_sample.shape[1]), bb=2)
    return (outs_p[0], outs_s[0]) + outs_p[1:] + outs_s[1:]
```

```python
import functools
import math

import jax
import jax.numpy as jnp
import numpy as np
from jax import lax
from jax.experimental import pallas as pl
from jax.experimental.pallas import tpu as pltpu

F32 = jnp.float32
BF16 = jnp.bfloat16
EPS = 1e-6

LANES = 128
SUBLANES = 8
VMEM_BYTES_V7X = 64 * 1024 * 1024
VMEM_LIMIT = VMEM_BYTES_V7X * 3 // 4

D_MODEL = 1024
CONV_CH = 256
CONV_W = 31
SSM_HEADS = 6
HEAD_DIM = 64
SSM_INNER = SSM_HEADS * HEAD_DIM
SSM_STATE = 64
SSM_GROUPS = 2
SSM_XBC = SSM_INNER + 2 * SSM_GROUPS * SSM_STATE
SSM_CONV_W = 4
GDN_HEADS = 6
GDN_INNER = GDN_HEADS * HEAD_DIM
GDN_QKV = 3 * GDN_INNER
GDN_CONV_W = 4
N_PAIRS = 3
N_EXPERTS = 8
SMALL_W = LANES


def _cparams(*sem):
    return pltpu.CompilerParams(dimension_semantics=sem, vmem_limit_bytes=VMEM_LIMIT)


def _dot(a, b):
    return jnp.dot(a.astype(BF16), b.astype(BF16), preferred_element_type=F32)


def _dot_nt(a, b):
    return lax.dot_general(a.astype(BF16), b.astype(BF16), (((1,), (1,)), ((), ())),
                           preferred_element_type=F32)


def _split3(x):
    hi = x.astype(BF16)
    r1 = x - hi.astype(F32)
    mid = r1.astype(BF16)
    lo = (r1 - mid.astype(F32)).astype(BF16)
    return hi, mid, lo


def _dot_sel(x, sel_bf16, pieces=2):
    d = lambda p: jnp.dot(p, sel_bf16, preferred_element_type=F32)
    hi = x.astype(BF16)
    r1 = x - hi.astype(F32)
    mid = r1.astype(BF16)
    if pieces == 2:
        return d(hi) + d(mid)
    return d(hi) + d(mid) + d((r1 - mid.astype(F32)).astype(BF16))


def _sel_dot(sel_bf16, x):
    hi, mid, lo = _split3(x)
    d = lambda p: jnp.dot(sel_bf16, p, preferred_element_type=F32)
    return d(hi) + d(mid) + d(lo)


def _dot_hp(a, b):
    ah = a.astype(BF16)
    al = (a - ah.astype(F32)).astype(BF16)
    bh = b.astype(BF16)
    bl = (b - bh.astype(F32)).astype(BF16)
    d = lambda p, q: jnp.dot(p, q, preferred_element_type=F32)
    return d(ah, bh) + (d(ah, bl) + d(al, bh))


def _silu(x):
    hx = 0.5 * x
    return hx + hx * jnp.tanh(hx)


def _sigmoid(x):
    return 0.5 + 0.5 * jnp.tanh(0.5 * x)


def _softplus(x):
    return jnp.maximum(x, 0.0) + jnp.log(1.0 + jnp.exp(-jnp.abs(x)))


def _iota(shape, dim):
    return lax.broadcasted_iota(jnp.int32, shape, dim)


def _norm_proj_kernel(x_ref, g_ref, wa_ref, wzx_ref, wqg_ref, ws_ref, hx_ref, hq_ref, cwx_ref, cbx_ref, cwq_ref,
                      a_ref, zx_ref, qg_ref, s_ref, nhx_ref, nhq_ref, bx_ref, bq_ref, *, nseq, seg, tiles_per_seq):
    first = (pl.program_id(0) % tiles_per_seq) == 0

    @pl.when(first)
    def _():
        bx_ref[:, 0:HIST_PAD, :] = hx_ref[...]
        bq_ref[:, 0:HIST_PAD, :] = hq_ref[...]

    x = x_ref[...]
    u = x * lax.rsqrt(jnp.mean(x * x, axis=-1, keepdims=True) + EPS) * g_ref[...]
    ub = u.astype(BF16)
    a_ref[...] = jnp.dot(ub, wa_ref[...], preferred_element_type=F32)
    s_ref[...] = jnp.dot(ub, ws_ref[...], preferred_element_type=F32)
    zx = jnp.dot(ub, wzx_ref[...], preferred_element_type=F32)
    qg = jnp.dot(ub, wqg_ref[...], preferred_element_type=F32)
    zx_ref[:, :SSM_INNER] = zx[:, :SSM_INNER]
    qg_ref[:, GDN_QKV:] = qg[:, GDN_QKV:]
    for s in range(nseq):
        rows = slice(s * seg, (s + 1) * seg)
        xbc = zx[rows, SSM_INNER:]
        nhx_ref[s] = _last_rows(bx_ref, s, xbc, seg)
        zx_ref[rows, SSM_INNER:] = _silu(_short_conv(xbc, bx_ref, s, cwx_ref, SSM_CONV_W, seg) + cbx_ref[...])
        qkv = qg[rows, :GDN_QKV]
        nhq_ref[s] = _last_rows(bq_ref, s, qkv, seg)
        qg_ref[rows, :GDN_QKV] = _silu(_short_conv(qkv, bq_ref, s, cwq_ref, GDN_CONV_W, seg))


def _last_rows(buf_ref, s, x, rows):
    if rows >= HIST_PAD:
        return x[rows - HIST_PAD:, :]
    return jnp.concatenate([buf_ref[s, rows:HIST_PAD, :], x], axis=0)


def _row_tile(n, want):
    t = min(want, n)
    while n % t:
        t //= 2
    return t


def _norm_proj(x, g, wa, wzx, wqg, ws, hist_x, hist_q, cwx, cbx, cwq):
    bsz, t, _ = x.shape
    n = bsz * t
    tm = _row_tile(n, 512)
    nseq = max(1, tm // t)
    seg = tm // nseq
    tiles_per_seq = max(1, t // tm)
    assert nseq * seg == tm and (t % tm == 0 or tm % t == 0)
    pad = lambda h: jnp.pad(h, ((0, 0), (HIST_PAD - h.shape[1], 0), (0, 0)))
    full = lambda w: pl.BlockSpec(w.shape, lambda i: (0, 0))
    rows = lambda c: pl.BlockSpec((tm, c), lambda i: (i, 0))
    hist = lambda c: pl.BlockSpec((nseq, HIST_PAD, c), lambda i: (i // tiles_per_seq, 0, 0))
    widths = (wa.shape[1], wzx.shape[1], wqg.shape[1], ws.shape[1])
    return pl.pallas_call(
        functools.partial(_norm_proj_kernel, nseq=nseq, seg=seg, tiles_per_seq=tiles_per_seq),
        grid=(n // tm,),
        in_specs=[rows(D_MODEL), full(g), full(wa), full(wzx), full(wqg), full(ws),
                  hist(SSM_XBC), hist(GDN_QKV), full(cwx), full(cbx), full(cwq)],
        out_specs=[rows(c) for c in widths] + [hist(SSM_XBC), hist(GDN_QKV)],
        out_shape=[jax.ShapeDtypeStruct((n, c), F32) for c in widths]
                  + [jax.ShapeDtypeStruct((bsz, HIST_PAD, SSM_XBC), F32),
                     jax.ShapeDtypeStruct((bsz, HIST_PAD, GDN_QKV), F32)],
        scratch_shapes=[pltpu.VMEM((nseq, seg + HIST_PAD, SSM_XBC), F32),
                        pltpu.VMEM((nseq, seg + HIST_PAD, GDN_QKV), F32)],
        compiler_params=_cparams("arbitrary"),
        name="norm_proj",
    )(x.reshape(n, D_MODEL), g, wa, wzx, wqg, ws, pad(hist_x), pad(hist_q), cwx, cbx, cwq)


CONV_PAD = 32
CONV_ROWS = 32


def _conv_a_kernel(a_ref, hist_ref, w_ref, b_ref, lg_ref, lb_ref, y_ref, nh_ref, buf_ref, sh_ref, *, tt):
    t = pl.program_id(1)

    @pl.when(t == 0)
    def _():
        buf_ref[0:CONV_PAD, :] = hist_ref[0]

    a = a_ref[0]
    glu = a[:, :CONV_CH] * _sigmoid(a[:, CONV_CH:])
    buf_ref[CONV_PAD:CONV_PAD + tt, :] = glu
    full = buf_ref[...]
    n = tt + CONV_PAD
    sh_ref[0] = full
    for s in range(1, SUBLANES):
        sh_ref[s] = pltpu.roll(full, n - s, axis=0)
    off = CONV_PAD - (CONV_W - 1)
    for r0 in range(0, tt, CONV_ROWS):
        acc = jnp.zeros((CONV_ROWS, CONV_CH), F32)
        for k in range(CONV_W):
            s = (off + k) % SUBLANES
            base = r0 + off + k - s
            acc = acc + w_ref[k:k + 1, :] * sh_ref[s, base:base + CONV_ROWS, :]
        y = acc + b_ref[...]
        mu = jnp.mean(y, axis=-1, keepdims=True)
        yc = y - mu
        var = jnp.mean(yc * yc, axis=-1, keepdims=True)
        y = yc * lax.rsqrt(var + EPS) * lg_ref[...] + lb_ref[...]
        y_ref[0, r0:r0 + CONV_ROWS, :] = _silu(y).astype(y_ref.dtype)
    nh_ref[0] = buf_ref[tt:tt + CONV_PAD, :]
    buf_ref[0:CONV_PAD, :] = buf_ref[tt:tt + CONV_PAD, :]


def _conv_a(a_in, hist, w, b, lg, lb):
    bsz, t, _ = a_in.shape
    tt = _row_tile(t, 256)
    hist_p = jnp.pad(hist, ((0, 0), (CONV_PAD - (CONV_W - 1), 0), (0, 0)))
    vec = lambda v: pl.BlockSpec(v.shape, lambda i, j: (0, 0))
    y, nh = pl.pallas_call(
        functools.partial(_conv_a_kernel, tt=tt),
        grid=(bsz, t // tt),
        in_specs=[pl.BlockSpec((1, tt, 2 * CONV_CH), lambda i, j: (i, j, 0)),
                  pl.BlockSpec((1, CONV_PAD, CONV_CH), lambda i, j: (i, 0, 0)),
                  vec(w), vec(b), vec(lg), vec(lb)],
        out_specs=[pl.BlockSpec((1, tt, CONV_CH), lambda i, j: (i, j, 0)),
                   pl.BlockSpec((1, CONV_PAD, CONV_CH), lambda i, j: (i, 0, 0))],
        out_shape=[jax.ShapeDtypeStruct((bsz, t, CONV_CH), BF16),
                   jax.ShapeDtypeStruct((bsz, CONV_PAD, CONV_CH), F32)],
        scratch_shapes=[pltpu.VMEM((tt + CONV_PAD, CONV_CH), F32),
                        pltpu.VMEM((SUBLANES, tt + CONV_PAD, CONV_CH), F32)],
        compiler_params=_cparams("arbitrary", "arbitrary"),
        name="conv_a",
    )(a_in, hist_p, w, b, lg, lb)
    return y, nh[:, CONV_PAD - (CONV_W - 1):, :]


HIST_PAD = 8


def _short_conv(x, buf_ref, b, w_ref, width, rows):
    buf_ref[b, HIST_PAD:HIST_PAD + rows, :] = x
    off = HIST_PAD - (width - 1)
    acc = w_ref[0:1, :] * buf_ref[b, off:off + rows, :]
    for k in range(1, width):
        acc = acc + w_ref[k:k + 1, :] * buf_ref[b, off + k:off + k + rows, :]
    buf_ref[b, 0:HIST_PAD, :] = buf_ref[b, rows:rows + HIST_PAD, :]
    return acc


def _lower_tri(rows):
    return _iota((rows, rows), 0) >= _iota((rows, rows), 1)


def _pair_mask():
    return _iota((1, LANES), 1) < HEAD_DIM


def _ssd_kernel(zx_ref, s_ref, h0_ref, dtb_ref, arow_ref, dx_ref, ng_ref,
                ex_ref, y_ref, hout_ref, h_ref, *, rows, bb):
    c = pl.program_id(1)

    hpg = SSM_HEADS // SSM_GROUPS

    @pl.when(c == 0)
    def _():
        for b in range(bb):
            for p in range(N_PAIRS):
                cols = []
                for hh in range(2):
                    ht = jnp.transpose(h0_ref[b, 2 * p + hh])
                    z = jnp.zeros_like(ht)
                    cols.append(jnp.concatenate([ht, z] if (2 * p + hh) // hpg == 0 else [z, ht], axis=0))
                h_ref[b, p] = jnp.concatenate(cols, axis=1)

    causal = _lower_tri(rows)
    lane = _iota((1, LANES), 1)
    first_half = _pair_mask()
    srow_g = _iota((LANES, LANES), 0) // SSM_STATE
    scol_h = _iota((LANES, LANES), 1) // HEAD_DIM
    ex = ex_ref[...]
    tril = causal.astype(BF16)
    assert SSM_INNER // SSM_GROUPS == LANES + HEAD_DIM and N_PAIRS == 3
    top_rows = _iota((LANES, LANES), 0) < HEAD_DIM
    ones_all = jnp.ones((LANES, LANES), BF16)
    ones_top = top_rows.astype(BF16)
    ones_bot = jnp.logical_not(top_rows).astype(BF16)

    for b in range(bb):
        zx = zx_ref[b]
        z = zx[:, :SSM_INNER]
        xbc = zx[:, SSM_INNER:]
        xs = xbc[:, :SSM_INNER]
        bm = xbc[:, SSM_INNER:SSM_INNER + LANES]
        cm = xbc[:, SSM_INNER + LANES:]

        dt = _softplus(s_ref[b] + dtb_ref[...])
        a = dt * arow_ref[...]
        acum = _sel_dot(tril, a)
        a_last = acum[rows - 1:rows, :]
        dt_x = _dot_sel(dt, ex)
        ea_x = _dot_sel(jnp.exp(acum), ex)
        te_x = _dot_sel(jnp.exp(a_last - acum), ex)
        cd_x = _dot_sel(jnp.broadcast_to(jnp.exp(a_last), (SUBLANES, LANES)), ex)[0:1, :]

        scores = []
        for g in range(SSM_GROUPS):
            cm_g = jnp.where(lane // SSM_STATE == g, cm, 0.0)
            scores.append(_dot_nt(cm_g, bm))
        bm_t = jnp.transpose(bm)

        ys = []
        for p in range(N_PAIRS):
            sl = slice(p * LANES, (p + 1) * LANES)
            x_p = xs[:, sl]
            xdt = x_p * dt_x[:, sl]
            yd = []
            for hh in range(2):
                h = 2 * p + hh
                col = jnp.broadcast_to(acum[:, h:h + 1], (rows, rows))
                dec = jnp.exp(jnp.where(causal, col - jnp.transpose(col), -1e30))
                yd.append(_dot(scores[h // (SSM_HEADS // SSM_GROUPS)] * dec, xdt))
            y_diag = jnp.where(first_half, yd[0], yd[1])
            h_p = h_ref[b, p]
            y_off = _dot(cm, h_p) * ea_x[:, sl]
            keep = srow_g == (2 * p + scol_h) // (SSM_HEADS // SSM_GROUPS)
            upd = _dot(bm_t, xdt * te_x[:, sl])
            h_ref[b, p] = h_p * cd_x[:, sl] + jnp.where(keep, upd, 0.0)
            ys.append(y_diag + y_off + dx_ref[:, sl] * x_p)
        y = jnp.concatenate(ys, axis=-1) * _silu(z)
        sq = [jnp.square(y[:, p * LANES:(p + 1) * LANES]) for p in range(N_PAIRS)]
        g0 = _dot_sel(sq[0], ones_all) + _dot_sel(sq[1], ones_top)
        g1 = _dot_sel(sq[1], ones_bot) + _dot_sel(sq[2], ones_all)
        ms = jnp.concatenate([g0, jnp.where(first_half, g0, g1), g1], axis=-1) * (1.0 / (SSM_INNER // SSM_GROUPS))
        y = y * lax.rsqrt(ms + EPS) * ng_ref[...]
        y_ref[b] = y.astype(y_ref.dtype)

    @pl.when(c == pl.num_programs(1) - 1)
    def _():
        for b in range(bb):
            for h in range(SSM_HEADS):
                g, hh = h // hpg, h % 2
                blk = h_ref[b, h // 2][g * SSM_STATE:(g + 1) * SSM_STATE, hh * HEAD_DIM:(hh + 1) * HEAD_DIM]
                hout_ref[b, h] = jnp.transpose(blk)


def _expand_matrix(first_lane):
    m = np.zeros((LANES, SSM_INNER), np.float32)
    for h in range(SSM_HEADS):
        m[first_lane + h, h * HEAD_DIM:(h + 1) * HEAD_DIM] = 1.0
    return jnp.asarray(m, BF16)


def _lane_row(vals, first_lane):
    return jnp.zeros((1, LANES), F32).at[0, first_lane:first_lane + vals.shape[0]].set(vals.astype(F32))


def _ssd(zx, small, h0, dt_bias, a_log, d_skip, norm_g, rows, bb):
    bsz, t, _ = zx.shape
    dtb = _lane_row(dt_bias, 0)
    arow = _lane_row(-jnp.exp(a_log.astype(F32)), 0)
    dx = jnp.repeat(d_skip.astype(F32), HEAD_DIM)[None, :]
    full = lambda v: pl.BlockSpec(v.shape, lambda i, j: (0,) * v.ndim)
    blk = lambda c: pl.BlockSpec((bb, rows, c), lambda i, j: (i, j, 0))
    st = pl.BlockSpec((bb, SSM_HEADS, HEAD_DIM, SSM_STATE), lambda i, j: (i, 0, 0, 0))
    consts = (dtb, arow, dx, norm_g[None, :], _expand_matrix(0))
    return pl.pallas_call(
        functools.partial(_ssd_kernel, rows=rows, bb=bb),
        grid=(bsz // bb, t // rows),
        in_specs=[blk(zx.shape[-1]), blk(LANES), st] + [full(v) for v in consts],
        out_specs=[blk(SSM_INNER), st],
        out_shape=[jax.ShapeDtypeStruct((bsz, t, SSM_INNER), BF16),
                   jax.ShapeDtypeStruct((bsz, SSM_HEADS, HEAD_DIM, SSM_STATE), F32)],
        scratch_shapes=[pltpu.VMEM((bb, N_PAIRS, LANES, LANES), F32)],
        compiler_params=_cparams("arbitrary", "arbitrary"),
        name="ssd",
    )(zx, small, h0, *consts)


BETA_LANE = 6
DECAY_LANE = 12


def _bdot(a, b):
    return lax.dot_general(a.astype(BF16), b.astype(BF16), (((2,), (1,)), ((0,), (0,))),
                           preferred_element_type=F32)


def _bdot_nt(a, b):
    return lax.dot_general(a.astype(BF16), b.astype(BF16), (((2,), (2,)), ((0,), (0,))),
                           preferred_element_type=F32)


def _unit_lower_inverse(m, block):
    rdim = m.shape[-1]
    eye = (_iota((rdim, rdim), 0) == _iota((rdim, rdim), 1)).astype(F32)
    x = -m
    t = eye + x
    p = x
    for _ in range(int(math.log2(block)) - 2):
        p = _bdot(p, p)
        t = t + _bdot(t, p)
    resid = (eye - t) - _bdot(m, t)
    return t + _bdot(t, resid)


def _gdn_kernel(qg_ref, s_ref, s0_ref, bias_ref, arow_ref, ng_ref, eb_ref, eg_ref,
                y_ref, sout_ref, st_ref, *, tt, ck, bb):
    c = pl.program_id(1)
    nck = tt // ck
    rdim = 2 * ck

    @pl.when(c == 0)
    def _():
        z = jnp.zeros((HEAD_DIM, HEAD_DIM), F32)
        for b in range(bb):
            for p in range(N_PAIRS):
                top = jnp.concatenate([s0_ref[b, 2 * p], z], axis=1)
                bot = jnp.concatenate([z, s0_ref[b, 2 * p + 1]], axis=1)
                st_ref[b, p] = jnp.concatenate([top, bot], axis=0)

    r_i = _iota((rdim, rdim), 0)
    c_i = _iota((rdim, rdim), 1)
    same_blk = (r_i // ck) == (c_i // ck)
    incl = same_blk & (r_i >= c_i)
    strict = same_blk & (r_i > c_i)
    first_half = _pair_mask()
    same_head = (_iota((LANES, LANES), 0) // HEAD_DIM) == (_iota((LANES, LANES), 1) // HEAD_DIM)
    tr = _iota((tt, tt), 0)
    tc = _iota((tt, tt), 1)
    blk_tril = (((tr // ck) == (tc // ck)) & (tr >= tc)).astype(BF16)
    eb = eb_ref[...]
    eg = eg_ref[...]
    hm = same_head.astype(BF16)
    head_sumsq = lambda x: jnp.concatenate(
        [_dot_sel(jnp.square(x[:, p * LANES:(p + 1) * LANES]), hm) for p in range(N_PAIRS)], axis=-1)

    tiles = {name: [] for name in ('q', 'k', 'kb', 'rhs', 'qd', 'kd', 'col', 'cd')}
    gates = []
    for b in range(bb):
        qg = qg_ref[b]
        gates.append(qg[:, GDN_QKV:])
        qkv = qg[:, :GDN_QKV]
        q = qkv[:, :GDN_INNER]
        k = qkv[:, GDN_INNER:2 * GDN_INNER]
        v = qkv[:, 2 * GDN_INNER:]
        q = q * lax.rsqrt(head_sumsq(q) + EPS) * (HEAD_DIM ** -0.5)
        k = k * lax.rsqrt(head_sumsq(k) + EPS)
        s = s_ref[b]
        beta = _sigmoid(s)
        g = _softplus(s + bias_ref[...]) * arow_ref[...]
        gc = _sel_dot(blk_tril, g)
        g_last = jnp.concatenate(
            [jnp.broadcast_to(gc[(i + 1) * ck - 1:(i + 1) * ck, :], (ck, LANES)) for i in range(nck)], axis=0)
        beta_x = _dot_sel(beta, eb)
        eg_x = _dot_sel(jnp.exp(gc), eg)
        kd_x = _dot_sel(jnp.exp(g_last - gc), eg)
        cd_x = _dot_sel(jnp.exp(g_last), eg)
        kb = k * beta_x
        full = dict(q=q, k=k, kb=kb, qd=q * eg_x, kd=k * kd_x, cd=cd_x)
        vb = v * beta_x
        kbe = kb * eg_x
        for i in range(nck):
            rs = slice(i * ck, (i + 1) * ck)
            for p in range(N_PAIRS):
                sl = slice(p * LANES, (p + 1) * LANES)
                for name, arr in full.items():
                    tiles[name].append(arr[rs, sl])
                tiles['rhs'].append(jnp.concatenate([vb[rs, sl], kbe[rs, sl]], axis=-1))
                lane0 = DECAY_LANE + 2 * p
                tiles['col'].append(jnp.concatenate(
                    [jnp.broadcast_to(gc[rs, lane0 + hh:lane0 + hh + 1], (ck, rdim)) for hh in range(2)], axis=0))

    st = lambda name: jnp.stack(tiles[name], axis=0)
    stack2 = lambda x: jnp.concatenate([jnp.where(first_half, x, 0.0), jnp.where(first_half, 0.0, x)], axis=1)
    k_st = stack2(st('k'))
    col = st('col')
    diff = col - jnp.swapaxes(col, 1, 2)
    dec = jnp.exp(jnp.where(incl, diff, -1e30))
    m = _bdot_nt(stack2(st('kb')), k_st) * jnp.where(strict, dec, 0.0)
    t_inv = _unit_lower_inverse(m, ck)
    rhs = st('rhs')
    sol = _bdot(t_inv, jnp.concatenate([rhs, rhs], axis=1))
    u = jnp.where(first_half, sol[:, :ck, :LANES], sol[:, ck:, :LANES])
    w = jnp.where(first_half, sol[:, :ck, LANES:], sol[:, ck:, LANES:])
    attn = _bdot_nt(stack2(st('q')), k_st) * dec
    qd = st('qd')
    kd_t = jnp.swapaxes(st('kd'), 1, 2)
    cd = st('cd')

    gsel = lambda x, i: jnp.stack([x[(b * nck + i) * N_PAIRS + p] for b in range(bb) for p in range(N_PAIRS)], axis=0)
    state = st_ref[...].reshape(bb * N_PAIRS, LANES, LANES)
    o_chunks = []
    for i in range(nck):
        v_new = gsel(u, i) - _bdot(gsel(w, i), state)
        intra = _bdot(gsel(attn, i), jnp.concatenate([v_new, v_new], axis=1))
        o_chunks.append(_bdot(gsel(qd, i), state) + jnp.where(first_half, intra[:, :ck], intra[:, ck:]))
        upd = _bdot(gsel(kd_t, i), v_new)
        state = state * gsel(cd, i)[:, 0:1, :] + jnp.where(same_head, upd, 0.0)
    st_ref[...] = state.reshape(bb, N_PAIRS, LANES, LANES)

    for b in range(bb):
        o = jnp.concatenate(
            [jnp.concatenate([o_chunks[i][b * N_PAIRS + p] for p in range(N_PAIRS)], axis=-1) for i in range(nck)],
            axis=0)
        ms = head_sumsq(o) * (1.0 / HEAD_DIM)
        o = o * lax.rsqrt(ms + EPS) * ng_ref[...] * _silu(gates[b])
        y_ref[b] = o.astype(y_ref.dtype)

    @pl.when(c == pl.num_programs(1) - 1)
    def _():
        for b in range(bb):
            for h in range(GDN_HEADS):
                lo = (h % 2) * HEAD_DIM
                sout_ref[b, h] = st_ref[b, h // 2][lo:lo + HEAD_DIM, lo:lo + HEAD_DIM]


def _gdn(qg, small, s0, a_log, dt_bias, norm_g, tt, ck, bb):
    bsz, t, _ = qg.shape
    bias = _lane_row(dt_bias, DECAY_LANE)
    arow = _lane_row(-jnp.exp(a_log.astype(F32)), DECAY_LANE)
    ng = jnp.tile(norm_g.astype(F32), GDN_HEADS)[None, :]
    full = lambda v: pl.BlockSpec(v.shape, lambda i, j: (0,) * v.ndim)
    blk = lambda c: pl.BlockSpec((bb, tt, c), lambda i, j: (i, j, 0))
    st = pl.BlockSpec((bb, GDN_HEADS, HEAD_DIM, HEAD_DIM), lambda i, j: (i, 0, 0, 0))
    consts = (bias, arow, ng, _expand_matrix(BETA_LANE), _expand_matrix(DECAY_LANE))
    return pl.pallas_call(
        functools.partial(_gdn_kernel, tt=tt, ck=ck, bb=bb),
        grid=(bsz // bb, t // tt),
        in_specs=[blk(qg.shape[-1]), blk(LANES), st] + [full(v) for v in consts],
        out_specs=[blk(GDN_INNER), st],
        out_shape=[jax.ShapeDtypeStruct((bsz, t, GDN_INNER), BF16),
                   jax.ShapeDtypeStruct((bsz, GDN_HEADS, HEAD_DIM, HEAD_DIM), F32)],
        scratch_shapes=[pltpu.VMEM((bb, N_PAIRS, LANES, LANES), F32)],
        compiler_params=_cparams("arbitrary", "arbitrary"),
        name="gdn",
    )(qg, small, s0, *consts)


def _mix_residual(x_ref, ya_ref, yb_ref, yc_ref, woa_ref, wob_ref, woc_ref):
    mix = (jnp.dot(ya_ref[...], woa_ref[...], preferred_element_type=F32)
           + jnp.dot(yb_ref[...], wob_ref[...], preferred_element_type=F32)
           + jnp.dot(yc_ref[...], woc_ref[...], preferred_element_type=F32))
    return x_ref[...] + mix


def _rms(x, g):
    return x * lax.rsqrt(jnp.mean(x * x, axis=-1, keepdims=True) + EPS) * g


def _ffn_kernel(x_ref, ya_ref, yb_ref, yc_ref, woa_ref, wob_ref, woc_ref, g_ref, wg_ref, wu_ref, wd_ref,
                o_ref, xn_ref, h_ref, acc_ref):
    j = pl.program_id(1)

    @pl.when(j == 0)
    def _():
        xn = _mix_residual(x_ref, ya_ref, yb_ref, yc_ref, woa_ref, wob_ref, woc_ref)
        xn_ref[...] = xn
        h_ref[...] = _rms(xn, g_ref[...]).astype(BF16)
        acc_ref[...] = jnp.zeros_like(acc_ref)

    h = h_ref[...]
    act = _silu(jnp.dot(h, wg_ref[...], preferred_element_type=F32)) * jnp.dot(h, wu_ref[...], preferred_element_type=F32)
    acc_ref[...] += jnp.dot(act.astype(BF16), wd_ref[...], preferred_element_type=F32)

    @pl.when(j == pl.num_programs(1) - 1)
    def _():
        o_ref[...] = xn_ref[...] + acc_ref[...]


FFN_CHUNKS = 2


def _out_ffn(x2d, ya, yb, yc, wo, g, wg, wu, wd):
    n = x2d.shape[0]
    tm = _row_tile(n, 512)
    f = wg.shape[1]
    tf = f // FFN_CHUNKS
    rows = lambda c: pl.BlockSpec((tm, c), lambda i, j: (i, 0))
    full = lambda w: pl.BlockSpec(w.shape, lambda i, j: (0, 0))
    return pl.pallas_call(
        _ffn_kernel,
        grid=(n // tm, FFN_CHUNKS),
        in_specs=[rows(D_MODEL), rows(ya.shape[1]), rows(yb.shape[1]), rows(yc.shape[1]),
                  full(wo[0]), full(wo[1]), full(wo[2]), full(g),
                  pl.BlockSpec((D_MODEL, tf), lambda i, j: (0, j)),
                  pl.BlockSpec((D_MODEL, tf), lambda i, j: (0, j)),
                  pl.BlockSpec((tf, D_MODEL), lambda i, j: (j, 0))],
        out_specs=rows(D_MODEL),
        out_shape=jax.ShapeDtypeStruct((n, D_MODEL), F32),
        scratch_shapes=[pltpu.VMEM((tm, D_MODEL), F32), pltpu.VMEM((tm, D_MODEL), BF16),
                        pltpu.VMEM((tm, D_MODEL), F32)],
        compiler_params=_cparams("parallel", "arbitrary"),
        name="out_ffn",
    )(x2d, ya, yb, yc, wo[0], wo[1], wo[2], g, wg, wu, wd)


ROUTE_TILE = 512


def _router_kernel(x_ref, ya_ref, yb_ref, yc_ref, woa_ref, wob_ref, woc_ref, g_ref, wr_ref, br_ref,
                   xn_ref, h_ref, gate_ref, slot_ref, cnt_ref, run_ref, *, cap):
    i = pl.program_id(0)

    @pl.when(i == 0)
    def _():
        run_ref[...] = jnp.zeros_like(run_ref)

    xn = _mix_residual(x_ref, ya_ref, yb_ref, yc_ref, woa_ref, wob_ref, woc_ref)
    xn_ref[...] = xn
    h = _rms(xn, g_ref[...])
    h_ref[...] = h
    tm = h.shape[0]
    lane = _iota((1, LANES), 1)
    logits = jnp.where(lane < N_EXPERTS, _dot_hp(h, wr_ref[...]) + br_ref[...], -jnp.inf)
    m1 = jnp.max(logits, axis=-1, keepdims=True)
    i1 = jnp.min(jnp.where(logits == m1, lane, LANES), axis=-1, keepdims=True)
    rest = jnp.where(lane == i1, -jnp.inf, logits)
    m2 = jnp.max(rest, axis=-1, keepdims=True)
    i2 = jnp.min(jnp.where(rest == m2, lane, LANES), axis=-1, keepdims=True)
    e2 = jnp.exp(m2 - m1)
    inv = 1.0 / (1.0 + e2)
    gate_ref[...] = jnp.where(lane == 0, inv, 0.0) + jnp.where(lane == 1, e2 * inv, 0.0)

    chosen = jnp.where((lane == i1) | (lane == i2), 1.0, 0.0)
    before = _iota((tm, tm), 0) > _iota((tm, tm), 1)
    rank = jnp.dot(before.astype(BF16), chosen.astype(BF16), preferred_element_type=F32) + run_ref[0:1, :]
    base = lane.astype(F32) * float(cap)
    pick = lambda idx: jnp.sum(jnp.where(lane == idx, rank + base, 0.0), axis=-1, keepdims=True)
    slots = jnp.where(lane == 0, pick(i1), 0.0) + jnp.where(lane == 1, pick(i2), 0.0)
    slots_t = jnp.transpose(slots).astype(jnp.int32)
    for k in range(2):
        slot_ref[0, k] = jnp.concatenate(
            [slots_t[k:k + 1, c * LANES:(c + 1) * LANES] for c in range(tm // LANES)], axis=0)
    run_ref[...] = run_ref[...] + jnp.sum(chosen, axis=0, keepdims=True)
    cnt_ref[...] = run_ref[...].astype(jnp.int32)


def _out_router(x2d, ya, yb, yc, wo, g, w_router, b_router, cap):
    n = x2d.shape[0]
    tm = _row_tile(n, ROUTE_TILE)
    wr = jnp.zeros((D_MODEL, LANES), F32).at[:, :N_EXPERTS].set(w_router.astype(F32))
    br = _lane_row(b_router, 0)
    rows = lambda c: pl.BlockSpec((tm, c), lambda i: (i, 0))
    full = lambda w: pl.BlockSpec(w.shape, lambda i: (0, 0))
    return pl.pallas_call(
        functools.partial(_router_kernel, cap=cap),
        grid=(n // tm,),
        in_specs=[rows(D_MODEL), rows(ya.shape[1]), rows(yb.shape[1]), rows(yc.shape[1]),
                  full(wo[0]), full(wo[1]), full(wo[2]), full(g), full(wr), full(br)],
        out_specs=[rows(D_MODEL), rows(D_MODEL), rows(LANES),
                   pl.BlockSpec((1, 2, tm // LANES, LANES), lambda i: (i, 0, 0, 0)),
                   pl.BlockSpec((SUBLANES, LANES), lambda i: (0, 0))],
        out_shape=[jax.ShapeDtypeStruct((n, D_MODEL), F32), jax.ShapeDtypeStruct((n, D_MODEL), F32),
                   jax.ShapeDtypeStruct((n, LANES), F32),
                   jax.ShapeDtypeStruct((n // tm, 2, tm // LANES, LANES), jnp.int32),
                   jax.ShapeDtypeStruct((SUBLANES, LANES), jnp.int32)],
        scratch_shapes=[pltpu.VMEM((SUBLANES, LANES), F32)],
        compiler_params=_cparams("arbitrary"),
        name="out_router",
    )(x2d, ya, yb, yc, wo[0], wo[1], wo[2], g, wr, br)


def _row_copy(src_ref, src_row, dst_ref, dst_row, sem):
    return pltpu.make_async_copy(src_ref.at[pl.ds(src_row, 1)], dst_ref.at[pl.ds(dst_row, 1)], sem)


def _slot_fetch(slots_hbm, tile, smem_ref, buf, sem):
    return pltpu.make_async_copy(slots_hbm.at[tile], smem_ref.at[buf], sem.at[buf])


def _for_rows(tm, fn):
    for c in range(tm // LANES):
        for base in range(0, LANES, SUBLANES):
            for j in range(SUBLANES):
                fn(c, base, j)


def _vmem_row(ref, base, j):
    return ref.at[pl.ds(base, SUBLANES)].at[pl.ds(j, 1)]


def _dispatch_kernel(cnt_ref, slots_hbm, h_ref, hs_hbm, slot_smem, zero_ref, slot_sem, row_sem, pad_sem,
                     *, tm, cap, bm):
    i = pl.program_id(0)
    n_tiles = pl.num_programs(0)

    @pl.when(i == 0)
    def _():
        _slot_fetch(slots_hbm, 0, slot_smem, 0, slot_sem).start()

    @pl.when(i + 1 < n_tiles)
    def _():
        _slot_fetch(slots_hbm, i + 1, slot_smem, (i + 1) % 2, slot_sem).start()

    _slot_fetch(slots_hbm, i, slot_smem, i % 2, slot_sem).wait()

    def issue(c, base, j):
        for k in range(2):
            dst = hs_hbm.at[pl.ds(slot_smem[i % 2, k, c, base + j], 1)]
            pltpu.make_async_copy(_vmem_row(h_ref, c * LANES + base, j), dst, row_sem).start()

    _for_rows(tm, issue)

    def drain(r, carry):
        for k in range(2):
            _row_copy(h_ref, 0, hs_hbm, 0, row_sem).wait()
        return carry

    lax.fori_loop(0, tm, drain, 0, unroll=8)

    @pl.when(i == n_tiles - 1)
    def _():
        zero_ref[...] = jnp.zeros_like(zero_ref)
        for e in range(N_EXPERTS):
            c = cnt_ref[e]
            n_pad = ((c + bm - 1) // bm) * bm - c

            def fill(r, carry):
                _row_copy(zero_ref, 0, hs_hbm, e * cap + c + r, pad_sem).start()
                return carry

            def fill_wait(r, carry):
                _row_copy(zero_ref, 0, hs_hbm, 0, pad_sem).wait()
                return carry

            lax.fori_loop(0, n_pad, fill, 0)
            lax.fori_loop(0, n_pad, fill_wait, 0)


def _dispatch(counts, slots, h, cap, bm):
    n = h.shape[0]
    n_tiles = slots.shape[0]
    tm = slots.shape[2] * LANES
    return pl.pallas_call(
        functools.partial(_dispatch_kernel, tm=tm, cap=cap, bm=bm),
        grid_spec=pltpu.PrefetchScalarGridSpec(
            num_scalar_prefetch=1,
            grid=(n_tiles,),
            in_specs=[pl.BlockSpec(memory_space=pl.ANY), pl.BlockSpec((tm, D_MODEL), lambda i, cnt: (i, 0))],
            out_specs=pl.BlockSpec(memory_space=pl.ANY),
            scratch_shapes=[pltpu.SMEM((2, 2, tm // LANES, LANES), jnp.int32), pltpu.VMEM((SUBLANES, D_MODEL), F32),
                            pltpu.SemaphoreType.DMA((2,)), pltpu.SemaphoreType.DMA, pltpu.SemaphoreType.DMA],
        ),
        out_shape=jax.ShapeDtypeStruct((N_EXPERTS * cap, D_MODEL), F32),
        compiler_params=_cparams("arbitrary"),
        name="moe_dispatch",
    )(counts, slots, h)


def _experts_kernel(tbl_ref, hs_ref, wg_ref, wu_ref, wd_ref, ys_ref, hb_ref, acc_ref):
    s = pl.program_id(0)
    j = pl.program_id(1)

    @pl.when(tbl_ref[2, s] == 1)
    def _():
        @pl.when(j == 0)
        def _():
            hb_ref[...] = hs_ref[...].astype(BF16)
            acc_ref[...] = jnp.zeros_like(acc_ref)

        h = hb_ref[...]
        act = (_silu(jnp.dot(h, wg_ref[0], preferred_element_type=F32))
               * jnp.dot(h, wu_ref[0], preferred_element_type=F32))
        acc_ref[...] += jnp.dot(act.astype(BF16), wd_ref[0], preferred_element_type=F32)

        @pl.when(j == pl.num_programs(1) - 1)
        def _():
            ys_ref[...] = acc_ref[...]


def _block_table(counts, cap, bm, n_steps):
    nblk = (counts + bm - 1) // bm
    cum = jnp.cumsum(nblk)
    total = cum[-1]
    step = jnp.arange(n_steps, dtype=jnp.int32)
    last = jnp.maximum(total - 1, 0)
    eff = jnp.minimum(step, last)
    expert = jnp.minimum(jnp.sum(eff[:, None] >= cum[None, :], axis=1), N_EXPERTS - 1).astype(jnp.int32)
    blk = eff - (cum[expert] - nblk[expert])
    return jnp.stack([expert * (cap // bm) + blk, expert, (step < total).astype(jnp.int32)]).astype(jnp.int32)


def _experts(counts, hs, wg, wu, wd, cap, bm, n_assign):
    n_steps = n_assign // bm + N_EXPERTS
    tbl = _block_table(counts, cap, bm, n_steps)
    f = wg.shape[2]
    tf = f // FFN_CHUNKS
    chunk = lambda s, j, t: jnp.where(t[2, s] == 1, j, FFN_CHUNKS - 1)
    rows = pl.BlockSpec((bm, D_MODEL), lambda s, j, t: (t[0, s], 0))
    return pl.pallas_call(
        _experts_kernel,
        grid_spec=pltpu.PrefetchScalarGridSpec(
            num_scalar_prefetch=1,
            grid=(n_steps, FFN_CHUNKS),
            in_specs=[rows,
                      pl.BlockSpec((1, D_MODEL, tf), lambda s, j, t: (t[1, s], 0, chunk(s, j, t))),
                      pl.BlockSpec((1, D_MODEL, tf), lambda s, j, t: (t[1, s], 0, chunk(s, j, t))),
                      pl.BlockSpec((1, tf, D_MODEL), lambda s, j, t: (t[1, s], chunk(s, j, t), 0))],
            out_specs=rows,
            scratch_shapes=[pltpu.VMEM((bm, D_MODEL), BF16), pltpu.VMEM((bm, D_MODEL), F32)],
        ),
        out_shape=jax.ShapeDtypeStruct(hs.shape, F32),
        compiler_params=_cparams("arbitrary", "arbitrary"),
        name="moe_experts",
    )(tbl, hs, wg, wu, wd)


def _combine_kernel(slots_hbm, ys_hbm, xn_ref, gate_ref, gf_ref, o_ref, slot_smem, ybuf_ref, slot_sem, row_sem,
                    *, tm):
    i = pl.program_id(0)
    n_tiles = pl.num_programs(0)

    def gather(tile, buf):
        def issue(c, base, j):
            for k in range(2):
                src = ys_hbm.at[pl.ds(slot_smem[buf, k, c, base + j], 1)]
                dst = _vmem_row(ybuf_ref.at[buf, k], c * LANES + base, j)
                pltpu.make_async_copy(src, dst, row_sem.at[buf]).start()
        _for_rows(tm, issue)

    @pl.when(i == 0)
    def _():
        first = _slot_fetch(slots_hbm, 0, slot_smem, 0, slot_sem)
        first.start()
        first.wait()
        gather(0, 0)

        @pl.when(n_tiles > 1)
        def _():
            _slot_fetch(slots_hbm, 1, slot_smem, 1, slot_sem).start()

    @pl.when(i + 1 < n_tiles)
    def _():
        _slot_fetch(slots_hbm, i + 1, slot_smem, (i + 1) % 2, slot_sem).wait()
        gather(i + 1, (i + 1) % 2)

    def drain(r, carry):
        for k in range(2):
            _row_copy(ys_hbm, 0, ybuf_ref.at[i % 2, k], 0, row_sem.at[i % 2]).wait()
        return carry

    lax.fori_loop(0, tm, drain, 0, unroll=8)

    @pl.when(i + 2 < n_tiles)
    def _():
        _slot_fetch(slots_hbm, i + 2, slot_smem, i % 2, slot_sem).start()

    g = gate_ref[...]
    y = g[:, 0:1] * ybuf_ref[i % 2, 0] + g[:, 1:2] * ybuf_ref[i % 2, 1]
    o_ref[...] = _rms(xn_ref[...] + y, gf_ref[...])


def _combine(slots, ys, xn, gates, g_final):
    n = xn.shape[0]
    n_tiles = slots.shape[0]
    tm = slots.shape[2] * LANES
    rows = lambda c: pl.BlockSpec((tm, c), lambda i: (i, 0))
    return pl.pallas_call(
        functools.partial(_combine_kernel, tm=tm),
        grid=(n_tiles,),
        in_specs=[pl.BlockSpec(memory_space=pl.ANY), pl.BlockSpec(memory_space=pl.ANY),
                  rows(D_MODEL), rows(LANES), pl.BlockSpec(g_final.shape, lambda i: (0, 0))],
        out_specs=rows(D_MODEL),
        out_shape=jax.ShapeDtypeStruct((n, D_MODEL), F32),
        scratch_shapes=[pltpu.SMEM((2, 2, tm // LANES, LANES), jnp.int32), pltpu.VMEM((2, 2, tm, D_MODEL), F32),
                        pltpu.SemaphoreType.DMA((2,)), pltpu.SemaphoreType.DMA((2,))],
        compiler_params=_cparams("arbitrary"),
        name="moe_combine",
    )(slots, ys, xn, gates, g_final)


def _moe(x2d, ya, yb, yc, wo, g_ffn, w_router, b_router, wg, wu, wd, g_final):
    n = x2d.shape[0]
    bm = _row_tile(n, 512)
    cap = n
    xn, h, gates, slots, counts = _out_router(x2d, ya, yb, yc, wo, g_ffn, w_router, b_router, cap)
    counts = counts[0, :N_EXPERTS]
    hs = _dispatch(counts, slots, h, cap, bm)
    ys = _experts(counts, hs, wg, wu, wd, cap, bm, 2 * n)
    return _combine(slots, ys, xn, gates, g_final)


IN_SIZES = (2 * CONV_CH, SSM_INNER, SSM_XBC, SSM_HEADS, GDN_QKV, GDN_INNER, GDN_HEADS, GDN_HEADS)


def _prep_layer(l, p):
    off = np.concatenate([[0], np.cumsum(IN_SIZES)])
    w_in = p['w_in_bf16'][l]
    col = lambda i: w_in[:, off[i]:off[i + 1]]
    a_in, z, xbc, dt, qkv, gate, b_raw, a_raw = (col(i) for i in range(8))
    small = jnp.concatenate([dt, b_raw, a_raw, jnp.zeros((D_MODEL, SMALL_W - 3 * SSM_HEADS), BF16)], axis=1)
    wo = p['w_out'][l].astype(BF16)
    return dict(
        wa=a_in,
        wzx=jnp.concatenate([z, xbc], axis=1),
        wqg=jnp.concatenate([qkv, gate], axis=1),
        ws=small,
        wo=(wo[:CONV_CH], wo[CONV_CH:CONV_CH + SSM_INNER], wo[CONV_CH + SSM_INNER:]),
    )


def _trunk(x, st_conv_a, st_ssm_conv, st_ssm, st_gdn_conv, st_gdn, p, prep, ssd_rows, gdn_rows, bb):
    bsz, t, _ = x.shape
    n = bsz * t
    depth = p['g_mix'].shape[0]
    x2d = x.reshape(n, D_MODEL)
    new = [[] for _ in range(5)]
    for l in range(depth):
        w = prep[l]
        a_in, zx, qg, small, hist_x, hist_q = _norm_proj(
            x2d.reshape(bsz, t, D_MODEL), p['g_mix'][l][None, :], w['wa'], w['wzx'], w['wqg'], w['ws'],
            st_ssm_conv[l], st_gdn_conv[l], p['ssm_conv_w'][l], p['ssm_conv_b'][l][None, :], p['gdn_conv_w'][l])
        a_in = a_in.reshape(bsz, t, -1)
        zx = zx.reshape(bsz, t, -1)
        qg = qg.reshape(bsz, t, -1)
        small = small.reshape(bsz, t, -1)
        ya, conv_a = _conv_a(a_in, st_conv_a[l], p['conv_a_w'][l], p['conv_a_b'][l][None, :],
                             p['ln_a_g'][l][None, :], p['ln_a_b'][l][None, :])
        yb, ssm = _ssd(zx, small, st_ssm[l].astype(F32), p['ssm_dt_bias'][l], p['ssm_a_log'][l],
                       p['ssm_d'][l], p['ssm_norm_g'][l], ssd_rows, bb)
        yc, gdn = _gdn(qg, small, st_gdn[l].astype(F32), p['gdn_a_log'][l], p['gdn_dt_bias'][l],
                       p['gdn_norm_g'][l], gdn_rows[0], gdn_rows[1], bb)
        new[0].append(conv_a)
        new[1].append(hist_x[:, HIST_PAD - (SSM_CONV_W - 1):])
        new[2].append(ssm)
        new[3].append(hist_q[:, HIST_PAD - (GDN_CONV_W - 1):])
        new[4].append(gdn)
        flat = lambda y: y.reshape(n, -1)
        g_ffn = p['g_ffn'][l][None, :]
        if l % 2 == 0:
            x2d = _out_ffn(x2d, flat(ya), flat(yb), flat(yc), w['wo'], g_ffn,
                           prep['ffn'][l // 2][0], prep['ffn'][l // 2][1], prep['ffn'][l // 2][2])
        else:
            x2d = _moe(x2d, flat(ya), flat(yb), flat(yc), w['wo'], g_ffn, p['moe_w_router'][l // 2],
                       p['moe_b_router'][l // 2], *prep['moe'][l // 2], p['g_final'][None, :])
    return (x2d.reshape(bsz, t, D_MODEL),) + tuple(jnp.stack(s) for s in new)


def kernel(x_prompt, x_sample, state_conv_a, state_ssm_conv, state_ssm, state_gdn_conv, state_gdn, g_mix, w_in, conv_a_w, conv_a_b, ln_a_g, ln_a_b, ssm_conv_w, ssm_conv_b, ssm_dt_bias, ssm_a_log, ssm_d, ssm_norm_g, gdn_conv_w, gdn_a_log, gdn_dt_bias, gdn_norm_g, w_out, g_ffn, ffn_w_gate, ffn_w_up, ffn_w_down, moe_w_router, moe_b_router, moe_w_gate, moe_w_up, moe_w_down, g_final):
    p = dict(g_mix=g_mix, w_in=w_in, conv_a_w=conv_a_w, conv_a_b=conv_a_b, ln_a_g=ln_a_g, ln_a_b=ln_a_b,
             ssm_conv_w=ssm_conv_w, ssm_conv_b=ssm_conv_b, ssm_dt_bias=ssm_dt_bias, ssm_a_log=ssm_a_log,
             ssm_d=ssm_d, ssm_norm_g=ssm_norm_g, gdn_conv_w=gdn_conv_w, gdn_a_log=gdn_a_log,
             gdn_dt_bias=gdn_dt_bias, gdn_norm_g=gdn_norm_g, w_out=w_out, g_ffn=g_ffn,
             moe_w_router=moe_w_router, moe_b_router=moe_b_router, g_final=g_final)
    depth = g_mix.shape[0]
    assert depth % 2 == 0, "the final RMSNorm is fused into the expert layer, which must come last"
    p['w_in_bf16'] = w_in.astype(BF16)
    prep = {l: _prep_layer(l, p) for l in range(depth)}
    prep['ffn'] = [(ffn_w_gate[i].astype(BF16), ffn_w_up[i].astype(BF16), ffn_w_down[i].astype(BF16))
                   for i in range(ffn_w_gate.shape[0])]
    prep['moe'] = [(moe_w_gate[i].astype(BF16), moe_w_up[i].astype(BF16), moe_w_down[i].astype(BF16))
                   for i in range(moe_w_gate.shape[0])]
    bp, dt = x_prompt.shape[0], x_prompt.dtype
    zeros = lambda *s: jnp.zeros((depth, bp) + s, dt)
    outs_p = _trunk(x_prompt, zeros(CONV_W - 1, CONV_CH), zeros(SSM_CONV_W - 1, SSM_XBC),
                    zeros(SSM_HEADS, HEAD_DIM, SSM_STATE), zeros(GDN_CONV_W - 1, GDN_QKV),
                    zeros(GDN_HEADS, HEAD_DIM, HEAD_DIM), p, prep, ssd_rows=min(128, x_prompt.shape[1]),
                    gdn_rows=(min(256, x_prompt.shape[1]), min(64, x_prompt.shape[1])), bb=2)
    outs_s = _trunk(x_sample, state_conv_a, state_ssm_conv, state_ssm, state_gdn_conv, state_gdn, p, prep,
                    ssd_rows=x_sample.shape[1], gdn_rows=(x_sample.shape[1], x_sample.shape[1]), bb=2)
    return (outs_p[0], outs_s[0]) + outs_p[1:] + outs_s[1:]
```

```python
import functools
import math

import jax
import jax.numpy as jnp
import numpy as np
from jax import lax
from jax.experimental import pallas as pl
from jax.experimental.pallas import tpu as pltpu

F32 = jnp.float32
BF16 = jnp.bfloat16
EPS = 1e-6

LANES = 128
SUBLANES = 8
VMEM_BYTES_V7X = 64 * 1024 * 1024
VMEM_LIMIT = VMEM_BYTES_V7X * 3 // 4

D_MODEL = 1024
CONV_CH = 256
CONV_W = 31
SSM_HEADS = 6
HEAD_DIM = 64
SSM_INNER = SSM_HEADS * HEAD_DIM
SSM_STATE = 64
SSM_GROUPS = 2
SSM_XBC = SSM_INNER + 2 * SSM_GROUPS * SSM_STATE
SSM_CONV_W = 4
GDN_HEADS = 6
GDN_INNER = GDN_HEADS * HEAD_DIM
GDN_QKV = 3 * GDN_INNER
GDN_CONV_W = 4
N_PAIRS = 3
N_EXPERTS = 8
SMALL_W = LANES


def _cparams(*sem):
    return pltpu.CompilerParams(dimension_semantics=sem, vmem_limit_bytes=VMEM_LIMIT)


def _dot(a, b):
    return jnp.dot(a.astype(BF16), b.astype(BF16), preferred_element_type=F32)


def _dot_nt(a, b):
    return lax.dot_general(a.astype(BF16), b.astype(BF16), (((1,), (1,)), ((), ())),
                           preferred_element_type=F32)


def _split3(x):
    hi = x.astype(BF16)
    r1 = x - hi.astype(F32)
    mid = r1.astype(BF16)
    lo = (r1 - mid.astype(F32)).astype(BF16)
    return hi, mid, lo


def _dot_sel(x, sel_bf16, pieces=2):
    d = lambda p: jnp.dot(p, sel_bf16, preferred_element_type=F32)
    hi = x.astype(BF16)
    r1 = x - hi.astype(F32)
    mid = r1.astype(BF16)
    if pieces == 2:
        return d(hi) + d(mid)
    return d(hi) + d(mid) + d((r1 - mid.astype(F32)).astype(BF16))


def _sel_dot(sel_bf16, x):
    hi, mid, lo = _split3(x)
    d = lambda p: jnp.dot(sel_bf16, p, preferred_element_type=F32)
    return d(hi) + d(mid) + d(lo)


def _dot_hp(a, b):
    ah = a.astype(BF16)
    al = (a - ah.astype(F32)).astype(BF16)
    bh = b.astype(BF16)
    bl = (b - bh.astype(F32)).astype(BF16)
    d = lambda p, q: jnp.dot(p, q, preferred_element_type=F32)
    return d(ah, bh) + (d(ah, bl) + d(al, bh))


def _silu(x):
    hx = 0.5 * x
    return hx + hx * jnp.tanh(hx)


def _sigmoid(x):
    return 0.5 + 0.5 * jnp.tanh(0.5 * x)


def _softplus(x):
    return jnp.maximum(x, 0.0) + jnp.log(1.0 + jnp.exp(-jnp.abs(x)))


def _iota(shape, dim):
    return lax.broadcasted_iota(jnp.int32, shape, dim)


def _norm_proj_kernel(x_ref, g_ref, wa_ref, wzx_ref, wqg_ref, ws_ref, hx_ref, hq_ref, cwx_ref, cbx_ref, cwq_ref,
                      a_ref, zx_ref, qg_ref, s_ref, nhx_ref, nhq_ref, bx_ref, bq_ref, *, nseq, seg, tiles_per_seq):
    first = (pl.program_id(0) % tiles_per_seq) == 0

    @pl.when(first)
    def _():
        bx_ref[:, 0:HIST_PAD, :] = hx_ref[...]
        bq_ref[:, 0:HIST_PAD, :] = hq_ref[...]

    x = x_ref[...]
    u = x * lax.rsqrt(jnp.mean(x * x, axis=-1, keepdims=True) + EPS) * g_ref[...]
    ub = u.astype(BF16)
    a_ref[...] = jnp.dot(ub, wa_ref[...], preferred_element_type=F32)
    s_ref[...] = jnp.dot(ub, ws_ref[...], preferred_element_type=F32)
    zx = jnp.dot(ub, wzx_ref[...], preferred_element_type=F32)
    qg = jnp.dot(ub, wqg_ref[...], preferred_element_type=F32)
    zx_ref[:, :SSM_INNER] = zx[:, :SSM_INNER]
    qg_ref[:, GDN_QKV:] = qg[:, GDN_QKV:]
    for s in range(nseq):
        rows = slice(s * seg, (s + 1) * seg)
        xbc = zx[rows, SSM_INNER:]
        nhx_ref[s] = _last_rows(bx_ref, s, xbc, seg)
        zx_ref[rows, SSM_INNER:] = _silu(_short_conv(xbc, bx_ref, s, cwx_ref, SSM_CONV_W, seg) + cbx_ref[...])
        qkv = qg[rows, :GDN_QKV]
        nhq_ref[s] = _last_rows(bq_ref, s, qkv, seg)
        qg_ref[rows, :GDN_QKV] = _silu(_short_conv(qkv, bq_ref, s, cwq_ref, GDN_CONV_W, seg))


def _last_rows(buf_ref, s, x, rows):
    if rows >= HIST_PAD:
        return x[rows - HIST_PAD:, :]
    return jnp.concatenate([buf_ref[s, rows:HIST_PAD, :], x], axis=0)


def _row_tile(n, want):
    t = min(want, n)
    while n % t:
        t //= 2
    return t


def _norm_proj(x, g, wa, wzx, wqg, ws, hist_x, hist_q, cwx, cbx, cwq):
    bsz, t, _ = x.shape
    n = bsz * t
    tm = _row_tile(n, 512)
    nseq = max(1, tm // t)
    seg = tm // nseq
    tiles_per_seq = max(1, t // tm)
    assert nseq * seg == tm and (t % tm == 0 or tm % t == 0)
    pad = lambda h: jnp.pad(h, ((0, 0), (HIST_PAD - h.shape[1], 0), (0, 0)))
    full = lambda w: pl.BlockSpec(w.shape, lambda i: (0, 0))
    rows = lambda c: pl.BlockSpec((tm, c), lambda i: (i, 0))
    hist = lambda c: pl.BlockSpec((nseq, HIST_PAD, c), lambda i: (i // tiles_per_seq, 0, 0))
    widths = (wa.shape[1], wzx.shape[1], wqg.shape[1], ws.shape[1])
    return pl.pallas_call(
        functools.partial(_norm_proj_kernel, nseq=nseq, seg=seg, tiles_per_seq=tiles_per_seq),
        grid=(n // tm,),
        in_specs=[rows(D_MODEL), full(g), full(wa), full(wzx), full(wqg), full(ws),
                  hist(SSM_XBC), hist(GDN_QKV), full(cwx), full(cbx), full(cwq)],
        out_specs=[rows(c) for c in widths] + [hist(SSM_XBC), hist(GDN_QKV)],
        out_shape=[jax.ShapeDtypeStruct((n, c), F32) for c in widths]
                  + [jax.ShapeDtypeStruct((bsz, HIST_PAD, SSM_XBC), F32),
                     jax.ShapeDtypeStruct((bsz, HIST_PAD, GDN_QKV), F32)],
        scratch_shapes=[pltpu.VMEM((nseq, seg + HIST_PAD, SSM_XBC), F32),
                        pltpu.VMEM((nseq, seg + HIST_PAD, GDN_QKV), F32)],
        compiler_params=_cparams("arbitrary"),
        name="norm_proj",
    )(x.reshape(n, D_MODEL), g, wa, wzx, wqg, ws, pad(hist_x), pad(hist_q), cwx, cbx, cwq)


CONV_PAD = 32
CONV_ROWS = 32


def _conv_a_kernel(a_ref, hist_ref, w_ref, b_ref, lg_ref, lb_ref, y_ref, nh_ref, buf_ref, sh_ref, *, tt):
    t = pl.program_id(1)

    @pl.when(t == 0)
    def _():
        buf_ref[0:CONV_PAD, :] = hist_ref[0]

    a = a_ref[0]
    glu = a[:, :CONV_CH] * _sigmoid(a[:, CONV_CH:])
    buf_ref[CONV_PAD:CONV_PAD + tt, :] = glu
    full = buf_ref[...]
    n = tt + CONV_PAD
    sh_ref[0] = full
    for s in range(1, SUBLANES):
        sh_ref[s] = pltpu.roll(full, n - s, axis=0)
    off = CONV_PAD - (CONV_W - 1)
    for r0 in range(0, tt, CONV_ROWS):
        acc = jnp.zeros((CONV_ROWS, CONV_CH), F32)
        for k in range(CONV_W):
            s = (off + k) % SUBLANES
            base = r0 + off + k - s
            acc = acc + w_ref[k:k + 1, :] * sh_ref[s, base:base + CONV_ROWS, :]
        y = acc + b_ref[...]
        mu = jnp.mean(y, axis=-1, keepdims=True)
        yc = y - mu
        var = jnp.mean(yc * yc, axis=-1, keepdims=True)
        y = yc * lax.rsqrt(var + EPS) * lg_ref[...] + lb_ref[...]
        y_ref[0, r0:r0 + CONV_ROWS, :] = _silu(y).astype(y_ref.dtype)
    nh_ref[0] = buf_ref[tt:tt + CONV_PAD, :]
    buf_ref[0:CONV_PAD, :] = buf_ref[tt:tt + CONV_PAD, :]


def _conv_a(a_in, hist, w, b, lg, lb):
    bsz, t, _ = a_in.shape
    tt = _row_tile(t, 256)
    hist_p = jnp.pad(hist, ((0, 0), (CONV_PAD - (CONV_W - 1), 0), (0, 0)))
    vec = lambda v: pl.BlockSpec(v.shape, lambda i, j: (0, 0))
    y, nh = pl.pallas_call(
        functools.partial(_conv_a_kernel, tt=tt),
        grid=(bsz, t // tt),
        in_specs=[pl.BlockSpec((1, tt, 2 * CONV_CH), lambda i, j: (i, j, 0)),
                  pl.BlockSpec((1, CONV_PAD, CONV_CH), lambda i, j: (i, 0, 0)),
                  vec(w), vec(b), vec(lg), vec(lb)],
        out_specs=[pl.BlockSpec((1, tt, CONV_CH), lambda i, j: (i, j, 0)),
                   pl.BlockSpec((1, CONV_PAD, CONV_CH), lambda i, j: (i, 0, 0))],
        out_shape=[jax.ShapeDtypeStruct((bsz, t, CONV_CH), BF16),
                   jax.ShapeDtypeStruct((bsz, CONV_PAD, CONV_CH), F32)],
        scratch_shapes=[pltpu.VMEM((tt + CONV_PAD, CONV_CH), F32),
                        pltpu.VMEM((SUBLANES, tt + CONV_PAD, CONV_CH), F32)],
        compiler_params=_cparams("arbitrary", "arbitrary"),
        name="conv_a",
    )(a_in, hist_p, w, b, lg, lb)
    return y, nh[:, CONV_PAD - (CONV_W - 1):, :]


HIST_PAD = 8


def _short_conv(x, buf_ref, b, w_ref, width, rows):
    buf_ref[b, HIST_PAD:HIST_PAD + rows, :] = x
    off = HIST_PAD - (width - 1)
    acc = w_ref[0:1, :] * buf_ref[b, off:off + rows, :]
    for k in range(1, width):
        acc = acc + w_ref[k:k + 1, :] * buf_ref[b, off + k:off + k + rows, :]
    buf_ref[b, 0:HIST_PAD, :] = buf_ref[b, rows:rows + HIST_PAD, :]
    return acc


def _lower_tri(rows):
    return _iota((rows, rows), 0) >= _iota((rows, rows), 1)


def _pair_mask():
    return _iota((1, LANES), 1) < HEAD_DIM


def _ssd_kernel(zx_ref, s_ref, h0_ref, dtb_ref, arow_ref, dx_ref, ng_ref,
                ex_ref, y_ref, hout_ref, h_ref, *, rows, bb):
    c = pl.program_id(1)

    hpg = SSM_HEADS // SSM_GROUPS

    @pl.when(c == 0)
    def _():
        for b in range(bb):
            for p in range(N_PAIRS):
                cols = []
                for hh in range(2):
                    ht = jnp.transpose(h0_ref[b, 2 * p + hh])
                    z = jnp.zeros_like(ht)
                    cols.append(jnp.concatenate([ht, z] if (2 * p + hh) // hpg == 0 else [z, ht], axis=0))
                h_ref[b, p] = jnp.concatenate(cols, axis=1)

    causal = _lower_tri(rows)
    lane = _iota((1, LANES), 1)
    first_half = _pair_mask()
    srow_g = _iota((LANES, LANES), 0) // SSM_STATE
    scol_h = _iota((LANES, LANES), 1) // HEAD_DIM
    ex = ex_ref[...]
    tril = causal.astype(BF16)
    assert SSM_INNER // SSM_GROUPS == LANES + HEAD_DIM and N_PAIRS == 3
    top_rows = _iota((LANES, LANES), 0) < HEAD_DIM
    ones_all = jnp.ones((LANES, LANES), BF16)
    ones_top = top_rows.astype(BF16)
    ones_bot = jnp.logical_not(top_rows).astype(BF16)

    for b in range(bb):
        zx = zx_ref[b]
        z = zx[:, :SSM_INNER]
        xbc = zx[:, SSM_INNER:]
        xs = xbc[:, :SSM_INNER]
        bm = xbc[:, SSM_INNER:SSM_INNER + LANES]
        cm = xbc[:, SSM_INNER + LANES:]

        dt = _softplus(s_ref[b] + dtb_ref[...])
        a = dt * arow_ref[...]
        acum = _sel_dot(tril, a)
        a_last = acum[rows - 1:rows, :]
        dt_x = _dot_sel(dt, ex)
        ea_x = _dot_sel(jnp.exp(acum), ex)
        te_x = _dot_sel(jnp.exp(a_last - acum), ex)
        cd_x = _dot_sel(jnp.broadcast_to(jnp.exp(a_last), (SUBLANES, LANES)), ex)[0:1, :]

        scores = []
        for g in range(SSM_GROUPS):
            cm_g = jnp.where(lane // SSM_STATE == g, cm, 0.0)
            scores.append(_dot_nt(cm_g, bm))
        bm_t = jnp.transpose(bm)

        ys = []
        for p in range(N_PAIRS):
            sl = slice(p * LANES, (p + 1) * LANES)
            x_p = xs[:, sl]
            xdt = x_p * dt_x[:, sl]
            yd = []
            for hh in range(2):
                h = 2 * p + hh
                col = jnp.broadcast_to(acum[:, h:h + 1], (rows, rows))
                dec = jnp.exp(jnp.where(causal, col - jnp.transpose(col), -1e30))
                yd.append(_dot(scores[h // (SSM_HEADS // SSM_GROUPS)] * dec, xdt))
            y_diag = jnp.where(first_half, yd[0], yd[1])
            h_p = h_ref[b, p]
            y_off = _dot(cm, h_p) * ea_x[:, sl]
            keep = srow_g == (2 * p + scol_h) // (SSM_HEADS // SSM_GROUPS)
            upd = _dot(bm_t, xdt * te_x[:, sl])
            h_ref[b, p] = h_p * cd_x[:, sl] + jnp.where(keep, upd, 0.0)
            ys.append(y_diag + y_off + dx_ref[:, sl] * x_p)
        y = jnp.concatenate(ys, axis=-1) * _silu(z)
        sq = [jnp.square(y[:, p * LANES:(p + 1) * LANES]) for p in range(N_PAIRS)]
        g0 = _dot_sel(sq[0], ones_all) + _dot_sel(sq[1], ones_top)
        g1 = _dot_sel(sq[1], ones_bot) + _dot_sel(sq[2], ones_all)
        ms = jnp.concatenate([g0, jnp.where(first_half, g0, g1), g1], axis=-1) * (1.0 / (SSM_INNER // SSM_GROUPS))
        y = y * lax.rsqrt(ms + EPS) * ng_ref[...]
        y_ref[b] = y.astype(y_ref.dtype)

    @pl.when(c == pl.num_programs(1) - 1)
    def _():
        for b in range(bb):
            for h in range(SSM_HEADS):
                g, hh = h // hpg, h % 2
                blk = h_ref[b, h // 2][g * SSM_STATE:(g + 1) * SSM_STATE, hh * HEAD_DIM:(hh + 1) * HEAD_DIM]
                hout_ref[b, h] = jnp.transpose(blk)


def _expand_matrix(first_lane):
    m = np.zeros((LANES, SSM_INNER), np.float32)
    for h in range(SSM_HEADS):
        m[first_lane + h, h * HEAD_DIM:(h + 1) * HEAD_DIM] = 1.0
    return jnp.asarray(m, BF16)


def _lane_row(vals, first_lane):
    return jnp.zeros((1, LANES), F32).at[0, first_lane:first_lane + vals.shape[0]].set(vals.astype(F32))


def _ssd(zx, small, h0, dt_bias, a_log, d_skip, norm_g, rows, bb):
    bsz, t, _ = zx.shape
    dtb = _lane_row(dt_bias, 0)
    arow = _lane_row(-jnp.exp(a_log.astype(F32)), 0)
    dx = jnp.repeat(d_skip.astype(F32), HEAD_DIM)[None, :]
    full = lambda v: pl.BlockSpec(v.shape, lambda i, j: (0,) * v.ndim)
    blk = lambda c: pl.BlockSpec((bb, rows, c), lambda i, j: (i, j, 0))
    st = pl.BlockSpec((bb, SSM_HEADS, HEAD_DIM, SSM_STATE), lambda i, j: (i, 0, 0, 0))
    consts = (dtb, arow, dx, norm_g[None, :], _expand_matrix(0))
    return pl.pallas_call(
        functools.partial(_ssd_kernel, rows=rows, bb=bb),
        grid=(bsz // bb, t // rows),
        in_specs=[blk(zx.shape[-1]), blk(LANES), st] + [full(v) for v in consts],
        out_specs=[blk(SSM_INNER), st],
        out_shape=[jax.ShapeDtypeStruct((bsz, t, SSM_INNER), BF16),
                   jax.ShapeDtypeStruct((bsz, SSM_HEADS, HEAD_DIM, SSM_STATE), F32)],
        scratch_shapes=[pltpu.VMEM((bb, N_PAIRS, LANES, LANES), F32)],
        compiler_params=_cparams("arbitrary", "arbitrary"),
        name="ssd",
    )(zx, small, h0, *consts)


BETA_LANE = 6
DECAY_LANE = 12


def _bdot(a, b):
    return lax.dot_general(a.astype(BF16), b.astype(BF16), (((2,), (1,)), ((0,), (0,))),
                           preferred_element_type=F32)


def _bdot_nt(a, b):
    return lax.dot_general(a.astype(BF16), b.astype(BF16), (((2,), (2,)), ((0,), (0,))),
                           preferred_element_type=F32)


def _unit_lower_inverse(m, block):
    rdim = m.shape[-1]
    eye = (_iota((rdim, rdim), 0) == _iota((rdim, rdim), 1)).astype(F32)
    x = -m
    t = eye + x
    p = x
    for _ in range(int(math.log2(block)) - 2):
        p = _bdot(p, p)
        t = t + _bdot(t, p)
    resid = (eye - t) - _bdot(m, t)
    return t + _bdot(t, resid)


def _gdn_kernel(qg_ref, s_ref, s0_ref, bias_ref, arow_ref, ng_ref, eb_ref, eg_ref,
                y_ref, sout_ref, st_ref, *, tt, ck, bb):
    c = pl.program_id(1)
    nck = tt // ck
    rdim = 2 * ck

    @pl.when(c == 0)
    def _():
        z = jnp.zeros((HEAD_DIM, HEAD_DIM), F32)
        for b in range(bb):
            for p in range(N_PAIRS):
                top = jnp.concatenate([s0_ref[b, 2 * p], z], axis=1)
                bot = jnp.concatenate([z, s0_ref[b, 2 * p + 1]], axis=1)
                st_ref[b, p] = jnp.concatenate([top, bot], axis=0)

    r_i = _iota((rdim, rdim), 0)
    c_i = _iota((rdim, rdim), 1)
    same_blk = (r_i // ck) == (c_i // ck)
    incl = same_blk & (r_i >= c_i)
    strict = same_blk & (r_i > c_i)
    first_half = _pair_mask()
    same_head = (_iota((LANES, LANES), 0) // HEAD_DIM) == (_iota((LANES, LANES), 1) // HEAD_DIM)
    tr = _iota((tt, tt), 0)
    tc = _iota((tt, tt), 1)
    blk_tril = (((tr // ck) == (tc // ck)) & (tr >= tc)).astype(BF16)
    eb = eb_ref[...]
    eg = eg_ref[...]
    hm = same_head.astype(BF16)
    head_sumsq = lambda x: jnp.concatenate(
        [_dot_sel(jnp.square(x[:, p * LANES:(p + 1) * LANES]), hm) for p in range(N_PAIRS)], axis=-1)

    tiles = {name: [] for name in ('q', 'k', 'kb', 'rhs', 'qd', 'kd', 'col', 'cd')}
    gates = []
    for b in range(bb):
        qg = qg_ref[b]
        gates.append(qg[:, GDN_QKV:])
        qkv = qg[:, :GDN_QKV]
        q = qkv[:, :GDN_INNER]
        k = qkv[:, GDN_INNER:2 * GDN_INNER]
        v = qkv[:, 2 * GDN_INNER:]
        q = q * lax.rsqrt(head_sumsq(q) + EPS) * (HEAD_DIM ** -0.5)
        k = k * lax.rsqrt(head_sumsq(k) + EPS)
        s = s_ref[b]
        beta = _sigmoid(s)
        g = _softplus(s + bias_ref[...]) * arow_ref[...]
        gc = _sel_dot(blk_tril, g)
        g_last = jnp.concatenate(
            [jnp.broadcast_to(gc[(i + 1) * ck - 1:(i + 1) * ck, :], (ck, LANES)) for i in range(nck)], axis=0)
        beta_x = _dot_sel(beta, eb)
        eg_x = _dot_sel(jnp.exp(gc), eg)
        kd_x = _dot_sel(jnp.exp(g_last - gc), eg)
        cd_x = _dot_sel(jnp.exp(g_last), eg)
        kb = k * beta_x
        full = dict(q=q, k=k, kb=kb, qd=q * eg_x, kd=k * kd_x, cd=cd_x)
        vb = v * beta_x
        kbe = kb * eg_x
        for i in range(nck):
            rs = slice(i * ck, (i + 1) * ck)
            for p in range(N_PAIRS):
                sl = slice(p * LANES, (p + 1) * LANES)
                for name, arr in full.items():
                    tiles[name].append(arr[rs, sl])
                tiles['rhs'].append(jnp.concatenate([vb[rs, sl], kbe[rs, sl]], axis=-1))
                lane0 = DECAY_LANE + 2 * p
                tiles['col'].append(jnp.concatenate(
                    [jnp.broadcast_to(gc[rs, lane0 + hh:lane0 + hh + 1], (ck, rdim)) for hh in range(2)], axis=0))

    st = lambda name: jnp.stack(tiles[name], axis=0)
    stack2 = lambda x: jnp.concatenate([jnp.where(first_half, x, 0.0), jnp.where(first_half, 0.0, x)], axis=1)
    k_st = stack2(st('k'))
    col = st('col')
    diff = col - jnp.swapaxes(col, 1, 2)
    dec = jnp.exp(jnp.where(incl, diff, -1e30))
    m = _bdot_nt(stack2(st('kb')), k_st) * jnp.where(strict, dec, 0.0)
    t_inv = _unit_lower_inverse(m, ck)
    rhs = st('rhs')
    sol = _bdot(t_inv, jnp.concatenate([rhs, rhs], axis=1))
    u = jnp.where(first_half, sol[:, :ck, :LANES], sol[:, ck:, :LANES])
    w = jnp.where(first_half, sol[:, :ck, LANES:], sol[:, ck:, LANES:])
    attn = _bdot_nt(stack2(st('q')), k_st) * dec
    qd = st('qd')
    kd_t = jnp.swapaxes(st('kd'), 1, 2)
    cd = st('cd')

    gsel = lambda x, i: jnp.stack([x[(b * nck + i) * N_PAIRS + p] for b in range(bb) for p in range(N_PAIRS)], axis=0)
    state = st_ref[...].reshape(bb * N_PAIRS, LANES, LANES)
    o_chunks = []
    for i in range(nck):
        v_new = gsel(u, i) - _bdot(gsel(w, i), state)
        intra = _bdot(gsel(attn, i), jnp.concatenate([v_new, v_new], axis=1))
        o_chunks.append(_bdot(gsel(qd, i), state) + jnp.where(first_half, intra[:, :ck], intra[:, ck:]))
        upd = _bdot(gsel(kd_t, i), v_new)
        state = state * gsel(cd, i)[:, 0:1, :] + jnp.where(same_head, upd, 0.0)
    st_ref[...] = state.reshape(bb, N_PAIRS, LANES, LANES)

    for b in range(bb):
        o = jnp.concatenate(
            [jnp.concatenate([o_chunks[i][b * N_PAIRS + p] for p in range(N_PAIRS)], axis=-1) for i in range(nck)],
            axis=0)
        ms = head_sumsq(o) * (1.0 / HEAD_DIM)
        o = o * lax.rsqrt(ms + EPS) * ng_ref[...] * _silu(gates[b])
        y_ref[b] = o.astype(y_ref.dtype)

    @pl.when(c == pl.num_programs(1) - 1)
    def _():
        for b in range(bb):
            for h in range(GDN_HEADS):
                lo = (h % 2) * HEAD_DIM
                sout_ref[b, h] = st_ref[b, h // 2][lo:lo + HEAD_DIM, lo:lo + HEAD_DIM]


def _gdn(qg, small, s0, a_log, dt_bias, norm_g, tt, ck, bb):
    bsz, t, _ = qg.shape
    bias = _lane_row(dt_bias, DECAY_LANE)
    arow = _lane_row(-jnp.exp(a_log.astype(F32)), DECAY_LANE)
    ng = jnp.tile(norm_g.astype(F32), GDN_HEADS)[None, :]
    full = lambda v: pl.BlockSpec(v.shape, lambda i, j: (0,) * v.ndim)
    blk = lambda c: pl.BlockSpec((bb, tt, c), lambda i, j: (i, j, 0))
    st = pl.BlockSpec((bb, GDN_HEADS, HEAD_DIM, HEAD_DIM), lambda i, j: (i, 0, 0, 0))
    consts = (bias, arow, ng, _expand_matrix(BETA_LANE), _expand_matrix(DECAY_LANE))
    return pl.pallas_call(
        functools.partial(_gdn_kernel, tt=tt, ck=ck, bb=bb),
        grid=(bsz // bb, t // tt),
        in_specs=[blk(qg.shape[-1]), blk(LANES), st] + [full(v) for v in consts],
        out_specs=[blk(GDN_INNER), st],
        out_shape=[jax.ShapeDtypeStruct((bsz, t, GDN_INNER), BF16),
                   jax.ShapeDtypeStruct((bsz, GDN_HEADS, HEAD_DIM, HEAD_DIM), F32)],
        scratch_shapes=[pltpu.VMEM((bb, N_PAIRS, LANES, LANES), F32)],
        compiler_params=_cparams("arbitrary", "arbitrary"),
        name="gdn",
    )(qg, small, s0, *consts)


def _mix_residual(x_ref, ya_ref, yb_ref, yc_ref, woa_ref, wob_ref, woc_ref):
    mix = (jnp.dot(ya_ref[...], woa_ref[...], preferred_element_type=F32)
           + jnp.dot(yb_ref[...], wob_ref[...], preferred_element_type=F32)
           + jnp.dot(yc_ref[...], woc_ref[...], preferred_element_type=F32))
    return x_ref[...] + mix


def _rms(x, g):
    return x * lax.rsqrt(jnp.mean(x * x, axis=-1, keepdims=True) + EPS) * g


def _ffn_kernel(x_ref, ya_ref, yb_ref, yc_ref, woa_ref, wob_ref, woc_ref, g_ref, wg_ref, wu_ref, wd_ref,
                o_ref, xn_ref, h_ref, acc_ref):
    j = pl.program_id(1)

    @pl.when(j == 0)
    def _():
        xn = _mix_residual(x_ref, ya_ref, yb_ref, yc_ref, woa_ref, wob_ref, woc_ref)
        xn_ref[...] = xn
        h_ref[...] = _rms(xn, g_ref[...]).astype(BF16)
        acc_ref[...] = jnp.zeros_like(acc_ref)

    h = h_ref[...]
    act = _silu(jnp.dot(h, wg_ref[...], preferred_element_type=F32)) * jnp.dot(h, wu_ref[...], preferred_element_type=F32)
    acc_ref[...] += jnp.dot(act.astype(BF16), wd_ref[...], preferred_element_type=F32)

    @pl.when(j == pl.num_programs(1) - 1)
    def _():
        o_ref[...] = xn_ref[...] + acc_ref[...]


FFN_CHUNKS = 2


def _out_ffn(x2d, ya, yb, yc, wo, g, wg, wu, wd):
    n = x2d.shape[0]
    tm = _row_tile(n, 512)
    f = wg.shape[1]
    tf = f // FFN_CHUNKS
    rows = lambda c: pl.BlockSpec((tm, c), lambda i, j: (i, 0))
    full = lambda w: pl.BlockSpec(w.shape, lambda i, j: (0, 0))
    return pl.pallas_call(
        _ffn_kernel,
        grid=(n // tm, FFN_CHUNKS),
        in_specs=[rows(D_MODEL), rows(ya.shape[1]), rows(yb.shape[1]), rows(yc.shape[1]),
                  full(wo[0]), full(wo[1]), full(wo[2]), full(g),
                  pl.BlockSpec((D_MODEL, tf), lambda i, j: (0, j)),
                  pl.BlockSpec((D_MODEL, tf), lambda i, j: (0, j)),
                  pl.BlockSpec((tf, D_MODEL), lambda i, j: (j, 0))],
        out_specs=rows(D_MODEL),
        out_shape=jax.ShapeDtypeStruct((n, D_MODEL), F32),
        scratch_shapes=[pltpu.VMEM((tm, D_MODEL), F32), pltpu.VMEM((tm, D_MODEL), BF16),
                        pltpu.VMEM((tm, D_MODEL), F32)],
        compiler_params=_cparams("parallel", "arbitrary"),
        name="out_ffn",
    )(x2d, ya, yb, yc, wo[0], wo[1], wo[2], g, wg, wu, wd)


ROUTE_TILE = 512


def _router_kernel(x_ref, ya_ref, yb_ref, yc_ref, woa_ref, wob_ref, woc_ref, g_ref, wr_ref, br_ref,
                   xn_ref, h_ref, gate_ref, slot_ref, cnt_ref, run_ref, *, cap):
    i = pl.program_id(0)

    @pl.when(i == 0)
    def _():
        run_ref[...] = jnp.zeros_like(run_ref)

    xn = _mix_residual(x_ref, ya_ref, yb_ref, yc_ref, woa_ref, wob_ref, woc_ref)
    xn_ref[...] = xn
    h = _rms(xn, g_ref[...])
    h_ref[...] = h
    tm = h.shape[0]
    lane = _iota((1, LANES), 1)
    logits = jnp.where(lane < N_EXPERTS, _dot_hp(h, wr_ref[...]) + br_ref[...], -jnp.inf)
    m1 = jnp.max(logits, axis=-1, keepdims=True)
    i1 = jnp.min(jnp.where(logits == m1, lane, LANES), axis=-1, keepdims=True)
    rest = jnp.where(lane == i1, -jnp.inf, logits)
    m2 = jnp.max(rest, axis=-1, keepdims=True)
    i2 = jnp.min(jnp.where(rest == m2, lane, LANES), axis=-1, keepdims=True)
    e2 = jnp.exp(m2 - m1)
    inv = 1.0 / (1.0 + e2)
    gate_ref[...] = jnp.where(lane == 0, inv, 0.0) + jnp.where(lane == 1, e2 * inv, 0.0)

    chosen = jnp.where((lane == i1) | (lane == i2), 1.0, 0.0)
    before = _iota((tm, tm), 0) > _iota((tm, tm), 1)
    rank = jnp.dot(before.astype(BF16), chosen.astype(BF16), preferred_element_type=F32) + run_ref[0:1, :]
    base = lane.astype(F32) * float(cap)
    pick = lambda idx: jnp.sum(jnp.where(lane == idx, rank + base, 0.0), axis=-1, keepdims=True)
    slots = jnp.where(lane == 0, pick(i1), 0.0) + jnp.where(lane == 1, pick(i2), 0.0)
    slots_t = jnp.transpose(slots).astype(jnp.int32)
    for k in range(2):
        slot_ref[0, k] = jnp.concatenate(
            [slots_t[k:k + 1, c * LANES:(c + 1) * LANES] for c in range(tm // LANES)], axis=0)
    run_ref[...] = run_ref[...] + jnp.sum(chosen, axis=0, keepdims=True)
    cnt_ref[...] = run_ref[...].astype(jnp.int32)


def _out_router(x2d, ya, yb, yc, wo, g, w_router, b_router, cap):
    n = x2d.shape[0]
    tm = _row_tile(n, ROUTE_TILE)
    wr = jnp.zeros((D_MODEL, LANES), F32).at[:, :N_EXPERTS].set(w_router.astype(F32))
    br = _lane_row(b_router, 0)
    rows = lambda c: pl.BlockSpec((tm, c), lambda i: (i, 0))
    full = lambda w: pl.BlockSpec(w.shape, lambda i: (0, 0))
    return pl.pallas_call(
        functools.partial(_router_kernel, cap=cap),
        grid=(n // tm,),
        in_specs=[rows(D_MODEL), rows(ya.shape[1]), rows(yb.shape[1]), rows(yc.shape[1]),
                  full(wo[0]), full(wo[1]), full(wo[2]), full(g), full(wr), full(br)],
        out_specs=[rows(D_MODEL), rows(D_MODEL), rows(LANES),
                   pl.BlockSpec((1, 2, tm // LANES, LANES), lambda i: (i, 0, 0, 0)),
                   pl.BlockSpec((SUBLANES, LANES), lambda i: (0, 0))],
        out_shape=[jax.ShapeDtypeStruct((n, D_MODEL), F32), jax.ShapeDtypeStruct((n, D_MODEL), F32),
                   jax.ShapeDtypeStruct((n, LANES), F32),
                   jax.ShapeDtypeStruct((n // tm, 2, tm // LANES, LANES), jnp.int32),
                   jax.ShapeDtypeStruct((SUBLANES, LANES), jnp.int32)],
        scratch_shapes=[pltpu.VMEM((SUBLANES, LANES), F32)],
        compiler_params=_cparams("arbitrary"),
        name="out_router",
    )(x2d, ya, yb, yc, wo[0], wo[1], wo[2], g, wr, br)


def _row_copy(src_ref, src_row, dst_ref, dst_row, sem):
    return pltpu.make_async_copy(src_ref.at[pl.ds(src_row, 1)], dst_ref.at[pl.ds(dst_row, 1)], sem)


def _slot_fetch(slots_hbm, tile, smem_ref, buf, sem):
    return pltpu.make_async_copy(slots_hbm.at[tile], smem_ref.at[buf], sem.at[buf])


def _for_rows(tm, fn):
    for c in range(tm // LANES):
        for base in range(0, LANES, SUBLANES):
            for j in range(SUBLANES):
                fn(c, base, j)


def _vmem_row(ref, base, j):
    return ref.at[pl.ds(base, SUBLANES)].at[pl.ds(j, 1)]


def _dispatch_kernel(cnt_ref, slots_hbm, h_ref, hs_hbm, slot_smem, zero_ref, slot_sem, row_sem, pad_sem,
                     *, tm, cap, bm):
    i = pl.program_id(0)
    n_tiles = pl.num_programs(0)

    @pl.when(i == 0)
    def _():
        _slot_fetch(slots_hbm, 0, slot_smem, 0, slot_sem).start()

    @pl.when(i + 1 < n_tiles)
    def _():
        _slot_fetch(slots_hbm, i + 1, slot_smem, (i + 1) % 2, slot_sem).start()

    _slot_fetch(slots_hbm, i, slot_smem, i % 2, slot_sem).wait()

    def issue(c, base, j):
        for k in range(2):
            dst = hs_hbm.at[pl.ds(slot_smem[i % 2, k, c, base + j], 1)]
            pltpu.make_async_copy(_vmem_row(h_ref, c * LANES + base, j), dst, row_sem).start(priority=k)

    _for_rows(tm, issue)

    def drain(r, carry):
        for k in range(2):
            _row_copy(h_ref, 0, hs_hbm, 0, row_sem).wait()
        return carry

    lax.fori_loop(0, tm, drain, 0, unroll=8)

    @pl.when(i == n_tiles - 1)
    def _():
        zero_ref[...] = jnp.zeros_like(zero_ref)
        for e in range(N_EXPERTS):
            c = cnt_ref[e]
            n_pad = ((c + bm - 1) // bm) * bm - c

            def fill(r, carry):
                _row_copy(zero_ref, 0, hs_hbm, e * cap + c + r, pad_sem).start()
                return carry

            def fill_wait(r, carry):
                _row_copy(zero_ref, 0, hs_hbm, 0, pad_sem).wait()
                return carry

            lax.fori_loop(0, n_pad, fill, 0)
            lax.fori_loop(0, n_pad, fill_wait, 0)


def _dispatch(counts, slots, h, cap, bm):
    n = h.shape[0]
    n_tiles = slots.shape[0]
    tm = slots.shape[2] * LANES
    return pl.pallas_call(
        functools.partial(_dispatch_kernel, tm=tm, cap=cap, bm=bm),
        grid_spec=pltpu.PrefetchScalarGridSpec(
            num_scalar_prefetch=1,
            grid=(n_tiles,),
            in_specs=[pl.BlockSpec(memory_space=pl.ANY), pl.BlockSpec((tm, D_MODEL), lambda i, cnt: (i, 0))],
            out_specs=pl.BlockSpec(memory_space=pl.ANY),
            scratch_shapes=[pltpu.SMEM((2, 2, tm // LANES, LANES), jnp.int32), pltpu.VMEM((SUBLANES, D_MODEL), F32),
                            pltpu.SemaphoreType.DMA((2,)), pltpu.SemaphoreType.DMA, pltpu.SemaphoreType.DMA],
        ),
        out_shape=jax.ShapeDtypeStruct((N_EXPERTS * cap, D_MODEL), F32),
        compiler_params=_cparams("arbitrary"),
        name="moe_dispatch",
    )(counts, slots, h)


def _experts_kernel(tbl_ref, hs_ref, wg_ref, wu_ref, wd_ref, ys_ref, hb_ref, acc_ref):
    s = pl.program_id(0)
    j = pl.program_id(1)

    @pl.when(tbl_ref[2, s] == 1)
    def _():
        @pl.when(j == 0)
        def _():
            hb_ref[...] = hs_ref[...].astype(BF16)
            acc_ref[...] = jnp.zeros_like(acc_ref)

        h = hb_ref[...]
        act = (_silu(jnp.dot(h, wg_ref[0], preferred_element_type=F32))
               * jnp.dot(h, wu_ref[0], preferred_element_type=F32))
        acc_ref[...] += jnp.dot(act.astype(BF16), wd_ref[0], preferred_element_type=F32)

        @pl.when(j == pl.num_programs(1) - 1)
        def _():
            ys_ref[...] = acc_ref[...]


def _block_table(counts, cap, bm, n_steps):
    nblk = (counts + bm - 1) // bm
    cum = jnp.cumsum(nblk)
    total = cum[-1]
    step = jnp.arange(n_steps, dtype=jnp.int32)
    last = jnp.maximum(total - 1, 0)
    eff = jnp.minimum(step, last)
    expert = jnp.minimum(jnp.sum(eff[:, None] >= cum[None, :], axis=1), N_EXPERTS - 1).astype(jnp.int32)
    blk = eff - (cum[expert] - nblk[expert])
    return jnp.stack([expert * (cap // bm) + blk, expert, (step < total).astype(jnp.int32)]).astype(jnp.int32)


def _experts(counts, hs, wg, wu, wd, cap, bm, n_assign):
    n_steps = n_assign // bm + N_EXPERTS
    tbl = _block_table(counts, cap, bm, n_steps)
    f = wg.shape[2]
    tf = f // FFN_CHUNKS
    chunk = lambda s, j, t: jnp.where(t[2, s] == 1, j, FFN_CHUNKS - 1)
    rows = pl.BlockSpec((bm, D_MODEL), lambda s, j, t: (t[0, s], 0))
    return pl.pallas_call(
        _experts_kernel,
        grid_spec=pltpu.PrefetchScalarGridSpec(
            num_scalar_prefetch=1,
            grid=(n_steps, FFN_CHUNKS),
            in_specs=[rows,
                      pl.BlockSpec((1, D_MODEL, tf), lambda s, j, t: (t[1, s], 0, chunk(s, j, t))),
                      pl.BlockSpec((1, D_MODEL, tf), lambda s, j, t: (t[1, s], 0, chunk(s, j, t))),
                      pl.BlockSpec((1, tf, D_MODEL), lambda s, j, t: (t[1, s], chunk(s, j, t), 0))],
            out_specs=rows,
            scratch_shapes=[pltpu.VMEM((bm, D_MODEL), BF16), pltpu.VMEM((bm, D_MODEL), F32)],
        ),
        out_shape=jax.ShapeDtypeStruct(hs.shape, F32),
        compiler_params=_cparams("arbitrary", "arbitrary"),
        name="moe_experts",
    )(tbl, hs, wg, wu, wd)


def _combine_kernel(slots_hbm, ys_hbm, xn_ref, gate_ref, gf_ref, o_ref, slot_smem, ybuf_ref, slot_sem, row_sem,
                    *, tm):
    i = pl.program_id(0)
    n_tiles = pl.num_programs(0)

    def gather(tile, buf):
        def issue(c, base, j):
            for k in range(2):
                src = ys_hbm.at[pl.ds(slot_smem[buf, k, c, base + j], 1)]
                dst = _vmem_row(ybuf_ref.at[buf, k], c * LANES + base, j)
                pltpu.make_async_copy(src, dst, row_sem.at[buf]).start()
        _for_rows(tm, issue)

    @pl.when(i == 0)
    def _():
        first = _slot_fetch(slots_hbm, 0, slot_smem, 0, slot_sem)
        first.start()
        first.wait()
        gather(0, 0)

        @pl.when(n_tiles > 1)
        def _():
            _slot_fetch(slots_hbm, 1, slot_smem, 1, slot_sem).start()

    @pl.when(i + 1 < n_tiles)
    def _():
        _slot_fetch(slots_hbm, i + 1, slot_smem, (i + 1) % 2, slot_sem).wait()
        gather(i + 1, (i + 1) % 2)

    def drain(r, carry):
        for k in range(2):
            _row_copy(ys_hbm, 0, ybuf_ref.at[i % 2, k], 0, row_sem.at[i % 2]).wait()
        return carry

    lax.fori_loop(0, tm, drain, 0, unroll=8)

    @pl.when(i + 2 < n_tiles)
    def _():
        _slot_fetch(slots_hbm, i + 2, slot_smem, i % 2, slot_sem).start()

    g = gate_ref[...]
    y = g[:, 0:1] * ybuf_ref[i % 2, 0] + g[:, 1:2] * ybuf_ref[i % 2, 1]
    o_ref[...] = _rms(xn_ref[...] + y, gf_ref[...])


def _combine(slots, ys, xn, gates, g_final):
    n = xn.shape[0]
    n_tiles = slots.shape[0]
    tm = slots.shape[2] * LANES
    rows = lambda c: pl.BlockSpec((tm, c), lambda i: (i, 0))
    return pl.pallas_call(
        functools.partial(_combine_kernel, tm=tm),
        grid=(n_tiles,),
        in_specs=[pl.BlockSpec(memory_space=pl.ANY), pl.BlockSpec(memory_space=pl.ANY),
                  rows(D_MODEL), rows(LANES), pl.BlockSpec(g_final.shape, lambda i: (0, 0))],
        out_specs=rows(D_MODEL),
        out_shape=jax.ShapeDtypeStruct((n, D_MODEL), F32),
        scratch_shapes=[pltpu.SMEM((2, 2, tm // LANES, LANES), jnp.int32), pltpu.VMEM((2, 2, tm, D_MODEL), F32),
                        pltpu.SemaphoreType.DMA((2,)), pltpu.SemaphoreType.DMA((2,))],
        compiler_params=_cparams("arbitrary"),
        name="moe_combine",
    )(slots, ys, xn, gates, g_final)


def _moe(x2d, ya, yb, yc, wo, g_ffn, w_router, b_router, wg, wu, wd, g_final):
    n = x2d.shape[0]
    bm = _row_tile(n, 512)
    cap = n
    xn, h, gates, slots, counts = _out_router(x2d, ya, yb, yc, wo, g_ffn, w_router, b_router, cap)
    counts = counts[0, :N_EXPERTS]
    hs = _dispatch(counts, slots, h, cap, bm)
    ys = _experts(counts, hs, wg, wu, wd, cap, bm, 2 * n)
    return _combine(slots, ys, xn, gates, g_final)


IN_SIZES = (2 * CONV_CH, SSM_INNER, SSM_XBC, SSM_HEADS, GDN_QKV, GDN_INNER, GDN_HEADS, GDN_HEADS)


def _prep_layer(l, p):
    off = np.concatenate([[0], np.cumsum(IN_SIZES)])
    w_in = p['w_in_bf16'][l]
    col = lambda i: w_in[:, off[i]:off[i + 1]]
    a_in, z, xbc, dt, qkv, gate, b_raw, a_raw = (col(i) for i in range(8))
    small = jnp.concatenate([dt, b_raw, a_raw, jnp.zeros((D_MODEL, SMALL_W - 3 * SSM_HEADS), BF16)], axis=1)
    wo = p['w_out'][l].astype(BF16)
    return dict(
        wa=a_in,
        wzx=jnp.concatenate([z, xbc], axis=1),
        wqg=jnp.concatenate([qkv, gate], axis=1),
        ws=small,
        wo=(wo[:CONV_CH], wo[CONV_CH:CONV_CH + SSM_INNER], wo[CONV_CH + SSM_INNER:]),
    )


def _trunk(x, st_conv_a, st_ssm_conv, st_ssm, st_gdn_conv, st_gdn, p, prep, ssd_rows, gdn_rows, bb):
    bsz, t, _ = x.shape
    n = bsz * t
    depth = p['g_mix'].shape[0]
    x2d = x.reshape(n, D_MODEL)
    new = [[] for _ in range(5)]
    for l in range(depth):
        w = prep[l]
        a_in, zx, qg, small, hist_x, hist_q = _norm_proj(
            x2d.reshape(bsz, t, D_MODEL), p['g_mix'][l][None, :], w['wa'], w['wzx'], w['wqg'], w['ws'],
            st_ssm_conv[l], st_gdn_conv[l], p['ssm_conv_w'][l], p['ssm_conv_b'][l][None, :], p['gdn_conv_w'][l])
        a_in = a_in.reshape(bsz, t, -1)
        zx = zx.reshape(bsz, t, -1)
        qg = qg.reshape(bsz, t, -1)
        small = small.reshape(bsz, t, -1)
        ya, conv_a = _conv_a(a_in, st_conv_a[l], p['conv_a_w'][l], p['conv_a_b'][l][None, :],
                             p['ln_a_g'][l][None, :], p['ln_a_b'][l][None, :])
        yb, ssm = _ssd(zx, small, st_ssm[l].astype(F32), p['ssm_dt_bias'][l], p['ssm_a_log'][l],
                       p['ssm_d'][l], p['ssm_norm_g'][l], ssd_rows, bb)
        yc, gdn = _gdn(qg, small, st_gdn[l].astype(F32), p['gdn_a_log'][l], p['gdn_dt_bias'][l],
                       p['gdn_norm_g'][l], gdn_rows[0], gdn_rows[1], bb)
        new[0].append(conv_a)
        new[1].append(hist_x[:, HIST_PAD - (SSM_CONV_W - 1):])
        new[2].append(ssm)
        new[3].append(hist_q[:, HIST_PAD - (GDN_CONV_W - 1):])
        new[4].append(gdn)
        flat = lambda y: y.reshape(n, -1)
        g_ffn = p['g_ffn'][l][None, :]
        if l % 2 == 0:
            x2d = _out_ffn(x2d, flat(ya), flat(yb), flat(yc), w['wo'], g_ffn,
                           prep['ffn'][l // 2][0], prep['ffn'][l // 2][1], prep['ffn'][l // 2][2])
        else:
            x2d = _moe(x2d, flat(ya), flat(yb), flat(yc), w['wo'], g_ffn, p['moe_w_router'][l // 2],
                       p['moe_b_router'][l // 2], *prep['moe'][l // 2], p['g_final'][None, :])
    return (x2d.reshape(bsz, t, D_MODEL),) + tuple(jnp.stack(s) for s in new)


def kernel(x_prompt, x_sample, state_conv_a, state_ssm_conv, state_ssm, state_gdn_conv, state_gdn, g_mix, w_in, conv_a_w, conv_a_b, ln_a_g, ln_a_b, ssm_conv_w, ssm_conv_b, ssm_dt_bias, ssm_a_log, ssm_d, ssm_norm_g, gdn_conv_w, gdn_a_log, gdn_dt_bias, gdn_norm_g, w_out, g_ffn, ffn_w_gate, ffn_w_up, ffn_w_down, moe_w_router, moe_b_router, moe_w_gate, moe_w_up, moe_w_down, g_final):
    p = dict(g_mix=g_mix, w_in=w_in, conv_a_w=conv_a_w, conv_a_b=conv_a_b, ln_a_g=ln_a_g, ln_a_b=ln_a_b,
             ssm_conv_w=ssm_conv_w, ssm_conv_b=ssm_conv_b, ssm_dt_bias=ssm_dt_bias, ssm_a_log=ssm_a_log,
             ssm_d=ssm_d, ssm_norm_g=ssm_norm_g, gdn_conv_w=gdn_conv_w, gdn_a_log=gdn_a_log,
             gdn_dt_bias=gdn_dt_bias, gdn_norm_g=gdn_norm_g, w_out=w_out, g_ffn=g_ffn,
             moe_w_router=moe_w_router, moe_b_router=moe_b_router, g_final=g_final)
    depth = g_mix.shape[0]
    assert depth % 2 == 0, "the final RMSNorm is fused into the expert layer, which must come last"
    p['w_in_bf16'] = w_in.astype(BF16)
    prep = {l: _prep_layer(l, p) for l in range(depth)}
    prep['ffn'] = [(ffn_w_gate[i].astype(BF16), ffn_w_up[i].astype(BF16), ffn_w_down[i].astype(BF16))
                   for i in range(ffn_w_gate.shape[0])]
    prep['moe'] = [(moe_w_gate[i].astype(BF16), moe_w_up[i].astype(BF16), moe_w_down[i].astype(BF16))
                   for i in range(moe_w_gate.shape[0])]
    bp, dt = x_prompt.shape[0], x_prompt.dtype
    zeros = lambda *s: jnp.zeros((depth, bp) + s, dt)
    outs_p = _trunk(x_prompt, zeros(CONV_W - 1, CONV_CH), zeros(SSM_CONV_W - 1, SSM_XBC),
                    zeros(SSM_HEADS, HEAD_DIM, SSM_STATE), zeros(GDN_CONV_W - 1, GDN_QKV),
                    zeros(GDN_HEADS, HEAD_DIM, HEAD_DIM), p, prep, ssd_rows=min(128, x_prompt.shape[1]),
                    gdn_rows=(min(256, x_prompt.shape[1]), min(64, x_prompt.shape[1])), bb=2)
    outs_s = _trunk(x_sample, state_conv_a, state_ssm_conv, state_ssm, state_gdn_conv, state_gdn, p, prep,
                    ssd_rows=x_sample.shape[1], gdn_rows=(x_sample.shape[1], x_sample.shape[1]), bb=2)
    return (outs_p[0], outs_s[0]) + outs_p[1:] + outs_s[1:]
```

```python
import functools
import math

import jax
import jax.numpy as jnp
import numpy as np
from jax import lax
from jax.experimental import pallas as pl
from jax.experimental.pallas import tpu as pltpu

F32 = jnp.float32
BF16 = jnp.bfloat16
EPS = 1e-6

LANES = 128
SUBLANES = 8
VMEM_BYTES_V7X = 64 * 1024 * 1024
VMEM_LIMIT = VMEM_BYTES_V7X * 3 // 4

D_MODEL = 1024
CONV_CH = 256
CONV_W = 31
SSM_HEADS = 6
HEAD_DIM = 64
SSM_INNER = SSM_HEADS * HEAD_DIM
SSM_STATE = 64
SSM_GROUPS = 2
SSM_XBC = SSM_INNER + 2 * SSM_GROUPS * SSM_STATE
SSM_CONV_W = 4
GDN_HEADS = 6
GDN_INNER = GDN_HEADS * HEAD_DIM
GDN_QKV = 3 * GDN_INNER
GDN_CONV_W = 4
N_PAIRS = 3
N_EXPERTS = 8
SMALL_W = LANES


def _cparams(*sem):
    return pltpu.CompilerParams(dimension_semantics=sem, vmem_limit_bytes=VMEM_LIMIT)


def _dot(a, b):
    return jnp.dot(a.astype(BF16), b.astype(BF16), preferred_element_type=F32)


def _dot_nt(a, b):
    return lax.dot_general(a.astype(BF16), b.astype(BF16), (((1,), (1,)), ((), ())),
                           preferred_element_type=F32)


def _split3(x):
    hi = x.astype(BF16)
    r1 = x - hi.astype(F32)
    mid = r1.astype(BF16)
    lo = (r1 - mid.astype(F32)).astype(BF16)
    return hi, mid, lo


def _dot_sel(x, sel_bf16, pieces=2):
    d = lambda p: jnp.dot(p, sel_bf16, preferred_element_type=F32)
    hi = x.astype(BF16)
    r1 = x - hi.astype(F32)
    mid = r1.astype(BF16)
    if pieces == 2:
        return d(hi) + d(mid)
    return d(hi) + d(mid) + d((r1 - mid.astype(F32)).astype(BF16))


def _sel_dot(sel_bf16, x):
    hi, mid, lo = _split3(x)
    d = lambda p: jnp.dot(sel_bf16, p, preferred_element_type=F32)
    return d(hi) + d(mid) + d(lo)


def _dot_hp(a, b):
    ah = a.astype(BF16)
    al = (a - ah.astype(F32)).astype(BF16)
    bh = b.astype(BF16)
    bl = (b - bh.astype(F32)).astype(BF16)
    d = lambda p, q: jnp.dot(p, q, preferred_element_type=F32)
    return d(ah, bh) + (d(ah, bl) + d(al, bh))


def _silu(x):
    hx = 0.5 * x
    return hx + hx * jnp.tanh(hx)


def _sigmoid(x):
    return 0.5 + 0.5 * jnp.tanh(0.5 * x)


def _softplus(x):
    return jnp.maximum(x, 0.0) + jnp.log(1.0 + jnp.exp(-jnp.abs(x)))


def _iota(shape, dim):
    return lax.broadcasted_iota(jnp.int32, shape, dim)


def _norm_proj_kernel(x_ref, g_ref, wa_ref, wzx_ref, wqg_ref, ws_ref, hx_ref, hq_ref, cwx_ref, cbx_ref, cwq_ref,
                      a_ref, zx_ref, qg_ref, s_ref, nhx_ref, nhq_ref, bx_ref, bq_ref, *, nseq, seg, tiles_per_seq):
    first = (pl.program_id(0) % tiles_per_seq) == 0

    @pl.when(first)
    def _():
        bx_ref[:, 0:HIST_PAD, :] = hx_ref[...]
        bq_ref[:, 0:HIST_PAD, :] = hq_ref[...]

    x = x_ref[...]
    u = x * lax.rsqrt(jnp.mean(x * x, axis=-1, keepdims=True) + EPS) * g_ref[...]
    ub = u.astype(BF16)
    a_ref[...] = jnp.dot(ub, wa_ref[...], preferred_element_type=F32)
    s_ref[...] = jnp.dot(ub, ws_ref[...], preferred_element_type=F32)
    zx = jnp.dot(ub, wzx_ref[...], preferred_element_type=F32)
    qg = jnp.dot(ub, wqg_ref[...], preferred_element_type=F32)
    zx_ref[:, :SSM_INNER] = zx[:, :SSM_INNER]
    qg_ref[:, GDN_QKV:] = qg[:, GDN_QKV:]
    for s in range(nseq):
        rows = slice(s * seg, (s + 1) * seg)
        xbc = zx[rows, SSM_INNER:]
        nhx_ref[s] = _last_rows(bx_ref, s, xbc, seg)
        zx_ref[rows, SSM_INNER:] = _silu(_short_conv(xbc, bx_ref, s, cwx_ref, SSM_CONV_W, seg) + cbx_ref[...])
        qkv = qg[rows, :GDN_QKV]
        nhq_ref[s] = _last_rows(bq_ref, s, qkv, seg)
        qg_ref[rows, :GDN_QKV] = _silu(_short_conv(qkv, bq_ref, s, cwq_ref, GDN_CONV_W, seg))


def _last_rows(buf_ref, s, x, rows):
    if rows >= HIST_PAD:
        return x[rows - HIST_PAD:, :]
    return jnp.concatenate([buf_ref[s, rows:HIST_PAD, :], x], axis=0)


def _row_tile(n, want):
    t = min(want, n)
    while n % t:
        t //= 2
    return t


def _norm_proj(x, g, wa, wzx, wqg, ws, hist_x, hist_q, cwx, cbx, cwq):
    bsz, t, _ = x.shape
    n = bsz * t
    tm = _row_tile(n, 512)
    nseq = max(1, tm // t)
    seg = tm // nseq
    tiles_per_seq = max(1, t // tm)
    assert nseq * seg == tm and (t % tm == 0 or tm % t == 0)
    pad = lambda h: jnp.pad(h, ((0, 0), (HIST_PAD - h.shape[1], 0), (0, 0)))
    full = lambda w: pl.BlockSpec(w.shape, lambda i: (0, 0))
    rows = lambda c: pl.BlockSpec((tm, c), lambda i: (i, 0))
    hist = lambda c: pl.BlockSpec((nseq, HIST_PAD, c), lambda i: (i // tiles_per_seq, 0, 0))
    widths = (wa.shape[1], wzx.shape[1], wqg.shape[1], ws.shape[1])
    return pl.pallas_call(
        functools.partial(_norm_proj_kernel, nseq=nseq, seg=seg, tiles_per_seq=tiles_per_seq),
        grid=(n // tm,),
        in_specs=[rows(D_MODEL), full(g), full(wa), full(wzx), full(wqg), full(ws),
                  hist(SSM_XBC), hist(GDN_QKV), full(cwx), full(cbx), full(cwq)],
        out_specs=[rows(c) for c in widths] + [hist(SSM_XBC), hist(GDN_QKV)],
        out_shape=[jax.ShapeDtypeStruct((n, c), F32) for c in widths]
                  + [jax.ShapeDtypeStruct((bsz, HIST_PAD, SSM_XBC), F32),
                     jax.ShapeDtypeStruct((bsz, HIST_PAD, GDN_QKV), F32)],
        scratch_shapes=[pltpu.VMEM((nseq, seg + HIST_PAD, SSM_XBC), F32),
                        pltpu.VMEM((nseq, seg + HIST_PAD, GDN_QKV), F32)],
        compiler_params=_cparams("arbitrary"),
        name="norm_proj",
    )(x.reshape(n, D_MODEL), g, wa, wzx, wqg, ws, pad(hist_x), pad(hist_q), cwx, cbx, cwq)


CONV_PAD = 32
CONV_ROWS = 32


def _conv_a_kernel(a_ref, hist_ref, w_ref, b_ref, lg_ref, lb_ref, y_ref, nh_ref, buf_ref, sh_ref, *, tt):
    t = pl.program_id(1)

    @pl.when(t == 0)
    def _():
        buf_ref[0:CONV_PAD, :] = hist_ref[0]

    a = a_ref[0]
    glu = a[:, :CONV_CH] * _sigmoid(a[:, CONV_CH:])
    buf_ref[CONV_PAD:CONV_PAD + tt, :] = glu
    full = buf_ref[...]
    n = tt + CONV_PAD
    sh_ref[0] = full
    for s in range(1, SUBLANES):
        sh_ref[s] = pltpu.roll(full, n - s, axis=0)
    off = CONV_PAD - (CONV_W - 1)
    for r0 in range(0, tt, CONV_ROWS):
        acc = jnp.zeros((CONV_ROWS, CONV_CH), F32)
        for k in range(CONV_W):
            s = (off + k) % SUBLANES
            base = r0 + off + k - s
            acc = acc + w_ref[k:k + 1, :] * sh_ref[s, base:base + CONV_ROWS, :]
        y = acc + b_ref[...]
        mu = jnp.mean(y, axis=-1, keepdims=True)
        yc = y - mu
        var = jnp.mean(yc * yc, axis=-1, keepdims=True)
        y = yc * lax.rsqrt(var + EPS) * lg_ref[...] + lb_ref[...]
        y_ref[0, r0:r0 + CONV_ROWS, :] = _silu(y).astype(y_ref.dtype)
    nh_ref[0] = buf_ref[tt:tt + CONV_PAD, :]
    buf_ref[0:CONV_PAD, :] = buf_ref[tt:tt + CONV_PAD, :]


def _conv_a(a_in, hist, w, b, lg, lb):
    bsz, t, _ = a_in.shape
    tt = _row_tile(t, 512)
    hist_p = jnp.pad(hist, ((0, 0), (CONV_PAD - (CONV_W - 1), 0), (0, 0)))
    vec = lambda v: pl.BlockSpec(v.shape, lambda i, j: (0, 0))
    y, nh = pl.pallas_call(
        functools.partial(_conv_a_kernel, tt=tt),
        grid=(bsz, t // tt),
        in_specs=[pl.BlockSpec((1, tt, 2 * CONV_CH), lambda i, j: (i, j, 0)),
                  pl.BlockSpec((1, CONV_PAD, CONV_CH), lambda i, j: (i, 0, 0)),
                  vec(w), vec(b), vec(lg), vec(lb)],
        out_specs=[pl.BlockSpec((1, tt, CONV_CH), lambda i, j: (i, j, 0)),
                   pl.BlockSpec((1, CONV_PAD, CONV_CH), lambda i, j: (i, 0, 0))],
        out_shape=[jax.ShapeDtypeStruct((bsz, t, CONV_CH), BF16),
                   jax.ShapeDtypeStruct((bsz, CONV_PAD, CONV_CH), F32)],
        scratch_shapes=[pltpu.VMEM((tt + CONV_PAD, CONV_CH), F32),
                        pltpu.VMEM((SUBLANES, tt + CONV_PAD, CONV_CH), F32)],
        compiler_params=_cparams("arbitrary", "arbitrary"),
        name="conv_a",
    )(a_in, hist_p, w, b, lg, lb)
    return y, nh[:, CONV_PAD - (CONV_W - 1):, :]


HIST_PAD = 8


def _short_conv(x, buf_ref, b, w_ref, width, rows):
    buf_ref[b, HIST_PAD:HIST_PAD + rows, :] = x
    off = HIST_PAD - (width - 1)
    acc = w_ref[0:1, :] * buf_ref[b, off:off + rows, :]
    for k in range(1, width):
        acc = acc + w_ref[k:k + 1, :] * buf_ref[b, off + k:off + k + rows, :]
    buf_ref[b, 0:HIST_PAD, :] = buf_ref[b, rows:rows + HIST_PAD, :]
    return acc


def _lower_tri(rows):
    return _iota((rows, rows), 0) >= _iota((rows, rows), 1)


def _pair_mask():
    return _iota((1, LANES), 1) < HEAD_DIM


def _ssd_kernel(zx_ref, s_ref, h0_ref, dtb_ref, arow_ref, dx_ref, ng_ref,
                ex_ref, y_ref, hout_ref, h_ref, *, rows, bb):
    c = pl.program_id(1)

    hpg = SSM_HEADS // SSM_GROUPS

    @pl.when(c == 0)
    def _():
        for b in range(bb):
            for p in range(N_PAIRS):
                cols = []
                for hh in range(2):
                    ht = jnp.transpose(h0_ref[b, 2 * p + hh])
                    z = jnp.zeros_like(ht)
                    cols.append(jnp.concatenate([ht, z] if (2 * p + hh) // hpg == 0 else [z, ht], axis=0))
                h_ref[b, p] = jnp.concatenate(cols, axis=1)

    causal = _lower_tri(rows)
    lane = _iota((1, LANES), 1)
    first_half = _pair_mask()
    srow_g = _iota((LANES, LANES), 0) // SSM_STATE
    scol_h = _iota((LANES, LANES), 1) // HEAD_DIM
    ex = ex_ref[...]
    tril = causal.astype(BF16)
    assert SSM_INNER // SSM_GROUPS == LANES + HEAD_DIM and N_PAIRS == 3
    top_rows = _iota((LANES, LANES), 0) < HEAD_DIM
    ones_all = jnp.ones((LANES, LANES), BF16)
    ones_top = top_rows.astype(BF16)
    ones_bot = jnp.logical_not(top_rows).astype(BF16)

    for b in range(bb):
        zx = zx_ref[b]
        z = zx[:, :SSM_INNER]
        xbc = zx[:, SSM_INNER:]
        xs = xbc[:, :SSM_INNER]
        bm = xbc[:, SSM_INNER:SSM_INNER + LANES]
        cm = xbc[:, SSM_INNER + LANES:]

        dt = _softplus(s_ref[b] + dtb_ref[...])
        a = dt * arow_ref[...]
        acum = _sel_dot(tril, a)
        a_last = acum[rows - 1:rows, :]
        dt_x = _dot_sel(dt, ex)
        ea_x = _dot_sel(jnp.exp(acum), ex)
        te_x = _dot_sel(jnp.exp(a_last - acum), ex)
        cd_x = _dot_sel(jnp.broadcast_to(jnp.exp(a_last), (SUBLANES, LANES)), ex)[0:1, :]

        scores = []
        for g in range(SSM_GROUPS):
            cm_g = jnp.where(lane // SSM_STATE == g, cm, 0.0)
            scores.append(_dot_nt(cm_g, bm))
        bm_t = jnp.transpose(bm)

        ys = []
        for p in range(N_PAIRS):
            sl = slice(p * LANES, (p + 1) * LANES)
            x_p = xs[:, sl]
            xdt = x_p * dt_x[:, sl]
            yd = []
            for hh in range(2):
                h = 2 * p + hh
                col = jnp.broadcast_to(acum[:, h:h + 1], (rows, rows))
                dec = jnp.exp(jnp.where(causal, col - jnp.transpose(col), -1e30))
                yd.append(_dot(scores[h // (SSM_HEADS // SSM_GROUPS)] * dec, xdt))
            y_diag = jnp.where(first_half, yd[0], yd[1])
            h_p = h_ref[b, p]
            y_off = _dot(cm, h_p) * ea_x[:, sl]
            keep = srow_g == (2 * p + scol_h) // (SSM_HEADS // SSM_GROUPS)
            upd = _dot(bm_t, xdt * te_x[:, sl])
            h_ref[b, p] = h_p * cd_x[:, sl] + jnp.where(keep, upd, 0.0)
            ys.append(y_diag + y_off + dx_ref[:, sl] * x_p)
        y = jnp.concatenate(ys, axis=-1) * _silu(z)
        sq = [jnp.square(y[:, p * LANES:(p + 1) * LANES]) for p in range(N_PAIRS)]
        g0 = _dot_sel(sq[0], ones_all) + _dot_sel(sq[1], ones_top)
        g1 = _dot_sel(sq[1], ones_bot) + _dot_sel(sq[2], ones_all)
        ms = jnp.concatenate([g0, jnp.where(first_half, g0, g1), g1], axis=-1) * (1.0 / (SSM_INNER // SSM_GROUPS))
        y = y * lax.rsqrt(ms + EPS) * ng_ref[...]
        y_ref[b] = y.astype(y_ref.dtype)

    @pl.when(c == pl.num_programs(1) - 1)
    def _():
        for b in range(bb):
            for h in range(SSM_HEADS):
                g, hh = h // hpg, h % 2
                blk = h_ref[b, h // 2][g * SSM_STATE:(g + 1) * SSM_STATE, hh * HEAD_DIM:(hh + 1) * HEAD_DIM]
                hout_ref[b, h] = jnp.transpose(blk)


def _expand_matrix(first_lane):
    m = np.zeros((LANES, SSM_INNER), np.float32)
    for h in range(SSM_HEADS):
        m[first_lane + h, h * HEAD_DIM:(h + 1) * HEAD_DIM] = 1.0
    return jnp.asarray(m, BF16)


def _lane_row(vals, first_lane):
    return jnp.zeros((1, LANES), F32).at[0, first_lane:first_lane + vals.shape[0]].set(vals.astype(F32))


def _ssd(zx, small, h0, dt_bias, a_log, d_skip, norm_g, rows, bb):
    bsz, t, _ = zx.shape
    dtb = _lane_row(dt_bias, 0)
    arow = _lane_row(-jnp.exp(a_log.astype(F32)), 0)
    dx = jnp.repeat(d_skip.astype(F32), HEAD_DIM)[None, :]
    full = lambda v: pl.BlockSpec(v.shape, lambda i, j: (0,) * v.ndim)
    blk = lambda c: pl.BlockSpec((bb, rows, c), lambda i, j: (i, j, 0))
    st = pl.BlockSpec((bb, SSM_HEADS, HEAD_DIM, SSM_STATE), lambda i, j: (i, 0, 0, 0))
    consts = (dtb, arow, dx, norm_g[None, :], _expand_matrix(0))
    return pl.pallas_call(
        functools.partial(_ssd_kernel, rows=rows, bb=bb),
        grid=(bsz // bb, t // rows),
        in_specs=[blk(zx.shape[-1]), blk(LANES), st] + [full(v) for v in consts],
        out_specs=[blk(SSM_INNER), st],
        out_shape=[jax.ShapeDtypeStruct((bsz, t, SSM_INNER), BF16),
                   jax.ShapeDtypeStruct((bsz, SSM_HEADS, HEAD_DIM, SSM_STATE), F32)],
        scratch_shapes=[pltpu.VMEM((bb, N_PAIRS, LANES, LANES), F32)],
        compiler_params=_cparams("arbitrary", "arbitrary"),
        name="ssd",
    )(zx, small, h0, *consts)


BETA_LANE = 6
DECAY_LANE = 12


def _bdot(a, b):
    return lax.dot_general(a.astype(BF16), b.astype(BF16), (((2,), (1,)), ((0,), (0,))),
                           preferred_element_type=F32)


def _bdot_nt(a, b):
    return lax.dot_general(a.astype(BF16), b.astype(BF16), (((2,), (2,)), ((0,), (0,))),
                           preferred_element_type=F32)


def _unit_lower_inverse(m, block):
    rdim = m.shape[-1]
    eye = (_iota((rdim, rdim), 0) == _iota((rdim, rdim), 1)).astype(F32)
    x = -m
    t = eye + x
    p = x
    for _ in range(int(math.log2(block)) - 2):
        p = _bdot(p, p)
        t = t + _bdot(t, p)
    resid = (eye - t) - _bdot(m, t)
    return t + _bdot(t, resid)


def _gdn_kernel(qg_ref, s_ref, s0_ref, bias_ref, arow_ref, ng_ref, eb_ref, eg_ref,
                y_ref, sout_ref, st_ref, *, tt, ck, bb):
    c = pl.program_id(1)
    nck = tt // ck
    rdim = 2 * ck

    @pl.when(c == 0)
    def _():
        z = jnp.zeros((HEAD_DIM, HEAD_DIM), F32)
        for b in range(bb):
            for p in range(N_PAIRS):
                top = jnp.concatenate([s0_ref[b, 2 * p], z], axis=1)
                bot = jnp.concatenate([z, s0_ref[b, 2 * p + 1]], axis=1)
                st_ref[b, p] = jnp.concatenate([top, bot], axis=0)

    r_i = _iota((rdim, rdim), 0)
    c_i = _iota((rdim, rdim), 1)
    same_blk = (r_i // ck) == (c_i // ck)
    incl = same_blk & (r_i >= c_i)
    strict = same_blk & (r_i > c_i)
    first_half = _pair_mask()
    same_head = (_iota((LANES, LANES), 0) // HEAD_DIM) == (_iota((LANES, LANES), 1) // HEAD_DIM)
    tr = _iota((tt, tt), 0)
    tc = _iota((tt, tt), 1)
    blk_tril = (((tr // ck) == (tc // ck)) & (tr >= tc)).astype(BF16)
    eb = eb_ref[...]
    eg = eg_ref[...]
    hm = same_head.astype(BF16)
    head_sumsq = lambda x: jnp.concatenate(
        [_dot_sel(jnp.square(x[:, p * LANES:(p + 1) * LANES]), hm) for p in range(N_PAIRS)], axis=-1)

    tiles = {name: [] for name in ('q', 'k', 'kb', 'rhs', 'qd', 'kd', 'col', 'cd')}
    gates = []
    for b in range(bb):
        qg = qg_ref[b]
        gates.append(qg[:, GDN_QKV:])
        qkv = qg[:, :GDN_QKV]
        q = qkv[:, :GDN_INNER]
        k = qkv[:, GDN_INNER:2 * GDN_INNER]
        v = qkv[:, 2 * GDN_INNER:]
        q = q * lax.rsqrt(head_sumsq(q) + EPS) * (HEAD_DIM ** -0.5)
        k = k * lax.rsqrt(head_sumsq(k) + EPS)
        s = s_ref[b]
        beta = _sigmoid(s)
        g = _softplus(s + bias_ref[...]) * arow_ref[...]
        gc = _sel_dot(blk_tril, g)
        g_last = jnp.concatenate(
            [jnp.broadcast_to(gc[(i + 1) * ck - 1:(i + 1) * ck, :], (ck, LANES)) for i in range(nck)], axis=0)
        beta_x = _dot_sel(beta, eb)
        eg_x = _dot_sel(jnp.exp(gc), eg)
        kd_x = _dot_sel(jnp.exp(g_last - gc), eg)
        cd_x = _dot_sel(jnp.exp(g_last), eg)
        kb = k * beta_x
        full = dict(q=q, k=k, kb=kb, qd=q * eg_x, kd=k * kd_x, cd=cd_x)
        vb = v * beta_x
        kbe = kb * eg_x
        for i in range(nck):
            rs = slice(i * ck, (i + 1) * ck)
            for p in range(N_PAIRS):
                sl = slice(p * LANES, (p + 1) * LANES)
                for name, arr in full.items():
                    tiles[name].append(arr[rs, sl])
                tiles['rhs'].append(jnp.concatenate([vb[rs, sl], kbe[rs, sl]], axis=-1))
                lane0 = DECAY_LANE + 2 * p
                tiles['col'].append(jnp.concatenate(
                    [jnp.broadcast_to(gc[rs, lane0 + hh:lane0 + hh + 1], (ck, rdim)) for hh in range(2)], axis=0))

    st = lambda name: jnp.stack(tiles[name], axis=0)
    stack2 = lambda x: jnp.concatenate([jnp.where(first_half, x, 0.0), jnp.where(first_half, 0.0, x)], axis=1)
    k_st = stack2(st('k'))
    col = st('col')
    diff = col - jnp.swapaxes(col, 1, 2)
    dec = jnp.exp(jnp.where(incl, diff, -1e30))
    m = _bdot_nt(stack2(st('kb')), k_st) * jnp.where(strict, dec, 0.0)
    t_inv = _unit_lower_inverse(m, ck)
    rhs = st('rhs')
    sol = _bdot(t_inv, jnp.concatenate([rhs, rhs], axis=1))
    u = jnp.where(first_half, sol[:, :ck, :LANES], sol[:, ck:, :LANES])
    w = jnp.where(first_half, sol[:, :ck, LANES:], sol[:, ck:, LANES:])
    attn = _bdot_nt(stack2(st('q')), k_st) * dec
    qd = st('qd')
    kd_t = jnp.swapaxes(st('kd'), 1, 2)
    cd = st('cd')

    gsel = lambda x, i: jnp.stack([x[(b * nck + i) * N_PAIRS + p] for b in range(bb) for p in range(N_PAIRS)], axis=0)
    state = st_ref[...].reshape(bb * N_PAIRS, LANES, LANES)
    o_chunks = []
    for i in range(nck):
        v_new = gsel(u, i) - _bdot(gsel(w, i), state)
        intra = _bdot(gsel(attn, i), jnp.concatenate([v_new, v_new], axis=1))
        o_chunks.append(_bdot(gsel(qd, i), state) + jnp.where(first_half, intra[:, :ck], intra[:, ck:]))
        upd = _bdot(gsel(kd_t, i), v_new)
        state = state * gsel(cd, i)[:, 0:1, :] + jnp.where(same_head, upd, 0.0)
    st_ref[...] = state.reshape(bb, N_PAIRS, LANES, LANES)

    for b in range(bb):
        o = jnp.concatenate(
            [jnp.concatenate([o_chunks[i][b * N_PAIRS + p] for p in range(N_PAIRS)], axis=-1) for i in range(nck)],
            axis=0)
        ms = head_sumsq(o) * (1.0 / HEAD_DIM)
        o = o * lax.rsqrt(ms + EPS) * ng_ref[...] * _silu(gates[b])
        y_ref[b] = o.astype(y_ref.dtype)

    @pl.when(c == pl.num_programs(1) - 1)
    def _():
        for b in range(bb):
            for h in range(GDN_HEADS):
                lo = (h % 2) * HEAD_DIM
                sout_ref[b, h] = st_ref[b, h // 2][lo:lo + HEAD_DIM, lo:lo + HEAD_DIM]


def _gdn(qg, small, s0, a_log, dt_bias, norm_g, tt, ck, bb):
    bsz, t, _ = qg.shape
    bias = _lane_row(dt_bias, DECAY_LANE)
    arow = _lane_row(-jnp.exp(a_log.astype(F32)), DECAY_LANE)
    ng = jnp.tile(norm_g.astype(F32), GDN_HEADS)[None, :]
    full = lambda v: pl.BlockSpec(v.shape, lambda i, j: (0,) * v.ndim)
    blk = lambda c: pl.BlockSpec((bb, tt, c), lambda i, j: (i, j, 0))
    st = pl.BlockSpec((bb, GDN_HEADS, HEAD_DIM, HEAD_DIM), lambda i, j: (i, 0, 0, 0))
    consts = (bias, arow, ng, _expand_matrix(BETA_LANE), _expand_matrix(DECAY_LANE))
    return pl.pallas_call(
        functools.partial(_gdn_kernel, tt=tt, ck=ck, bb=bb),
        grid=(bsz // bb, t // tt),
        in_specs=[blk(qg.shape[-1]), blk(LANES), st] + [full(v) for v in consts],
        out_specs=[blk(GDN_INNER), st],
        out_shape=[jax.ShapeDtypeStruct((bsz, t, GDN_INNER), BF16),
                   jax.ShapeDtypeStruct((bsz, GDN_HEADS, HEAD_DIM, HEAD_DIM), F32)],
        scratch_shapes=[pltpu.VMEM((bb, N_PAIRS, LANES, LANES), F32)],
        compiler_params=_cparams("arbitrary", "arbitrary"),
        name="gdn",
    )(qg, small, s0, *consts)


def _mix_residual(x_ref, ya_ref, yb_ref, yc_ref, woa_ref, wob_ref, woc_ref):
    mix = (jnp.dot(ya_ref[...], woa_ref[...], preferred_element_type=F32)
           + jnp.dot(yb_ref[...], wob_ref[...], preferred_element_type=F32)
           + jnp.dot(yc_ref[...], woc_ref[...], preferred_element_type=F32))
    return x_ref[...] + mix


def _rms(x, g):
    return x * lax.rsqrt(jnp.mean(x * x, axis=-1, keepdims=True) + EPS) * g


def _ffn_kernel(x_ref, ya_ref, yb_ref, yc_ref, woa_ref, wob_ref, woc_ref, g_ref, wg_ref, wu_ref, wd_ref,
                o_ref, xn_ref, h_ref, acc_ref):
    j = pl.program_id(1)

    @pl.when(j == 0)
    def _():
        xn = _mix_residual(x_ref, ya_ref, yb_ref, yc_ref, woa_ref, wob_ref, woc_ref)
        xn_ref[...] = xn
        h_ref[...] = _rms(xn, g_ref[...]).astype(BF16)
        acc_ref[...] = jnp.zeros_like(acc_ref)

    h = h_ref[...]
    act = _silu(jnp.dot(h, wg_ref[...], preferred_element_type=F32)) * jnp.dot(h, wu_ref[...], preferred_element_type=F32)
    acc_ref[...] += jnp.dot(act.astype(BF16), wd_ref[...], preferred_element_type=F32)

    @pl.when(j == pl.num_programs(1) - 1)
    def _():
        o_ref[...] = xn_ref[...] + acc_ref[...]


FFN_CHUNKS = 2


def _out_ffn(x2d, ya, yb, yc, wo, g, wg, wu, wd):
    n = x2d.shape[0]
    tm = _row_tile(n, 512)
    f = wg.shape[1]
    tf = f // FFN_CHUNKS
    rows = lambda c: pl.BlockSpec((tm, c), lambda i, j: (i, 0))
    full = lambda w: pl.BlockSpec(w.shape, lambda i, j: (0, 0))
    return pl.pallas_call(
        _ffn_kernel,
        grid=(n // tm, FFN_CHUNKS),
        in_specs=[rows(D_MODEL), rows(ya.shape[1]), rows(yb.shape[1]), rows(yc.shape[1]),
                  full(wo[0]), full(wo[1]), full(wo[2]), full(g),
                  pl.BlockSpec((D_MODEL, tf), lambda i, j: (0, j)),
                  pl.BlockSpec((D_MODEL, tf), lambda i, j: (0, j)),
                  pl.BlockSpec((tf, D_MODEL), lambda i, j: (j, 0))],
        out_specs=rows(D_MODEL),
        out_shape=jax.ShapeDtypeStruct((n, D_MODEL), F32),
        scratch_shapes=[pltpu.VMEM((tm, D_MODEL), F32), pltpu.VMEM((tm, D_MODEL), BF16),
                        pltpu.VMEM((tm, D_MODEL), F32)],
        compiler_params=_cparams("parallel", "arbitrary"),
        name="out_ffn",
    )(x2d, ya, yb, yc, wo[0], wo[1], wo[2], g, wg, wu, wd)


ROUTE_TILE = 512


def _router_kernel(x_ref, ya_ref, yb_ref, yc_ref, woa_ref, wob_ref, woc_ref, g_ref, wr_ref, br_ref,
                   xn_ref, h_ref, gate_ref, slot_ref, cnt_ref, run_ref, *, cap):
    i = pl.program_id(0)

    @pl.when(i == 0)
    def _():
        run_ref[...] = jnp.zeros_like(run_ref)

    xn = _mix_residual(x_ref, ya_ref, yb_ref, yc_ref, woa_ref, wob_ref, woc_ref)
    xn_ref[...] = xn
    h = _rms(xn, g_ref[...])
    h_ref[...] = h
    tm = h.shape[0]
    lane = _iota((1, LANES), 1)
    logits = jnp.where(lane < N_EXPERTS, _dot_hp(h, wr_ref[...]) + br_ref[...], -jnp.inf)
    m1 = jnp.max(logits, axis=-1, keepdims=True)
    i1 = jnp.min(jnp.where(logits == m1, lane, LANES), axis=-1, keepdims=True)
    rest = jnp.where(lane == i1, -jnp.inf, logits)
    m2 = jnp.max(rest, axis=-1, keepdims=True)
    i2 = jnp.min(jnp.where(rest == m2, lane, LANES), axis=-1, keepdims=True)
    e2 = jnp.exp(m2 - m1)
    inv = 1.0 / (1.0 + e2)
    gate_ref[...] = jnp.where(lane == 0, inv, 0.0) + jnp.where(lane == 1, e2 * inv, 0.0)

    chosen = jnp.where((lane == i1) | (lane == i2), 1.0, 0.0)
    before = _iota((tm, tm), 0) > _iota((tm, tm), 1)
    rank = jnp.dot(before.astype(BF16), chosen.astype(BF16), preferred_element_type=F32) + run_ref[0:1, :]
    base = lane.astype(F32) * float(cap)
    pick = lambda idx: jnp.sum(jnp.where(lane == idx, rank + base, 0.0), axis=-1, keepdims=True)
    slots = jnp.where(lane == 0, pick(i1), 0.0) + jnp.where(lane == 1, pick(i2), 0.0)
    slots_t = jnp.transpose(slots).astype(jnp.int32)
    for k in range(2):
        slot_ref[0, k] = jnp.concatenate(
            [slots_t[k:k + 1, c * LANES:(c + 1) * LANES] for c in range(tm // LANES)], axis=0)
    run_ref[...] = run_ref[...] + jnp.sum(chosen, axis=0, keepdims=True)
    cnt_ref[...] = run_ref[...].astype(jnp.int32)


def _out_router(x2d, ya, yb, yc, wo, g, w_router, b_router, cap):
    n = x2d.shape[0]
    tm = _row_tile(n, ROUTE_TILE)
    wr = jnp.zeros((D_MODEL, LANES), F32).at[:, :N_EXPERTS].set(w_router.astype(F32))
    br = _lane_row(b_router, 0)
    rows = lambda c: pl.BlockSpec((tm, c), lambda i: (i, 0))
    full = lambda w: pl.BlockSpec(w.shape, lambda i: (0, 0))
    return pl.pallas_call(
        functools.partial(_router_kernel, cap=cap),
        grid=(n // tm,),
        in_specs=[rows(D_MODEL), rows(ya.shape[1]), rows(yb.shape[1]), rows(yc.shape[1]),
                  full(wo[0]), full(wo[1]), full(wo[2]), full(g), full(wr), full(br)],
        out_specs=[rows(D_MODEL), rows(D_MODEL), rows(LANES),
                   pl.BlockSpec((1, 2, tm // LANES, LANES), lambda i: (i, 0, 0, 0)),
                   pl.BlockSpec((SUBLANES, LANES), lambda i: (0, 0))],
        out_shape=[jax.ShapeDtypeStruct((n, D_MODEL), F32), jax.ShapeDtypeStruct((n, D_MODEL), F32),
                   jax.ShapeDtypeStruct((n, LANES), F32),
                   jax.ShapeDtypeStruct((n // tm, 2, tm // LANES, LANES), jnp.int32),
                   jax.ShapeDtypeStruct((SUBLANES, LANES), jnp.int32)],
        scratch_shapes=[pltpu.VMEM((SUBLANES, LANES), F32)],
        compiler_params=_cparams("arbitrary"),
        name="out_router",
    )(x2d, ya, yb, yc, wo[0], wo[1], wo[2], g, wr, br)


def _row_copy(src_ref, src_row, dst_ref, dst_row, sem):
    return pltpu.make_async_copy(src_ref.at[pl.ds(src_row, 1)], dst_ref.at[pl.ds(dst_row, 1)], sem)


def _slot_fetch(slots_hbm, tile, smem_ref, buf, sem):
    return pltpu.make_async_copy(slots_hbm.at[tile], smem_ref.at[buf], sem.at[buf])


def _for_rows(tm, fn):
    for c in range(tm // LANES):
        for base in range(0, LANES, SUBLANES):
            for j in range(SUBLANES):
                fn(c, base, j)


def _vmem_row(ref, base, j):
    return ref.at[pl.ds(base, SUBLANES)].at[pl.ds(j, 1)]


def _dispatch_kernel(cnt_ref, slots_hbm, h_ref, hs_hbm, slot_smem, zero_ref, slot_sem, row_sem, pad_sem,
                     *, tm, cap, bm):
    i = pl.program_id(0)
    n_tiles = pl.num_programs(0)

    @pl.when(i == 0)
    def _():
        _slot_fetch(slots_hbm, 0, slot_smem, 0, slot_sem).start()

    @pl.when(i + 1 < n_tiles)
    def _():
        _slot_fetch(slots_hbm, i + 1, slot_smem, (i + 1) % 2, slot_sem).start()

    _slot_fetch(slots_hbm, i, slot_smem, i % 2, slot_sem).wait()

    def issue(c, base, j):
        for k in range(2):
            dst = hs_hbm.at[pl.ds(slot_smem[i % 2, k, c, base + j], 1)]
            pltpu.make_async_copy(_vmem_row(h_ref, c * LANES + base, j), dst, row_sem).start(priority=k)

    _for_rows(tm, issue)

    def drain(r, carry):
        for k in range(2):
            _row_copy(h_ref, 0, hs_hbm, 0, row_sem).wait()
        return carry

    lax.fori_loop(0, tm, drain, 0, unroll=8)

    @pl.when(i == n_tiles - 1)
    def _():
        zero_ref[...] = jnp.zeros_like(zero_ref)
        for e in range(N_EXPERTS):
            c = cnt_ref[e]
            n_pad = ((c + bm - 1) // bm) * bm - c

            def fill(r, carry):
                _row_copy(zero_ref, 0, hs_hbm, e * cap + c + r, pad_sem).start()
                return carry

            def fill_wait(r, carry):
                _row_copy(zero_ref, 0, hs_hbm, 0, pad_sem).wait()
                return carry

            lax.fori_loop(0, n_pad, fill, 0)
            lax.fori_loop(0, n_pad, fill_wait, 0)


def _dispatch(counts, slots, h, cap, bm):
    n = h.shape[0]
    n_tiles = slots.shape[0]
    tm = slots.shape[2] * LANES
    return pl.pallas_call(
        functools.partial(_dispatch_kernel, tm=tm, cap=cap, bm=bm),
        grid_spec=pltpu.PrefetchScalarGridSpec(
            num_scalar_prefetch=1,
            grid=(n_tiles,),
            in_specs=[pl.BlockSpec(memory_space=pl.ANY), pl.BlockSpec((tm, D_MODEL), lambda i, cnt: (i, 0))],
            out_specs=pl.BlockSpec(memory_space=pl.ANY),
            scratch_shapes=[pltpu.SMEM((2, 2, tm // LANES, LANES), jnp.int32), pltpu.VMEM((SUBLANES, D_MODEL), F32),
                            pltpu.SemaphoreType.DMA((2,)), pltpu.SemaphoreType.DMA, pltpu.SemaphoreType.DMA],
        ),
        out_shape=jax.ShapeDtypeStruct((N_EXPERTS * cap, D_MODEL), F32),
        compiler_params=_cparams("arbitrary"),
        name="moe_dispatch",
    )(counts, slots, h)


def _experts_kernel(tbl_ref, hs_ref, wg_ref, wu_ref, wd_ref, ys_ref, hb_ref, acc_ref):
    s = pl.program_id(0)
    j = pl.program_id(1)

    @pl.when(tbl_ref[2, s] == 1)
    def _():
        @pl.when(j == 0)
        def _():
            hb_ref[...] = hs_ref[...].astype(BF16)
            acc_ref[...] = jnp.zeros_like(acc_ref)

        h = hb_ref[...]
        act = (_silu(jnp.dot(h, wg_ref[0], preferred_element_type=F32))
               * jnp.dot(h, wu_ref[0], preferred_element_type=F32))
        acc_ref[...] += jnp.dot(act.astype(BF16), wd_ref[0], preferred_element_type=F32)

        @pl.when(j == pl.num_programs(1) - 1)
        def _():
            ys_ref[...] = acc_ref[...]


def _block_table(counts, cap, bm, n_steps):
    nblk = (counts + bm - 1) // bm
    cum = jnp.cumsum(nblk)
    total = cum[-1]
    step = jnp.arange(n_steps, dtype=jnp.int32)
    last = jnp.maximum(total - 1, 0)
    eff = jnp.minimum(step, last)
    expert = jnp.minimum(jnp.sum(eff[:, None] >= cum[None, :], axis=1), N_EXPERTS - 1).astype(jnp.int32)
    blk = eff - (cum[expert] - nblk[expert])
    return jnp.stack([expert * (cap // bm) + blk, expert, (step < total).astype(jnp.int32)]).astype(jnp.int32)


def _experts(counts, hs, wg, wu, wd, cap, bm, n_assign):
    n_steps = n_assign // bm + N_EXPERTS
    tbl = _block_table(counts, cap, bm, n_steps)
    f = wg.shape[2]
    tf = f // FFN_CHUNKS
    chunk = lambda s, j, t: jnp.where(t[2, s] == 1, j, FFN_CHUNKS - 1)
    rows = pl.BlockSpec((bm, D_MODEL), lambda s, j, t: (t[0, s], 0))
    return pl.pallas_call(
        _experts_kernel,
        grid_spec=pltpu.PrefetchScalarGridSpec(
            num_scalar_prefetch=1,
            grid=(n_steps, FFN_CHUNKS),
            in_specs=[rows,
                      pl.BlockSpec((1, D_MODEL, tf), lambda s, j, t: (t[1, s], 0, chunk(s, j, t))),
                      pl.BlockSpec((1, D_MODEL, tf), lambda s, j, t: (t[1, s], 0, chunk(s, j, t))),
                      pl.BlockSpec((1, tf, D_MODEL), lambda s, j, t: (t[1, s], chunk(s, j, t), 0))],
            out_specs=rows,
            scratch_shapes=[pltpu.VMEM((bm, D_MODEL), BF16), pltpu.VMEM((bm, D_MODEL), F32)],
        ),
        out_shape=jax.ShapeDtypeStruct(hs.shape, F32),
        compiler_params=_cparams("arbitrary", "arbitrary"),
        name="moe_experts",
    )(tbl, hs, wg, wu, wd)


def _combine_kernel(slots_hbm, ys_hbm, xn_ref, gate_ref, gf_ref, o_ref, slot_smem, ybuf_ref, slot_sem, row_sem,
                    *, tm):
    i = pl.program_id(0)
    n_tiles = pl.num_programs(0)

    def gather(tile, buf):
        def issue(c, base, j):
            for k in range(2):
                src = ys_hbm.at[pl.ds(slot_smem[buf, k, c, base + j], 1)]
                dst = _vmem_row(ybuf_ref.at[buf, k], c * LANES + base, j)
                pltpu.make_async_copy(src, dst, row_sem.at[buf]).start()
        _for_rows(tm, issue)

    @pl.when(i == 0)
    def _():
        first = _slot_fetch(slots_hbm, 0, slot_smem, 0, slot_sem)
        first.start()
        first.wait()
        gather(0, 0)

        @pl.when(n_tiles > 1)
        def _():
            _slot_fetch(slots_hbm, 1, slot_smem, 1, slot_sem).start()

    @pl.when(i + 1 < n_tiles)
    def _():
        _slot_fetch(slots_hbm, i + 1, slot_smem, (i + 1) % 2, slot_sem).wait()
        gather(i + 1, (i + 1) % 2)

    def drain(r, carry):
        for k in range(2):
            _row_copy(ys_hbm, 0, ybuf_ref.at[i % 2, k], 0, row_sem.at[i % 2]).wait()
        return carry

    lax.fori_loop(0, tm, drain, 0, unroll=8)

    @pl.when(i + 2 < n_tiles)
    def _():
        _slot_fetch(slots_hbm, i + 2, slot_smem, i % 2, slot_sem).start()

    g = gate_ref[...]
    y = g[:, 0:1] * ybuf_ref[i % 2, 0] + g[:, 1:2] * ybuf_ref[i % 2, 1]
    o_ref[...] = _rms(xn_ref[...] + y, gf_ref[...])


def _combine(slots, ys, xn, gates, g_final):
    n = xn.shape[0]
    n_tiles = slots.shape[0]
    tm = slots.shape[2] * LANES
    rows = lambda c: pl.BlockSpec((tm, c), lambda i: (i, 0))
    return pl.pallas_call(
        functools.partial(_combine_kernel, tm=tm),
        grid=(n_tiles,),
        in_specs=[pl.BlockSpec(memory_space=pl.ANY), pl.BlockSpec(memory_space=pl.ANY),
                  rows(D_MODEL), rows(LANES), pl.BlockSpec(g_final.shape, lambda i: (0, 0))],
        out_specs=rows(D_MODEL),
        out_shape=jax.ShapeDtypeStruct((n, D_MODEL), F32),
        scratch_shapes=[pltpu.SMEM((2, 2, tm // LANES, LANES), jnp.int32), pltpu.VMEM((2, 2, tm, D_MODEL), F32),
                        pltpu.SemaphoreType.DMA((2,)), pltpu.SemaphoreType.DMA((2,))],
        compiler_params=_cparams("arbitrary"),
        name="moe_combine",
    )(slots, ys, xn, gates, g_final)


def _moe(x2d, ya, yb, yc, wo, g_ffn, w_router, b_router, wg, wu, wd, g_final):
    n = x2d.shape[0]
    bm = _row_tile(n, 512)
    cap = n
    xn, h, gates, slots, counts = _out_router(x2d, ya, yb, yc, wo, g_ffn, w_router, b_router, cap)
    counts = counts[0, :N_EXPERTS]
    hs = _dispatch(counts, slots, h, cap, bm)
    ys = _experts(counts, hs, wg, wu, wd, cap, bm, 2 * n)
    return _combine(slots, ys, xn, gates, g_final)


IN_SIZES = (2 * CONV_CH, SSM_INNER, SSM_XBC, SSM_HEADS, GDN_QKV, GDN_INNER, GDN_HEADS, GDN_HEADS)


def _prep_layer(l, p):
    off = np.concatenate([[0], np.cumsum(IN_SIZES)])
    w_in = p['w_in_bf16'][l]
    col = lambda i: w_in[:, off[i]:off[i + 1]]
    a_in, z, xbc, dt, qkv, gate, b_raw, a_raw = (col(i) for i in range(8))
    small = jnp.concatenate([dt, b_raw, a_raw, jnp.zeros((D_MODEL, SMALL_W - 3 * SSM_HEADS), BF16)], axis=1)
    wo = p['w_out'][l].astype(BF16)
    return dict(
        wa=a_in,
        wzx=jnp.concatenate([z, xbc], axis=1),
        wqg=jnp.concatenate([qkv, gate], axis=1),
        ws=small,
        wo=(wo[:CONV_CH], wo[CONV_CH:CONV_CH + SSM_INNER], wo[CONV_CH + SSM_INNER:]),
    )


def _trunk(x, st_conv_a, st_ssm_conv, st_ssm, st_gdn_conv, st_gdn, p, prep, ssd_rows, gdn_rows, bb):
    bsz, t, _ = x.shape
    n = bsz * t
    depth = p['g_mix'].shape[0]
    x2d = x.reshape(n, D_MODEL)
    new = [[] for _ in range(5)]
    for l in range(depth):
        w = prep[l]
        a_in, zx, qg, small, hist_x, hist_q = _norm_proj(
            x2d.reshape(bsz, t, D_MODEL), p['g_mix'][l][None, :], w['wa'], w['wzx'], w['wqg'], w['ws'],
            st_ssm_conv[l], st_gdn_conv[l], p['ssm_conv_w'][l], p['ssm_conv_b'][l][None, :], p['gdn_conv_w'][l])
        a_in = a_in.reshape(bsz, t, -1)
        zx = zx.reshape(bsz, t, -1)
        qg = qg.reshape(bsz, t, -1)
        small = small.reshape(bsz, t, -1)
        ya, conv_a = _conv_a(a_in, st_conv_a[l], p['conv_a_w'][l], p['conv_a_b'][l][None, :],
                             p['ln_a_g'][l][None, :], p['ln_a_b'][l][None, :])
        yb, ssm = _ssd(zx, small, st_ssm[l].astype(F32), p['ssm_dt_bias'][l], p['ssm_a_log'][l],
                       p['ssm_d'][l], p['ssm_norm_g'][l], ssd_rows, bb)
        yc, gdn = _gdn(qg, small, st_gdn[l].astype(F32), p['gdn_a_log'][l], p['gdn_dt_bias'][l],
                       p['gdn_norm_g'][l], gdn_rows[0], gdn_rows[1], bb)
        new[0].append(conv_a)
        new[1].append(hist_x[:, HIST_PAD - (SSM_CONV_W - 1):])
        new[2].append(ssm)
        new[3].append(hist_q[:, HIST_PAD - (GDN_CONV_W - 1):])
        new[4].append(gdn)
        flat = lambda y: y.reshape(n, -1)
        g_ffn = p['g_ffn'][l][None, :]
        if l % 2 == 0:
            x2d = _out_ffn(x2d, flat(ya), flat(yb), flat(yc), w['wo'], g_ffn,
                           prep['ffn'][l // 2][0], prep['ffn'][l // 2][1], prep['ffn'][l // 2][2])
        else:
            x2d = _moe(x2d, flat(ya), flat(yb), flat(yc), w['wo'], g_ffn, p['moe_w_router'][l // 2],
                       p['moe_b_router'][l // 2], *prep['moe'][l // 2], p['g_final'][None, :])
    return (x2d.reshape(bsz, t, D_MODEL),) + tuple(jnp.stack(s) for s in new)


def kernel(x_prompt, x_sample, state_conv_a, state_ssm_conv, state_ssm, state_gdn_conv, state_gdn, g_mix, w_in, conv_a_w, conv_a_b, ln_a_g, ln_a_b, ssm_conv_w, ssm_conv_b, ssm_dt_bias, ssm_a_log, ssm_d, ssm_norm_g, gdn_conv_w, gdn_a_log, gdn_dt_bias, gdn_norm_g, w_out, g_ffn, ffn_w_gate, ffn_w_up, ffn_w_down, moe_w_router, moe_b_router, moe_w_gate, moe_w_up, moe_w_down, g_final):
    p = dict(g_mix=g_mix, w_in=w_in, conv_a_w=conv_a_w, conv_a_b=conv_a_b, ln_a_g=ln_a_g, ln_a_b=ln_a_b,
             ssm_conv_w=ssm_conv_w, ssm_conv_b=ssm_conv_b, ssm_dt_bias=ssm_dt_bias, ssm_a_log=ssm_a_log,
             ssm_d=ssm_d, ssm_norm_g=ssm_norm_g, gdn_conv_w=gdn_conv_w, gdn_a_log=gdn_a_log,
             gdn_dt_bias=gdn_dt_bias, gdn_norm_g=gdn_norm_g, w_out=w_out, g_ffn=g_ffn,
             moe_w_router=moe_w_router, moe_b_router=moe_b_router, g_final=g_final)
    depth = g_mix.shape[0]
    assert depth % 2 == 0, "the final RMSNorm is fused into the expert layer, which must come last"
    p['w_in_bf16'] = w_in.astype(BF16)
    prep = {l: _prep_layer(l, p) for l in range(depth)}
    prep['ffn'] = [(ffn_w_gate[i].astype(BF16), ffn_w_up[i].astype(BF16), ffn_w_down[i].astype(BF16))
                   for i in range(ffn_w_gate.shape[0])]
    prep['moe'] = [(moe_w_gate[i].astype(BF16), moe_w_up[i].astype(BF16), moe_w_down[i].astype(BF16))
                   for i in range(moe_w_gate.shape[0])]
    bp, dt = x_prompt.shape[0], x_prompt.dtype
    zeros = lambda *s: jnp.zeros((depth, bp) + s, dt)
    outs_p = _trunk(x_prompt, zeros(CONV_W - 1, CONV_CH), zeros(SSM_CONV_W - 1, SSM_XBC),
                    zeros(SSM_HEADS, HEAD_DIM, SSM_STATE), zeros(GDN_CONV_W - 1, GDN_QKV),
                    zeros(GDN_HEADS, HEAD_DIM, HEAD_DIM), p, prep, ssd_rows=min(128, x_prompt.shape[1]),
                    gdn_rows=(min(256, x_prompt.shape[1]), min(64, x_prompt.shape[1])), bb=2)
    outs_s = _trunk(x_sample, state_conv_a, state_ssm_conv, state_ssm, state_gdn_conv, state_gdn, p, prep,
                    ssd_rows=x_sample.shape[1], gdn_rows=(x_sample.shape[1], x_sample.shape[1]), bb=2)
    return (outs_p[0], outs_s[0]) + outs_p[1:] + outs_s[1:]
```

```python
import functools
import math

import jax
import jax.numpy as jnp
import numpy as np
from jax import lax
from jax.experimental import pallas as pl
from jax.experimental.pallas import tpu as pltpu

F32 = jnp.float32
BF16 = jnp.bfloat16
EPS = 1e-6

LANES = 128
SUBLANES = 8
VMEM_BYTES_V7X = 64 * 1024 * 1024
VMEM_LIMIT = VMEM_BYTES_V7X * 3 // 4

D_MODEL = 1024
CONV_CH = 256
CONV_W = 31
SSM_HEADS = 6
HEAD_DIM = 64
SSM_INNER = SSM_HEADS * HEAD_DIM
SSM_STATE = 64
SSM_GROUPS = 2
SSM_XBC = SSM_INNER + 2 * SSM_GROUPS * SSM_STATE
SSM_CONV_W = 4
GDN_HEADS = 6
GDN_INNER = GDN_HEADS * HEAD_DIM
GDN_QKV = 3 * GDN_INNER
GDN_CONV_W = 4
N_PAIRS = 3
N_EXPERTS = 8
SMALL_W = LANES


def _cparams(*sem):
    return pltpu.CompilerParams(dimension_semantics=sem, vmem_limit_bytes=VMEM_LIMIT)


def _dot(a, b):
    return jnp.dot(a.astype(BF16), b.astype(BF16), preferred_element_type=F32)


def _dot_nt(a, b):
    return lax.dot_general(a.astype(BF16), b.astype(BF16), (((1,), (1,)), ((), ())),
                           preferred_element_type=F32)


def _split3(x):
    hi = x.astype(BF16)
    r1 = x - hi.astype(F32)
    mid = r1.astype(BF16)
    lo = (r1 - mid.astype(F32)).astype(BF16)
    return hi, mid, lo


def _dot_sel(x, sel_bf16, pieces=2):
    d = lambda p: jnp.dot(p, sel_bf16, preferred_element_type=F32)
    hi = x.astype(BF16)
    r1 = x - hi.astype(F32)
    mid = r1.astype(BF16)
    if pieces == 2:
        return d(hi) + d(mid)
    return d(hi) + d(mid) + d((r1 - mid.astype(F32)).astype(BF16))


def _sel_dot(sel_bf16, x):
    hi, mid, lo = _split3(x)
    d = lambda p: jnp.dot(sel_bf16, p, preferred_element_type=F32)
    return d(hi) + d(mid) + d(lo)


def _dot_hp(a, b):
    ah = a.astype(BF16)
    al = (a - ah.astype(F32)).astype(BF16)
    bh = b.astype(BF16)
    bl = (b - bh.astype(F32)).astype(BF16)
    d = lambda p, q: jnp.dot(p, q, preferred_element_type=F32)
    return d(ah, bh) + (d(ah, bl) + d(al, bh))


def _silu(x):
    hx = 0.5 * x
    return hx + hx * jnp.tanh(hx)


def _sigmoid(x):
    return 0.5 + 0.5 * jnp.tanh(0.5 * x)


def _softplus(x):
    return jnp.maximum(x, 0.0) + jnp.log(1.0 + jnp.exp(-jnp.abs(x)))


def _iota(shape, dim):
    return lax.broadcasted_iota(jnp.int32, shape, dim)


def _norm_proj_kernel(x_ref, g_ref, wa_ref, wzx_ref, wqg_ref, ws_ref, hx_ref, hq_ref, cwx_ref, cbx_ref, cwq_ref,
                      a_ref, zx_ref, qg_ref, s_ref, nhx_ref, nhq_ref, bx_ref, bq_ref, *, nseq, seg, tiles_per_seq):
    first = (pl.program_id(0) % tiles_per_seq) == 0

    @pl.when(first)
    def _():
        bx_ref[:, 0:HIST_PAD, :] = hx_ref[...]
        bq_ref[:, 0:HIST_PAD, :] = hq_ref[...]

    x = x_ref[...]
    u = x * lax.rsqrt(jnp.mean(x * x, axis=-1, keepdims=True) + EPS) * g_ref[...]
    ub = u.astype(BF16)
    a_ref[...] = jnp.dot(ub, wa_ref[...], preferred_element_type=F32)
    s_ref[...] = jnp.dot(ub, ws_ref[...], preferred_element_type=F32)
    zx = jnp.dot(ub, wzx_ref[...], preferred_element_type=F32)
    qg = jnp.dot(ub, wqg_ref[...], preferred_element_type=F32)
    zx_ref[:, :SSM_INNER] = zx[:, :SSM_INNER]
    qg_ref[:, GDN_QKV:] = qg[:, GDN_QKV:]
    for s in range(nseq):
        rows = slice(s * seg, (s + 1) * seg)
        xbc = zx[rows, SSM_INNER:]
        nhx_ref[s] = _last_rows(bx_ref, s, xbc, seg)
        zx_ref[rows, SSM_INNER:] = _silu(_short_conv(xbc, bx_ref, s, cwx_ref, SSM_CONV_W, seg) + cbx_ref[...])
        qkv = qg[rows, :GDN_QKV]
        nhq_ref[s] = _last_rows(bq_ref, s, qkv, seg)
        qg_ref[rows, :GDN_QKV] = _silu(_short_conv(qkv, bq_ref, s, cwq_ref, GDN_CONV_W, seg))


def _last_rows(buf_ref, s, x, rows):
    if rows >= HIST_PAD:
        return x[rows - HIST_PAD:, :]
    return jnp.concatenate([buf_ref[s, rows:HIST_PAD, :], x], axis=0)


def _row_tile(n, want):
    t = min(want, n)
    while n % t:
        t //= 2
    return t


def _norm_proj(x, g, wa, wzx, wqg, ws, hist_x, hist_q, cwx, cbx, cwq):
    bsz, t, _ = x.shape
    n = bsz * t
    tm = _row_tile(n, 512)
    nseq = max(1, tm // t)
    seg = tm // nseq
    tiles_per_seq = max(1, t // tm)
    assert nseq * seg == tm and (t % tm == 0 or tm % t == 0)
    pad = lambda h: jnp.pad(h, ((0, 0), (HIST_PAD - h.shape[1], 0), (0, 0)))
    full = lambda w: pl.BlockSpec(w.shape, lambda i: (0, 0))
    rows = lambda c: pl.BlockSpec((tm, c), lambda i: (i, 0))
    hist = lambda c: pl.BlockSpec((nseq, HIST_PAD, c), lambda i: (i // tiles_per_seq, 0, 0))
    widths = (wa.shape[1], wzx.shape[1], wqg.shape[1], ws.shape[1])
    return pl.pallas_call(
        functools.partial(_norm_proj_kernel, nseq=nseq, seg=seg, tiles_per_seq=tiles_per_seq),
        grid=(n // tm,),
        in_specs=[rows(D_MODEL), full(g), full(wa), full(wzx), full(wqg), full(ws),
                  hist(SSM_XBC), hist(GDN_QKV), full(cwx), full(cbx), full(cwq)],
        out_specs=[rows(c) for c in widths] + [hist(SSM_XBC), hist(GDN_QKV)],
        out_shape=[jax.ShapeDtypeStruct((n, c), F32) for c in widths]
                  + [jax.ShapeDtypeStruct((bsz, HIST_PAD, SSM_XBC), F32),
                     jax.ShapeDtypeStruct((bsz, HIST_PAD, GDN_QKV), F32)],
        scratch_shapes=[pltpu.VMEM((nseq, seg + HIST_PAD, SSM_XBC), F32),
                        pltpu.VMEM((nseq, seg + HIST_PAD, GDN_QKV), F32)],
        compiler_params=_cparams("arbitrary"),
        name="norm_proj",
    )(x.reshape(n, D_MODEL), g, wa, wzx, wqg, ws, pad(hist_x), pad(hist_q), cwx, cbx, cwq)


CONV_PAD = 32
CONV_ROWS = 32


def _conv_a_kernel(a_ref, hist_ref, w_ref, b_ref, lg_ref, lb_ref, y_ref, nh_ref, buf_ref, sh_ref, *, tt):
    t = pl.program_id(1)

    @pl.when(t == 0)
    def _():
        buf_ref[0:CONV_PAD, :] = hist_ref[0]

    a = a_ref[0]
    glu = a[:, :CONV_CH] * _sigmoid(a[:, CONV_CH:])
    buf_ref[CONV_PAD:CONV_PAD + tt, :] = glu
    full = buf_ref[...]
    n = tt + CONV_PAD
    sh_ref[0] = full
    for s in range(1, SUBLANES):
        sh_ref[s] = pltpu.roll(full, n - s, axis=0)
    off = CONV_PAD - (CONV_W - 1)
    for r0 in range(0, tt, CONV_ROWS):
        acc = jnp.zeros((CONV_ROWS, CONV_CH), F32)
        for k in range(CONV_W):
            s = (off + k) % SUBLANES
            base = r0 + off + k - s
            acc = acc + w_ref[k:k + 1, :] * sh_ref[s, base:base + CONV_ROWS, :]
        y = acc + b_ref[...]
        mu = jnp.mean(y, axis=-1, keepdims=True)
        yc = y - mu
        var = jnp.mean(yc * yc, axis=-1, keepdims=True)
        y = yc * lax.rsqrt(var + EPS) * lg_ref[...] + lb_ref[...]
        y_ref[0, r0:r0 + CONV_ROWS, :] = _silu(y).astype(y_ref.dtype)
    nh_ref[0] = buf_ref[tt:tt + CONV_PAD, :]
    buf_ref[0:CONV_PAD, :] = buf_ref[tt:tt + CONV_PAD, :]


def _conv_a(a_in, hist, w, b, lg, lb):
    bsz, t, _ = a_in.shape
    tt = _row_tile(t, 1024)
    hist_p = jnp.pad(hist, ((0, 0), (CONV_PAD - (CONV_W - 1), 0), (0, 0)))
    vec = lambda v: pl.BlockSpec(v.shape, lambda i, j: (0, 0))
    y, nh = pl.pallas_call(
        functools.partial(_conv_a_kernel, tt=tt),
        grid=(bsz, t // tt),
        in_specs=[pl.BlockSpec((1, tt, 2 * CONV_CH), lambda i, j: (i, j, 0)),
                  pl.BlockSpec((1, CONV_PAD, CONV_CH), lambda i, j: (i, 0, 0)),
                  vec(w), vec(b), vec(lg), vec(lb)],
        out_specs=[pl.BlockSpec((1, tt, CONV_CH), lambda i, j: (i, j, 0)),
                   pl.BlockSpec((1, CONV_PAD, CONV_CH), lambda i, j: (i, 0, 0))],
        out_shape=[jax.ShapeDtypeStruct((bsz, t, CONV_CH), BF16),
                   jax.ShapeDtypeStruct((bsz, CONV_PAD, CONV_CH), F32)],
        scratch_shapes=[pltpu.VMEM((tt + CONV_PAD, CONV_CH), F32),
                        pltpu.VMEM((SUBLANES, tt + CONV_PAD, CONV_CH), F32)],
        compiler_params=_cparams("arbitrary", "arbitrary"),
        name="conv_a",
    )(a_in, hist_p, w, b, lg, lb)
    return y, nh[:, CONV_PAD - (CONV_W - 1):, :]


HIST_PAD = 8


def _short_conv(x, buf_ref, b, w_ref, width, rows):
    buf_ref[b, HIST_PAD:HIST_PAD + rows, :] = x
    off = HIST_PAD - (width - 1)
    acc = w_ref[0:1, :] * buf_ref[b, off:off + rows, :]
    for k in range(1, width):
        acc = acc + w_ref[k:k + 1, :] * buf_ref[b, off + k:off + k + rows, :]
    buf_ref[b, 0:HIST_PAD, :] = buf_ref[b, rows:rows + HIST_PAD, :]
    return acc


def _lower_tri(rows):
    return _iota((rows, rows), 0) >= _iota((rows, rows), 1)


def _pair_mask():
    return _iota((1, LANES), 1) < HEAD_DIM


def _ssd_kernel(zx_ref, s_ref, h0_ref, dtb_ref, arow_ref, dx_ref, ng_ref,
                ex_ref, y_ref, hout_ref, h_ref, *, rows, bb):
    c = pl.program_id(1)

    hpg = SSM_HEADS // SSM_GROUPS

    @pl.when(c == 0)
    def _():
        for b in range(bb):
            for p in range(N_PAIRS):
                cols = []
                for hh in range(2):
                    ht = jnp.transpose(h0_ref[b, 2 * p + hh])
                    z = jnp.zeros_like(ht)
                    cols.append(jnp.concatenate([ht, z] if (2 * p + hh) // hpg == 0 else [z, ht], axis=0))
                h_ref[b, p] = jnp.concatenate(cols, axis=1)

    causal = _lower_tri(rows)
    lane = _iota((1, LANES), 1)
    first_half = _pair_mask()
    srow_g = _iota((LANES, LANES), 0) // SSM_STATE
    scol_h = _iota((LANES, LANES), 1) // HEAD_DIM
    ex = ex_ref[...]
    tril = causal.astype(BF16)
    assert SSM_INNER // SSM_GROUPS == LANES + HEAD_DIM and N_PAIRS == 3
    top_rows = _iota((LANES, LANES), 0) < HEAD_DIM
    ones_all = jnp.ones((LANES, LANES), BF16)
    ones_top = top_rows.astype(BF16)
    ones_bot = jnp.logical_not(top_rows).astype(BF16)

    for b in range(bb):
        zx = zx_ref[b]
        z = zx[:, :SSM_INNER]
        xbc = zx[:, SSM_INNER:]
        xs = xbc[:, :SSM_INNER]
        bm = xbc[:, SSM_INNER:SSM_INNER + LANES]
        cm = xbc[:, SSM_INNER + LANES:]

        dt = _softplus(s_ref[b] + dtb_ref[...])
        a = dt * arow_ref[...]
        acum = _sel_dot(tril, a)
        a_last = acum[rows - 1:rows, :]
        dt_x = _dot_sel(dt, ex)
        ea_x = _dot_sel(jnp.exp(acum), ex)
        te_x = _dot_sel(jnp.exp(a_last - acum), ex)
        cd_x = _dot_sel(jnp.broadcast_to(jnp.exp(a_last), (SUBLANES, LANES)), ex)[0:1, :]

        scores = []
        for g in range(SSM_GROUPS):
            cm_g = jnp.where(lane // SSM_STATE == g, cm, 0.0)
            scores.append(_dot_nt(cm_g, bm))
        bm_t = jnp.transpose(bm)

        ys = []
        for p in range(N_PAIRS):
            sl = slice(p * LANES, (p + 1) * LANES)
            x_p = xs[:, sl]
            xdt = x_p * dt_x[:, sl]
            yd = []
            for hh in range(2):
                h = 2 * p + hh
                col = jnp.broadcast_to(acum[:, h:h + 1], (rows, rows))
                dec = jnp.exp(jnp.where(causal, col - jnp.transpose(col), -1e30))
                yd.append(_dot(scores[h // (SSM_HEADS // SSM_GROUPS)] * dec, xdt))
            y_diag = jnp.where(first_half, yd[0], yd[1])
            h_p = h_ref[b, p]
            y_off = _dot(cm, h_p) * ea_x[:, sl]
            keep = srow_g == (2 * p + scol_h) // (SSM_HEADS // SSM_GROUPS)
            upd = _dot(bm_t, xdt * te_x[:, sl])
            h_ref[b, p] = h_p * cd_x[:, sl] + jnp.where(keep, upd, 0.0)
            ys.append(y_diag + y_off + dx_ref[:, sl] * x_p)
        y = jnp.concatenate(ys, axis=-1) * _silu(z)
        sq = [jnp.square(y[:, p * LANES:(p + 1) * LANES]) for p in range(N_PAIRS)]
        g0 = _dot_sel(sq[0], ones_all) + _dot_sel(sq[1], ones_top)
        g1 = _dot_sel(sq[1], ones_bot) + _dot_sel(sq[2], ones_all)
        ms = jnp.concatenate([g0, jnp.where(first_half, g0, g1), g1], axis=-1) * (1.0 / (SSM_INNER // SSM_GROUPS))
        y = y * lax.rsqrt(ms + EPS) * ng_ref[...]
        y_ref[b] = y.astype(y_ref.dtype)

    @pl.when(c == pl.num_programs(1) - 1)
    def _():
        for b in range(bb):
            for h in range(SSM_HEADS):
                g, hh = h // hpg, h % 2
                blk = h_ref[b, h // 2][g * SSM_STATE:(g + 1) * SSM_STATE, hh * HEAD_DIM:(hh + 1) * HEAD_DIM]
                hout_ref[b, h] = jnp.transpose(blk)


def _expand_matrix(first_lane):
    m = np.zeros((LANES, SSM_INNER), np.float32)
    for h in range(SSM_HEADS):
        m[first_lane + h, h * HEAD_DIM:(h + 1) * HEAD_DIM] = 1.0
    return jnp.asarray(m, BF16)


def _lane_row(vals, first_lane):
    return jnp.zeros((1, LANES), F32).at[0, first_lane:first_lane + vals.shape[0]].set(vals.astype(F32))


def _ssd(zx, small, h0, dt_bias, a_log, d_skip, norm_g, rows, bb):
    bsz, t, _ = zx.shape
    dtb = _lane_row(dt_bias, 0)
    arow = _lane_row(-jnp.exp(a_log.astype(F32)), 0)
    dx = jnp.repeat(d_skip.astype(F32), HEAD_DIM)[None, :]
    full = lambda v: pl.BlockSpec(v.shape, lambda i, j: (0,) * v.ndim)
    blk = lambda c: pl.BlockSpec((bb, rows, c), lambda i, j: (i, j, 0))
    st = pl.BlockSpec((bb, SSM_HEADS, HEAD_DIM, SSM_STATE), lambda i, j: (i, 0, 0, 0))
    consts = (dtb, arow, dx, norm_g[None, :], _expand_matrix(0))
    return pl.pallas_call(
        functools.partial(_ssd_kernel, rows=rows, bb=bb),
        grid=(bsz // bb, t // rows),
        in_specs=[blk(zx.shape[-1]), blk(LANES), st] + [full(v) for v in consts],
        out_specs=[blk(SSM_INNER), st],
        out_shape=[jax.ShapeDtypeStruct((bsz, t, SSM_INNER), BF16),
                   jax.ShapeDtypeStruct((bsz, SSM_HEADS, HEAD_DIM, SSM_STATE), F32)],
        scratch_shapes=[pltpu.VMEM((bb, N_PAIRS, LANES, LANES), F32)],
        compiler_params=_cparams("arbitrary", "arbitrary"),
        name="ssd",
    )(zx, small, h0, *consts)


BETA_LANE = 6
DECAY_LANE = 12


def _bdot(a, b):
    return lax.dot_general(a.astype(BF16), b.astype(BF16), (((2,), (1,)), ((0,), (0,))),
                           preferred_element_type=F32)


def _bdot_nt(a, b):
    return lax.dot_general(a.astype(BF16), b.astype(BF16), (((2,), (2,)), ((0,), (0,))),
                           preferred_element_type=F32)


def _unit_lower_inverse(m, block):
    rdim = m.shape[-1]
    eye = (_iota((rdim, rdim), 0) == _iota((rdim, rdim), 1)).astype(F32)
    x = -m
    t = eye + x
    p = x
    for _ in range(int(math.log2(block)) - 2):
        p = _bdot(p, p)
        t = t + _bdot(t, p)
    resid = (eye - t) - _bdot(m, t)
    return t + _bdot(t, resid)


def _gdn_kernel(qg_ref, s_ref, s0_ref, bias_ref, arow_ref, ng_ref, eb_ref, eg_ref,
                y_ref, sout_ref, st_ref, *, tt, ck, bb):
    c = pl.program_id(1)
    nck = tt // ck
    rdim = 2 * ck

    @pl.when(c == 0)
    def _():
        z = jnp.zeros((HEAD_DIM, HEAD_DIM), F32)
        for b in range(bb):
            for p in range(N_PAIRS):
                top = jnp.concatenate([s0_ref[b, 2 * p], z], axis=1)
                bot = jnp.concatenate([z, s0_ref[b, 2 * p + 1]], axis=1)
                st_ref[b, p] = jnp.concatenate([top, bot], axis=0)

    r_i = _iota((rdim, rdim), 0)
    c_i = _iota((rdim, rdim), 1)
    same_blk = (r_i // ck) == (c_i // ck)
    incl = same_blk & (r_i >= c_i)
    strict = same_blk & (r_i > c_i)
    first_half = _pair_mask()
    same_head = (_iota((LANES, LANES), 0) // HEAD_DIM) == (_iota((LANES, LANES), 1) // HEAD_DIM)
    tr = _iota((tt, tt), 0)
    tc = _iota((tt, tt), 1)
    blk_tril = (((tr // ck) == (tc // ck)) & (tr >= tc)).astype(BF16)
    eb = eb_ref[...]
    eg = eg_ref[...]
    hm = same_head.astype(BF16)
    head_sumsq = lambda x: jnp.concatenate(
        [_dot_sel(jnp.square(x[:, p * LANES:(p + 1) * LANES]), hm) for p in range(N_PAIRS)], axis=-1)

    tiles = {name: [] for name in ('q', 'k', 'kb', 'rhs', 'qd', 'kd', 'col', 'cd')}
    gates = []
    for b in range(bb):
        qg = qg_ref[b]
        gates.append(qg[:, GDN_QKV:])
        qkv = qg[:, :GDN_QKV]
        q = qkv[:, :GDN_INNER]
        k = qkv[:, GDN_INNER:2 * GDN_INNER]
        v = qkv[:, 2 * GDN_INNER:]
        q = q * lax.rsqrt(head_sumsq(q) + EPS) * (HEAD_DIM ** -0.5)
        k = k * lax.rsqrt(head_sumsq(k) + EPS)
        s = s_ref[b]
        beta = _sigmoid(s)
        g = _softplus(s + bias_ref[...]) * arow_ref[...]
        gc = _sel_dot(blk_tril, g)
        g_last = jnp.concatenate(
            [jnp.broadcast_to(gc[(i + 1) * ck - 1:(i + 1) * ck, :], (ck, LANES)) for i in range(nck)], axis=0)
        beta_x = _dot_sel(beta, eb)
        eg_x = _dot_sel(jnp.exp(gc), eg)
        kd_x = _dot_sel(jnp.exp(g_last - gc), eg)
        cd_x = _dot_sel(jnp.exp(g_last), eg)
        kb = k * beta_x
        full = dict(q=q, k=k, kb=kb, qd=q * eg_x, kd=k * kd_x, cd=cd_x)
        vb = v * beta_x
        kbe = kb * eg_x
        for i in range(nck):
            rs = slice(i * ck, (i + 1) * ck)
            for p in range(N_PAIRS):
                sl = slice(p * LANES, (p + 1) * LANES)
                for name, arr in full.items():
                    tiles[name].append(arr[rs, sl])
                tiles['rhs'].append(jnp.concatenate([vb[rs, sl], kbe[rs, sl]], axis=-1))
                lane0 = DECAY_LANE + 2 * p
                tiles['col'].append(jnp.concatenate(
                    [jnp.broadcast_to(gc[rs, lane0 + hh:lane0 + hh + 1], (ck, rdim)) for hh in range(2)], axis=0))

    st = lambda name: jnp.stack(tiles[name], axis=0)
    stack2 = lambda x: jnp.concatenate([jnp.where(first_half, x, 0.0), jnp.where(first_half, 0.0, x)], axis=1)
    k_st = stack2(st('k'))
    col = st('col')
    diff = col - jnp.swapaxes(col, 1, 2)
    dec = jnp.exp(jnp.where(incl, diff, -1e30))
    m = _bdot_nt(stack2(st('kb')), k_st) * jnp.where(strict, dec, 0.0)
    t_inv = _unit_lower_inverse(m, ck)
    rhs = st('rhs')
    sol = _bdot(t_inv, jnp.concatenate([rhs, rhs], axis=1))
    u = jnp.where(first_half, sol[:, :ck, :LANES], sol[:, ck:, :LANES])
    w = jnp.where(first_half, sol[:, :ck, LANES:], sol[:, ck:, LANES:])
    attn = _bdot_nt(stack2(st('q')), k_st) * dec
    qd = st('qd')
    kd_t = jnp.swapaxes(st('kd'), 1, 2)
    cd = st('cd')

    gsel = lambda x, i: jnp.stack([x[(b * nck + i) * N_PAIRS + p] for b in range(bb) for p in range(N_PAIRS)], axis=0)
    state = st_ref[...].reshape(bb * N_PAIRS, LANES, LANES)
    o_chunks = []
    for i in range(nck):
        v_new = gsel(u, i) - _bdot(gsel(w, i), state)
        intra = _bdot(gsel(attn, i), jnp.concatenate([v_new, v_new], axis=1))
        o_chunks.append(_bdot(gsel(qd, i), state) + jnp.where(first_half, intra[:, :ck], intra[:, ck:]))
        upd = _bdot(gsel(kd_t, i), v_new)
        state = state * gsel(cd, i)[:, 0:1, :] + jnp.where(same_head, upd, 0.0)
    st_ref[...] = state.reshape(bb, N_PAIRS, LANES, LANES)

    for b in range(bb):
        o = jnp.concatenate(
            [jnp.concatenate([o_chunks[i][b * N_PAIRS + p] for p in range(N_PAIRS)], axis=-1) for i in range(nck)],
            axis=0)
        ms = head_sumsq(o) * (1.0 / HEAD_DIM)
        o = o * lax.rsqrt(ms + EPS) * ng_ref[...] * _silu(gates[b])
        y_ref[b] = o.astype(y_ref.dtype)

    @pl.when(c == pl.num_programs(1) - 1)
    def _():
        for b in range(bb):
            for h in range(GDN_HEADS):
                lo = (h % 2) * HEAD_DIM
                sout_ref[b, h] = st_ref[b, h // 2][lo:lo + HEAD_DIM, lo:lo + HEAD_DIM]


def _gdn(qg, small, s0, a_log, dt_bias, norm_g, tt, ck, bb):
    bsz, t, _ = qg.shape
    bias = _lane_row(dt_bias, DECAY_LANE)
    arow = _lane_row(-jnp.exp(a_log.astype(F32)), DECAY_LANE)
    ng = jnp.tile(norm_g.astype(F32), GDN_HEADS)[None, :]
    full = lambda v: pl.BlockSpec(v.shape, lambda i, j: (0,) * v.ndim)
    blk = lambda c: pl.BlockSpec((bb, tt, c), lambda i, j: (i, j, 0))
    st = pl.BlockSpec((bb, GDN_HEADS, HEAD_DIM, HEAD_DIM), lambda i, j: (i, 0, 0, 0))
    consts = (bias, arow, ng, _expand_matrix(BETA_LANE), _expand_matrix(DECAY_LANE))
    return pl.pallas_call(
        functools.partial(_gdn_kernel, tt=tt, ck=ck, bb=bb),
        grid=(bsz // bb, t // tt),
        in_specs=[blk(qg.shape[-1]), blk(LANES), st] + [full(v) for v in consts],
        out_specs=[blk(GDN_INNER), st],
        out_shape=[jax.ShapeDtypeStruct((bsz, t, GDN_INNER), BF16),
                   jax.ShapeDtypeStruct((bsz, GDN_HEADS, HEAD_DIM, HEAD_DIM), F32)],
        scratch_shapes=[pltpu.VMEM((bb, N_PAIRS, LANES, LANES), F32)],
        compiler_params=_cparams("arbitrary", "arbitrary"),
        name="gdn",
    )(qg, small, s0, *consts)


def _mix_residual(x_ref, ya_ref, yb_ref, yc_ref, woa_ref, wob_ref, woc_ref):
    mix = (jnp.dot(ya_ref[...], woa_ref[...], preferred_element_type=F32)
           + jnp.dot(yb_ref[...], wob_ref[...], preferred_element_type=F32)
           + jnp.dot(yc_ref[...], woc_ref[...], preferred_element_type=F32))
    return x_ref[...] + mix


def _rms(x, g):
    return x * lax.rsqrt(jnp.mean(x * x, axis=-1, keepdims=True) + EPS) * g


def _ffn_kernel(x_ref, ya_ref, yb_ref, yc_ref, woa_ref, wob_ref, woc_ref, g_ref, wg_ref, wu_ref, wd_ref,
                o_ref, xn_ref, h_ref, acc_ref):
    j = pl.program_id(1)

    @pl.when(j == 0)
    def _():
        xn = _mix_residual(x_ref, ya_ref, yb_ref, yc_ref, woa_ref, wob_ref, woc_ref)
        xn_ref[...] = xn
        h_ref[...] = _rms(xn, g_ref[...]).astype(BF16)
        acc_ref[...] = jnp.zeros_like(acc_ref)

    h = h_ref[...]
    act = _silu(jnp.dot(h, wg_ref[...], preferred_element_type=F32)) * jnp.dot(h, wu_ref[...], preferred_element_type=F32)
    acc_ref[...] += jnp.dot(act.astype(BF16), wd_ref[...], preferred_element_type=F32)

    @pl.when(j == pl.num_programs(1) - 1)
    def _():
        o_ref[...] = xn_ref[...] + acc_ref[...]


FFN_CHUNKS = 2


def _out_ffn(x2d, ya, yb, yc, wo, g, wg, wu, wd):
    n = x2d.shape[0]
    tm = _row_tile(n, 512)
    f = wg.shape[1]
    tf = f // FFN_CHUNKS
    rows = lambda c: pl.BlockSpec((tm, c), lambda i, j: (i, 0))
    full = lambda w: pl.BlockSpec(w.shape, lambda i, j: (0, 0))
    return pl.pallas_call(
        _ffn_kernel,
        grid=(n // tm, FFN_CHUNKS),
        in_specs=[rows(D_MODEL), rows(ya.shape[1]), rows(yb.shape[1]), rows(yc.shape[1]),
                  full(wo[0]), full(wo[1]), full(wo[2]), full(g),
                  pl.BlockSpec((D_MODEL, tf), lambda i, j: (0, j)),
                  pl.BlockSpec((D_MODEL, tf), lambda i, j: (0, j)),
                  pl.BlockSpec((tf, D_MODEL), lambda i, j: (j, 0))],
        out_specs=rows(D_MODEL),
        out_shape=jax.ShapeDtypeStruct((n, D_MODEL), F32),
        scratch_shapes=[pltpu.VMEM((tm, D_MODEL), F32), pltpu.VMEM((tm, D_MODEL), BF16),
                        pltpu.VMEM((tm, D_MODEL), F32)],
        compiler_params=_cparams("parallel", "arbitrary"),
        name="out_ffn",
    )(x2d, ya, yb, yc, wo[0], wo[1], wo[2], g, wg, wu, wd)


ROUTE_TILE = 512


def _router_kernel(x_ref, ya_ref, yb_ref, yc_ref, woa_ref, wob_ref, woc_ref, g_ref, wr_ref, br_ref,
                   xn_ref, h_ref, gate_ref, slot_ref, cnt_ref, run_ref, *, cap):
    i = pl.program_id(0)

    @pl.when(i == 0)
    def _():
        run_ref[...] = jnp.zeros_like(run_ref)

    xn = _mix_residual(x_ref, ya_ref, yb_ref, yc_ref, woa_ref, wob_ref, woc_ref)
    xn_ref[...] = xn
    h = _rms(xn, g_ref[...])
    h_ref[...] = h
    tm = h.shape[0]
    lane = _iota((1, LANES), 1)
    logits = jnp.where(lane < N_EXPERTS, _dot_hp(h, wr_ref[...]) + br_ref[...], -jnp.inf)
    m1 = jnp.max(logits, axis=-1, keepdims=True)
    i1 = jnp.min(jnp.where(logits == m1, lane, LANES), axis=-1, keepdims=True)
    rest = jnp.where(lane == i1, -jnp.inf, logits)
    m2 = jnp.max(rest, axis=-1, keepdims=True)
    i2 = jnp.min(jnp.where(rest == m2, lane, LANES), axis=-1, keepdims=True)
    e2 = jnp.exp(m2 - m1)
    inv = 1.0 / (1.0 + e2)
    gate_ref[...] = jnp.where(lane == 0, inv, 0.0) + jnp.where(lane == 1, e2 * inv, 0.0)

    chosen = jnp.where((lane == i1) | (lane == i2), 1.0, 0.0)
    before = _iota((tm, tm), 0) > _iota((tm, tm), 1)
    rank = jnp.dot(before.astype(BF16), chosen.astype(BF16), preferred_element_type=F32) + run_ref[0:1, :]
    base = lane.astype(F32) * float(cap)
    pick = lambda idx: jnp.sum(jnp.where(lane == idx, rank + base, 0.0), axis=-1, keepdims=True)
    slots = jnp.where(lane == 0, pick(i1), 0.0) + jnp.where(lane == 1, pick(i2), 0.0)
    slots_t = jnp.transpose(slots).astype(jnp.int32)
    for k in range(2):
        slot_ref[0, k] = jnp.concatenate(
            [slots_t[k:k + 1, c * LANES:(c + 1) * LANES] for c in range(tm // LANES)], axis=0)
    run_ref[...] = run_ref[...] + jnp.sum(chosen, axis=0, keepdims=True)
    cnt_ref[...] = run_ref[...].astype(jnp.int32)


def _out_router(x2d, ya, yb, yc, wo, g, w_router, b_router, cap):
    n = x2d.shape[0]
    tm = _row_tile(n, ROUTE_TILE)
    wr = jnp.zeros((D_MODEL, LANES), F32).at[:, :N_EXPERTS].set(w_router.astype(F32))
    br = _lane_row(b_router, 0)
    rows = lambda c: pl.BlockSpec((tm, c), lambda i: (i, 0))
    full = lambda w: pl.BlockSpec(w.shape, lambda i: (0, 0))
    return pl.pallas_call(
        functools.partial(_router_kernel, cap=cap),
        grid=(n // tm,),
        in_specs=[rows(D_MODEL), rows(ya.shape[1]), rows(yb.shape[1]), rows(yc.shape[1]),
                  full(wo[0]), full(wo[1]), full(wo[2]), full(g), full(wr), full(br)],
        out_specs=[rows(D_MODEL), rows(D_MODEL), rows(LANES),
                   pl.BlockSpec((1, 2, tm // LANES, LANES), lambda i: (i, 0, 0, 0)),
                   pl.BlockSpec((SUBLANES, LANES), lambda i: (0, 0))],
        out_shape=[jax.ShapeDtypeStruct((n, D_MODEL), F32), jax.ShapeDtypeStruct((n, D_MODEL), F32),
                   jax.ShapeDtypeStruct((n, LANES), F32),
                   jax.ShapeDtypeStruct((n // tm, 2, tm // LANES, LANES), jnp.int32),
                   jax.ShapeDtypeStruct((SUBLANES, LANES), jnp.int32)],
        scratch_shapes=[pltpu.VMEM((SUBLANES, LANES), F32)],
        compiler_params=_cparams("arbitrary"),
        name="out_router",
    )(x2d, ya, yb, yc, wo[0], wo[1], wo[2], g, wr, br)


def _row_copy(src_ref, src_row, dst_ref, dst_row, sem):
    return pltpu.make_async_copy(src_ref.at[pl.ds(src_row, 1)], dst_ref.at[pl.ds(dst_row, 1)], sem)


def _slot_fetch(slots_hbm, tile, smem_ref, buf, sem):
    return pltpu.make_async_copy(slots_hbm.at[tile], smem_ref.at[buf], sem.at[buf])


def _for_rows(tm, fn):
    for c in range(tm // LANES):
        for base in range(0, LANES, SUBLANES):
            for j in range(SUBLANES):
                fn(c, base, j)


def _vmem_row(ref, base, j):
    return ref.at[pl.ds(base, SUBLANES)].at[pl.ds(j, 1)]


def _dispatch_kernel(cnt_ref, slots_hbm, h_ref, hs_hbm, slot_smem, zero_ref, slot_sem, row_sem, pad_sem,
                     *, tm, cap, bm):
    i = pl.program_id(0)
    n_tiles = pl.num_programs(0)

    @pl.when(i == 0)
    def _():
        _slot_fetch(slots_hbm, 0, slot_smem, 0, slot_sem).start()

    @pl.when(i + 1 < n_tiles)
    def _():
        _slot_fetch(slots_hbm, i + 1, slot_smem, (i + 1) % 2, slot_sem).start()

    _slot_fetch(slots_hbm, i, slot_smem, i % 2, slot_sem).wait()

    def issue(c, base, j):
        for k in range(2):
            dst = hs_hbm.at[pl.ds(slot_smem[i % 2, k, c, base + j], 1)]
            pltpu.make_async_copy(_vmem_row(h_ref, c * LANES + base, j), dst, row_sem).start(priority=k)

    _for_rows(tm, issue)

    def drain(r, carry):
        for k in range(2):
            _row_copy(h_ref, 0, hs_hbm, 0, row_sem).wait()
        return carry

    lax.fori_loop(0, tm, drain, 0, unroll=8)

    @pl.when(i == n_tiles - 1)
    def _():
        zero_ref[...] = jnp.zeros_like(zero_ref)
        for e in range(N_EXPERTS):
            c = cnt_ref[e]
            n_pad = ((c + bm - 1) // bm) * bm - c

            def fill(r, carry):
                _row_copy(zero_ref, 0, hs_hbm, e * cap + c + r, pad_sem).start()
                return carry

            def fill_wait(r, carry):
                _row_copy(zero_ref, 0, hs_hbm, 0, pad_sem).wait()
                return carry

            lax.fori_loop(0, n_pad, fill, 0)
            lax.fori_loop(0, n_pad, fill_wait, 0)


def _dispatch(counts, slots, h, cap, bm):
    n = h.shape[0]
    n_tiles = slots.shape[0]
    tm = slots.shape[2] * LANES
    return pl.pallas_call(
        functools.partial(_dispatch_kernel, tm=tm, cap=cap, bm=bm),
        grid_spec=pltpu.PrefetchScalarGridSpec(
            num_scalar_prefetch=1,
            grid=(n_tiles,),
            in_specs=[pl.BlockSpec(memory_space=pl.ANY), pl.BlockSpec((tm, D_MODEL), lambda i, cnt: (i, 0))],
            out_specs=pl.BlockSpec(memory_space=pl.ANY),
            scratch_shapes=[pltpu.SMEM((2, 2, tm // LANES, LANES), jnp.int32), pltpu.VMEM((SUBLANES, D_MODEL), F32),
                            pltpu.SemaphoreType.DMA((2,)), pltpu.SemaphoreType.DMA, pltpu.SemaphoreType.DMA],
        ),
        out_shape=jax.ShapeDtypeStruct((N_EXPERTS * cap, D_MODEL), F32),
        compiler_params=_cparams("arbitrary"),
        name="moe_dispatch",
    )(counts, slots, h)


def _experts_kernel(tbl_ref, hs_ref, wg_ref, wu_ref, wd_ref, ys_ref, hb_ref, acc_ref):
    s = pl.program_id(0)
    j = pl.program_id(1)

    @pl.when(tbl_ref[2, s] == 1)
    def _():
        @pl.when(j == 0)
        def _():
            hb_ref[...] = hs_ref[...].astype(BF16)
            acc_ref[...] = jnp.zeros_like(acc_ref)

        h = hb_ref[...]
        act = (_silu(jnp.dot(h, wg_ref[0], preferred_element_type=F32))
               * jnp.dot(h, wu_ref[0], preferred_element_type=F32))
        acc_ref[...] += jnp.dot(act.astype(BF16), wd_ref[0], preferred_element_type=F32)

        @pl.when(j == pl.num_programs(1) - 1)
        def _():
            ys_ref[...] = acc_ref[...]


def _block_table(counts, cap, bm, n_steps):
    nblk = (counts + bm - 1) // bm
    cum = jnp.cumsum(nblk)
    total = cum[-1]
    step = jnp.arange(n_steps, dtype=jnp.int32)
    last = jnp.maximum(total - 1, 0)
    eff = jnp.minimum(step, last)
    expert = jnp.minimum(jnp.sum(eff[:, None] >= cum[None, :], axis=1), N_EXPERTS - 1).astype(jnp.int32)
    blk = eff - (cum[expert] - nblk[expert])
    return jnp.stack([expert * (cap // bm) + blk, expert, (step < total).astype(jnp.int32)]).astype(jnp.int32)


def _experts(counts, hs, wg, wu, wd, cap, bm, n_assign):
    n_steps = n_assign // bm + N_EXPERTS
    tbl = _block_table(counts, cap, bm, n_steps)
    f = wg.shape[2]
    tf = f // FFN_CHUNKS
    chunk = lambda s, j, t: jnp.where(t[2, s] == 1, j, FFN_CHUNKS - 1)
    rows = pl.BlockSpec((bm, D_MODEL), lambda s, j, t: (t[0, s], 0))
    return pl.pallas_call(
        _experts_kernel,
        grid_spec=pltpu.PrefetchScalarGridSpec(
            num_scalar_prefetch=1,
            grid=(n_steps, FFN_CHUNKS),
            in_specs=[rows,
                      pl.BlockSpec((1, D_MODEL, tf), lambda s, j, t: (t[1, s], 0, chunk(s, j, t))),
                      pl.BlockSpec((1, D_MODEL, tf), lambda s, j, t: (t[1, s], 0, chunk(s, j, t))),
                      pl.BlockSpec((1, tf, D_MODEL), lambda s, j, t: (t[1, s], chunk(s, j, t), 0))],
            out_specs=rows,
            scratch_shapes=[pltpu.VMEM((bm, D_MODEL), BF16), pltpu.VMEM((bm, D_MODEL), F32)],
        ),
        out_shape=jax.ShapeDtypeStruct(hs.shape, F32),
        compiler_params=_cparams("arbitrary", "arbitrary"),
        name="moe_experts",
    )(tbl, hs, wg, wu, wd)


def _combine_kernel(slots_hbm, ys_hbm, xn_ref, gate_ref, gf_ref, o_ref, slot_smem, ybuf_ref, slot_sem, row_sem,
                    *, tm):
    i = pl.program_id(0)
    n_tiles = pl.num_programs(0)

    def gather(tile, buf):
        def issue(c, base, j):
            for k in range(2):
                src = ys_hbm.at[pl.ds(slot_smem[buf, k, c, base + j], 1)]
                dst = _vmem_row(ybuf_ref.at[buf, k], c * LANES + base, j)
                pltpu.make_async_copy(src, dst, row_sem.at[buf]).start()
        _for_rows(tm, issue)

    @pl.when(i == 0)
    def _():
        first = _slot_fetch(slots_hbm, 0, slot_smem, 0, slot_sem)
        first.start()
        first.wait()
        gather(0, 0)

        @pl.when(n_tiles > 1)
        def _():
            _slot_fetch(slots_hbm, 1, slot_smem, 1, slot_sem).start()

    @pl.when(i + 1 < n_tiles)
    def _():
        _slot_fetch(slots_hbm, i + 1, slot_smem, (i + 1) % 2, slot_sem).wait()
        gather(i + 1, (i + 1) % 2)

    def drain(r, carry):
        for k in range(2):
            _row_copy(ys_hbm, 0, ybuf_ref.at[i % 2, k], 0, row_sem.at[i % 2]).wait()
        return carry

    lax.fori_loop(0, tm, drain, 0, unroll=8)

    @pl.when(i + 2 < n_tiles)
    def _():
        _slot_fetch(slots_hbm, i + 2, slot_smem, i % 2, slot_sem).start()

    g = gate_ref[...]
    y = g[:, 0:1] * ybuf_ref[i % 2, 0] + g[:, 1:2] * ybuf_ref[i % 2, 1]
    o_ref[...] = _rms(xn_ref[...] + y, gf_ref[...])


def _combine(slots, ys, xn, gates, g_final):
    n = xn.shape[0]
    n_tiles = slots.shape[0]
    tm = slots.shape[2] * LANES
    rows = lambda c: pl.BlockSpec((tm, c), lambda i: (i, 0))
    return pl.pallas_call(
        functools.partial(_combine_kernel, tm=tm),
        grid=(n_tiles,),
        in_specs=[pl.BlockSpec(memory_space=pl.ANY), pl.BlockSpec(memory_space=pl.ANY),
                  rows(D_MODEL), rows(LANES), pl.BlockSpec(g_final.shape, lambda i: (0, 0))],
        out_specs=rows(D_MODEL),
        out_shape=jax.ShapeDtypeStruct((n, D_MODEL), F32),
        scratch_shapes=[pltpu.SMEM((2, 2, tm // LANES, LANES), jnp.int32), pltpu.VMEM((2, 2, tm, D_MODEL), F32),
                        pltpu.SemaphoreType.DMA((2,)), pltpu.SemaphoreType.DMA((2,))],
        compiler_params=_cparams("arbitrary"),
        name="moe_combine",
    )(slots, ys, xn, gates, g_final)


def _moe(x2d, ya, yb, yc, wo, g_ffn, w_router, b_router, wg, wu, wd, g_final):
    n = x2d.shape[0]
    bm = _row_tile(n, 512)
    cap = n
    xn, h, gates, slots, counts = _out_router(x2d, ya, yb, yc, wo, g_ffn, w_router, b_router, cap)
    counts = counts[0, :N_EXPERTS]
    hs = _dispatch(counts, slots, h, cap, bm)
    ys = _experts(counts, hs, wg, wu, wd, cap, bm, 2 * n)
    return _combine(slots, ys, xn, gates, g_final)


IN_SIZES = (2 * CONV_CH, SSM_INNER, SSM_XBC, SSM_HEADS, GDN_QKV, GDN_INNER, GDN_HEADS, GDN_HEADS)


def _prep_layer(l, p):
    off = np.concatenate([[0], np.cumsum(IN_SIZES)])
    w_in = p['w_in_bf16'][l]
    col = lambda i: w_in[:, off[i]:off[i + 1]]
    a_in, z, xbc, dt, qkv, gate, b_raw, a_raw = (col(i) for i in range(8))
    small = jnp.concatenate([dt, b_raw, a_raw, jnp.zeros((D_MODEL, SMALL_W - 3 * SSM_HEADS), BF16)], axis=1)
    wo = p['w_out'][l].astype(BF16)
    return dict(
        wa=a_in,
        wzx=jnp.concatenate([z, xbc], axis=1),
        wqg=jnp.concatenate([qkv, gate], axis=1),
        ws=small,
        wo=(wo[:CONV_CH], wo[CONV_CH:CONV_CH + SSM_INNER], wo[CONV_CH + SSM_INNER:]),
    )


def _trunk(x, st_conv_a, st_ssm_conv, st_ssm, st_gdn_conv, st_gdn, p, prep, ssd_rows, gdn_rows, bb):
    bsz, t, _ = x.shape
    n = bsz * t
    depth = p['g_mix'].shape[0]
    x2d = x.reshape(n, D_MODEL)
    new = [[] for _ in range(5)]
    for l in range(depth):
        w = prep[l]
        a_in, zx, qg, small, hist_x, hist_q = _norm_proj(
            x2d.reshape(bsz, t, D_MODEL), p['g_mix'][l][None, :], w['wa'], w['wzx'], w['wqg'], w['ws'],
            st_ssm_conv[l], st_gdn_conv[l], p['ssm_conv_w'][l], p['ssm_conv_b'][l][None, :], p['gdn_conv_w'][l])
        a_in = a_in.reshape(bsz, t, -1)
        zx = zx.reshape(bsz, t, -1)
        qg = qg.reshape(bsz, t, -1)
        small = small.reshape(bsz, t, -1)
        ya, conv_a = _conv_a(a_in, st_conv_a[l], p['conv_a_w'][l], p['conv_a_b'][l][None, :],
                             p['ln_a_g'][l][None, :], p['ln_a_b'][l][None, :])
        yb, ssm = _ssd(zx, small, st_ssm[l].astype(F32), p['ssm_dt_bias'][l], p['ssm_a_log'][l],
                       p['ssm_d'][l], p['ssm_norm_g'][l], ssd_rows, bb)
        yc, gdn = _gdn(qg, small, st_gdn[l].astype(F32), p['gdn_a_log'][l], p['gdn_dt_bias'][l],
                       p['gdn_norm_g'][l], gdn_rows[0], gdn_rows[1], bb)
        new[0].append(conv_a)
        new[1].append(hist_x[:, HIST_PAD - (SSM_CONV_W - 1):])
        new[2].append(ssm)
        new[3].append(hist_q[:, HIST_PAD - (GDN_CONV_W - 1):])
        new[4].append(gdn)
        flat = lambda y: y.reshape(n, -1)
        g_ffn = p['g_ffn'][l][None, :]
        if l % 2 == 0:
            x2d = _out_ffn(x2d, flat(ya), flat(yb), flat(yc), w['wo'], g_ffn,
                           prep['ffn'][l // 2][0], prep['ffn'][l // 2][1], prep['ffn'][l // 2][2])
        else:
            x2d = _moe(x2d, flat(ya), flat(yb), flat(yc), w['wo'], g_ffn, p['moe_w_router'][l // 2],
                       p['moe_b_router'][l // 2], *prep['moe'][l // 2], p['g_final'][None, :])
    return (x2d.reshape(bsz, t, D_MODEL),) + tuple(jnp.stack(s) for s in new)


def kernel(x_prompt, x_sample, state_conv_a, state_ssm_conv, state_ssm, state_gdn_conv, state_gdn, g_mix, w_in, conv_a_w, conv_a_b, ln_a_g, ln_a_b, ssm_conv_w, ssm_conv_b, ssm_dt_bias, ssm_a_log, ssm_d, ssm_norm_g, gdn_conv_w, gdn_a_log, gdn_dt_bias, gdn_norm_g, w_out, g_ffn, ffn_w_gate, ffn_w_up, ffn_w_down, moe_w_router, moe_b_router, moe_w_gate, moe_w_up, moe_w_down, g_final):
    p = dict(g_mix=g_mix, w_in=w_in, conv_a_w=conv_a_w, conv_a_b=conv_a_b, ln_a_g=ln_a_g, ln_a_b=ln_a_b,
             ssm_conv_w=ssm_conv_w, ssm_conv_b=ssm_conv_b, ssm_dt_bias=ssm_dt_bias, ssm_a_log=ssm_a_log,
             ssm_d=ssm_d, ssm_norm_g=ssm_norm_g, gdn_conv_w=gdn_conv_w, gdn_a_log=gdn_a_log,
             gdn_dt_bias=gdn_dt_bias, gdn_norm_g=gdn_norm_g, w_out=w_out, g_ffn=g_ffn,
             moe_w_router=moe_w_router, moe_b_router=moe_b_router, g_final=g_final)
    depth = g_mix.shape[0]
    assert depth % 2 == 0, "the final RMSNorm is fused into the expert layer, which must come last"
    p['w_in_bf16'] = w_in.astype(BF16)
    prep = {l: _prep_layer(l, p) for l in range(depth)}
    prep['ffn'] = [(ffn_w_gate[i].astype(BF16), ffn_w_up[i].astype(BF16), ffn_w_down[i].astype(BF16))
                   for i in range(ffn_w_gate.shape[0])]
    prep['moe'] = [(moe_w_gate[i].astype(BF16), moe_w_up[i].astype(BF16), moe_w_down[i].astype(BF16))
                   for i in range(moe_w_gate.shape[0])]
    bp, dt = x_prompt.shape[0], x_prompt.dtype
    zeros = lambda *s: jnp.zeros((depth, bp) + s, dt)
    outs_p = _trunk(x_prompt, zeros(CONV_W - 1, CONV_CH), zeros(SSM_CONV_W - 1, SSM_XBC),
                    zeros(SSM_HEADS, HEAD_DIM, SSM_STATE), zeros(GDN_CONV_W - 1, GDN_QKV),
                    zeros(GDN_HEADS, HEAD_DIM, HEAD_DIM), p, prep, ssd_rows=min(128, x_prompt.shape[1]),
                    gdn_rows=(min(256, x_prompt.shape[1]), min(64, x_prompt.shape[1])), bb=2)
    outs_s = _trunk(x_sample, state_conv_a, state_ssm_conv, state_ssm, state_gdn_conv, state_gdn, p, prep,
                    ssd_rows=x_sample.shape[1], gdn_rows=(x_sample.shape[1], x_sample.shape[1]), bb=2)
    return (outs_p[0], outs_s[0]) + outs_p[1:] + outs_s[1:]
```
